```python
import math
import jax, jax.numpy as jnp
from jax import lax
import numpy as np

D_MODEL = 2048
BATCH = 4
SEQ = 2048
DEPTH = 4

EPS = 1e-6
D_FF = 5632
MLA_HEADS = 8
MLA_Q_LORA = 512
MLA_KV_LORA = 512
MLA_NOPE = 128
MLA_ROPE = 64
MLA_V = 128
ROPE_BASE = 10000.0
NSA_HEADS = 8
NSA_KV_HEADS = 2
NSA_GROUP = NSA_HEADS // NSA_KV_HEADS
NSA_DK = 192
NSA_DV = 128
CMP_LEN = 32
CMP_STRIDE = 16
CMP_HIDDEN = 256
SLC_LEN = 64
SLC_TOPN = 16
WINDOW = 512
FORCED_SCORE = 1e6
REL_BUCKETS = 32
REL_MAX_DIST = 128
Q_BLOCK = 128
SLC_Q_BLOCK = 64
NEG = -1e30

IN_SPLITS = [MLA_Q_LORA, MLA_KV_LORA, MLA_ROPE,
             NSA_HEADS * NSA_DK,
             NSA_KV_HEADS * NSA_DK, NSA_KV_HEADS * NSA_DV,
             NSA_KV_HEADS * NSA_DK, NSA_KV_HEADS * NSA_DV,
             NSA_KV_HEADS * NSA_DK, NSA_KV_HEADS * NSA_DV,
             NSA_HEADS * 3, 2 * D_MODEL]
IN_OFFSETS = [int(v) for v in np.cumsum(IN_SPLITS)[:-1]]
D_IN = int(sum(IN_SPLITS))

kernel_name = "hybrid_mla_nsa_macaron_block"


def rmsnorm(x, g):
    x32 = x.astype(jnp.float32)
    y = x32 * lax.rsqrt(jnp.mean(x32 * x32, axis=-1, keepdims=True) + EPS)
    return (y * g.astype(jnp.float32)).astype(x.dtype)


def swiglu_half_step(x, pre_g, post_g, w_gate, w_up, w_down):
    u = rmsnorm(x, pre_g)
    y = (jax.nn.silu(u @ w_gate) * (u @ w_up)) @ w_down
    return x + 0.5 * rmsnorm(y, post_g)


def rope(x, positions):
    d = x.shape[-1]
    half = d // 2
    inv = ROPE_BASE ** (-jnp.arange(half, dtype=jnp.float32) * 2.0 / d)
    ang = positions.astype(jnp.float32)[..., None] * inv
    cos = jnp.cos(ang)[:, :, None, :]
    sin = jnp.sin(ang)[:, :, None, :]
    x32 = x.astype(jnp.float32)
    x1, x2 = x32[..., :half], x32[..., half:]
    return jnp.concatenate([x1 * cos - x2 * sin, x1 * sin + x2 * cos], axis=-1).astype(x.dtype)


def rel_bucket(dist):
    n = jnp.maximum(dist, 0)
    max_exact = REL_BUCKETS // 2
    large = max_exact + (jnp.log(jnp.maximum(n, 1).astype(jnp.float32) / max_exact)
                         / math.log(REL_MAX_DIST / max_exact)
                         * (REL_BUCKETS - max_exact)).astype(jnp.int32)
    large = jnp.minimum(large, REL_BUCKETS - 1)
    return jnp.where(n < max_exact, n, large).astype(jnp.int32)


def mla_mixer(c_q, c_kv, k_rope, positions, q_norm_g, w_q_up, kv_norm_g, w_uk, w_uv):
    B, S, _ = c_q.shape
    H = MLA_HEADS
    q = (rmsnorm(c_q, q_norm_g) @ w_q_up).reshape(B, S, H, MLA_NOPE + MLA_ROPE)
    q = jnp.concatenate([q[..., :MLA_NOPE], rope(q[..., MLA_NOPE:], positions)], axis=-1)
    ckv = rmsnorm(c_kv, kv_norm_g)
    k_nope = (ckv @ w_uk).reshape(B, S, H, MLA_NOPE)
    v = (ckv @ w_uv).reshape(B, S, H, MLA_V)
    k_pe = rope(k_rope[:, :, None, :], positions)
    k = jnp.concatenate([k_nope, jnp.broadcast_to(k_pe, (B, S, H, MLA_ROPE))], axis=-1)
    scale = (MLA_NOPE + MLA_ROPE) ** -0.5
    n_blk = S // Q_BLOCK
    q_blocks = q.reshape(B, n_blk, Q_BLOCK, H, -1).transpose(1, 0, 2, 3, 4)
    kpos = jnp.arange(S)

    def block(args):
        qi, i = args
        tq = i * Q_BLOCK + jnp.arange(Q_BLOCK)
        s = jnp.einsum('bqhd,bkhd->bhqk', qi, k).astype(jnp.float32) * scale
        s = jnp.where(kpos[None, :] <= tq[:, None], s, NEG)
        p = jax.nn.softmax(s, axis=-1).astype(v.dtype)
        return jnp.einsum('bhqk,bkhd->bqhd', p, v)

    out = lax.map(block, (q_blocks, jnp.arange(n_blk, dtype=jnp.int32)))
    return out.transpose(1, 0, 2, 3, 4).reshape(B, S, H * MLA_V)


def nsa_mixer(q, k_c, v_c, k_s, v_s, k_w, v_w, gate_logits, positions, rel_bias,
              pe_k, w1_k, w2_k, pe_v, w1_v, w2_v):
    B, S, _ = q.shape
    G, J = NSA_KV_HEADS, NSA_GROUP
    scale = NSA_DK ** -0.5
    q = q.reshape(B, S, G, J, NSA_DK)
    k_c = k_c.reshape(B, S, G, NSA_DK)
    v_c = v_c.reshape(B, S, G, NSA_DV)
    k_s = k_s.reshape(B, S, G, NSA_DK)
    v_s = v_s.reshape(B, S, G, NSA_DV)
    k_w = k_w.reshape(B, S, G, NSA_DK)
    v_w = v_w.reshape(B, S, G, NSA_DV)
    t = jnp.arange(S)

    n_cmp = (S - CMP_LEN) // CMP_STRIDE + 1
    idx = np.arange(n_cmp)[:, None] * CMP_STRIDE + np.arange(CMP_LEN)[None, :]

    def compress(z, pe, w1, w2):
        zb = z[:, idx] + pe[None, None, :, None, :]
        zb = zb.transpose(0, 1, 3, 2, 4).reshape(B, n_cmp, G, -1)
        return jax.nn.silu(zb @ w1) @ w2

    kc = compress(k_c, pe_k, w1_k, w2_k)
    vc = compress(v_c, pe_v, w1_v, w2_v)
    ends = jnp.asarray(idx[:, -1])
    mask_c = ends[None, :] <= t[:, None]
    dist_c = positions[:, :, None] - positions[:, ends][:, None, :]
    bias_c = rel_bias[rel_bucket(dist_c)].astype(jnp.float32)
    bias_c = bias_c.reshape(B, S, n_cmp, G, J).transpose(0, 3, 4, 1, 2)
    s_c = jnp.einsum('bsgjd,bngd->bgjsn', q, kc).astype(jnp.float32) * scale + bias_c
    p_c = jnp.where(mask_c, jax.nn.softmax(jnp.where(mask_c, s_c, NEG), axis=-1), 0.0)
    o_cmp = jnp.einsum('bgjsn,bngd->bsgjd', p_c.astype(vc.dtype), vc)

    n_slc = S // SLC_LEN
    cs = np.arange(n_cmp) * CMP_STRIDE
    ce = cs + CMP_LEN - 1
    bs = np.arange(n_slc) * SLC_LEN
    be = bs + SLC_LEN - 1
    overlap = ((cs[:, None] <= be[None, :]) & (ce[:, None] >= bs[None, :])).astype(np.float32)
    imp = jnp.einsum('bgjsn,nm->bgsm', p_c, jnp.asarray(overlap))
    cur = t // SLC_LEN
    jb = jnp.arange(n_slc)
    valid = jb[None, :] <= cur[:, None]
    forced = valid & ((jb[None, :] == 0) | (jb[None, :] >= cur[:, None] - 1))
    score = jnp.where(forced, FORCED_SCORE, jnp.where(valid, imp, -1.0))
    topn = min(SLC_TOPN, n_slc)
    vals, sel = lax.top_k(score, topn)
    sel_ok = vals >= 0.0

    k_blk = k_s.reshape(B, n_slc, SLC_LEN, G, NSA_DK).transpose(0, 3, 1, 2, 4)
    v_blk = v_s.reshape(B, n_slc, SLC_LEN, G, NSA_DV).transpose(0, 3, 1, 2, 4)
    nq = S // SLC_Q_BLOCK
    q_ch = q.reshape(B, nq, SLC_Q_BLOCK, G, J, NSA_DK).transpose(1, 0, 2, 3, 4, 5)
    sel_ch = sel.reshape(B, G, nq, SLC_Q_BLOCK, topn).transpose(2, 0, 1, 3, 4)
    ok_ch = sel_ok.reshape(B, G, nq, SLC_Q_BLOCK, topn).transpose(2, 0, 1, 3, 4)
    table_g = rel_bias.reshape(REL_BUCKETS, G, J)
    g_idx = jnp.arange(G).reshape(1, G, 1, 1, 1)
    gather = jax.vmap(jax.vmap(lambda blocks, ix: blocks[ix]))
    pos_gather = jax.vmap(lambda p, k: p[k])

    def slc_block(args):
        qi, si, oki, i = args
        tq = i * SLC_Q_BLOCK + jnp.arange(SLC_Q_BLOCK)
        ks = gather(k_blk, si)
        vs = gather(v_blk, si)
        kpos = si[..., None] * SLC_LEN + jnp.arange(SLC_LEN)
        mask = oki[..., None] & (kpos <= tq[None, None, :, None, None])
        qpos = lax.dynamic_slice_in_dim(positions, i * SLC_Q_BLOCK, SLC_Q_BLOCK, axis=1)
        kp = pos_gather(positions, kpos)
        bias = table_g[rel_bucket(qpos[:, None, :, None, None] - kp), g_idx]
        bias = jnp.moveaxis(bias, -1, 2).astype(jnp.float32)
        s = jnp.einsum('bqgjd,bgqnrd->bgjqnr', qi, ks).astype(jnp.float32) * scale + bias
        s = jnp.where(mask[:, :, None], s, NEG)
        shp = s.shape
        p = jax.nn.softmax(s.reshape(shp[:4] + (-1,)), axis=-1).reshape(shp)
        return jnp.einsum('bgjqnr,bgqnrd->bqgjd', p.astype(vs.dtype), vs)

    o_slc = lax.map(slc_block, (q_ch, sel_ch, ok_ch, jnp.arange(nq, dtype=jnp.int32)))
    o_slc = o_slc.transpose(1, 0, 2, 3, 4, 5).reshape(B, S, G, J, NSA_DV)

    span = WINDOW + Q_BLOCK
    kp_w = jnp.pad(k_w, ((0, 0), (WINDOW, 0), (0, 0), (0, 0)))
    vp_w = jnp.pad(v_w, ((0, 0), (WINDOW, 0), (0, 0), (0, 0)))
    pos_p = jnp.pad(positions, ((0, 0), (WINDOW, 0)))
    nb = S // Q_BLOCK
    q_wb = q.reshape(B, nb, Q_BLOCK, G, J, NSA_DK).transpose(1, 0, 2, 3, 4, 5)

    def win_block(args):
        qi, i = args
        s0 = i * Q_BLOCK
        kw = lax.dynamic_slice_in_dim(kp_w, s0, span, axis=1)
        vw = lax.dynamic_slice_in_dim(vp_w, s0, span, axis=1)
        pw = lax.dynamic_slice_in_dim(pos_p, s0, span, axis=1)
        kidx = s0 - WINDOW + jnp.arange(span)
        tq = s0 + jnp.arange(Q_BLOCK)
        diff = tq[:, None] - kidx[None, :]
        mask = (diff >= 0) & (diff < WINDOW) & (kidx >= 0)[None, :]
        qpos = lax.dynamic_slice_in_dim(positions, s0, Q_BLOCK, axis=1)
        bias = rel_bias[rel_bucket(qpos[:, :, None] - pw[:, None, :])].astype(jnp.float32)
        bias = bias.reshape(B, Q_BLOCK, span, G, J).transpose(0, 3, 4, 1, 2)
        s = jnp.einsum('bqgjd,bkgd->bgjqk', qi, kw).astype(jnp.float32) * scale + bias
        p = jax.nn.softmax(jnp.where(mask, s, NEG), axis=-1)
        return jnp.einsum('bgjqk,bkgd->bqgjd', p.astype(vw.dtype), vw)

    o_win = lax.map(win_block, (q_wb, jnp.arange(nb, dtype=jnp.int32)))
    o_win = o_win.transpose(1, 0, 2, 3, 4, 5).reshape(B, S, G, J, NSA_DV)

    g = jax.nn.sigmoid(gate_logits.astype(jnp.float32)).reshape(B, S, G, J, 3).astype(q.dtype)
    o = g[..., 0:1] * o_cmp + g[..., 1:2] * o_slc + g[..., 2:3] * o_win
    return o.reshape(B, S, NSA_HEADS * NSA_DV)


def setup_inputs(seed: int = 0) -> dict:
    key = jax.random.key(seed)
    ks = iter(jax.random.split(key, 40))
    f32 = jnp.float32

    def dense(shape, fan_in):
        return jax.random.normal(next(ks), shape, f32) * (fan_in ** -0.5)

    def gain(n):
        return 1.0 + 0.05 * jax.random.normal(next(ks), (DEPTH, n), f32)

    x = jax.random.normal(next(ks), (BATCH, SEQ, D_MODEL), f32)
    offs = jax.random.randint(next(ks), (BATCH, 1), 0, 4096, dtype=jnp.int32)
    positions = offs + jnp.arange(SEQ, dtype=jnp.int32)[None, :]
    rel_bias = 0.5 * jax.random.normal(next(ks), (REL_BUCKETS, NSA_HEADS), f32)
    inp = {"x": x, "positions": positions, "rel_bias": rel_bias}
    inp["ffn1_pre_g"] = gain(D_MODEL)
    inp["ffn1_post_g"] = gain(D_MODEL)
    inp["ffn1_w_gate"] = dense((DEPTH, D_MODEL, D_FF), D_MODEL)
    inp["ffn1_w_up"] = dense((DEPTH, D_MODEL, D_FF), D_MODEL)
    inp["ffn1_w_down"] = dense((DEPTH, D_FF, D_MODEL), D_FF)
    inp["mix_pre_g"] = gain(D_MODEL)
    inp["mix_post_g"] = gain(D_MODEL)
    inp["w_in"] = dense((DEPTH, D_MODEL, D_IN), D_MODEL)
    inp["mla_q_norm_g"] = gain(MLA_Q_LORA)
    inp["mla_w_q_up"] = dense((DEPTH, MLA_Q_LORA, MLA_HEADS * (MLA_NOPE + MLA_ROPE)), MLA_Q_LORA)
    inp["mla_kv_norm_g"] = gain(MLA_KV_LORA)
    inp["mla_w_uk"] = dense((DEPTH, MLA_KV_LORA, MLA_HEADS * MLA_NOPE), MLA_KV_LORA)
    inp["mla_w_uv"] = dense((DEPTH, MLA_KV_LORA, MLA_HEADS * MLA_V), MLA_KV_LORA)
    inp["cmp_pe_k"] = 0.1 * jax.random.normal(next(ks), (DEPTH, CMP_LEN, NSA_DK), f32)
    inp["cmp_w1_k"] = dense((DEPTH, CMP_LEN * NSA_DK, CMP_HIDDEN), CMP_LEN * NSA_DK)
    inp["cmp_w2_k"] = dense((DEPTH, CMP_HIDDEN, NSA_DK), CMP_HIDDEN)
    inp["cmp_pe_v"] = 0.1 * jax.random.normal(next(ks), (DEPTH, CMP_LEN, NSA_DV), f32)
    inp["cmp_w1_v"] = dense((DEPTH, CMP_LEN * NSA_DV, CMP_HIDDEN), CMP_LEN * NSA_DV)
    inp["cmp_w2_v"] = dense((DEPTH, CMP_HIDDEN, NSA_DV), CMP_HIDDEN)
    inp["w_branch_mla"] = dense((DEPTH, MLA_HEADS * MLA_V, D_MODEL), MLA_HEADS * MLA_V)
    inp["w_branch_nsa"] = dense((DEPTH, NSA_HEADS * NSA_DV, D_MODEL), NSA_HEADS * NSA_DV)
    inp["w_out"] = dense((DEPTH, D_MODEL, D_MODEL), D_MODEL)
    inp["ffn2_pre_g"] = gain(D_MODEL)
    inp["ffn2_post_g"] = gain(D_MODEL)
    inp["ffn2_w_gate"] = dense((DEPTH, D_MODEL, D_FF), D_MODEL)
    inp["ffn2_w_up"] = dense((DEPTH, D_MODEL, D_FF), D_MODEL)
    inp["ffn2_w_down"] = dense((DEPTH, D_FF, D_MODEL), D_FF)
    return inp


def reference(x, positions, rel_bias,
              ffn1_pre_g, ffn1_post_g, ffn1_w_gate, ffn1_w_up, ffn1_w_down,
              mix_pre_g, mix_post_g, w_in,
              mla_q_norm_g, mla_w_q_up, mla_kv_norm_g, mla_w_uk, mla_w_uv,
              cmp_pe_k, cmp_w1_k, cmp_w2_k, cmp_pe_v, cmp_w1_v, cmp_w2_v,
              w_branch_mla, w_branch_nsa, w_out,
              ffn2_pre_g, ffn2_post_g, ffn2_w_gate, ffn2_w_up, ffn2_w_down):
    for l in range(DEPTH):
        h = swiglu_half_step(x, ffn1_pre_g[l], ffn1_post_g[l], ffn1_w_gate[l], ffn1_w_up[l], ffn1_w_down[l])
        u = rmsnorm(h, mix_pre_g[l])
        z = u @ w_in[l]
        (c_q, c_kv, k_rope, nsa_q, k_c, v_c, k_s, v_s, k_w, v_w,
         nsa_g, merge_g) = jnp.split(z, IN_OFFSETS, axis=-1)
        a = mla_mixer(c_q, c_kv, k_rope, positions, mla_q_norm_g[l], mla_w_q_up[l],
                      mla_kv_norm_g[l], mla_w_uk[l], mla_w_uv[l])
        b = nsa_mixer(nsa_q, k_c, v_c, k_s, v_s, k_w, v_w, nsa_g, positions, rel_bias,
                      cmp_pe_k[l], cmp_w1_k[l], cmp_w2_k[l], cmp_pe_v[l], cmp_w1_v[l], cmp_w2_v[l])
        gates = jax.nn.sigmoid(merge_g.astype(jnp.float32)).astype(h.dtype)
        m = gates[..., :D_MODEL] * (a @ w_branch_mla[l]) + gates[..., D_MODEL:] * (b @ w_branch_nsa[l])
        h = h + rmsnorm(m @ w_out[l], mix_post_g[l])
        x = swiglu_half_step(h, ffn2_pre_g[l], ffn2_post_g[l], ffn2_w_gate[l], ffn2_w_up[l], ffn2_w_down[l])
    return x
```

```python
import functools
import math

import numpy as np
import jax
import jax.numpy as jnp
from jax import lax
from jax.experimental import pallas as pl
from jax.experimental.pallas import tpu as pltpu

EPS = 1e-6
MLA_HEADS = 8
MLA_Q_LORA = 512
MLA_KV_LORA = 512
MLA_NOPE = 128
MLA_ROPE = 64
MLA_V = 128
ROPE_BASE = 10000.0
NSA_HEADS = 8
NSA_KV_HEADS = 2
NSA_GROUP = NSA_HEADS // NSA_KV_HEADS
NSA_DK = 192
NSA_DV = 128
CMP_LEN = 32
CMP_STRIDE = 16
CMP_HIDDEN = 256
SLC_LEN = 64
SLC_TOPN = 16
WINDOW = 512
FORCED_SCORE = 1e6
REL_BUCKETS = 32
REL_MAX_DIST = 128
NEG = -1e30

LANE = 128
HEAD_PAD = 256
VMEM_LIMIT = 56 * 1024 * 1024
BF16 = jnp.bfloat16
F32 = jnp.float32

FFN_TM, FFN_TF = 512, 512
PROJ_TM = 1024
PREP_TM = 512
MLA_TQ, MLA_TK = 256, 512
NSA_TQ, NSA_TK = 128, 256
MERGE_TM = 256


def _bucket_of_distance():
    n = np.arange(LANE)
    max_exact = REL_BUCKETS // 2
    large = max_exact + (np.log(np.maximum(n, 1) / max_exact) / math.log(REL_MAX_DIST / max_exact)
                         * (REL_BUCKETS - max_exact)).astype(np.int32)
    bucket = np.where(n < max_exact, n, np.minimum(large, REL_BUCKETS - 1)).astype(np.int32)
    assert bucket[-1] == REL_BUCKETS - 1
    return bucket


BUCKET_OF_DIST = _bucket_of_distance()
FAR_DIST = int(np.max(np.nonzero(BUCKET_OF_DIST != REL_BUCKETS - 1)[0])) + 1


def _params(*sem):
    return pltpu.CompilerParams(dimension_semantics=sem, vmem_limit_bytes=VMEM_LIMIT)


def _rms(x, g):
    return x * lax.rsqrt(jnp.mean(x * x, axis=-1, keepdims=True) + EPS) * g


def _dot(a, b):
    return jnp.dot(a, b, preferred_element_type=F32)


def _dot_nt(a, b):
    return lax.dot_general(a, b, (((1,), (1,)), ((), ())), preferred_element_type=F32)


def _tile_n(n, cap):
    best = LANE
    for t in range(LANE, cap + 1, LANE):
        if n % t == 0:
            best = t
    return best


def _rmsnorm_kernel(x_ref, g_ref, o_ref):
    o_ref[...] = _rms(x_ref[...], g_ref[...]).astype(o_ref.dtype)


def _rmsnorm(x, g, layer):
    m, d = x.shape
    tm = 512
    return pl.pallas_call(
        _rmsnorm_kernel,
        grid=(m // tm,),
        in_specs=[pl.BlockSpec((tm, d), lambda i: (i, 0)),
                  pl.BlockSpec((None, 1, d), lambda i: (layer, 0, 0))],
        out_specs=pl.BlockSpec((tm, d), lambda i: (i, 0)),
        out_shape=jax.ShapeDtypeStruct((m, d), BF16),
        compiler_params=_params("parallel"),
        name="rmsnorm",
    )(x, g)


def _rope_table_kernel(pos_ref, inv_ref, o_ref):
    ang = pos_ref[...].astype(F32) * inv_ref[...]
    o_ref[...] = jnp.concatenate([jnp.cos(ang), jnp.sin(ang)], axis=1)


def _rope_table(pos_col, inv):
    m = pos_col.shape[0]
    tm = 512
    return pl.pallas_call(
        _rope_table_kernel,
        grid=(m // tm,),
        in_specs=[pl.BlockSpec((tm, 1), lambda i: (i, 0)),
                  pl.BlockSpec((1, MLA_ROPE), lambda i: (0, 0))],
        out_specs=pl.BlockSpec((tm, 2 * MLA_ROPE), lambda i: (i, 0)),
        out_shape=jax.ShapeDtypeStruct((m, 2 * MLA_ROPE), F32),
        compiler_params=_params("parallel"),
        name="rope_table",
    )(pos_col, inv)


def _ffn_kernel(x_ref, u_ref, post_g_ref, next_g_ref, wg_ref, wu_ref, wd_ref,
                o_ref, un_ref, acc_ref):
    j = pl.program_id(1)
    u = u_ref[...]
    gate = _dot(u, wg_ref[...])
    up = _dot(u, wu_ref[...])
    hidden = (gate * jax.nn.sigmoid(gate) * up).astype(BF16)
    part = _dot(hidden, wd_ref[...])

    @pl.when(j == 0)
    def _():
        acc_ref[...] = part

    @pl.when(j > 0)
    def _():
        acc_ref[...] += part

    @pl.when(j == pl.num_programs(1) - 1)
    def _():
        out = x_ref[...] + 0.5 * _rms(acc_ref[...], post_g_ref[...])
        o_ref[...] = out
        un_ref[...] = _rms(out, next_g_ref[...]).astype(BF16)


def _ffn(x, u, post_g, next_g, next_layer, wg, wu, wd, layer):
    m, d = x.shape
    f = wg.shape[-1]
    tm, tf = FFN_TM, _tile_n(f, FFN_TF)
    return pl.pallas_call(
        _ffn_kernel,
        grid=(m // tm, f // tf),
        in_specs=[pl.BlockSpec((tm, d), lambda i, j: (i, 0)),
                  pl.BlockSpec((tm, d), lambda i, j: (i, 0)),
                  pl.BlockSpec((None, 1, d), lambda i, j: (layer, 0, 0)),
                  pl.BlockSpec((None, 1, d), lambda i, j: (next_layer, 0, 0)),
                  pl.BlockSpec((None, d, tf), lambda i, j: (layer, 0, j)),
                  pl.BlockSpec((None, d, tf), lambda i, j: (layer, 0, j)),
                  pl.BlockSpec((None, tf, d), lambda i, j: (layer, j, 0))],
        out_specs=[pl.BlockSpec((tm, d), lambda i, j: (i, 0)),
                   pl.BlockSpec((tm, d), lambda i, j: (i, 0))],
        out_shape=[jax.ShapeDtypeStruct((m, d), F32), jax.ShapeDtypeStruct((m, d), BF16)],
        scratch_shapes=[pltpu.VMEM((tm, d), F32)],
        compiler_params=_params("parallel", "arbitrary"),
        name="ffn",
    )(x, u, post_g, next_g, wg, wu, wd)


def _matmul_kernel(x_ref, w_ref, o_ref):
    o_ref[...] = _dot(x_ref[...], w_ref[...]).astype(o_ref.dtype)


def _in_proj(u, w, layer):
    m, d = u.shape
    n = w.shape[-1]
    tm, tn = PROJ_TM, _tile_n(n, 2560)
    return pl.pallas_call(
        _matmul_kernel,
        grid=(n // tn, m // tm),
        in_specs=[pl.BlockSpec((tm, d), lambda j, i: (i, 0)),
                  pl.BlockSpec((None, d, tn), lambda j, i: (layer, 0, j))],
        out_specs=pl.BlockSpec((tm, tn), lambda j, i: (i, j)),
        out_shape=jax.ShapeDtypeStruct((m, n), BF16),
        compiler_params=_params("parallel", "parallel"),
        name="in_proj",
    )(u, w)


def _mla_prep_kernel(cq_ref, ckv_ref, kr_ref, cs_ref, gq_ref, gkv_ref, wq_ref, wkv_ref,
                     q_ref, k_ref, v_ref):
    cs = cs_ref[...]
    qn = _rms(cq_ref[...].astype(F32), gq_ref[...]).astype(BF16)
    q = _dot(qn, wq_ref[...])
    kvn = _rms(ckv_ref[...].astype(F32), gkv_ref[...]).astype(BF16)
    kv = _dot(kvn, wkv_ref[...])
    t = kr_ref[...].astype(F32) * cs
    k_pe = (t + pltpu.roll(t, MLA_ROPE, 1)).astype(BF16)
    for h in range(MLA_HEADS):
        lo = h * HEAD_PAD
        q_ref[:, lo:lo + MLA_NOPE] = q[:, lo:lo + MLA_NOPE].astype(BF16)
        q_ref[:, lo + MLA_NOPE:lo + HEAD_PAD] = (q[:, lo + MLA_NOPE:lo + HEAD_PAD] * cs).astype(BF16)
        k_ref[:, lo:lo + MLA_NOPE] = kv[:, h * MLA_NOPE:(h + 1) * MLA_NOPE].astype(BF16)
        k_ref[:, lo + MLA_NOPE:lo + HEAD_PAD] = k_pe
    v_ref[...] = kv[:, MLA_HEADS * MLA_NOPE:].astype(BF16)


def _mla_prep(z, cs, gq, gkv, wq, wkv, layer, zoff):
    m = z.shape[0]
    tm = PREP_TM
    hq = MLA_HEADS * HEAD_PAD
    hv = MLA_HEADS * MLA_V
    return pl.pallas_call(
        _mla_prep_kernel,
        grid=(m // tm,),
        in_specs=[pl.BlockSpec((tm, MLA_Q_LORA), lambda i: (i, zoff["cq"] // MLA_Q_LORA)),
                  pl.BlockSpec((tm, MLA_KV_LORA), lambda i: (i, zoff["ckv"] // MLA_KV_LORA)),
                  pl.BlockSpec((tm, LANE), lambda i: (i, zoff["kr"] // LANE)),
                  pl.BlockSpec((tm, LANE), lambda i: (i, 0)),
                  pl.BlockSpec((None, 1, MLA_Q_LORA), lambda i: (layer, 0, 0)),
                  pl.BlockSpec((None, 1, MLA_KV_LORA), lambda i: (layer, 0, 0)),
                  pl.BlockSpec((None, MLA_Q_LORA, hq), lambda i: (layer, 0, 0)),
                  pl.BlockSpec((None, MLA_KV_LORA, 2 * hv), lambda i: (layer, 0, 0))],
        out_specs=[pl.BlockSpec((tm, hq), lambda i: (i, 0)),
                   pl.BlockSpec((tm, hq), lambda i: (i, 0)),
                   pl.BlockSpec((tm, hv), lambda i: (i, 0))],
        out_shape=[jax.ShapeDtypeStruct((m, hq), BF16),
                   jax.ShapeDtypeStruct((m, hq), BF16),
                   jax.ShapeDtypeStruct((m, hv), BF16)],
        compiler_params=_params("parallel"),
        name="mla_prep",
    )(z, z, z, cs, gq, gkv, wq, wkv)


def _masked(mask, x, fill):
    rows, tk = x.shape
    tq = mask.shape[0]
    return jnp.where(mask[None], x.reshape(rows // tq, tq, tk), fill).reshape(rows, tk)


def _softmax_tile(s, mask, v, m_ref, l_ref, acc_ref):
    if mask is not None:
        s = _masked(mask, s, NEG)
    m_old = m_ref[...]
    m_new = jnp.maximum(m_old, jnp.max(s, axis=-1, keepdims=True))
    alpha = jnp.exp(m_old - m_new)
    p = jnp.exp(s - m_new)
    if mask is not None:
        p = _masked(mask, p, 0.0)
    l_ref[...] = alpha * l_ref[...] + jnp.sum(p, axis=-1, keepdims=True)
    acc_ref[...] = alpha * acc_ref[...] + _dot(p.astype(BF16), v)
    m_ref[...] = m_new


def _mla_attn_kernel(q_ref, k_ref, v_ref, o_ref, m_ref, l_ref, acc_ref, *, tq, tk, scale):
    i = pl.program_id(2)
    q = q_ref[...]
    m_ref[...] = jnp.full(m_ref.shape, NEG, F32)
    l_ref[...] = jnp.zeros(l_ref.shape, F32)
    acc_ref[...] = jnp.zeros(acc_ref.shape, F32)
    row = i * tq + lax.broadcasted_iota(jnp.int32, (tq, 1), 0)
    col = lax.broadcasted_iota(jnp.int32, (1, tk), 1)

    def tile(kt, masked):
        ks = pl.multiple_of(kt * tk, tk)
        s = _dot_nt(q, k_ref[pl.ds(ks, tk), :]) * scale
        mask = (ks + col <= row) if masked else None
        _softmax_tile(s, mask, v_ref[pl.ds(ks, tk), :], m_ref, l_ref, acc_ref)

    n_full = (i * tq) // tk
    n_all = (i * tq + tq + tk - 1) // tk

    def full_body(kt, c):
        tile(kt, False)
        return c

    def diag_body(kt, c):
        tile(kt, True)
        return c

    lax.fori_loop(0, n_full, full_body, 0)
    lax.fori_loop(n_full, n_all, diag_body, 0)
    o_ref[...] = (acc_ref[...] / l_ref[...]).astype(o_ref.dtype)


def _mla_attn(qp, kp, v, batch, seq):
    tq, tk = MLA_TQ, MLA_TK
    nq = seq // tq
    kern = functools.partial(_mla_attn_kernel, tq=tq, tk=tk,
                             scale=(MLA_NOPE + MLA_ROPE) ** -0.5)
    return pl.pallas_call(
        kern,
        grid=(batch, MLA_HEADS, nq),
        in_specs=[pl.BlockSpec((tq, HEAD_PAD), lambda b, h, i: (b * nq + i, h)),
                  pl.BlockSpec((seq, HEAD_PAD), lambda b, h, i: (b, h)),
                  pl.BlockSpec((seq, MLA_V), lambda b, h, i: (b, h))],
        out_specs=pl.BlockSpec((tq, MLA_V), lambda b, h, i: (b * nq + i, h)),
        out_shape=jax.ShapeDtypeStruct((batch * seq, MLA_HEADS * MLA_V), BF16),
        scratch_shapes=[pltpu.VMEM((tq, 1), F32), pltpu.VMEM((tq, 1), F32),
                        pltpu.VMEM((tq, MLA_V), F32)],
        compiler_params=_params("parallel", "parallel", "parallel"),
        name="mla_attn",
    )(qp, kp, v)


def _compress(a_ref, pe_ref, w1_ref, w2_ref, o_ref):
    a = a_ref[...].astype(F32)
    half = a.shape[1]
    first = _dot((a + pe_ref[0:1, :]).astype(BF16), w1_ref[0:half, :])
    second = _dot((a + pe_ref[1:2, :]).astype(BF16), w1_ref[half:2 * half, :])
    n = a.shape[0]
    hidden = first + pltpu.roll(second, n - 1, 0)
    o_ref[...] = _dot((hidden * jax.nn.sigmoid(hidden)).astype(BF16), w2_ref[...]).astype(o_ref.dtype)


def _nsa_cmp_kernel(ak_ref, av_ref, pek_ref, pev_ref, w1k_ref, w2k_ref, w1v_ref, w2v_ref,
                    kc_ref, vc_ref):
    _compress(ak_ref, pek_ref, w1k_ref, w2k_ref, kc_ref)
    _compress(av_ref, pev_ref, w1v_ref, w2v_ref, vc_ref)


def _nsa_cmp(ak, av, pek, pev, w1k, w2k, w1v, w2v, layer):
    bg, nh, dk16 = ak.shape
    dv16 = av.shape[-1]
    return pl.pallas_call(
        _nsa_cmp_kernel,
        grid=(bg,),
        in_specs=[pl.BlockSpec((None, nh, dk16), lambda i: (i, 0, 0)),
                  pl.BlockSpec((None, nh, dv16), lambda i: (i, 0, 0)),
                  pl.BlockSpec((None, 2, dk16), lambda i: (layer, 0, 0)),
                  pl.BlockSpec((None, 2, dv16), lambda i: (layer, 0, 0)),
                  pl.BlockSpec((None, 2 * dk16, CMP_HIDDEN), lambda i: (layer, 0, 0)),
                  pl.BlockSpec((None, CMP_HIDDEN, HEAD_PAD), lambda i: (layer, 0, 0)),
                  pl.BlockSpec((None, 2 * dv16, CMP_HIDDEN), lambda i: (layer, 0, 0)),
                  pl.BlockSpec((None, CMP_HIDDEN, NSA_DV), lambda i: (layer, 0, 0))],
        out_specs=[pl.BlockSpec((None, nh, HEAD_PAD), lambda i: (i, 0, 0)),
                   pl.BlockSpec((None, nh, NSA_DV), lambda i: (i, 0, 0))],
        out_shape=[jax.ShapeDtypeStruct((bg, nh, HEAD_PAD), BF16),
                   jax.ShapeDtypeStruct((bg, nh, NSA_DV), BF16)],
        compiler_params=_params("parallel"),
        name="nsa_cmp",
    )(ak, av, pek, pev, w1k, w2k, w1v, w2v)


def _split3(x):
    a = x.astype(BF16)
    r = x - a.astype(F32)
    b = r.astype(BF16)
    c = (r - b.astype(F32)).astype(BF16)
    return a, b, c


def _nsa_attn_kernel(far_ref, q_ref, kc_ref, vc_ref, ks_ref, vs_ref, kw_ref, vw_ref, ng_ref,
                     pcol_ref, prow_ref, pcmp_ref, tbl_ref, o_ref,
                     m_ref, l_ref, acc_ref, *, tq, tk, seq, scale):
    b, i = pl.program_id(0), pl.program_id(2)
    nq, nk = seq // tq, seq // tk
    J = NSA_GROUP
    R = J * tq
    n_slc = seq // SLC_LEN
    t0 = i * tq

    qs = jnp.concatenate([q_ref[:, j * HEAD_PAD:(j + 1) * HEAD_PAD] for j in range(J)], axis=0)
    qpos = pcol_ref[...]
    row_t = t0 + lax.broadcasted_iota(jnp.int32, (tq, 1), 0)
    lane = lax.broadcasted_iota(jnp.int32, (tq, LANE), 1)
    tbl_rows = [jnp.broadcast_to(tbl_ref[j:j + 1, :], (tq, LANE)) for j in range(J)]
    far_bias = jnp.stack([tbl_ref[j:j + 1, LANE - 1:LANE] for j in range(J)], axis=0)

    def gathered_bias(kpos):
        idx = jnp.clip(qpos - kpos, 0, LANE - 1)
        per_head = []
        for j in range(J):
            chunks = [jnp.take_along_axis(tbl_rows[j], idx[:, c:c + LANE], axis=1,
                                          mode="promise_in_bounds")
                      for c in range(0, idx.shape[1], LANE)]
            per_head.append(chunks[0] if len(chunks) == 1 else jnp.concatenate(chunks, axis=1))
        return jnp.stack(per_head, axis=0)

    s = (_dot_nt(qs, kc_ref[...]) * scale).reshape(J, tq, LANE) + gathered_bias(pcmp_ref[...])
    n_cmp = (seq - CMP_LEN) // CMP_STRIDE + 1
    mask_c = ((lane * CMP_STRIDE + (CMP_LEN - 1) <= row_t) & (lane < n_cmp))[None]
    s = jnp.where(mask_c, s, NEG)
    e = jnp.where(mask_c, jnp.exp(s - jnp.max(s, axis=-1, keepdims=True)), 0.0)
    den = jnp.sum(e, axis=-1, keepdims=True)
    p_c = e / jnp.where(den > 0.0, den, 1.0)
    o_cmp = _dot(p_c.reshape(R, LANE).astype(BF16), vc_ref[...])

    blk_n = lax.broadcasted_iota(jnp.int32, (LANE, LANE), 0)
    blk_m = lax.broadcasted_iota(jnp.int32, (LANE, LANE), 1)
    per = SLC_LEN // CMP_STRIDE
    back = (CMP_LEN - 1) // CMP_STRIDE
    overlap = ((blk_n >= per * blk_m - back) & (blk_n <= per * blk_m + per - 1)
               & (blk_n < n_cmp) & (blk_m < n_slc)).astype(BF16)
    imp = sum(_dot(part, overlap) for part in _split3(jnp.sum(p_c, axis=0)))
    cur = row_t // SLC_LEN
    valid = lane <= cur
    forced = valid & ((lane == 0) | (lane >= cur - 1))
    score = jnp.where(forced, FORCED_SCORE, jnp.where(valid, imp, -1.0))
    rank = jnp.zeros((tq, LANE), jnp.int32)
    for mp in range(n_slc):
        other = score[:, mp:mp + 1]
        ahead = (other > score) | ((other == score) & (lane > mp))
        rank = rank + ahead.astype(jnp.int32)
    sel = (valid & (rank < min(SLC_TOPN, n_slc))).astype(BF16)

    def branch(k_ref, v_ref, first, last, mask_fn):
        m_ref[...] = jnp.full(m_ref.shape, NEG, F32)
        l_ref[...] = jnp.zeros(l_ref.shape, F32)
        acc_ref[...] = jnp.zeros(acc_ref.shape, F32)

        def body(kt, c):
            ks = pl.multiple_of(kt * tk, tk)
            kpos = prow_ref[:, pl.ds(ks, tk)]
            kidx = ks + lax.broadcasted_iota(jnp.int32, (1, tk), 1)
            far = far_ref[(b * nq + i) * nk + kt]
            bias = lax.cond(far == 1,
                            lambda: jnp.broadcast_to(far_bias, (J, tq, tk)),
                            lambda: gathered_bias(kpos))
            s = (_dot_nt(qs, k_ref[pl.ds(ks, tk), :]) * scale).reshape(J, tq, tk) + bias
            _softmax_tile(s.reshape(R, tk), mask_fn(kidx), v_ref[pl.ds(ks, tk), :],
                          m_ref, l_ref, acc_ref)
            return c

        lax.fori_loop(first, last, body, 0)
        return acc_ref[...] / l_ref[...]

    def slc_mask(kidx):
        expand = (lax.broadcasted_iota(jnp.int32, (LANE, tk), 0) == kidx // SLC_LEN).astype(BF16)
        return (_dot(sel, expand) > 0.5) & (kidx <= row_t)

    def win_mask(kidx):
        diff = row_t - kidx
        return (diff >= 0) & (diff < WINDOW)

    last = (t0 + tq + tk - 1) // tk
    o_slc = branch(ks_ref, vs_ref, 0, last, slc_mask)
    o_win = branch(kw_ref, vw_ref, jnp.maximum(t0 - (WINDOW - 1), 0) // tk, last, win_mask)

    gates = jax.nn.sigmoid(ng_ref[...].astype(F32))
    for j in range(J):
        rows = slice(j * tq, (j + 1) * tq)
        o = (gates[:, 3 * j:3 * j + 1] * o_cmp[rows] + gates[:, 3 * j + 1:3 * j + 2] * o_slc[rows]
             + gates[:, 3 * j + 2:3 * j + 3] * o_win[rows])
        o_ref[:, j * NSA_DV:(j + 1) * NSA_DV] = o.astype(o_ref.dtype)


def _nsa_attn(far, z, kc, vc, pos_col, pos_row, pos_cmp, tbl, batch, seq, zoff):
    tq, tk = NSA_TQ, NSA_TK
    nq = seq // tq
    G, J = NSA_KV_HEADS, NSA_GROUP
    R = J * tq
    kern = functools.partial(_nsa_attn_kernel, tq=tq, tk=tk, seq=seq, scale=NSA_DK ** -0.5)
    row_blk = lambda b, g, i, far: b * nq + i
    grid_spec = pltpu.PrefetchScalarGridSpec(
        num_scalar_prefetch=1,
        grid=(batch, G, nq),
        in_specs=[
            pl.BlockSpec((tq, J * HEAD_PAD), lambda b, g, i, far: (row_blk(b, g, i, far), g)),
            pl.BlockSpec((None, LANE, HEAD_PAD), lambda b, g, i, far: (b * G + g, 0, 0)),
            pl.BlockSpec((None, LANE, NSA_DV), lambda b, g, i, far: (b * G + g, 0, 0)),
            pl.BlockSpec((seq, HEAD_PAD), lambda b, g, i, far: (b, zoff["ks"] // HEAD_PAD + g)),
            pl.BlockSpec((seq, NSA_DV), lambda b, g, i, far: (b, zoff["vs"] // NSA_DV + g)),
            pl.BlockSpec((seq, HEAD_PAD), lambda b, g, i, far: (b, zoff["kw"] // HEAD_PAD + g)),
            pl.BlockSpec((seq, NSA_DV), lambda b, g, i, far: (b, zoff["vw"] // NSA_DV + g)),
            pl.BlockSpec((tq, LANE), lambda b, g, i, far: (row_blk(b, g, i, far), zoff["ng"] // LANE + g)),
            pl.BlockSpec((tq, 1), lambda b, g, i, far: (row_blk(b, g, i, far), 0)),
            pl.BlockSpec((None, 1, seq), lambda b, g, i, far: (b, 0, 0)),
            pl.BlockSpec((None, 1, LANE), lambda b, g, i, far: (b, 0, 0)),
            pl.BlockSpec((None, J, LANE), lambda b, g, i, far: (g, 0, 0)),
        ],
        out_specs=pl.BlockSpec((tq, J * NSA_DV), lambda b, g, i, far: (row_blk(b, g, i, far), g)),
        scratch_shapes=[pltpu.VMEM((R, 1), F32), pltpu.VMEM((R, 1), F32),
                        pltpu.VMEM((R, NSA_DV), F32)],
    )
    return pl.pallas_call(
        kern,
        grid_spec=grid_spec,
        out_shape=jax.ShapeDtypeStruct((batch * seq, NSA_HEADS * NSA_DV), BF16),
        compiler_params=_params("parallel", "parallel", "parallel"),
        name="nsa_attn",
    )(far, z, kc, vc, z, z, z, z, z, pos_col, pos_row, pos_cmp, tbl)


def _merge_kernel(a_ref, b_ref, ga_ref, gb_ref, h_ref, wa_ref, wb_ref, wo_ref, post_g_ref,
                  next_g_ref, o_ref, un_ref):
    ya = _dot(a_ref[...], wa_ref[...])
    yb = _dot(b_ref[...], wb_ref[...])
    m = (jax.nn.sigmoid(ga_ref[...].astype(F32)) * ya
         + jax.nn.sigmoid(gb_ref[...].astype(F32)) * yb).astype(BF16)
    y = _dot(m, wo_ref[...])
    out = h_ref[...] + _rms(y, post_g_ref[...])
    o_ref[...] = out
    un_ref[...] = _rms(out, next_g_ref[...]).astype(BF16)


def _merge(a, bb, z, h, wa, wb, wo, post_g, next_g, layer, zoff):
    m, d = h.shape
    tm = MERGE_TM
    da, db = a.shape[1], bb.shape[1]
    gblk = zoff["mg"] // d
    const = lambda i: (layer, 0, 0)
    return pl.pallas_call(
        _merge_kernel,
        grid=(m // tm,),
        in_specs=[pl.BlockSpec((tm, da), lambda i: (i, 0)),
                  pl.BlockSpec((tm, db), lambda i: (i, 0)),
                  pl.BlockSpec((tm, d), lambda i: (i, gblk)),
                  pl.BlockSpec((tm, d), lambda i: (i, gblk + 1)),
                  pl.BlockSpec((tm, d), lambda i: (i, 0)),
                  pl.BlockSpec((None, da, d), const),
                  pl.BlockSpec((None, db, d), const),
                  pl.BlockSpec((None, d, d), const),
                  pl.BlockSpec((None, 1, d), const),
                  pl.BlockSpec((None, 1, d), const)],
        out_specs=[pl.BlockSpec((tm, d), lambda i: (i, 0)),
                   pl.BlockSpec((tm, d), lambda i: (i, 0))],
        out_shape=[jax.ShapeDtypeStruct((m, d), F32), jax.ShapeDtypeStruct((m, d), BF16)],
        compiler_params=_params("parallel"),
        name="merge_out",
    )(a, bb, z, z, h, wa, wb, wo, post_g, next_g)


def _z_layout(d_model):
    G = NSA_KV_HEADS
    widths = [("q", NSA_HEADS * HEAD_PAD), ("mg", 2 * d_model), ("cq", MLA_Q_LORA),
              ("ckv", MLA_KV_LORA), ("kc", G * HEAD_PAD), ("ks", G * HEAD_PAD),
              ("kw", G * HEAD_PAD), ("vc", G * NSA_DV), ("vs", G * NSA_DV), ("vw", G * NSA_DV),
              ("kr", LANE), ("ng", G * LANE)]
    off, pos = {}, 0
    for name, w in widths:
        off[name] = pos
        pos += w
    off["total"] = pos
    assert off["q"] == 0 and off["mg"] % d_model == 0
    assert off["cq"] % MLA_Q_LORA == 0 and off["ckv"] % MLA_KV_LORA == 0
    assert all(off[k] % HEAD_PAD == 0 for k in ("kc", "ks", "kw"))
    return off


def _swap_halves(w):
    half = w.shape[-1] // 2
    return jnp.concatenate([-w[..., half:], w[..., :half]], axis=-1)


def _pad_last(w, width):
    return jnp.pad(w, [(0, 0)] * (w.ndim - 1) + [(0, width - w.shape[-1])])


def _layout_w_in(w_in):
    L, D, _ = w_in.shape
    G, J = NSA_KV_HEADS, NSA_GROUP
    splits = [MLA_Q_LORA, MLA_KV_LORA, MLA_ROPE, NSA_HEADS * NSA_DK,
              G * NSA_DK, G * NSA_DV, G * NSA_DK, G * NSA_DV, G * NSA_DK, G * NSA_DV,
              NSA_HEADS * 3, 2 * D]
    (c_q, c_kv, k_rope, nsa_q, k_c, v_c, k_s, v_s, k_w, v_w, nsa_g, merge_g) = jnp.split(
        w_in, [int(v) for v in np.cumsum(splits)[:-1]], axis=-1)

    def heads(w, n):
        return _pad_last(w.reshape(L, D, n, NSA_DK), HEAD_PAD).reshape(L, D, n * HEAD_PAD)

    gates = _pad_last(nsa_g.reshape(L, D, G, J * 3), LANE).reshape(L, D, G * LANE)
    return jnp.concatenate(
        [heads(nsa_q, NSA_HEADS), merge_g, c_q, c_kv, heads(k_c, G), heads(k_s, G), heads(k_w, G),
         v_c, v_s, v_w, k_rope, _swap_halves(k_rope), gates], axis=-1).astype(BF16)


def _layout_w_q_up(w):
    L, r, _ = w.shape
    w = w.reshape(L, r, MLA_HEADS, MLA_NOPE + MLA_ROPE)
    rope = w[..., MLA_NOPE:]
    return jnp.concatenate([w[..., :MLA_NOPE], rope, _swap_halves(rope)], axis=-1).reshape(
        L, r, MLA_HEADS * HEAD_PAD).astype(BF16)


def kernel(x, positions, rel_bias, ffn1_pre_g, ffn1_post_g, ffn1_w_gate, ffn1_w_up, ffn1_w_down, mix_pre_g, mix_post_g, w_in, mla_q_norm_g, mla_w_q_up, mla_kv_norm_g, mla_w_uk, mla_w_uv, cmp_pe_k, cmp_w1_k, cmp_w2_k, cmp_pe_v, cmp_w1_v, cmp_w2_v, w_branch_mla, w_branch_nsa, w_out, ffn2_pre_g, ffn2_post_g, ffn2_w_gate, ffn2_w_up, ffn2_w_down):
    B, S, D = x.shape
    L = w_in.shape[0]
    M = B * S
    G, J = NSA_KV_HEADS, NSA_GROUP
    zoff = _z_layout(D)
    n_half = S // CMP_STRIDE
    assert n_half == LANE and (S - CMP_LEN) // CMP_STRIDE + 1 <= LANE and S // SLC_LEN <= LANE

    gain = lambda g: g.reshape(L, 1, -1)
    bf = lambda w: w.astype(BF16)
    w_z = _layout_w_in(w_in)
    w_q = _layout_w_q_up(mla_w_q_up)
    w_kv = bf(jnp.concatenate([mla_w_uk, mla_w_uv], axis=-1))
    pe_k = cmp_pe_k.reshape(L, 2, CMP_STRIDE * NSA_DK)
    pe_v = cmp_pe_v.reshape(L, 2, CMP_STRIDE * NSA_DV)
    w2_k = bf(_pad_last(cmp_w2_k, HEAD_PAD))
    f1 = (bf(ffn1_w_gate), bf(ffn1_w_up), bf(ffn1_w_down))
    f2 = (bf(ffn2_w_gate), bf(ffn2_w_up), bf(ffn2_w_down))
    w1_k, w1_v, w2_v = bf(cmp_w1_k), bf(cmp_w1_v), bf(cmp_w2_v)
    w_a, w_b, w_o = bf(w_branch_mla), bf(w_branch_nsa), bf(w_out)
    g_f1pre, g_f1post, g_mpre, g_mpost = gain(ffn1_pre_g), gain(ffn1_post_g), gain(mix_pre_g), gain(mix_post_g)
    g_f2pre, g_f2post, g_q, g_kv = gain(ffn2_pre_g), gain(ffn2_post_g), gain(mla_q_norm_g), gain(mla_kv_norm_g)

    pos_col = positions.reshape(M, 1)
    pos_row = positions.reshape(B, 1, S)
    n_cmp = (S - CMP_LEN) // CMP_STRIDE + 1
    pos_cmp = _pad_last(positions[:, CMP_LEN - 1::CMP_STRIDE][:, :n_cmp], LANE).reshape(B, 1, LANE)
    q_min = positions.reshape(B, S // NSA_TQ, NSA_TQ).min(axis=-1)
    k_max = positions.reshape(B, S // NSA_TK, NSA_TK).max(axis=-1)
    far = (q_min[:, :, None] - k_max[:, None, :] >= FAR_DIST).astype(jnp.int32).reshape(-1)
    tbl = jnp.take(rel_bias, jnp.asarray(BUCKET_OF_DIST), axis=0).T.reshape(G, J, LANE)
    half = MLA_ROPE // 2
    inv = ROPE_BASE ** (-jnp.arange(half, dtype=F32) * 2.0 / MLA_ROPE)
    cs = _rope_table(pos_col, jnp.concatenate([inv, inv]).reshape(1, MLA_ROPE))

    h = x.reshape(M, D)
    u = _rmsnorm(h, g_f1pre, 0)
    for l in range(L):
        h, u = _ffn(h, u, g_f1post, g_mpre, l, *f1, l)
        z = _in_proj(u, w_z, l)
        qp, kp, v = _mla_prep(z, cs, g_q, g_kv, w_q, w_kv, l, zoff)
        a = _mla_attn(qp, kp, v, B, S)

        def half_blocks(off, d):
            t = z[:, off:off + G * (HEAD_PAD if d == NSA_DK else d)]
            t = t.reshape(B, S, G, -1)[..., :d]
            return t.transpose(0, 2, 1, 3).reshape(B * G, n_half, CMP_STRIDE * d)

        kc, vc = _nsa_cmp(half_blocks(zoff["kc"], NSA_DK), half_blocks(zoff["vc"], NSA_DV),
                          pe_k, pe_v, w1_k, w2_k, w1_v, w2_v, l)
        nsa = _nsa_attn(far, z, kc, vc, pos_col, pos_row, pos_cmp, tbl, B, S, zoff)
        h, u = _merge(a, nsa, z, h, w_a, w_b, w_o, g_mpost, g_f2pre, l, zoff)
        h, u = _ffn(h, u, g_f2post, g_f1pre, min(l + 1, L - 1), *f2, l)
    return h.reshape(B, S, D)
```

```python
import functools
import math

import numpy as np
import jax
import jax.numpy as jnp
from jax import lax
from jax.experimental import pallas as pl
from jax.experimental.pallas import tpu as pltpu

EPS = 1e-6
MLA_HEADS = 8
MLA_Q_LORA = 512
MLA_KV_LORA = 512
MLA_NOPE = 128
MLA_ROPE = 64
MLA_V = 128
ROPE_BASE = 10000.0
NSA_HEADS = 8
NSA_KV_HEADS = 2
NSA_GROUP = NSA_HEADS // NSA_KV_HEADS
NSA_DK = 192
NSA_DV = 128
CMP_LEN = 32
CMP_STRIDE = 16
CMP_HIDDEN = 256
SLC_LEN = 64
SLC_TOPN = 16
WINDOW = 512
FORCED_SCORE = 1e6
REL_BUCKETS = 32
REL_MAX_DIST = 128
NEG = -1e30
MASKED = 2 * NEG

LANE = 128
HEAD_PAD = 256
VMEM_LIMIT = 56 * 1024 * 1024
BF16 = jnp.bfloat16
F32 = jnp.float32

FFN_TM, FFN_TF = 512, 512
PROJ_TM = 1024
PROJ_TN_CAP = 2560
Z_PAD = 5 * LANE
PREP_TM = 512
MLA_TQ, MLA_TK = 512, 256
NSA_TQ, NSA_TK = LANE, 256
MERGE_TM = 256


def _bucket_of_distance():
    n = np.arange(LANE)
    max_exact = REL_BUCKETS // 2
    large = max_exact + (np.log(np.maximum(n, 1) / max_exact) / math.log(REL_MAX_DIST / max_exact)
                         * (REL_BUCKETS - max_exact)).astype(np.int32)
    bucket = np.where(n < max_exact, n, np.minimum(large, REL_BUCKETS - 1)).astype(np.int32)
    assert bucket[-1] == REL_BUCKETS - 1
    return bucket


BUCKET_OF_DIST = _bucket_of_distance()
FAR_DIST = int(np.max(np.nonzero(BUCKET_OF_DIST != REL_BUCKETS - 1)[0])) + 1


def _params(*sem):
    return pltpu.CompilerParams(dimension_semantics=sem, vmem_limit_bytes=VMEM_LIMIT)


def _rms(x, g):
    return x * lax.rsqrt(jnp.mean(x * x, axis=-1, keepdims=True) + EPS) * g


def _dot(a, b):
    return jnp.dot(a, b, preferred_element_type=F32)


def _dot_nt(a, b):
    return lax.dot_general(a, b, (((1,), (1,)), ((), ())), preferred_element_type=F32)


def _tile_n(n, cap):
    best = LANE
    for t in range(LANE, cap + 1, LANE):
        if n % t == 0:
            best = t
    return best


def _rmsnorm_kernel(x_ref, g_ref, o_ref):
    o_ref[...] = _rms(x_ref[...], g_ref[...]).astype(o_ref.dtype)


def _rmsnorm(x, g, layer):
    m, d = x.shape
    tm = 512
    return pl.pallas_call(
        _rmsnorm_kernel,
        grid=(m // tm,),
        in_specs=[pl.BlockSpec((tm, d), lambda i: (i, 0)),
                  pl.BlockSpec((None, 1, d), lambda i: (layer, 0, 0))],
        out_specs=pl.BlockSpec((tm, d), lambda i: (i, 0)),
        out_shape=jax.ShapeDtypeStruct((m, d), BF16),
        compiler_params=_params("parallel"),
        name="rmsnorm",
    )(x, g)


def _rope_table_kernel(pos_ref, inv_ref, o_ref):
    ang = pos_ref[...].astype(F32) * inv_ref[...]
    o_ref[...] = jnp.concatenate([jnp.cos(ang), jnp.sin(ang)], axis=1)


def _rope_table(pos_col, inv):
    m = pos_col.shape[0]
    tm = 512
    return pl.pallas_call(
        _rope_table_kernel,
        grid=(m // tm,),
        in_specs=[pl.BlockSpec((tm, 1), lambda i: (i, 0)),
                  pl.BlockSpec((1, MLA_ROPE), lambda i: (0, 0))],
        out_specs=pl.BlockSpec((tm, 2 * MLA_ROPE), lambda i: (i, 0)),
        out_shape=jax.ShapeDtypeStruct((m, 2 * MLA_ROPE), F32),
        compiler_params=_params("parallel"),
        name="rope_table",
    )(pos_col, inv)


def _ffn_kernel(x_ref, u_ref, post_g_ref, next_g_ref, wg_ref, wu_ref, wd_ref,
                o_ref, un_ref, acc_ref):
    j = pl.program_id(1)
    u = u_ref[...]
    gate = _dot(u, wg_ref[...])
    up = _dot(u, wu_ref[...])
    hidden = (gate * jax.nn.sigmoid(gate) * up).astype(BF16)
    part = _dot(hidden, wd_ref[...])

    @pl.when(j == 0)
    def _():
        acc_ref[...] = part

    @pl.when(j > 0)
    def _():
        acc_ref[...] += part

    @pl.when(j == pl.num_programs(1) - 1)
    def _():
        out = x_ref[...] + 0.5 * _rms(acc_ref[...], post_g_ref[...])
        o_ref[...] = out
        un_ref[...] = _rms(out, next_g_ref[...]).astype(BF16)


def _ffn(x, u, post_g, next_g, next_layer, wg, wu, wd, layer):
    m, d = x.shape
    f = wg.shape[-1]
    tm, tf = FFN_TM, _tile_n(f, FFN_TF)
    return pl.pallas_call(
        _ffn_kernel,
        grid=(m // tm, f // tf),
        in_specs=[pl.BlockSpec((tm, d), lambda i, j: (i, 0)),
                  pl.BlockSpec((tm, d), lambda i, j: (i, 0)),
                  pl.BlockSpec((None, 1, d), lambda i, j: (layer, 0, 0)),
                  pl.BlockSpec((None, 1, d), lambda i, j: (next_layer, 0, 0)),
                  pl.BlockSpec((None, d, tf), lambda i, j: (layer, 0, j)),
                  pl.BlockSpec((None, d, tf), lambda i, j: (layer, 0, j)),
                  pl.BlockSpec((None, tf, d), lambda i, j: (layer, j, 0))],
        out_specs=[pl.BlockSpec((tm, d), lambda i, j: (i, 0)),
                   pl.BlockSpec((tm, d), lambda i, j: (i, 0))],
        out_shape=[jax.ShapeDtypeStruct((m, d), F32), jax.ShapeDtypeStruct((m, d), BF16)],
        scratch_shapes=[pltpu.VMEM((tm, d), F32)],
        compiler_params=_params("parallel", "arbitrary"),
        name="ffn",
    )(x, u, post_g, next_g, wg, wu, wd)


def _matmul_kernel(x_ref, w_ref, o_ref):
    o_ref[...] = _dot(x_ref[...], w_ref[...]).astype(o_ref.dtype)


def _in_proj(u, w, layer):
    m, d = u.shape
    n = w.shape[-1]
    tm, tn = PROJ_TM, _tile_n(n, PROJ_TN_CAP)
    return pl.pallas_call(
        _matmul_kernel,
        grid=(n // tn, m // tm),
        in_specs=[pl.BlockSpec((tm, d), lambda j, i: (i, 0)),
                  pl.BlockSpec((None, d, tn), lambda j, i: (layer, 0, j))],
        out_specs=pl.BlockSpec((tm, tn), lambda j, i: (i, j)),
        out_shape=jax.ShapeDtypeStruct((m, n), BF16),
        compiler_params=_params("parallel", "parallel"),
        name="in_proj",
    )(u, w)


def _matmul_t_kernel(w_ref, x_ref, o_ref):
    o_ref[...] = _dot_nt(w_ref[...], x_ref[...]).astype(o_ref.dtype)


def _in_proj_t(u, w_t, layer):
    m, d = u.shape
    n = w_t.shape[1]
    tm = PROJ_TM
    return pl.pallas_call(
        _matmul_t_kernel,
        grid=(m // tm,),
        in_specs=[pl.BlockSpec((None, n, d), lambda i: (layer, 0, 0)),
                  pl.BlockSpec((tm, d), lambda i: (i, 0))],
        out_specs=pl.BlockSpec((n, tm), lambda i: (0, i)),
        out_shape=jax.ShapeDtypeStruct((n, m), BF16),
        compiler_params=_params("parallel"),
        name="in_proj_t",
    )(w_t, u)


def _softmax_tile_t(s_ref, adds, masks, v_t, m_ref, l_ref, acc_ref):
    probs, alphas = [], []
    for c in range(s_ref.shape[1] // LANE):
        cols = slice(c * LANE, (c + 1) * LANE)
        s = s_ref[:, cols]
        if adds[c] is not None:
            s = s + adds[c]
        if masks[c] is not None:
            s = jnp.where(masks[c], s, MASKED)
        m_old = m_ref[:, cols]
        m_new = jnp.maximum(m_old, jnp.max(s, axis=0, keepdims=True))
        alpha = jnp.exp(m_old - m_new)
        p = jnp.exp(s - m_new)
        l_ref[:, cols] = alpha * l_ref[:, cols] + jnp.sum(p, axis=0, keepdims=True)
        m_ref[:, cols] = m_new
        probs.append(p.astype(BF16))
        alphas.append(alpha)
    p_t = probs[0] if len(probs) == 1 else jnp.concatenate(probs, axis=1)
    alpha = alphas[0] if len(alphas) == 1 else jnp.concatenate(alphas, axis=1)
    acc_ref[...] = alpha * acc_ref[...] + _dot(v_t, p_t)


def _softmax_init(m_ref, l_ref, acc_ref):
    m_ref[...] = jnp.full(m_ref.shape, NEG, F32)
    l_ref[...] = jnp.zeros(l_ref.shape, F32)
    acc_ref[...] = jnp.zeros(acc_ref.shape, F32)


def _mla_prep_kernel(cq_ref, ckv_ref, kr_ref, cs_ref, gq_ref, gkv_ref, wq_ref, wuk_ref, wuvt_ref,
                     q_ref, k_ref, vt_ref):
    cs = cs_ref[...]
    qn = _rms(cq_ref[...].astype(F32), gq_ref[...]).astype(BF16)
    q = _dot(qn, wq_ref[...])
    kvn = _rms(ckv_ref[...].astype(F32), gkv_ref[...]).astype(BF16)
    k_nope = _dot(kvn, wuk_ref[...])
    t = kr_ref[...].astype(F32) * cs
    k_pe = (t + pltpu.roll(t, MLA_ROPE, 1)).astype(BF16)
    for h in range(MLA_HEADS):
        lo = h * HEAD_PAD
        q_ref[:, lo:lo + MLA_NOPE] = q[:, lo:lo + MLA_NOPE].astype(BF16)
        q_ref[:, lo + MLA_NOPE:lo + HEAD_PAD] = (q[:, lo + MLA_NOPE:lo + HEAD_PAD] * cs).astype(BF16)
        k_ref[:, lo:lo + MLA_NOPE] = k_nope[:, h * MLA_NOPE:(h + 1) * MLA_NOPE].astype(BF16)
        k_ref[:, lo + MLA_NOPE:lo + HEAD_PAD] = k_pe
    vt_ref[...] = _dot_nt(wuvt_ref[...], kvn).astype(BF16)


def _mla_prep(z, cs, gq, gkv, wq, wuk, wuv_t, layer, zoff):
    m = z.shape[0]
    tm = PREP_TM
    hq = MLA_HEADS * HEAD_PAD
    hv = MLA_HEADS * MLA_V
    const = lambda i: (layer, 0, 0)
    return pl.pallas_call(
        _mla_prep_kernel,
        grid=(m // tm,),
        in_specs=[pl.BlockSpec((tm, MLA_Q_LORA), lambda i: (i, zoff["cq"] // MLA_Q_LORA)),
                  pl.BlockSpec((tm, MLA_KV_LORA), lambda i: (i, zoff["ckv"] // MLA_KV_LORA)),
                  pl.BlockSpec((tm, LANE), lambda i: (i, zoff["kr"] // LANE)),
                  pl.BlockSpec((tm, LANE), lambda i: (i, 0)),
                  pl.BlockSpec((None, 1, MLA_Q_LORA), const),
                  pl.BlockSpec((None, 1, MLA_KV_LORA), const),
                  pl.BlockSpec((None, MLA_Q_LORA, hq), const),
                  pl.BlockSpec((None, MLA_KV_LORA, MLA_HEADS * MLA_NOPE), const),
                  pl.BlockSpec((None, hv, MLA_KV_LORA), const)],
        out_specs=[pl.BlockSpec((tm, hq), lambda i: (i, 0)),
                   pl.BlockSpec((tm, hq), lambda i: (i, 0)),
                   pl.BlockSpec((hv, tm), lambda i: (0, i))],
        out_shape=[jax.ShapeDtypeStruct((m, hq), BF16),
                   jax.ShapeDtypeStruct((m, hq), BF16),
                   jax.ShapeDtypeStruct((hv, m), BF16)],
        compiler_params=_params("parallel"),
        name="mla_prep",
    )(z, z, z, cs, gq, gkv, wq, wuk, wuv_t)


def _mla_attn_kernel(q_ref, k_ref, vt_ref, o_ref, s_ref, m_ref, l_ref, acc_ref, *, tq, tk):
    i = pl.program_id(2)
    q = q_ref[...]
    nc = tq // LANE
    _softmax_init(m_ref, l_ref, acc_ref)
    qt = i * tq + lax.broadcasted_iota(jnp.int32, (1, tq), 1)
    kcol = lax.broadcasted_iota(jnp.int32, (tk, 1), 0)

    def tile(kt, masked):
        ks = pl.multiple_of(kt * tk, tk)
        s_ref[...] = _dot_nt(k_ref[pl.ds(ks, tk), :], q)
        masks = [None] * nc
        if masked:
            masks = [ks + kcol <= qt[:, c * LANE:(c + 1) * LANE] for c in range(nc)]
        _softmax_tile_t(s_ref, [None] * nc, masks, vt_ref[:, pl.ds(ks, tk)], m_ref, l_ref, acc_ref)

    n_full = (i * tq) // tk
    n_all = (i * tq + tq + tk - 1) // tk

    def full_body(kt, c):
        tile(kt, False)
        return c

    def diag_body(kt, c):
        tile(kt, True)
        return c

    lax.fori_loop(0, n_full, full_body, 0)
    lax.fori_loop(n_full, n_all, diag_body, 0)
    o_ref[...] = (acc_ref[...] / l_ref[...]).T.astype(o_ref.dtype)


def _mla_attn(qp, kp, vt, batch, seq):
    tq, tk = MLA_TQ, MLA_TK
    nq = seq // tq
    kern = functools.partial(_mla_attn_kernel, tq=tq, tk=tk)
    return pl.pallas_call(
        kern,
        grid=(batch, MLA_HEADS, nq),
        in_specs=[pl.BlockSpec((tq, HEAD_PAD), lambda b, h, i: (b * nq + i, h)),
                  pl.BlockSpec((seq, HEAD_PAD), lambda b, h, i: (b, h)),
                  pl.BlockSpec((MLA_V, seq), lambda b, h, i: (h, b))],
        out_specs=pl.BlockSpec((tq, MLA_V), lambda b, h, i: (b * nq + i, h)),
        out_shape=jax.ShapeDtypeStruct((batch * seq, MLA_HEADS * MLA_V), BF16),
        scratch_shapes=[pltpu.VMEM((tk, tq), F32), pltpu.VMEM((1, tq), F32),
                        pltpu.VMEM((1, tq), F32), pltpu.VMEM((MLA_V, tq), F32)],
        compiler_params=_params("parallel", "parallel", "parallel"),
        name="mla_attn",
    )(qp, kp, vt)


def _compress(a_ref, pe_ref, w1_ref, w2_ref):
    a = a_ref[...].astype(F32)
    half = a.shape[1]
    first = _dot((a + pe_ref[0:1, :]).astype(BF16), w1_ref[0:half, :])
    second = _dot((a + pe_ref[1:2, :]).astype(BF16), w1_ref[half:2 * half, :])
    n = a.shape[0]
    hidden = first + pltpu.roll(second, n - 1, 0)
    return _dot((hidden * jax.nn.sigmoid(hidden)).astype(BF16), w2_ref[...])


def _nsa_cmp_kernel(ak_ref, av_ref, pek_ref, pev_ref, w1k_ref, w2k_ref, w1v_ref, w2v_ref,
                    kc_ref, vct_ref):
    kc_ref[...] = _compress(ak_ref, pek_ref, w1k_ref, w2k_ref).astype(BF16)
    vct_ref[...] = _compress(av_ref, pev_ref, w1v_ref, w2v_ref).T.astype(BF16)


def _nsa_cmp(ak, av, pek, pev, w1k, w2k, w1v, w2v, layer):
    bg, nh, dk16 = ak.shape
    dv16 = av.shape[-1]
    const = lambda i: (layer, 0, 0)
    return pl.pallas_call(
        _nsa_cmp_kernel,
        grid=(bg,),
        in_specs=[pl.BlockSpec((None, nh, dk16), lambda i: (i, 0, 0)),
                  pl.BlockSpec((None, nh, dv16), lambda i: (i, 0, 0)),
                  pl.BlockSpec((None, 2, dk16), const),
                  pl.BlockSpec((None, 2, dv16), const),
                  pl.BlockSpec((None, 2 * dk16, CMP_HIDDEN), const),
                  pl.BlockSpec((None, CMP_HIDDEN, HEAD_PAD), const),
                  pl.BlockSpec((None, 2 * dv16, CMP_HIDDEN), const),
                  pl.BlockSpec((None, CMP_HIDDEN, NSA_DV), const)],
        out_specs=[pl.BlockSpec((None, nh, HEAD_PAD), lambda i: (i, 0, 0)),
                   pl.BlockSpec((None, NSA_DV, nh), lambda i: (i, 0, 0))],
        out_shape=[jax.ShapeDtypeStruct((bg, nh, HEAD_PAD), BF16),
                   jax.ShapeDtypeStruct((bg, NSA_DV, nh), BF16)],
        compiler_params=_params("parallel"),
        name="nsa_cmp",
    )(ak, av, pek, pev, w1k, w2k, w1v, w2v)


def _split3(x):
    a = x.astype(BF16)
    r = x - a.astype(F32)
    b = r.astype(BF16)
    c = (r - b.astype(F32)).astype(BF16)
    return a, b, c


def _lookup(table_row, idx):
    return jnp.take_along_axis(jnp.broadcast_to(table_row, idx.shape), idx, axis=1,
                               mode="promise_in_bounds")


def _nsa_attn_kernel(far_ref, q_ref, kc_ref, vct_ref, ks_ref, vst_ref, kw_ref, vwt_ref, ng_ref,
                     prow_ref, pcol_ref, pcmp_ref, tbl_ref, o_ref,
                     s_ref, m_ref, l_ref, acc_ref, *, tq, tk, seq):
    b, i = pl.program_id(0), pl.program_id(2)
    nq, nk = seq // tq, seq // tk
    J = NSA_GROUP
    n_slc = seq // SLC_LEN
    n_cmp = (seq - CMP_LEN) // CMP_STRIDE + 1
    t0 = pl.multiple_of(i * tq, tq)

    qs = jnp.concatenate([q_ref[:, j * HEAD_PAD:(j + 1) * HEAD_PAD] for j in range(J)], axis=0)
    qpos = prow_ref[:, pl.ds(t0, tq)]
    qt = t0 + lax.broadcasted_iota(jnp.int32, (1, tq), 1)
    tbl = tbl_ref[...]
    far_bias = [tbl[j:j + 1, LANE - 1:LANE] for j in range(J)]
    head = [slice(j * tq, (j + 1) * tq) for j in range(J)]

    blk = lax.broadcasted_iota(jnp.int32, (LANE, 1), 0)
    keep_c = (blk * CMP_STRIDE + (CMP_LEN - 1) <= qt) & (blk < n_cmp)
    idx_c = jnp.clip(qpos - pcmp_ref[...], 0, LANE - 1)
    s_c = _dot_nt(kc_ref[...], qs)
    p_heads = []
    for j in range(J):
        s = jnp.where(keep_c, s_c[:, head[j]] + _lookup(tbl[j:j + 1, :], idx_c), NEG)
        e = jnp.where(keep_c, jnp.exp(s - jnp.max(s, axis=0, keepdims=True)), 0.0)
        den = jnp.sum(e, axis=0, keepdims=True)
        p_heads.append(e / jnp.where(den > 0.0, den, 1.0))
    o_cmp = _dot(vct_ref[...], jnp.concatenate([p.astype(BF16) for p in p_heads], axis=1))

    rows = 32
    assert n_slc <= rows
    m_row = lax.broadcasted_iota(jnp.int32, (rows, LANE), 0)
    n_col = lax.broadcasted_iota(jnp.int32, (rows, LANE), 1)
    per = SLC_LEN // CMP_STRIDE
    back = (CMP_LEN - 1) // CMP_STRIDE
    overlap = ((n_col >= per * m_row - back) & (n_col <= per * m_row + per - 1)
               & (n_col < n_cmp) & (m_row < n_slc)).astype(BF16)
    imp = sum(_dot(overlap, part) for part in _split3(sum(p_heads)))
    m_blk = lax.broadcasted_iota(jnp.int32, (rows, 1), 0)
    cur = qt // SLC_LEN
    valid = m_blk <= cur
    forced = valid & ((m_blk == 0) | (m_blk >= cur - 1))
    score = jnp.where(forced, FORCED_SCORE, jnp.where(valid, imp, -1.0))
    rank = jnp.zeros((rows, tq), jnp.int32)
    for mp in range(n_slc):
        other = score[mp:mp + 1, :]
        ahead = (other > score) | ((other == score) & (m_blk > mp))
        rank = rank + ahead.astype(jnp.int32)
    sel = (valid & (rank < min(SLC_TOPN, n_slc))).astype(BF16)

    def branch(k_ref, vt_ref, first, last, keep_fn):
        _softmax_init(m_ref, l_ref, acc_ref)

        def body(kt, c):
            ks = pl.multiple_of(kt * tk, tk)
            s_ref[...] = _dot_nt(k_ref[pl.ds(ks, tk), :], qs)

            @pl.when(far_ref[(b * nq + i) * nk + kt] == 0)
            def _():
                idx = jnp.clip(qpos - pcol_ref[pl.ds(ks, tk), :], 0, LANE - 1)
                for j in range(J):
                    s_ref[:, head[j]] += _lookup(tbl[j:j + 1, :] - far_bias[j], idx)

            keep = keep_fn(ks + lax.broadcasted_iota(jnp.int32, (tk, 1), 0))
            _softmax_tile_t(s_ref, far_bias, [keep] * J, vt_ref[:, pl.ds(ks, tk)],
                            m_ref, l_ref, acc_ref)
            return c

        lax.fori_loop(first, last, body, 0)
        return acc_ref[...] / l_ref[...]

    def slc_keep(kidx):
        expand = (kidx // SLC_LEN == lax.broadcasted_iota(jnp.int32, (1, rows), 1)).astype(BF16)
        return (_dot(expand, sel) > 0.5) & (kidx <= qt)

    def win_keep(kidx):
        diff = qt - kidx
        return (diff >= 0) & (diff < WINDOW)

    last = (t0 + tq + tk - 1) // tk
    o_slc = branch(ks_ref, vst_ref, 0, last, slc_keep)
    o_win = branch(kw_ref, vwt_ref, jnp.maximum(t0 - (WINDOW - 1), 0) // tk, last, win_keep)

    gates = jax.nn.sigmoid(ng_ref[...].astype(F32)).T
    for j in range(J):
        o = (gates[3 * j:3 * j + 1, :] * o_cmp[:, head[j]]
             + gates[3 * j + 1:3 * j + 2, :] * o_slc[:, head[j]]
             + gates[3 * j + 2:3 * j + 3, :] * o_win[:, head[j]])
        o_ref[:, j * NSA_DV:(j + 1) * NSA_DV] = o.T.astype(o_ref.dtype)


def _nsa_attn(far, z, kc, vct, v_t, pos_row, pos_col, pos_cmp, tbl, batch, seq, zoff):
    tq, tk = NSA_TQ, NSA_TK
    nq = seq // tq
    G, J = NSA_KV_HEADS, NSA_GROUP
    R = J * tq
    kern = functools.partial(_nsa_attn_kernel, tq=tq, tk=tk, seq=seq)
    grid_spec = pltpu.PrefetchScalarGridSpec(
        num_scalar_prefetch=1,
        grid=(batch, G, nq),
        in_specs=[
            pl.BlockSpec((tq, J * HEAD_PAD), lambda b, g, i, far: (b * nq + i, g)),
            pl.BlockSpec((None, LANE, HEAD_PAD), lambda b, g, i, far: (b * G + g, 0, 0)),
            pl.BlockSpec((None, NSA_DV, LANE), lambda b, g, i, far: (b * G + g, 0, 0)),
            pl.BlockSpec((seq, HEAD_PAD), lambda b, g, i, far: (b, zoff["ks"] // HEAD_PAD + g)),
            pl.BlockSpec((NSA_DV, seq), lambda b, g, i, far: (g, b)),
            pl.BlockSpec((seq, HEAD_PAD), lambda b, g, i, far: (b, zoff["kw"] // HEAD_PAD + g)),
            pl.BlockSpec((NSA_DV, seq), lambda b, g, i, far: (G + g, b)),
            pl.BlockSpec((tq, LANE), lambda b, g, i, far: (b * nq + i, zoff["ng"] // LANE + g)),
            pl.BlockSpec((None, 1, seq), lambda b, g, i, far: (b, 0, 0)),
            pl.BlockSpec((seq, 1), lambda b, g, i, far: (b, 0)),
            pl.BlockSpec((LANE, 1), lambda b, g, i, far: (b, 0)),
            pl.BlockSpec((None, J, LANE), lambda b, g, i, far: (g, 0, 0)),
        ],
        out_specs=pl.BlockSpec((tq, J * NSA_DV), lambda b, g, i, far: (b * nq + i, g)),
        scratch_shapes=[pltpu.VMEM((tk, R), F32), pltpu.VMEM((1, R), F32),
                        pltpu.VMEM((1, R), F32), pltpu.VMEM((NSA_DV, R), F32)],
    )
    return pl.pallas_call(
        kern,
        grid_spec=grid_spec,
        out_shape=jax.ShapeDtypeStruct((batch * seq, NSA_HEADS * NSA_DV), BF16),
        compiler_params=_params("parallel", "parallel", "parallel"),
        name="nsa_attn",
    )(far, z, kc, vct, z, v_t, z, v_t, z, pos_row, pos_col, pos_cmp, tbl)


def _merge_kernel(a_ref, b_ref, ga_ref, gb_ref, h_ref, wa_ref, wb_ref, wo_ref, post_g_ref,
                  next_g_ref, o_ref, un_ref):
    ya = _dot(a_ref[...], wa_ref[...])
    yb = _dot(b_ref[...], wb_ref[...])
    m = (jax.nn.sigmoid(ga_ref[...].astype(F32)) * ya
         + jax.nn.sigmoid(gb_ref[...].astype(F32)) * yb).astype(BF16)
    y = _dot(m, wo_ref[...])
    out = h_ref[...] + _rms(y, post_g_ref[...])
    o_ref[...] = out
    un_ref[...] = _rms(out, next_g_ref[...]).astype(BF16)


def _merge(a, bb, z, h, wa, wb, wo, post_g, next_g, layer, zoff):
    m, d = h.shape
    tm = MERGE_TM
    da, db = a.shape[1], bb.shape[1]
    gblk = zoff["mg"] // d
    const = lambda i: (layer, 0, 0)
    return pl.pallas_call(
        _merge_kernel,
        grid=(m // tm,),
        in_specs=[pl.BlockSpec((tm, da), lambda i: (i, 0)),
                  pl.BlockSpec((tm, db), lambda i: (i, 0)),
                  pl.BlockSpec((tm, d), lambda i: (i, gblk)),
                  pl.BlockSpec((tm, d), lambda i: (i, gblk + 1)),
                  pl.BlockSpec((tm, d), lambda i: (i, 0)),
                  pl.BlockSpec((None, da, d), const),
                  pl.BlockSpec((None, db, d), const),
                  pl.BlockSpec((None, d, d), const),
                  pl.BlockSpec((None, 1, d), const),
                  pl.BlockSpec((None, 1, d), const)],
        out_specs=[pl.BlockSpec((tm, d), lambda i: (i, 0)),
                   pl.BlockSpec((tm, d), lambda i: (i, 0))],
        out_shape=[jax.ShapeDtypeStruct((m, d), F32), jax.ShapeDtypeStruct((m, d), BF16)],
        compiler_params=_params("parallel"),
        name="merge_out",
    )(a, bb, z, z, h, wa, wb, wo, post_g, next_g)


def _z_layout(d_model):
    G = NSA_KV_HEADS
    widths = [("q", NSA_HEADS * HEAD_PAD), ("mg", 2 * d_model), ("cq", MLA_Q_LORA),
              ("ckv", MLA_KV_LORA), ("kc", G * HEAD_PAD), ("ks", G * HEAD_PAD),
              ("kw", G * HEAD_PAD), ("vc", G * NSA_DV), ("kr", LANE), ("ng", G * LANE)]
    off, pos = {}, 0
    for name, w in widths:
        off[name] = pos
        pos += w
    off["used"] = pos
    off["total"] = -(-pos // Z_PAD) * Z_PAD
    assert off["q"] == 0 and off["mg"] % d_model == 0
    assert off["cq"] % MLA_Q_LORA == 0 and off["ckv"] % MLA_KV_LORA == 0
    assert all(off[k] % HEAD_PAD == 0 for k in ("kc", "ks", "kw"))
    return off


def _swap_halves(w):
    half = w.shape[-1] // 2
    return jnp.concatenate([-w[..., half:], w[..., :half]], axis=-1)


def _pad_last(w, width):
    return jnp.pad(w, [(0, 0)] * (w.ndim - 1) + [(0, width - w.shape[-1])])


def _layout_w_in(w_in, zoff):
    L, D, _ = w_in.shape
    G, J = NSA_KV_HEADS, NSA_GROUP
    splits = [MLA_Q_LORA, MLA_KV_LORA, MLA_ROPE, NSA_HEADS * NSA_DK,
              G * NSA_DK, G * NSA_DV, G * NSA_DK, G * NSA_DV, G * NSA_DK, G * NSA_DV,
              NSA_HEADS * 3, 2 * D]
    (c_q, c_kv, k_rope, nsa_q, k_c, v_c, k_s, v_s, k_w, v_w, nsa_g, merge_g) = jnp.split(
        w_in, [int(v) for v in np.cumsum(splits)[:-1]], axis=-1)

    def heads(w, n):
        return _pad_last(w.reshape(L, D, n, NSA_DK), HEAD_PAD).reshape(L, D, n * HEAD_PAD).astype(BF16)

    gates = _pad_last(nsa_g.reshape(L, D, G, J * 3), LANE).reshape(L, D, G * LANE)
    cols = [heads(nsa_q * (NSA_DK ** -0.5), NSA_HEADS), merge_g, c_q, c_kv, heads(k_c, G),
            heads(k_s, G), heads(k_w, G), v_c, k_rope, _swap_halves(k_rope), gates,
            jnp.zeros((L, D, zoff["total"] - zoff["used"]), BF16)]
    w_z = jnp.concatenate([c.astype(BF16) for c in cols], axis=-1)
    w_vt = jnp.concatenate([v_s, v_w], axis=-1).astype(BF16).transpose(0, 2, 1)
    return w_z, w_vt


def _layout_w_q_up(w):
    L, r, _ = w.shape
    w = w.reshape(L, r, MLA_HEADS, MLA_NOPE + MLA_ROPE) * ((MLA_NOPE + MLA_ROPE) ** -0.5)
    rope = w[..., MLA_NOPE:]
    return jnp.concatenate([w[..., :MLA_NOPE], rope, _swap_halves(rope)], axis=-1).reshape(
        L, r, MLA_HEADS * HEAD_PAD).astype(BF16)


def kernel(x, positions, rel_bias, ffn1_pre_g, ffn1_post_g, ffn1_w_gate, ffn1_w_up, ffn1_w_down, mix_pre_g, mix_post_g, w_in, mla_q_norm_g, mla_w_q_up, mla_kv_norm_g, mla_w_uk, mla_w_uv, cmp_pe_k, cmp_w1_k, cmp_w2_k, cmp_pe_v, cmp_w1_v, cmp_w2_v, w_branch_mla, w_branch_nsa, w_out, ffn2_pre_g, ffn2_post_g, ffn2_w_gate, ffn2_w_up, ffn2_w_down):
    B, S, D = x.shape
    L = w_in.shape[0]
    M = B * S
    G, J = NSA_KV_HEADS, NSA_GROUP
    zoff = _z_layout(D)
    n_half = S // CMP_STRIDE
    n_cmp = (S - CMP_LEN) // CMP_STRIDE + 1
    assert n_half == LANE and n_cmp <= LANE

    gain = lambda g: g.reshape(L, 1, -1)
    bf = lambda w: w.astype(BF16)
    w_z, w_vt = _layout_w_in(w_in, zoff)
    w_q = _layout_w_q_up(mla_w_q_up)
    w_uk, w_uv_t = bf(mla_w_uk), bf(mla_w_uv).transpose(0, 2, 1)
    pe_k = cmp_pe_k.reshape(L, 2, CMP_STRIDE * NSA_DK)
    pe_v = cmp_pe_v.reshape(L, 2, CMP_STRIDE * NSA_DV)
    w2_k = bf(_pad_last(cmp_w2_k, HEAD_PAD))
    f1 = (bf(ffn1_w_gate), bf(ffn1_w_up), bf(ffn1_w_down))
    f2 = (bf(ffn2_w_gate), bf(ffn2_w_up), bf(ffn2_w_down))
    w1_k, w1_v, w2_v = bf(cmp_w1_k), bf(cmp_w1_v), bf(cmp_w2_v)
    w_a, w_b, w_o = bf(w_branch_mla), bf(w_branch_nsa), bf(w_out)
    g_f1pre, g_f1post, g_mpre, g_mpost = gain(ffn1_pre_g), gain(ffn1_post_g), gain(mix_pre_g), gain(mix_post_g)
    g_f2pre, g_f2post, g_q, g_kv = gain(ffn2_pre_g), gain(ffn2_post_g), gain(mla_q_norm_g), gain(mla_kv_norm_g)

    pos_col = positions.reshape(M, 1)
    pos_row = positions.reshape(B, 1, S)
    pos_cmp = _pad_last(positions[:, CMP_LEN - 1::CMP_STRIDE][:, :n_cmp], LANE).reshape(B * LANE, 1)
    q_min = positions.reshape(B, S // NSA_TQ, NSA_TQ).min(axis=-1)
    k_max = positions.reshape(B, S // NSA_TK, NSA_TK).max(axis=-1)
    far = (q_min[:, :, None] - k_max[:, None, :] >= FAR_DIST).astype(jnp.int32).reshape(-1)
    tbl = jnp.take(rel_bias, jnp.asarray(BUCKET_OF_DIST), axis=0).T.reshape(G, J, LANE)
    half = MLA_ROPE // 2
    inv = ROPE_BASE ** (-jnp.arange(half, dtype=F32) * 2.0 / MLA_ROPE)
    cs = _rope_table(pos_col, jnp.concatenate([inv, inv]).reshape(1, MLA_ROPE))

    h = x.reshape(M, D)
    u = _rmsnorm(h, g_f1pre, 0)
    for l in range(L):
        h, u = _ffn(h, u, g_f1post, g_mpre, l, *f1, l)
        z = _in_proj(u, w_z, l)
        v_t = _in_proj_t(u, w_vt, l)
        qp, kp, vt = _mla_prep(z, cs, g_q, g_kv, w_q, w_uk, w_uv_t, l, zoff)
        a = _mla_attn(qp, kp, vt, B, S)

        def half_blocks(off, width, d):
            t = z[:, off:off + G * width].reshape(B, S, G, width)[..., :d]
            return t.transpose(0, 2, 1, 3).reshape(B * G, n_half, CMP_STRIDE * d)

        kc, vct = _nsa_cmp(half_blocks(zoff["kc"], HEAD_PAD, NSA_DK),
                           half_blocks(zoff["vc"], NSA_DV, NSA_DV),
                           pe_k, pe_v, w1_k, w2_k, w1_v, w2_v, l)
        nsa = _nsa_attn(far, z, kc, vct, v_t, pos_row, pos_col, pos_cmp, tbl, B, S, zoff)
        h, u = _merge(a, nsa, z, h, w_a, w_b, w_o, g_mpost, g_f2pre, l, zoff)
        h, u = _ffn(h, u, g_f2post, g_f1pre, min(l + 1, L - 1), *f2, l)
    return h.reshape(B, S, D)
```

```python
import functools
import math

import numpy as np
import jax
import jax.numpy as jnp
from jax import lax
from jax.experimental import pallas as pl
from jax.experimental.pallas import tpu as pltpu

EPS = 1e-6
MLA_HEADS = 8
MLA_Q_LORA = 512
MLA_KV_LORA = 512
MLA_NOPE = 128
MLA_ROPE = 64
MLA_V = 128
ROPE_BASE = 10000.0
NSA_HEADS = 8
NSA_KV_HEADS = 2
NSA_GROUP = NSA_HEADS // NSA_KV_HEADS
NSA_DK = 192
NSA_DV = 128
CMP_LEN = 32
CMP_STRIDE = 16
CMP_HIDDEN = 256
SLC_LEN = 64
SLC_TOPN = 16
WINDOW = 512
FORCED_SCORE = 1e6
REL_BUCKETS = 32
REL_MAX_DIST = 128
NEG = -1e30
MASKED = 2 * NEG

LANE = 128
HEAD_PAD = 256
VMEM_LIMIT = 56 * 1024 * 1024
BF16 = jnp.bfloat16
F32 = jnp.float32

FFN_TM, FFN_TF = 512, 512
PROJ_TM = 1024
PROJ_TN_CAP = 2560
Z_PAD = 5 * LANE
PREP_TM = 512
MLA_TQ, MLA_TK = 512, 256
MLA_HEADS_PER_STEP = 2
NSA_TQ, NSA_TK = LANE, 256
MERGE_TM = 256


def _bucket_of_distance():
    n = np.arange(LANE)
    max_exact = REL_BUCKETS // 2
    large = max_exact + (np.log(np.maximum(n, 1) / max_exact) / math.log(REL_MAX_DIST / max_exact)
                         * (REL_BUCKETS - max_exact)).astype(np.int32)
    bucket = np.where(n < max_exact, n, np.minimum(large, REL_BUCKETS - 1)).astype(np.int32)
    assert bucket[-1] == REL_BUCKETS - 1
    return bucket


BUCKET_OF_DIST = _bucket_of_distance()
FAR_DIST = int(np.max(np.nonzero(BUCKET_OF_DIST != REL_BUCKETS - 1)[0])) + 1


def _params(*sem):
    return pltpu.CompilerParams(dimension_semantics=sem, vmem_limit_bytes=VMEM_LIMIT)


def _rms(x, g):
    return x * lax.rsqrt(jnp.mean(x * x, axis=-1, keepdims=True) + EPS) * g


def _dot(a, b):
    return jnp.dot(a, b, preferred_element_type=F32)


def _dot_nt(a, b):
    return lax.dot_general(a, b, (((1,), (1,)), ((), ())), preferred_element_type=F32)


def _tile_n(n, cap):
    best = LANE
    for t in range(LANE, cap + 1, LANE):
        if n % t == 0:
            best = t
    return best


def _rmsnorm_kernel(x_ref, g_ref, o_ref):
    o_ref[...] = _rms(x_ref[...], g_ref[...]).astype(o_ref.dtype)


def _rmsnorm(x, g, layer):
    m, d = x.shape
    tm = 512
    return pl.pallas_call(
        _rmsnorm_kernel,
        grid=(m // tm,),
        in_specs=[pl.BlockSpec((tm, d), lambda i: (i, 0)),
                  pl.BlockSpec((None, 1, d), lambda i: (layer, 0, 0))],
        out_specs=pl.BlockSpec((tm, d), lambda i: (i, 0)),
        out_shape=jax.ShapeDtypeStruct((m, d), BF16),
        compiler_params=_params("parallel"),
        name="rmsnorm",
    )(x, g)


def _rope_table_kernel(pos_ref, inv_ref, o_ref):
    ang = pos_ref[...].astype(F32) * inv_ref[...]
    o_ref[...] = jnp.concatenate([jnp.cos(ang), jnp.sin(ang)], axis=1)


def _rope_table(pos_col, inv):
    m = pos_col.shape[0]
    tm = 512
    return pl.pallas_call(
        _rope_table_kernel,
        grid=(m // tm,),
        in_specs=[pl.BlockSpec((tm, 1), lambda i: (i, 0)),
                  pl.BlockSpec((1, MLA_ROPE), lambda i: (0, 0))],
        out_specs=pl.BlockSpec((tm, 2 * MLA_ROPE), lambda i: (i, 0)),
        out_shape=jax.ShapeDtypeStruct((m, 2 * MLA_ROPE), F32),
        compiler_params=_params("parallel"),
        name="rope_table",
    )(pos_col, inv)


def _ffn_kernel(x_ref, u_ref, post_g_ref, next_g_ref, wg_ref, wu_ref, wd_ref,
                o_ref, un_ref, acc_ref):
    j = pl.program_id(1)

    @pl.when(j == 0)
    def _():
        acc_ref[...] = jnp.zeros(acc_ref.shape, F32)

    u = u_ref[...]
    gate = _dot(u, wg_ref[...])
    up = _dot(u, wu_ref[...])
    hidden = (gate * jax.nn.sigmoid(gate) * up).astype(BF16)
    acc_ref[...] += _dot(hidden, wd_ref[...])

    @pl.when(j == pl.num_programs(1) - 1)
    def _():
        out = x_ref[...] + 0.5 * _rms(acc_ref[...], post_g_ref[...])
        o_ref[...] = out
        un_ref[...] = _rms(out, next_g_ref[...]).astype(BF16)


def _ffn(x, u, post_g, next_g, next_layer, wg, wu, wd, layer):
    m, d = x.shape
    f = wg.shape[-1]
    tm, tf = FFN_TM, _tile_n(f, FFN_TF)
    return pl.pallas_call(
        _ffn_kernel,
        grid=(m // tm, f // tf),
        in_specs=[pl.BlockSpec((tm, d), lambda i, j: (i, 0)),
                  pl.BlockSpec((tm, d), lambda i, j: (i, 0)),
                  pl.BlockSpec((None, 1, d), lambda i, j: (layer, 0, 0)),
                  pl.BlockSpec((None, 1, d), lambda i, j: (next_layer, 0, 0)),
                  pl.BlockSpec((None, d, tf), lambda i, j: (layer, 0, j)),
                  pl.BlockSpec((None, d, tf), lambda i, j: (layer, 0, j)),
                  pl.BlockSpec((None, tf, d), lambda i, j: (layer, j, 0))],
        out_specs=[pl.BlockSpec((tm, d), lambda i, j: (i, 0)),
                   pl.BlockSpec((tm, d), lambda i, j: (i, 0))],
        out_shape=[jax.ShapeDtypeStruct((m, d), F32), jax.ShapeDtypeStruct((m, d), BF16)],
        scratch_shapes=[pltpu.VMEM((tm, d), F32)],
        compiler_params=_params("parallel", "arbitrary"),
        name="ffn",
    )(x, u, post_g, next_g, wg, wu, wd)


def _matmul_kernel(x_ref, w_ref, o_ref):
    o_ref[...] = _dot(x_ref[...], w_ref[...]).astype(o_ref.dtype)


def _in_proj(u, w, layer):
    m, d = u.shape
    n = w.shape[-1]
    tm, tn = PROJ_TM, _tile_n(n, PROJ_TN_CAP)
    return pl.pallas_call(
        _matmul_kernel,
        grid=(n // tn, m // tm),
        in_specs=[pl.BlockSpec((tm, d), lambda j, i: (i, 0)),
                  pl.BlockSpec((None, d, tn), lambda j, i: (layer, 0, j))],
        out_specs=pl.BlockSpec((tm, tn), lambda j, i: (i, j)),
        out_shape=jax.ShapeDtypeStruct((m, n), BF16),
        compiler_params=_params("parallel", "parallel"),
        name="in_proj",
    )(u, w)


def _matmul_t_kernel(w_ref, x_ref, o_ref):
    o_ref[...] = _dot_nt(w_ref[...], x_ref[...]).astype(o_ref.dtype)


def _in_proj_t(u, w_t, layer):
    m, d = u.shape
    n = w_t.shape[1]
    tm = PROJ_TM
    return pl.pallas_call(
        _matmul_t_kernel,
        grid=(m // tm,),
        in_specs=[pl.BlockSpec((None, n, d), lambda i: (layer, 0, 0)),
                  pl.BlockSpec((tm, d), lambda i: (i, 0))],
        out_specs=pl.BlockSpec((n, tm), lambda i: (0, i)),
        out_shape=jax.ShapeDtypeStruct((n, m), BF16),
        compiler_params=_params("parallel"),
        name="in_proj_t",
    )(w_t, u)


def _softmax_tile_t(s_ref, adds, masks, v_t, m_ref, l_ref, acc_ref):
    probs, alphas = [], []
    for c in range(s_ref.shape[1] // LANE):
        cols = slice(c * LANE, (c + 1) * LANE)
        s = s_ref[:, cols]
        if masks[c] is not None:
            s = jnp.where(masks[c], s, MASKED)
        m_old = m_ref[:, cols]
        m_tile = jnp.max(s, axis=0, keepdims=True)
        if adds[c] is not None:
            m_tile = m_tile + adds[c]
        m_new = jnp.maximum(m_old, m_tile)
        alpha = jnp.exp(m_old - m_new)
        p = jnp.exp(s - (m_new if adds[c] is None else m_new - adds[c]))
        l_ref[:, cols] = alpha * l_ref[:, cols] + jnp.sum(p, axis=0, keepdims=True)
        m_ref[:, cols] = m_new
        probs.append(p.astype(BF16))
        alphas.append(alpha)
    p_t = probs[0] if len(probs) == 1 else jnp.concatenate(probs, axis=1)
    alpha = alphas[0] if len(alphas) == 1 else jnp.concatenate(alphas, axis=1)
    acc_ref[...] = alpha * acc_ref[...] + _dot(v_t, p_t)


def _softmax_init(m_ref, l_ref, acc_ref):
    m_ref[...] = jnp.full(m_ref.shape, NEG, F32)
    l_ref[...] = jnp.zeros(l_ref.shape, F32)
    acc_ref[...] = jnp.zeros(acc_ref.shape, F32)


def _mla_prep_kernel(cq_ref, ckv_ref, kr_ref, cs_ref, gq_ref, gkv_ref, wq_ref, wuk_ref, wuvt_ref,
                     q_ref, k_ref, vt_ref):
    cs = cs_ref[...]
    qn = _rms(cq_ref[...].astype(F32), gq_ref[...]).astype(BF16)
    q = _dot(qn, wq_ref[...])
    kvn = _rms(ckv_ref[...].astype(F32), gkv_ref[...]).astype(BF16)
    k_nope = _dot(kvn, wuk_ref[...])
    t = kr_ref[...].astype(F32) * cs
    k_pe = (t + pltpu.roll(t, MLA_ROPE, 1)).astype(BF16)
    for h in range(MLA_HEADS):
        lo = h * HEAD_PAD
        q_ref[:, lo:lo + MLA_NOPE] = q[:, lo:lo + MLA_NOPE].astype(BF16)
        q_ref[:, lo + MLA_NOPE:lo + HEAD_PAD] = (q[:, lo + MLA_NOPE:lo + HEAD_PAD] * cs).astype(BF16)
        k_ref[:, lo:lo + MLA_NOPE] = k_nope[:, h * MLA_NOPE:(h + 1) * MLA_NOPE].astype(BF16)
        k_ref[:, lo + MLA_NOPE:lo + HEAD_PAD] = k_pe
    vt_ref[...] = _dot_nt(wuvt_ref[...], kvn).astype(BF16)


def _mla_prep(z, cs, gq, gkv, wq, wuk, wuv_t, layer, zoff):
    m = z.shape[0]
    tm = PREP_TM
    hq = MLA_HEADS * HEAD_PAD
    hv = MLA_HEADS * MLA_V
    const = lambda i: (layer, 0, 0)
    return pl.pallas_call(
        _mla_prep_kernel,
        grid=(m // tm,),
        in_specs=[pl.BlockSpec((tm, MLA_Q_LORA), lambda i: (i, zoff["cq"] // MLA_Q_LORA)),
                  pl.BlockSpec((tm, MLA_KV_LORA), lambda i: (i, zoff["ckv"] // MLA_KV_LORA)),
                  pl.BlockSpec((tm, LANE), lambda i: (i, zoff["kr"] // LANE)),
                  pl.BlockSpec((tm, LANE), lambda i: (i, 0)),
                  pl.BlockSpec((None, 1, MLA_Q_LORA), const),
                  pl.BlockSpec((None, 1, MLA_KV_LORA), const),
                  pl.BlockSpec((None, MLA_Q_LORA, hq), const),
                  pl.BlockSpec((None, MLA_KV_LORA, MLA_HEADS * MLA_NOPE), const),
                  pl.BlockSpec((None, hv, MLA_KV_LORA), const)],
        out_specs=[pl.BlockSpec((tm, hq), lambda i: (i, 0)),
                   pl.BlockSpec((tm, hq), lambda i: (i, 0)),
                   pl.BlockSpec((hv, tm), lambda i: (0, i))],
        out_shape=[jax.ShapeDtypeStruct((m, hq), BF16),
                   jax.ShapeDtypeStruct((m, hq), BF16),
                   jax.ShapeDtypeStruct((hv, m), BF16)],
        compiler_params=_params("parallel"),
        name="mla_prep",
    )(z, z, z, cs, gq, gkv, wq, wuk, wuv_t)


def _mla_attn_kernel(q_ref, k_ref, vt_ref, o_ref, s_ref, m_ref, l_ref, acc_ref, *, tq, tk, heads):
    i = pl.program_id(2)
    nc = tq // LANE
    _softmax_init(m_ref, l_ref, acc_ref)
    qt = i * tq + lax.broadcasted_iota(jnp.int32, (1, tq), 1)
    kcol = lax.broadcasted_iota(jnp.int32, (tk, 1), 0)
    qk = [slice(h * HEAD_PAD, (h + 1) * HEAD_PAD) for h in range(heads)]
    vd = [slice(h * MLA_V, (h + 1) * MLA_V) for h in range(heads)]

    def tile(kt, masked):
        ks = pl.multiple_of(kt * tk, tk)
        for h in range(heads):
            s_ref[h] = _dot_nt(k_ref[pl.ds(ks, tk), qk[h]], q_ref[:, qk[h]])
        masks = [None] * nc
        if masked:
            masks = [ks + kcol <= qt[:, c * LANE:(c + 1) * LANE] for c in range(nc)]
        for h in range(heads):
            _softmax_tile_t(s_ref.at[h], [None] * nc, masks, vt_ref[vd[h], pl.ds(ks, tk)],
                            m_ref.at[h], l_ref.at[h], acc_ref.at[h])

    n_full = (i * tq) // tk
    n_all = (i * tq + tq + tk - 1) // tk

    def full_body(kt, c):
        tile(kt, False)
        return c

    def diag_body(kt, c):
        tile(kt, True)
        return c

    lax.fori_loop(0, n_full, full_body, 0)
    lax.fori_loop(n_full, n_all, diag_body, 0)
    for h in range(heads):
        o_ref[:, vd[h]] = (acc_ref[h] / l_ref[h]).T.astype(o_ref.dtype)


def _mla_attn(qp, kp, vt, batch, seq):
    tq, tk, heads = MLA_TQ, MLA_TK, MLA_HEADS_PER_STEP
    nq = seq // tq
    kern = functools.partial(_mla_attn_kernel, tq=tq, tk=tk, heads=heads)
    return pl.pallas_call(
        kern,
        grid=(batch, MLA_HEADS // heads, nq),
        in_specs=[pl.BlockSpec((tq, heads * HEAD_PAD), lambda b, h, i: (b * nq + i, h)),
                  pl.BlockSpec((seq, heads * HEAD_PAD), lambda b, h, i: (b, h)),
                  pl.BlockSpec((heads * MLA_V, seq), lambda b, h, i: (h, b))],
        out_specs=pl.BlockSpec((tq, heads * MLA_V), lambda b, h, i: (b * nq + i, h)),
        out_shape=jax.ShapeDtypeStruct((batch * seq, MLA_HEADS * MLA_V), BF16),
        scratch_shapes=[pltpu.VMEM((heads, tk, tq), F32), pltpu.VMEM((heads, 1, tq), F32),
                        pltpu.VMEM((heads, 1, tq), F32), pltpu.VMEM((heads, MLA_V, tq), F32)],
        compiler_params=_params("parallel", "parallel", "parallel"),
        name="mla_attn",
    )(qp, kp, vt)


def _compress(a_ref, pe_ref, w1_ref, w2_ref):
    a = a_ref[...].astype(F32)
    half = a.shape[1]
    first = _dot((a + pe_ref[0:1, :]).astype(BF16), w1_ref[0:half, :])
    second = _dot((a + pe_ref[1:2, :]).astype(BF16), w1_ref[half:2 * half, :])
    n = a.shape[0]
    hidden = first + pltpu.roll(second, n - 1, 0)
    return _dot((hidden * jax.nn.sigmoid(hidden)).astype(BF16), w2_ref[...])


def _nsa_cmp_kernel(ak_ref, av_ref, pek_ref, pev_ref, w1k_ref, w2k_ref, w1v_ref, w2v_ref,
                    kc_ref, vct_ref):
    kc_ref[...] = _compress(ak_ref, pek_ref, w1k_ref, w2k_ref).astype(BF16)
    vct_ref[...] = _compress(av_ref, pev_ref, w1v_ref, w2v_ref).T.astype(BF16)


def _nsa_cmp(ak, av, pek, pev, w1k, w2k, w1v, w2v, layer):
    bg, nh, dk16 = ak.shape
    dv16 = av.shape[-1]
    const = lambda i: (layer, 0, 0)
    return pl.pallas_call(
        _nsa_cmp_kernel,
        grid=(bg,),
        in_specs=[pl.BlockSpec((None, nh, dk16), lambda i: (i, 0, 0)),
                  pl.BlockSpec((None, nh, dv16), lambda i: (i, 0, 0)),
                  pl.BlockSpec((None, 2, dk16), const),
                  pl.BlockSpec((None, 2, dv16), const),
                  pl.BlockSpec((None, 2 * dk16, CMP_HIDDEN), const),
                  pl.BlockSpec((None, CMP_HIDDEN, HEAD_PAD), const),
                  pl.BlockSpec((None, 2 * dv16, CMP_HIDDEN), const),
                  pl.BlockSpec((None, CMP_HIDDEN, NSA_DV), const)],
        out_specs=[pl.BlockSpec((None, nh, HEAD_PAD), lambda i: (i, 0, 0)),
                   pl.BlockSpec((None, NSA_DV, nh), lambda i: (i, 0, 0))],
        out_shape=[jax.ShapeDtypeStruct((bg, nh, HEAD_PAD), BF16),
                   jax.ShapeDtypeStruct((bg, NSA_DV, nh), BF16)],
        compiler_params=_params("parallel"),
        name="nsa_cmp",
    )(ak, av, pek, pev, w1k, w2k, w1v, w2v)


def _split3(x):
    a = x.astype(BF16)
    r = x - a.astype(F32)
    b = r.astype(BF16)
    c = (r - b.astype(F32)).astype(BF16)
    return a, b, c


def _lookup(table_row, idx):
    return jnp.take_along_axis(jnp.broadcast_to(table_row, idx.shape), idx, axis=1,
                               mode="promise_in_bounds")


def _nsa_attn_kernel(far_ref, q_ref, kc_ref, vct_ref, ks_ref, vst_ref, kw_ref, vwt_ref, ng_ref,
                     prow_ref, pcol_ref, pcmp_ref, tbl_ref, o_ref,
                     s_ref, m_ref, l_ref, acc_ref, *, tq, tk, seq):
    b, i = pl.program_id(0), pl.program_id(2)
    nq, nk = seq // tq, seq // tk
    J = NSA_GROUP
    n_slc = seq // SLC_LEN
    n_cmp = (seq - CMP_LEN) // CMP_STRIDE + 1
    t0 = pl.multiple_of(i * tq, tq)

    qs = jnp.concatenate([q_ref[:, j * HEAD_PAD:(j + 1) * HEAD_PAD] for j in range(J)], axis=0)
    qpos = prow_ref[:, pl.ds(t0, tq)]
    qt = t0 + lax.broadcasted_iota(jnp.int32, (1, tq), 1)
    tbl = tbl_ref[...]
    far_bias = [tbl[j:j + 1, LANE - 1:LANE] for j in range(J)]
    head = [slice(j * tq, (j + 1) * tq) for j in range(J)]

    blk = lax.broadcasted_iota(jnp.int32, (LANE, 1), 0)
    keep_c = (blk * CMP_STRIDE + (CMP_LEN - 1) <= qt) & (blk < n_cmp)
    idx_c = jnp.clip(qpos - pcmp_ref[...], 0, LANE - 1)
    s_c = _dot_nt(kc_ref[...], qs)
    p_heads = []
    for j in range(J):
        s = jnp.where(keep_c, s_c[:, head[j]] + _lookup(tbl[j:j + 1, :], idx_c), NEG)
        e = jnp.where(keep_c, jnp.exp(s - jnp.max(s, axis=0, keepdims=True)), 0.0)
        den = jnp.sum(e, axis=0, keepdims=True)
        p_heads.append(e / jnp.where(den > 0.0, den, 1.0))
    o_cmp = _dot(vct_ref[...], jnp.concatenate([p.astype(BF16) for p in p_heads], axis=1))

    rows = 32
    assert n_slc <= rows
    m_row = lax.broadcasted_iota(jnp.int32, (rows, LANE), 0)
    n_col = lax.broadcasted_iota(jnp.int32, (rows, LANE), 1)
    per = SLC_LEN // CMP_STRIDE
    back = (CMP_LEN - 1) // CMP_STRIDE
    overlap = ((n_col >= per * m_row - back) & (n_col <= per * m_row + per - 1)
               & (n_col < n_cmp) & (m_row < n_slc)).astype(BF16)
    imp = sum(_dot(overlap, part) for part in _split3(sum(p_heads)))
    m_blk = lax.broadcasted_iota(jnp.int32, (rows, 1), 0)
    cur = qt // SLC_LEN
    valid = m_blk <= cur
    forced = valid & ((m_blk == 0) | (m_blk >= cur - 1))
    score = jnp.where(forced, FORCED_SCORE, jnp.where(valid, imp, -1.0))
    rank = jnp.zeros((rows, tq), jnp.int32)
    for mp in range(n_slc):
        other = score[mp:mp + 1, :]
        ahead = (other > score) | ((other == score) & (m_blk > mp))
        rank = rank + ahead.astype(jnp.int32)
    sel = (valid & (rank < min(SLC_TOPN, n_slc))).astype(BF16)

    SLC, WIN = 0, 1
    _softmax_init(m_ref, l_ref, acc_ref)
    sub = tk // LANE

    def slc_keep(kidx):
        expand = (kidx // SLC_LEN == lax.broadcasted_iota(jnp.int32, (1, rows), 1)).astype(BF16)
        return (_dot(expand, sel) > 0.5) & (kidx <= qt)

    def win_keep(kidx):
        diff = qt - kidx
        return (diff >= 0) & (diff < WINDOW)

    def tile(kt, slots):
        ks = pl.multiple_of(kt * tk, tk)
        k_refs, vt_refs, keeps = (ks_ref, kw_ref), (vst_ref, vwt_ref), (slc_keep, win_keep)
        for slot in slots:
            s_ref[slot] = _dot_nt(k_refs[slot][pl.ds(ks, tk), :], qs)
        for hb in range(sub):
            kb = kt * sub + hb
            near = (far_ref[(b * nq + i) * (nk * sub) + kb] == 0) & (kb * LANE < t0 + tq)

            @pl.when(near)
            def _():
                kpos = pcol_ref[pl.ds(pl.multiple_of(ks + hb * LANE, LANE), LANE), :]
                idx = jnp.clip(qpos - kpos, 0, LANE - 1)
                for j in range(J):
                    delta = _lookup(tbl[j:j + 1, :] - far_bias[j], idx)
                    for slot in slots:
                        s_ref[slot, hb * LANE:(hb + 1) * LANE, head[j]] += delta

        kidx = ks + lax.broadcasted_iota(jnp.int32, (tk, 1), 0)
        for slot in slots:
            _softmax_tile_t(s_ref.at[slot], far_bias, [keeps[slot](kidx)] * J,
                            vt_refs[slot][:, pl.ds(ks, tk)],
                            m_ref.at[slot], l_ref.at[slot], acc_ref.at[slot])

    def slc_body(kt, c):
        tile(kt, (SLC,))
        return c

    def both_body(kt, c):
        tile(kt, (SLC, WIN))
        return c

    last = (t0 + tq + tk - 1) // tk
    first_win = jnp.maximum(t0 - (WINDOW - 1), 0) // tk
    lax.fori_loop(0, first_win, slc_body, 0)
    lax.fori_loop(first_win, last, both_body, 0)
    o_slc = acc_ref[SLC] / l_ref[SLC]
    o_win = acc_ref[WIN] / l_ref[WIN]

    gates = jax.nn.sigmoid(ng_ref[...].astype(F32)).T
    for j in range(J):
        o = (gates[3 * j:3 * j + 1, :] * o_cmp[:, head[j]]
             + gates[3 * j + 1:3 * j + 2, :] * o_slc[:, head[j]]
             + gates[3 * j + 2:3 * j + 3, :] * o_win[:, head[j]])
        o_ref[:, j * NSA_DV:(j + 1) * NSA_DV] = o.T.astype(o_ref.dtype)


def _nsa_attn(far, z, kc, vct, v_t, pos_row, pos_col, pos_cmp, tbl, batch, seq, zoff):
    tq, tk = NSA_TQ, NSA_TK
    nq = seq // tq
    G, J = NSA_KV_HEADS, NSA_GROUP
    R = J * tq
    kern = functools.partial(_nsa_attn_kernel, tq=tq, tk=tk, seq=seq)
    grid_spec = pltpu.PrefetchScalarGridSpec(
        num_scalar_prefetch=1,
        grid=(batch, G, nq),
        in_specs=[
            pl.BlockSpec((tq, J * HEAD_PAD), lambda b, g, i, far: (b * nq + i, g)),
            pl.BlockSpec((None, LANE, HEAD_PAD), lambda b, g, i, far: (b * G + g, 0, 0)),
            pl.BlockSpec((None, NSA_DV, LANE), lambda b, g, i, far: (b * G + g, 0, 0)),
            pl.BlockSpec((seq, HEAD_PAD), lambda b, g, i, far: (b, zoff["ks"] // HEAD_PAD + g)),
            pl.BlockSpec((NSA_DV, seq), lambda b, g, i, far: (g, b)),
            pl.BlockSpec((seq, HEAD_PAD), lambda b, g, i, far: (b, zoff["kw"] // HEAD_PAD + g)),
            pl.BlockSpec((NSA_DV, seq), lambda b, g, i, far: (G + g, b)),
            pl.BlockSpec((tq, LANE), lambda b, g, i, far: (b * nq + i, zoff["ng"] // LANE + g)),
            pl.BlockSpec((None, 1, seq), lambda b, g, i, far: (b, 0, 0)),
            pl.BlockSpec((seq, 1), lambda b, g, i, far: (b, 0)),
            pl.BlockSpec((LANE, 1), lambda b, g, i, far: (b, 0)),
            pl.BlockSpec((None, J, LANE), lambda b, g, i, far: (g, 0, 0)),
        ],
        out_specs=pl.BlockSpec((tq, J * NSA_DV), lambda b, g, i, far: (b * nq + i, g)),
        scratch_shapes=[pltpu.VMEM((2, tk, R), F32), pltpu.VMEM((2, 1, R), F32),
                        pltpu.VMEM((2, 1, R), F32), pltpu.VMEM((2, NSA_DV, R), F32)],
    )
    return pl.pallas_call(
        kern,
        grid_spec=grid_spec,
        out_shape=jax.ShapeDtypeStruct((batch * seq, NSA_HEADS * NSA_DV), BF16),
        compiler_params=_params("parallel", "parallel", "parallel"),
        name="nsa_attn",
    )(far, z, kc, vct, z, v_t, z, v_t, z, pos_row, pos_col, pos_cmp, tbl)


def _merge_kernel(a_ref, b_ref, ga_ref, gb_ref, h_ref, wa_ref, wb_ref, wo_ref, post_g_ref,
                  next_g_ref, o_ref, un_ref):
    ya = _dot(a_ref[...], wa_ref[...])
    yb = _dot(b_ref[...], wb_ref[...])
    m = (jax.nn.sigmoid(ga_ref[...].astype(F32)) * ya
         + jax.nn.sigmoid(gb_ref[...].astype(F32)) * yb).astype(BF16)
    y = _dot(m, wo_ref[...])
    out = h_ref[...] + _rms(y, post_g_ref[...])
    o_ref[...] = out
    un_ref[...] = _rms(out, next_g_ref[...]).astype(BF16)


def _merge(a, bb, z, h, wa, wb, wo, post_g, next_g, layer, zoff):
    m, d = h.shape
    tm = MERGE_TM
    da, db = a.shape[1], bb.shape[1]
    gblk = zoff["mg"] // d
    const = lambda i: (layer, 0, 0)
    return pl.pallas_call(
        _merge_kernel,
        grid=(m // tm,),
        in_specs=[pl.BlockSpec((tm, da), lambda i: (i, 0)),
                  pl.BlockSpec((tm, db), lambda i: (i, 0)),
                  pl.BlockSpec((tm, d), lambda i: (i, gblk)),
                  pl.BlockSpec((tm, d), lambda i: (i, gblk + 1)),
                  pl.BlockSpec((tm, d), lambda i: (i, 0)),
                  pl.BlockSpec((None, da, d), const),
                  pl.BlockSpec((None, db, d), const),
                  pl.BlockSpec((None, d, d), const),
                  pl.BlockSpec((None, 1, d), const),
                  pl.BlockSpec((None, 1, d), const)],
        out_specs=[pl.BlockSpec((tm, d), lambda i: (i, 0)),
                   pl.BlockSpec((tm, d), lambda i: (i, 0))],
        out_shape=[jax.ShapeDtypeStruct((m, d), F32), jax.ShapeDtypeStruct((m, d), BF16)],
        compiler_params=_params("parallel"),
        name="merge_out",
    )(a, bb, z, z, h, wa, wb, wo, post_g, next_g)


def _z_layout(d_model):
    G = NSA_KV_HEADS
    widths = [("q", NSA_HEADS * HEAD_PAD), ("mg", 2 * d_model), ("cq", MLA_Q_LORA),
              ("ckv", MLA_KV_LORA), ("kc", G * HEAD_PAD), ("ks", G * HEAD_PAD),
              ("kw", G * HEAD_PAD), ("vc", G * NSA_DV), ("kr", LANE), ("ng", G * LANE)]
    off, pos = {}, 0
    for name, w in widths:
        off[name] = pos
        pos += w
    off["used"] = pos
    off["total"] = -(-pos // Z_PAD) * Z_PAD
    assert off["q"] == 0 and off["mg"] % d_model == 0
    assert off["cq"] % MLA_Q_LORA == 0 and off["ckv"] % MLA_KV_LORA == 0
    assert all(off[k] % HEAD_PAD == 0 for k in ("kc", "ks", "kw"))
    return off


def _swap_halves(w):
    half = w.shape[-1] // 2
    return jnp.concatenate([-w[..., half:], w[..., :half]], axis=-1)


def _pad_last(w, width):
    return jnp.pad(w, [(0, 0)] * (w.ndim - 1) + [(0, width - w.shape[-1])])


def _layout_w_in(w_in, zoff):
    L, D, _ = w_in.shape
    G, J = NSA_KV_HEADS, NSA_GROUP
    splits = [MLA_Q_LORA, MLA_KV_LORA, MLA_ROPE, NSA_HEADS * NSA_DK,
              G * NSA_DK, G * NSA_DV, G * NSA_DK, G * NSA_DV, G * NSA_DK, G * NSA_DV,
              NSA_HEADS * 3, 2 * D]
    (c_q, c_kv, k_rope, nsa_q, k_c, v_c, k_s, v_s, k_w, v_w, nsa_g, merge_g) = jnp.split(
        w_in, [int(v) for v in np.cumsum(splits)[:-1]], axis=-1)

    def heads(w, n):
        return _pad_last(w.reshape(L, D, n, NSA_DK), HEAD_PAD).reshape(L, D, n * HEAD_PAD).astype(BF16)

    gates = _pad_last(nsa_g.reshape(L, D, G, J * 3), LANE).reshape(L, D, G * LANE)
    cols = [heads(nsa_q * (NSA_DK ** -0.5), NSA_HEADS), merge_g, c_q, c_kv, heads(k_c, G),
            heads(k_s, G), heads(k_w, G), v_c, k_rope, _swap_halves(k_rope), gates,
            jnp.zeros((L, D, zoff["total"] - zoff["used"]), BF16)]
    w_z = jnp.concatenate([c.astype(BF16) for c in cols], axis=-1)
    w_vt = jnp.concatenate([v_s, v_w], axis=-1).astype(BF16).transpose(0, 2, 1)
    return w_z, w_vt


def _layout_w_q_up(w):
    L, r, _ = w.shape
    w = w.reshape(L, r, MLA_HEADS, MLA_NOPE + MLA_ROPE) * ((MLA_NOPE + MLA_ROPE) ** -0.5)
    rope = w[..., MLA_NOPE:]
    return jnp.concatenate([w[..., :MLA_NOPE], rope, _swap_halves(rope)], axis=-1).reshape(
        L, r, MLA_HEADS * HEAD_PAD).astype(BF16)


def kernel(x, positions, rel_bias, ffn1_pre_g, ffn1_post_g, ffn1_w_gate, ffn1_w_up, ffn1_w_down, mix_pre_g, mix_post_g, w_in, mla_q_norm_g, mla_w_q_up, mla_kv_norm_g, mla_w_uk, mla_w_uv, cmp_pe_k, cmp_w1_k, cmp_w2_k, cmp_pe_v, cmp_w1_v, cmp_w2_v, w_branch_mla, w_branch_nsa, w_out, ffn2_pre_g, ffn2_post_g, ffn2_w_gate, ffn2_w_up, ffn2_w_down):
    B, S, D = x.shape
    L = w_in.shape[0]
    M = B * S
    G, J = NSA_KV_HEADS, NSA_GROUP
    zoff = _z_layout(D)
    n_half = S // CMP_STRIDE
    n_cmp = (S - CMP_LEN) // CMP_STRIDE + 1
    assert n_half == LANE and n_cmp <= LANE

    gain = lambda g: g.reshape(L, 1, -1)
    bf = lambda w: w.astype(BF16)
    w_z, w_vt = _layout_w_in(w_in, zoff)
    w_q = _layout_w_q_up(mla_w_q_up)
    w_uk, w_uv_t = bf(mla_w_uk), bf(mla_w_uv).transpose(0, 2, 1)
    pe_k = cmp_pe_k.reshape(L, 2, CMP_STRIDE * NSA_DK)
    pe_v = cmp_pe_v.reshape(L, 2, CMP_STRIDE * NSA_DV)
    w2_k = bf(_pad_last(cmp_w2_k, HEAD_PAD))
    f1 = (bf(ffn1_w_gate), bf(ffn1_w_up), bf(ffn1_w_down))
    f2 = (bf(ffn2_w_gate), bf(ffn2_w_up), bf(ffn2_w_down))
    w1_k, w1_v, w2_v = bf(cmp_w1_k), bf(cmp_w1_v), bf(cmp_w2_v)
    w_a, w_b, w_o = bf(w_branch_mla), bf(w_branch_nsa), bf(w_out)
    g_f1pre, g_f1post, g_mpre, g_mpost = gain(ffn1_pre_g), gain(ffn1_post_g), gain(mix_pre_g), gain(mix_post_g)
    g_f2pre, g_f2post, g_q, g_kv = gain(ffn2_pre_g), gain(ffn2_post_g), gain(mla_q_norm_g), gain(mla_kv_norm_g)

    pos_col = positions.reshape(M, 1)
    pos_row = positions.reshape(B, 1, S)
    pos_cmp = _pad_last(positions[:, CMP_LEN - 1::CMP_STRIDE][:, :n_cmp], LANE).reshape(B * LANE, 1)
    q_min = positions.reshape(B, S // NSA_TQ, NSA_TQ).min(axis=-1)
    k_max = positions.reshape(B, S // LANE, LANE).max(axis=-1)
    far = (q_min[:, :, None] - k_max[:, None, :] >= FAR_DIST).astype(jnp.int32).reshape(-1)
    tbl = jnp.take(rel_bias, jnp.asarray(BUCKET_OF_DIST), axis=0).T.reshape(G, J, LANE)
    half = MLA_ROPE // 2
    inv = ROPE_BASE ** (-jnp.arange(half, dtype=F32) * 2.0 / MLA_ROPE)
    cs = _rope_table(pos_col, jnp.concatenate([inv, inv]).reshape(1, MLA_ROPE))

    h = x.reshape(M, D)
    u = _rmsnorm(h, g_f1pre, 0)
    for l in range(L):
        h, u = _ffn(h, u, g_f1post, g_mpre, l, *f1, l)
        z = _in_proj(u, w_z, l)
        v_t = _in_proj_t(u, w_vt, l)
        qp, kp, vt = _mla_prep(z, cs, g_q, g_kv, w_q, w_uk, w_uv_t, l, zoff)
        a = _mla_attn(qp, kp, vt, B, S)

        def half_blocks(off, width, d):
            t = z[:, off:off + G * width].reshape(B, S, G, width)[..., :d]
            return t.transpose(0, 2, 1, 3).reshape(B * G, n_half, CMP_STRIDE * d)

        kc, vct = _nsa_cmp(half_blocks(zoff["kc"], HEAD_PAD, NSA_DK),
                           half_blocks(zoff["vc"], NSA_DV, NSA_DV),
                           pe_k, pe_v, w1_k, w2_k, w1_v, w2_v, l)
        nsa = _nsa_attn(far, z, kc, vct, v_t, pos_row, pos_col, pos_cmp, tbl, B, S, zoff)
        h, u = _merge(a, nsa, z, h, w_a, w_b, w_o, g_mpost, g_f2pre, l, zoff)
        h, u = _ffn(h, u, g_f2post, g_f1pre, min(l + 1, L - 1), *f2, l)
    return h.reshape(B, S, D)
```

```python
import functools
import math

import numpy as np
import jax
import jax.numpy as jnp
from jax import lax
from jax.experimental import pallas as pl
from jax.experimental.pallas import tpu as pltpu

EPS = 1e-6
MLA_HEADS = 8
MLA_Q_LORA = 512
MLA_KV_LORA = 512
MLA_NOPE = 128
MLA_ROPE = 64
MLA_V = 128
ROPE_BASE = 10000.0
NSA_HEADS = 8
NSA_KV_HEADS = 2
NSA_GROUP = NSA_HEADS // NSA_KV_HEADS
NSA_DK = 192
NSA_DV = 128
CMP_LEN = 32
CMP_STRIDE = 16
CMP_HIDDEN = 256
SLC_LEN = 64
SLC_TOPN = 16
WINDOW = 512
FORCED_SCORE = 1e6
REL_BUCKETS = 32
REL_MAX_DIST = 128
NEG = -1e30
MASKED = 2 * NEG
LOG2E = 1.0 / math.log(2.0)

LANE = 128
HEAD_PAD = 256
VMEM_LIMIT = 56 * 1024 * 1024
BF16 = jnp.bfloat16
F32 = jnp.float32

FFN_TM, FFN_TF = 512, 512
PROJ_TM = 1024
PROJ_TN_CAP = 2560
Z_PAD = 5 * LANE
PREP_TM = 512
MLA_TQ, MLA_TK = 512, 256
MLA_HEADS_PER_STEP = 4
NSA_TQ, NSA_TK = 256, 256
MERGE_TM = 256


def _bucket_of_distance():
    n = np.arange(LANE)
    max_exact = REL_BUCKETS // 2
    large = max_exact + (np.log(np.maximum(n, 1) / max_exact) / math.log(REL_MAX_DIST / max_exact)
                         * (REL_BUCKETS - max_exact)).astype(np.int32)
    bucket = np.where(n < max_exact, n, np.minimum(large, REL_BUCKETS - 1)).astype(np.int32)
    assert bucket[-1] == REL_BUCKETS - 1
    return bucket


BUCKET_OF_DIST = _bucket_of_distance()
FAR_DIST = int(np.max(np.nonzero(BUCKET_OF_DIST != REL_BUCKETS - 1)[0])) + 1


def _params(*sem):
    return pltpu.CompilerParams(dimension_semantics=sem, vmem_limit_bytes=VMEM_LIMIT)


def _rms(x, g):
    return x * lax.rsqrt(jnp.mean(x * x, axis=-1, keepdims=True) + EPS) * g


def _dot(a, b):
    return jnp.dot(a, b, preferred_element_type=F32)


def _dot_nt(a, b):
    return lax.dot_general(a, b, (((1,), (1,)), ((), ())), preferred_element_type=F32)


def _tile_n(n, cap):
    best = LANE
    for t in range(LANE, cap + 1, LANE):
        if n % t == 0:
            best = t
    return best


def _rmsnorm_kernel(x_ref, g_ref, o_ref):
    o_ref[...] = _rms(x_ref[...], g_ref[...]).astype(o_ref.dtype)


def _rmsnorm(x, g, layer):
    m, d = x.shape
    tm = 512
    return pl.pallas_call(
        _rmsnorm_kernel,
        grid=(m // tm,),
        in_specs=[pl.BlockSpec((tm, d), lambda i: (i, 0)),
                  pl.BlockSpec((None, 1, d), lambda i: (layer, 0, 0))],
        out_specs=pl.BlockSpec((tm, d), lambda i: (i, 0)),
        out_shape=jax.ShapeDtypeStruct((m, d), BF16),
        compiler_params=_params("parallel"),
        name="rmsnorm",
    )(x, g)


def _rope_table_kernel(pos_ref, inv_ref, o_ref):
    ang = pos_ref[...].astype(F32) * inv_ref[...]
    o_ref[...] = jnp.concatenate([jnp.cos(ang), jnp.sin(ang)], axis=1)


def _rope_table(pos_col, inv):
    m = pos_col.shape[0]
    tm = 512
    return pl.pallas_call(
        _rope_table_kernel,
        grid=(m // tm,),
        in_specs=[pl.BlockSpec((tm, 1), lambda i: (i, 0)),
                  pl.BlockSpec((1, MLA_ROPE), lambda i: (0, 0))],
        out_specs=pl.BlockSpec((tm, 2 * MLA_ROPE), lambda i: (i, 0)),
        out_shape=jax.ShapeDtypeStruct((m, 2 * MLA_ROPE), F32),
        compiler_params=_params("parallel"),
        name="rope_table",
    )(pos_col, inv)


def _ffn_kernel(x_ref, u_ref, post_g_ref, next_g_ref, wg_ref, wu_ref, wd_ref,
                o_ref, un_ref, acc_ref):
    j = pl.program_id(1)

    @pl.when(j == 0)
    def _():
        acc_ref[...] = jnp.zeros(acc_ref.shape, F32)

    u = u_ref[...]
    gate = _dot(u, wg_ref[...])
    up = _dot(u, wu_ref[...])
    hidden = (gate * jax.nn.sigmoid(gate) * up).astype(BF16)
    acc_ref[...] += _dot(hidden, wd_ref[...])

    @pl.when(j == pl.num_programs(1) - 1)
    def _():
        out = x_ref[...] + 0.5 * _rms(acc_ref[...], post_g_ref[...])
        o_ref[...] = out
        un_ref[...] = _rms(out, next_g_ref[...]).astype(BF16)


def _ffn(x, u, post_g, next_g, next_layer, wg, wu, wd, layer):
    m, d = x.shape
    f = wg.shape[-1]
    tm, tf = FFN_TM, _tile_n(f, FFN_TF)
    return pl.pallas_call(
        _ffn_kernel,
        grid=(m // tm, f // tf),
        in_specs=[pl.BlockSpec((tm, d), lambda i, j: (i, 0)),
                  pl.BlockSpec((tm, d), lambda i, j: (i, 0)),
                  pl.BlockSpec((None, 1, d), lambda i, j: (layer, 0, 0)),
                  pl.BlockSpec((None, 1, d), lambda i, j: (next_layer, 0, 0)),
                  pl.BlockSpec((None, d, tf), lambda i, j: (layer, 0, j)),
                  pl.BlockSpec((None, d, tf), lambda i, j: (layer, 0, j)),
                  pl.BlockSpec((None, tf, d), lambda i, j: (layer, j, 0))],
        out_specs=[pl.BlockSpec((tm, d), lambda i, j: (i, 0)),
                   pl.BlockSpec((tm, d), lambda i, j: (i, 0))],
        out_shape=[jax.ShapeDtypeStruct((m, d), F32), jax.ShapeDtypeStruct((m, d), BF16)],
        scratch_shapes=[pltpu.VMEM((tm, d), F32)],
        compiler_params=_params("parallel", "arbitrary"),
        name="ffn",
    )(x, u, post_g, next_g, wg, wu, wd)


def _matmul_kernel(x_ref, w_ref, o_ref):
    o_ref[...] = _dot(x_ref[...], w_ref[...]).astype(o_ref.dtype)


def _in_proj(u, w, layer):
    m, d = u.shape
    n = w.shape[-1]
    tm, tn = PROJ_TM, _tile_n(n, PROJ_TN_CAP)
    return pl.pallas_call(
        _matmul_kernel,
        grid=(n // tn, m // tm),
        in_specs=[pl.BlockSpec((tm, d), lambda j, i: (i, 0)),
                  pl.BlockSpec((None, d, tn), lambda j, i: (layer, 0, j))],
        out_specs=pl.BlockSpec((tm, tn), lambda j, i: (i, j)),
        out_shape=jax.ShapeDtypeStruct((m, n), BF16),
        compiler_params=_params("parallel", "parallel"),
        name="in_proj",
    )(u, w)


def _matmul_t_kernel(w_ref, x_ref, o_ref):
    o_ref[...] = _dot_nt(w_ref[...], x_ref[...]).astype(o_ref.dtype)


def _in_proj_t(u, w_t, layer):
    m, d = u.shape
    n = w_t.shape[1]
    tm = PROJ_TM
    return pl.pallas_call(
        _matmul_t_kernel,
        grid=(m // tm,),
        in_specs=[pl.BlockSpec((None, n, d), lambda i: (layer, 0, 0)),
                  pl.BlockSpec((tm, d), lambda i: (i, 0))],
        out_specs=pl.BlockSpec((n, tm), lambda i: (0, i)),
        out_shape=jax.ShapeDtypeStruct((n, m), BF16),
        compiler_params=_params("parallel"),
        name="in_proj_t",
    )(w_t, u)


def _softmax_tile_t(s_ref, adds, masks, v_t, m_ref, l_ref, acc_ref):
    probs, alphas = [], []
    for c in range(s_ref.shape[1] // LANE):
        cols = slice(c * LANE, (c + 1) * LANE)
        s = s_ref[:, cols]
        if masks[c] is not None:
            s = jnp.where(masks[c], s, MASKED)
        m_old = m_ref[:, cols]
        m_tile = jnp.max(s, axis=0, keepdims=True)
        if adds[c] is not None:
            m_tile = m_tile + adds[c]
        m_new = jnp.maximum(m_old, m_tile)
        alpha = jnp.exp2(m_old - m_new)
        p = jnp.exp2(s - (m_new if adds[c] is None else m_new - adds[c]))
        l_ref[:, cols] = alpha * l_ref[:, cols] + jnp.sum(p, axis=0, keepdims=True)
        m_ref[:, cols] = m_new
        probs.append(p.astype(BF16))
        alphas.append(alpha)
    p_t = probs[0] if len(probs) == 1 else jnp.concatenate(probs, axis=1)
    alpha = alphas[0] if len(alphas) == 1 else jnp.concatenate(alphas, axis=1)
    acc_ref[...] = alpha * acc_ref[...] + _dot(v_t, p_t)


def _softmax_init(m_ref, l_ref, acc_ref):
    m_ref[...] = jnp.full(m_ref.shape, NEG, F32)
    l_ref[...] = jnp.zeros(l_ref.shape, F32)
    acc_ref[...] = jnp.zeros(acc_ref.shape, F32)


def _mla_prep_kernel(cq_ref, ckv_ref, kr_ref, cs_ref, gq_ref, gkv_ref, wq_ref, wuk_ref, wuvt_ref,
                     q_ref, k_ref, vt_ref):
    cs = cs_ref[...]
    qn = _rms(cq_ref[...].astype(F32), gq_ref[...]).astype(BF16)
    q = _dot(qn, wq_ref[...])
    kvn = _rms(ckv_ref[...].astype(F32), gkv_ref[...]).astype(BF16)
    k_nope = _dot(kvn, wuk_ref[...])
    t = kr_ref[...].astype(F32) * cs
    k_pe = (t + pltpu.roll(t, MLA_ROPE, 1)).astype(BF16)
    for h in range(MLA_HEADS):
        lo = h * HEAD_PAD
        q_ref[:, lo:lo + MLA_NOPE] = q[:, lo:lo + MLA_NOPE].astype(BF16)
        q_ref[:, lo + MLA_NOPE:lo + HEAD_PAD] = (q[:, lo + MLA_NOPE:lo + HEAD_PAD] * cs).astype(BF16)
        k_ref[:, lo:lo + MLA_NOPE] = k_nope[:, h * MLA_NOPE:(h + 1) * MLA_NOPE].astype(BF16)
        k_ref[:, lo + MLA_NOPE:lo + HEAD_PAD] = k_pe
    vt_ref[...] = _dot_nt(wuvt_ref[...], kvn).astype(BF16)


def _mla_prep(z, cs, gq, gkv, wq, wuk, wuv_t, layer, zoff):
    m = z.shape[0]
    tm = PREP_TM
    hq = MLA_HEADS * HEAD_PAD
    hv = MLA_HEADS * MLA_V
    const = lambda i: (layer, 0, 0)
    return pl.pallas_call(
        _mla_prep_kernel,
        grid=(m // tm,),
        in_specs=[pl.BlockSpec((tm, MLA_Q_LORA), lambda i: (i, zoff["cq"] // MLA_Q_LORA)),
                  pl.BlockSpec((tm, MLA_KV_LORA), lambda i: (i, zoff["ckv"] // MLA_KV_LORA)),
                  pl.BlockSpec((tm, LANE), lambda i: (i, zoff["kr"] // LANE)),
                  pl.BlockSpec((tm, LANE), lambda i: (i, 0)),
                  pl.BlockSpec((None, 1, MLA_Q_LORA), const),
                  pl.BlockSpec((None, 1, MLA_KV_LORA), const),
                  pl.BlockSpec((None, MLA_Q_LORA, hq), const),
                  pl.BlockSpec((None, MLA_KV_LORA, MLA_HEADS * MLA_NOPE), const),
                  pl.BlockSpec((None, hv, MLA_KV_LORA), const)],
        out_specs=[pl.BlockSpec((tm, hq), lambda i: (i, 0)),
                   pl.BlockSpec((tm, hq), lambda i: (i, 0)),
                   pl.BlockSpec((hv, tm), lambda i: (0, i))],
        out_shape=[jax.ShapeDtypeStruct((m, hq), BF16),
                   jax.ShapeDtypeStruct((m, hq), BF16),
                   jax.ShapeDtypeStruct((hv, m), BF16)],
        compiler_params=_params("parallel"),
        name="mla_prep",
    )(z, z, z, cs, gq, gkv, wq, wuk, wuv_t)


def _mla_attn_kernel(q_ref, k_ref, vt_ref, o_ref, s_ref, m_ref, l_ref, acc_ref, *, tq, tk, heads):
    i = pl.program_id(2)
    nc = tq // LANE
    _softmax_init(m_ref, l_ref, acc_ref)
    qt = i * tq + lax.broadcasted_iota(jnp.int32, (1, tq), 1)
    kcol = lax.broadcasted_iota(jnp.int32, (tk, 1), 0)
    qk = [slice(h * HEAD_PAD, (h + 1) * HEAD_PAD) for h in range(heads)]
    vd = [slice(h * MLA_V, (h + 1) * MLA_V) for h in range(heads)]

    def tile(kt, masked):
        ks = pl.multiple_of(kt * tk, tk)
        for h in range(heads):
            s_ref[h] = _dot_nt(k_ref[pl.ds(ks, tk), qk[h]], q_ref[:, qk[h]])
        masks = [None] * nc
        if masked:
            masks = [ks + kcol <= qt[:, c * LANE:(c + 1) * LANE] for c in range(nc)]
        for h in range(heads):
            _softmax_tile_t(s_ref.at[h], [None] * nc, masks, vt_ref[vd[h], pl.ds(ks, tk)],
                            m_ref.at[h], l_ref.at[h], acc_ref.at[h])

    n_full = (i * tq) // tk
    n_all = (i * tq + tq + tk - 1) // tk

    def full_body(kt, c):
        tile(kt, False)
        return c

    def diag_body(kt, c):
        tile(kt, True)
        return c

    lax.fori_loop(0, n_full, full_body, 0)
    lax.fori_loop(n_full, n_all, diag_body, 0)
    for h in range(heads):
        o_ref[:, vd[h]] = (acc_ref[h] / l_ref[h]).T.astype(o_ref.dtype)


def _mla_attn(qp, kp, vt, batch, seq):
    tq, tk, heads = MLA_TQ, MLA_TK, MLA_HEADS_PER_STEP
    nq = seq // tq
    kern = functools.partial(_mla_attn_kernel, tq=tq, tk=tk, heads=heads)
    return pl.pallas_call(
        kern,
        grid=(batch, MLA_HEADS // heads, nq),
        in_specs=[pl.BlockSpec((tq, heads * HEAD_PAD), lambda b, h, i: (b * nq + i, h)),
                  pl.BlockSpec((seq, heads * HEAD_PAD), lambda b, h, i: (b, h)),
                  pl.BlockSpec((heads * MLA_V, seq), lambda b, h, i: (h, b))],
        out_specs=pl.BlockSpec((tq, heads * MLA_V), lambda b, h, i: (b * nq + i, h)),
        out_shape=jax.ShapeDtypeStruct((batch * seq, MLA_HEADS * MLA_V), BF16),
        scratch_shapes=[pltpu.VMEM((heads, tk, tq), F32), pltpu.VMEM((heads, 1, tq), F32),
                        pltpu.VMEM((heads, 1, tq), F32), pltpu.VMEM((heads, MLA_V, tq), F32)],
        compiler_params=_params("parallel", "parallel", "parallel"),
        name="mla_attn",
    )(qp, kp, vt)


def _compress(a_ref, pe_ref, w1_ref, w2_ref):
    a = a_ref[...].astype(F32)
    half = a.shape[1]
    first = _dot((a + pe_ref[0:1, :]).astype(BF16), w1_ref[0:half, :])
    second = _dot((a + pe_ref[1:2, :]).astype(BF16), w1_ref[half:2 * half, :])
    n = a.shape[0]
    hidden = first + pltpu.roll(second, n - 1, 0)
    return _dot((hidden * jax.nn.sigmoid(hidden)).astype(BF16), w2_ref[...])


def _nsa_cmp_kernel(ak_ref, av_ref, pek_ref, pev_ref, w1k_ref, w2k_ref, w1v_ref, w2v_ref,
                    kc_ref, vct_ref):
    kc_ref[...] = _compress(ak_ref, pek_ref, w1k_ref, w2k_ref).astype(BF16)
    vct_ref[...] = _compress(av_ref, pev_ref, w1v_ref, w2v_ref).T.astype(BF16)


def _nsa_cmp(ak, av, pek, pev, w1k, w2k, w1v, w2v, layer):
    bg, nh, dk16 = ak.shape
    dv16 = av.shape[-1]
    const = lambda i: (layer, 0, 0)
    return pl.pallas_call(
        _nsa_cmp_kernel,
        grid=(bg,),
        in_specs=[pl.BlockSpec((None, nh, dk16), lambda i: (i, 0, 0)),
                  pl.BlockSpec((None, nh, dv16), lambda i: (i, 0, 0)),
                  pl.BlockSpec((None, 2, dk16), const),
                  pl.BlockSpec((None, 2, dv16), const),
                  pl.BlockSpec((None, 2 * dk16, CMP_HIDDEN), const),
                  pl.BlockSpec((None, CMP_HIDDEN, HEAD_PAD), const),
                  pl.BlockSpec((None, 2 * dv16, CMP_HIDDEN), const),
                  pl.BlockSpec((None, CMP_HIDDEN, NSA_DV), const)],
        out_specs=[pl.BlockSpec((None, nh, HEAD_PAD), lambda i: (i, 0, 0)),
                   pl.BlockSpec((None, NSA_DV, nh), lambda i: (i, 0, 0))],
        out_shape=[jax.ShapeDtypeStruct((bg, nh, HEAD_PAD), BF16),
                   jax.ShapeDtypeStruct((bg, NSA_DV, nh), BF16)],
        compiler_params=_params("parallel"),
        name="nsa_cmp",
    )(ak, av, pek, pev, w1k, w2k, w1v, w2v)


def _split3(x):
    a = x.astype(BF16)
    r = x - a.astype(F32)
    b = r.astype(BF16)
    c = (r - b.astype(F32)).astype(BF16)
    return a, b, c


def _lookup(table_row, idx):
    rows, width = idx.shape
    table = jnp.broadcast_to(table_row, (rows, LANE))
    chunks = [jnp.take_along_axis(table, idx[:, c:c + LANE], axis=1, mode="promise_in_bounds")
              for c in range(0, width, LANE)]
    return chunks[0] if len(chunks) == 1 else jnp.concatenate(chunks, axis=1)


def _nsa_attn_kernel(far_ref, q_ref, kc_ref, vct_ref, ks_ref, vst_ref, kw_ref, vwt_ref, ng_ref,
                     prow_ref, pcol_ref, pcmp_ref, tbl_ref, o_ref,
                     s_ref, m_ref, l_ref, acc_ref, *, tq, tk, seq):
    b, i = pl.program_id(0), pl.program_id(2)
    nq, nk = seq // tq, seq // tk
    J = NSA_GROUP
    n_slc = seq // SLC_LEN
    n_cmp = (seq - CMP_LEN) // CMP_STRIDE + 1
    t0 = pl.multiple_of(i * tq, tq)

    qs = jnp.concatenate([q_ref[:, j * HEAD_PAD:(j + 1) * HEAD_PAD] for j in range(J)], axis=0)
    qpos = prow_ref[:, pl.ds(t0, tq)]
    qt = t0 + lax.broadcasted_iota(jnp.int32, (1, tq), 1)
    tbl = tbl_ref[...]
    far_bias = [tbl[j:j + 1, LANE - 1:LANE] for j in range(J)]
    head = [slice(j * tq, (j + 1) * tq) for j in range(J)]

    blk = lax.broadcasted_iota(jnp.int32, (LANE, 1), 0)
    keep_c = (blk * CMP_STRIDE + (CMP_LEN - 1) <= qt) & (blk < n_cmp)
    idx_c = jnp.clip(qpos - pcmp_ref[...], 0, LANE - 1)
    s_c = _dot_nt(kc_ref[...], qs)
    p_heads = []
    for j in range(J):
        s = jnp.where(keep_c, s_c[:, head[j]] + _lookup(tbl[j:j + 1, :], idx_c), NEG)
        e = jnp.where(keep_c, jnp.exp2(s - jnp.max(s, axis=0, keepdims=True)), 0.0)
        den = jnp.sum(e, axis=0, keepdims=True)
        p_heads.append(e / jnp.where(den > 0.0, den, 1.0))
    o_cmp = _dot(vct_ref[...], jnp.concatenate([p.astype(BF16) for p in p_heads], axis=1))

    rows = 32
    assert n_slc <= rows
    m_row = lax.broadcasted_iota(jnp.int32, (rows, LANE), 0)
    n_col = lax.broadcasted_iota(jnp.int32, (rows, LANE), 1)
    per = SLC_LEN // CMP_STRIDE
    back = (CMP_LEN - 1) // CMP_STRIDE
    overlap = ((n_col >= per * m_row - back) & (n_col <= per * m_row + per - 1)
               & (n_col < n_cmp) & (m_row < n_slc)).astype(BF16)
    imp = sum(_dot(overlap, part) for part in _split3(sum(p_heads)))
    m_blk = lax.broadcasted_iota(jnp.int32, (rows, 1), 0)
    cur = qt // SLC_LEN
    valid = m_blk <= cur
    forced = valid & ((m_blk == 0) | (m_blk >= cur - 1))
    score = jnp.where(forced, FORCED_SCORE, jnp.where(valid, imp, -1.0))
    rank = jnp.zeros((rows, tq), jnp.int32)
    for mp in range(n_slc):
        other = score[mp:mp + 1, :]
        ahead = (other > score) | ((other == score) & (m_blk > mp))
        rank = rank + ahead.astype(jnp.int32)
    sel = (valid & (rank < min(SLC_TOPN, n_slc))).astype(BF16)

    SLC, WIN = 0, 1
    _softmax_init(m_ref, l_ref, acc_ref)
    sub = tk // LANE

    qchunks = [slice(c, c + LANE) for c in range(0, tq, LANE)]

    def slc_keep(kidx):
        expand = (kidx // SLC_LEN == lax.broadcasted_iota(jnp.int32, (1, rows), 1)).astype(BF16)
        hit = _dot(expand, sel)
        return [(hit[:, c] > 0.5) & (kidx <= qt[:, c]) for c in qchunks]

    def win_keep(kidx):
        diffs = [qt[:, c] - kidx for c in qchunks]
        return [(d >= 0) & (d < WINDOW) for d in diffs]

    def tile(kt, slots):
        ks = pl.multiple_of(kt * tk, tk)
        k_refs, vt_refs, keeps = (ks_ref, kw_ref), (vst_ref, vwt_ref), (slc_keep, win_keep)
        for slot in slots:
            s_ref[slot] = _dot_nt(k_refs[slot][pl.ds(ks, tk), :], qs)
        for hb in range(sub):
            kb = kt * sub + hb
            near = (far_ref[(b * nq + i) * (nk * sub) + kb] == 0) & (kb * LANE < t0 + tq)

            @pl.when(near)
            def _():
                kpos = pcol_ref[pl.ds(pl.multiple_of(ks + hb * LANE, LANE), LANE), :]
                idx = jnp.clip(qpos - kpos, 0, LANE - 1)
                for j in range(J):
                    delta = _lookup(tbl[j:j + 1, :] - far_bias[j], idx)
                    for slot in slots:
                        s_ref[slot, hb * LANE:(hb + 1) * LANE, head[j]] += delta

        kidx = ks + lax.broadcasted_iota(jnp.int32, (tk, 1), 0)
        adds = [far_bias[j] for j in range(J) for _ in qchunks]
        for slot in slots:
            _softmax_tile_t(s_ref.at[slot], adds, keeps[slot](kidx) * J,
                            vt_refs[slot][:, pl.ds(ks, tk)],
                            m_ref.at[slot], l_ref.at[slot], acc_ref.at[slot])

    def slc_body(kt, c):
        tile(kt, (SLC,))
        return c

    def both_body(kt, c):
        tile(kt, (SLC, WIN))
        return c

    last = (t0 + tq + tk - 1) // tk
    first_win = jnp.maximum(t0 - (WINDOW - 1), 0) // tk
    lax.fori_loop(0, first_win, slc_body, 0)
    lax.fori_loop(first_win, last, both_body, 0)
    o_slc = acc_ref[SLC] / l_ref[SLC]
    o_win = acc_ref[WIN] / l_ref[WIN]

    gates = jax.nn.sigmoid(ng_ref[...].astype(F32)).T
    for j in range(J):
        o = (gates[3 * j:3 * j + 1, :] * o_cmp[:, head[j]]
             + gates[3 * j + 1:3 * j + 2, :] * o_slc[:, head[j]]
             + gates[3 * j + 2:3 * j + 3, :] * o_win[:, head[j]])
        o_ref[:, j * NSA_DV:(j + 1) * NSA_DV] = o.T.astype(o_ref.dtype)


def _nsa_attn(far, z, kc, vct, v_t, pos_row, pos_col, pos_cmp, tbl, batch, seq, zoff):
    tq, tk = NSA_TQ, NSA_TK
    nq = seq // tq
    G, J = NSA_KV_HEADS, NSA_GROUP
    R = J * tq
    kern = functools.partial(_nsa_attn_kernel, tq=tq, tk=tk, seq=seq)
    grid_spec = pltpu.PrefetchScalarGridSpec(
        num_scalar_prefetch=1,
        grid=(batch, G, nq),
        in_specs=[
            pl.BlockSpec((tq, J * HEAD_PAD), lambda b, g, i, far: (b * nq + i, g)),
            pl.BlockSpec((None, LANE, HEAD_PAD), lambda b, g, i, far: (b * G + g, 0, 0)),
            pl.BlockSpec((None, NSA_DV, LANE), lambda b, g, i, far: (b * G + g, 0, 0)),
            pl.BlockSpec((seq, HEAD_PAD), lambda b, g, i, far: (b, zoff["ks"] // HEAD_PAD + g)),
            pl.BlockSpec((NSA_DV, seq), lambda b, g, i, far: (g, b)),
            pl.BlockSpec((seq, HEAD_PAD), lambda b, g, i, far: (b, zoff["kw"] // HEAD_PAD + g)),
            pl.BlockSpec((NSA_DV, seq), lambda b, g, i, far: (G + g, b)),
            pl.BlockSpec((tq, LANE), lambda b, g, i, far: (b * nq + i, zoff["ng"] // LANE + g)),
            pl.BlockSpec((None, 1, seq), lambda b, g, i, far: (b, 0, 0)),
            pl.BlockSpec((seq, 1), lambda b, g, i, far: (b, 0)),
            pl.BlockSpec((LANE, 1), lambda b, g, i, far: (b, 0)),
            pl.BlockSpec((None, J, LANE), lambda b, g, i, far: (g, 0, 0)),
        ],
        out_specs=pl.BlockSpec((tq, J * NSA_DV), lambda b, g, i, far: (b * nq + i, g)),
        scratch_shapes=[pltpu.VMEM((2, tk, R), F32), pltpu.VMEM((2, 1, R), F32),
                        pltpu.VMEM((2, 1, R), F32), pltpu.VMEM((2, NSA_DV, R), F32)],
    )
    return pl.pallas_call(
        kern,
        grid_spec=grid_spec,
        out_shape=jax.ShapeDtypeStruct((batch * seq, NSA_HEADS * NSA_DV), BF16),
        compiler_params=_params("parallel", "parallel", "parallel"),
        name="nsa_attn",
    )(far, z, kc, vct, z, v_t, z, v_t, z, pos_row, pos_col, pos_cmp, tbl)


def _merge_kernel(a_ref, b_ref, ga_ref, gb_ref, h_ref, wa_ref, wb_ref, wo_ref, post_g_ref,
                  next_g_ref, o_ref, un_ref):
    ya = _dot(a_ref[...], wa_ref[...])
    yb = _dot(b_ref[...], wb_ref[...])
    m = (jax.nn.sigmoid(ga_ref[...].astype(F32)) * ya
         + jax.nn.sigmoid(gb_ref[...].astype(F32)) * yb).astype(BF16)
    y = _dot(m, wo_ref[...])
    out = h_ref[...] + _rms(y, post_g_ref[...])
    o_ref[...] = out
    un_ref[...] = _rms(out, next_g_ref[...]).astype(BF16)


def _merge(a, bb, z, h, wa, wb, wo, post_g, next_g, layer, zoff):
    m, d = h.shape
    tm = MERGE_TM
    da, db = a.shape[1], bb.shape[1]
    gblk = zoff["mg"] // d
    const = lambda i: (layer, 0, 0)
    return pl.pallas_call(
        _merge_kernel,
        grid=(m // tm,),
        in_specs=[pl.BlockSpec((tm, da), lambda i: (i, 0)),
                  pl.BlockSpec((tm, db), lambda i: (i, 0)),
                  pl.BlockSpec((tm, d), lambda i: (i, gblk)),
                  pl.BlockSpec((tm, d), lambda i: (i, gblk + 1)),
                  pl.BlockSpec((tm, d), lambda i: (i, 0)),
                  pl.BlockSpec((None, da, d), const),
                  pl.BlockSpec((None, db, d), const),
                  pl.BlockSpec((None, d, d), const),
                  pl.BlockSpec((None, 1, d), const),
                  pl.BlockSpec((None, 1, d), const)],
        out_specs=[pl.BlockSpec((tm, d), lambda i: (i, 0)),
                   pl.BlockSpec((tm, d), lambda i: (i, 0))],
        out_shape=[jax.ShapeDtypeStruct((m, d), F32), jax.ShapeDtypeStruct((m, d), BF16)],
        compiler_params=_params("parallel"),
        name="merge_out",
    )(a, bb, z, z, h, wa, wb, wo, post_g, next_g)


def _z_layout(d_model):
    G = NSA_KV_HEADS
    widths = [("q", NSA_HEADS * HEAD_PAD), ("mg", 2 * d_model), ("cq", MLA_Q_LORA),
              ("ckv", MLA_KV_LORA), ("kc", G * HEAD_PAD), ("ks", G * HEAD_PAD),
              ("kw", G * HEAD_PAD), ("vc", G * NSA_DV), ("kr", LANE), ("ng", G * LANE)]
    off, pos = {}, 0
    for name, w in widths:
        off[name] = pos
        pos += w
    off["used"] = pos
    off["total"] = -(-pos // Z_PAD) * Z_PAD
    assert off["q"] == 0 and off["mg"] % d_model == 0
    assert off["cq"] % MLA_Q_LORA == 0 and off["ckv"] % MLA_KV_LORA == 0
    assert all(off[k] % HEAD_PAD == 0 for k in ("kc", "ks", "kw"))
    return off


def _swap_halves(w):
    half = w.shape[-1] // 2
    return jnp.concatenate([-w[..., half:], w[..., :half]], axis=-1)


def _pad_last(w, width):
    return jnp.pad(w, [(0, 0)] * (w.ndim - 1) + [(0, width - w.shape[-1])])


def _layout_w_in(w_in, zoff):
    L, D, _ = w_in.shape
    G, J = NSA_KV_HEADS, NSA_GROUP
    splits = [MLA_Q_LORA, MLA_KV_LORA, MLA_ROPE, NSA_HEADS * NSA_DK,
              G * NSA_DK, G * NSA_DV, G * NSA_DK, G * NSA_DV, G * NSA_DK, G * NSA_DV,
              NSA_HEADS * 3, 2 * D]
    bounds = [0] + [int(v) for v in np.cumsum(splits)]
    w_bf = w_in.astype(BF16)
    (c_q, c_kv, k_rope, _, k_c, v_c, k_s, v_s, k_w, v_w, nsa_g, merge_g) = [
        w_bf[..., lo:hi] for lo, hi in zip(bounds[:-1], bounds[1:])]
    nsa_q = w_in[..., bounds[3]:bounds[4]] * (NSA_DK ** -0.5 * LOG2E)

    def heads(w, n):
        return _pad_last(w.reshape(L, D, n, NSA_DK), HEAD_PAD).reshape(L, D, n * HEAD_PAD).astype(BF16)

    gates = _pad_last(nsa_g.reshape(L, D, G, J * 3), LANE).reshape(L, D, G * LANE)
    cols = [heads(nsa_q, NSA_HEADS), merge_g, c_q, c_kv, heads(k_c, G),
            heads(k_s, G), heads(k_w, G), v_c, k_rope, _swap_halves(k_rope), gates,
            jnp.zeros((L, D, zoff["total"] - zoff["used"]), BF16)]
    w_z = jnp.concatenate([c.astype(BF16) for c in cols], axis=-1)
    w_vt = jnp.concatenate([v_s, v_w], axis=-1).astype(BF16).transpose(0, 2, 1)
    return w_z, w_vt


def _layout_w_q_up(w):
    L, r, _ = w.shape
    w = w.reshape(L, r, MLA_HEADS, MLA_NOPE + MLA_ROPE) * ((MLA_NOPE + MLA_ROPE) ** -0.5 * LOG2E)
    rope = w[..., MLA_NOPE:]
    return jnp.concatenate([w[..., :MLA_NOPE], rope, _swap_halves(rope)], axis=-1).reshape(
        L, r, MLA_HEADS * HEAD_PAD).astype(BF16)


def kernel(x, positions, rel_bias, ffn1_pre_g, ffn1_post_g, ffn1_w_gate, ffn1_w_up, ffn1_w_down, mix_pre_g, mix_post_g, w_in, mla_q_norm_g, mla_w_q_up, mla_kv_norm_g, mla_w_uk, mla_w_uv, cmp_pe_k, cmp_w1_k, cmp_w2_k, cmp_pe_v, cmp_w1_v, cmp_w2_v, w_branch_mla, w_branch_nsa, w_out, ffn2_pre_g, ffn2_post_g, ffn2_w_gate, ffn2_w_up, ffn2_w_down):
    B, S, D = x.shape
    L = w_in.shape[0]
    M = B * S
    G, J = NSA_KV_HEADS, NSA_GROUP
    zoff = _z_layout(D)
    n_half = S // CMP_STRIDE
    n_cmp = (S - CMP_LEN) // CMP_STRIDE + 1
    assert n_half == LANE and n_cmp <= LANE

    gain = lambda g: g.reshape(L, 1, -1)
    bf = lambda w: w.astype(BF16)
    w_z, w_vt = _layout_w_in(w_in, zoff)
    w_q = _layout_w_q_up(mla_w_q_up)
    w_uk, w_uv_t = bf(mla_w_uk), bf(mla_w_uv).transpose(0, 2, 1)
    pe_k = cmp_pe_k.reshape(L, 2, CMP_STRIDE * NSA_DK)
    pe_v = cmp_pe_v.reshape(L, 2, CMP_STRIDE * NSA_DV)
    w2_k = bf(_pad_last(cmp_w2_k, HEAD_PAD))
    f1 = (bf(ffn1_w_gate), bf(ffn1_w_up), bf(ffn1_w_down))
    f2 = (bf(ffn2_w_gate), bf(ffn2_w_up), bf(ffn2_w_down))
    w1_k, w1_v, w2_v = bf(cmp_w1_k), bf(cmp_w1_v), bf(cmp_w2_v)
    w_a, w_b, w_o = bf(w_branch_mla), bf(w_branch_nsa), bf(w_out)
    g_f1pre, g_f1post, g_mpre, g_mpost = gain(ffn1_pre_g), gain(ffn1_post_g), gain(mix_pre_g), gain(mix_post_g)
    g_f2pre, g_f2post, g_q, g_kv = gain(ffn2_pre_g), gain(ffn2_post_g), gain(mla_q_norm_g), gain(mla_kv_norm_g)

    pos_col = positions.reshape(M, 1)
    pos_row = positions.reshape(B, 1, S)
    pos_cmp = _pad_last(positions[:, CMP_LEN - 1::CMP_STRIDE][:, :n_cmp], LANE).reshape(B * LANE, 1)
    q_min = positions.reshape(B, S // NSA_TQ, NSA_TQ).min(axis=-1)
    k_max = positions.reshape(B, S // LANE, LANE).max(axis=-1)
    far = (q_min[:, :, None] - k_max[:, None, :] >= FAR_DIST).astype(jnp.int32).reshape(-1)
    tbl = (jnp.take(rel_bias, jnp.asarray(BUCKET_OF_DIST), axis=0).T * LOG2E).reshape(G, J, LANE)
    half = MLA_ROPE // 2
    inv = ROPE_BASE ** (-jnp.arange(half, dtype=F32) * 2.0 / MLA_ROPE)
    cs = _rope_table(pos_col, jnp.concatenate([inv, inv]).reshape(1, MLA_ROPE))

    h = x.reshape(M, D)
    u = _rmsnorm(h, g_f1pre, 0)
    for l in range(L):
        h, u = _ffn(h, u, g_f1post, g_mpre, l, *f1, l)
        z = _in_proj(u, w_z, l)
        v_t = _in_proj_t(u, w_vt, l)
        qp, kp, vt = _mla_prep(z, cs, g_q, g_kv, w_q, w_uk, w_uv_t, l, zoff)
        a = _mla_attn(qp, kp, vt, B, S)

        def half_blocks(off, width, d):
            t = z[:, off:off + G * width].reshape(B, S, G, width)[..., :d]
            return t.transpose(0, 2, 1, 3).reshape(B * G, n_half, CMP_STRIDE * d)

        kc, vct = _nsa_cmp(half_blocks(zoff["kc"], HEAD_PAD, NSA_DK),
                           half_blocks(zoff["vc"], NSA_DV, NSA_DV),
                           pe_k, pe_v, w1_k, w2_k, w1_v, w2_v, l)
        nsa = _nsa_attn(far, z, kc, vct, v_t, pos_row, pos_col, pos_cmp, tbl, B, S, zoff)
        h, u = _merge(a, nsa, z, h, w_a, w_b, w_o, g_mpost, g_f2pre, l, zoff)
        h, u = _ffn(h, u, g_f2post, g_f1pre, min(l + 1, L - 1), *f2, l)
    return h.reshape(B, S, D)
```

```python
import functools
import math

import numpy as np
import jax
import jax.numpy as jnp
from jax import lax
from jax.experimental import pallas as pl
from jax.experimental.pallas import tpu as pltpu

EPS = 1e-6
MLA_HEADS = 8
MLA_Q_LORA = 512
MLA_KV_LORA = 512
MLA_NOPE = 128
MLA_ROPE = 64
MLA_V = 128
ROPE_BASE = 10000.0
NSA_HEADS = 8
NSA_KV_HEADS = 2
NSA_GROUP = NSA_HEADS // NSA_KV_HEADS
NSA_DK = 192
NSA_DV = 128
CMP_LEN = 32
CMP_STRIDE = 16
CMP_HIDDEN = 256
SLC_LEN = 64
SLC_TOPN = 16
WINDOW = 512
FORCED_SCORE = 1e6
REL_BUCKETS = 32
REL_MAX_DIST = 128
NEG = -1e30
MASKED = 2 * NEG
LOG2E = 1.0 / math.log(2.0)

LANE = 128
ONES_ROWS = 16
HEAD_PAD = 256
VMEM_LIMIT = 56 * 1024 * 1024
BF16 = jnp.bfloat16
F32 = jnp.float32

FFN_TM, FFN_TF = 512, 512
PROJ_TM = 1024
PROJ_TN_CAP = 2560
Z_PAD = 5 * LANE
PREP_TM = 512
MLA_TQ, MLA_TK = 512, 256
MLA_HEADS_PER_STEP = 4
NSA_TQ, NSA_TK = 256, 256
MERGE_TM = 512


def _bucket_of_distance():
    n = np.arange(LANE)
    max_exact = REL_BUCKETS // 2
    large = max_exact + (np.log(np.maximum(n, 1) / max_exact) / math.log(REL_MAX_DIST / max_exact)
                         * (REL_BUCKETS - max_exact)).astype(np.int32)
    bucket = np.where(n < max_exact, n, np.minimum(large, REL_BUCKETS - 1)).astype(np.int32)
    assert bucket[-1] == REL_BUCKETS - 1
    return bucket


BUCKET_OF_DIST = _bucket_of_distance()
FAR_DIST = int(np.max(np.nonzero(BUCKET_OF_DIST != REL_BUCKETS - 1)[0])) + 1


def _params(*sem):
    return pltpu.CompilerParams(dimension_semantics=sem, vmem_limit_bytes=VMEM_LIMIT)


def _rms(x, g):
    return x * lax.rsqrt(jnp.mean(x * x, axis=-1, keepdims=True) + EPS) * g


def _dot(a, b):
    return jnp.dot(a, b, preferred_element_type=F32)


def _dot_nt(a, b):
    return lax.dot_general(a, b, (((1,), (1,)), ((), ())), preferred_element_type=F32)


def _tile_n(n, cap):
    best = LANE
    for t in range(LANE, cap + 1, LANE):
        if n % t == 0:
            best = t
    return best


def _rmsnorm_kernel(x_ref, g_ref, o_ref):
    o_ref[...] = _rms(x_ref[...], g_ref[...]).astype(o_ref.dtype)


def _rmsnorm(x, g, layer):
    m, d = x.shape
    tm = 512
    return pl.pallas_call(
        _rmsnorm_kernel,
        grid=(m // tm,),
        in_specs=[pl.BlockSpec((tm, d), lambda i: (i, 0)),
                  pl.BlockSpec((None, 1, d), lambda i: (layer, 0, 0))],
        out_specs=pl.BlockSpec((tm, d), lambda i: (i, 0)),
        out_shape=jax.ShapeDtypeStruct((m, d), BF16),
        compiler_params=_params("parallel"),
        name="rmsnorm",
    )(x, g)


def _rope_table_kernel(pos_ref, inv_ref, o_ref):
    ang = pos_ref[...].astype(F32) * inv_ref[...]
    o_ref[...] = jnp.concatenate([jnp.cos(ang), jnp.sin(ang)], axis=1)


def _rope_table(pos_col, inv):
    m = pos_col.shape[0]
    tm = 512
    return pl.pallas_call(
        _rope_table_kernel,
        grid=(m // tm,),
        in_specs=[pl.BlockSpec((tm, 1), lambda i: (i, 0)),
                  pl.BlockSpec((1, MLA_ROPE), lambda i: (0, 0))],
        out_specs=pl.BlockSpec((tm, 2 * MLA_ROPE), lambda i: (i, 0)),
        out_shape=jax.ShapeDtypeStruct((m, 2 * MLA_ROPE), F32),
        compiler_params=_params("parallel"),
        name="rope_table",
    )(pos_col, inv)


def _ffn_kernel(x_ref, u_ref, post_g_ref, next_g_ref, wg_ref, wu_ref, wd_ref,
                o_ref, un_ref, acc_ref):
    j = pl.program_id(1)

    @pl.when(j == 0)
    def _():
        acc_ref[...] = jnp.zeros(acc_ref.shape, F32)

    u = u_ref[...]
    gate = _dot(u, wg_ref[...])
    up = _dot(u, wu_ref[...])
    hidden = (gate * jax.nn.sigmoid(gate) * up).astype(BF16)
    acc_ref[...] += _dot(hidden, wd_ref[...])

    @pl.when(j == pl.num_programs(1) - 1)
    def _():
        out = x_ref[...] + 0.5 * _rms(acc_ref[...], post_g_ref[...])
        o_ref[...] = out
        un_ref[...] = _rms(out, next_g_ref[...]).astype(BF16)


def _ffn(x, u, post_g, next_g, next_layer, wg, wu, wd, layer):
    m, d = x.shape
    f = wg.shape[-1]
    tm, tf = FFN_TM, _tile_n(f, FFN_TF)
    return pl.pallas_call(
        _ffn_kernel,
        grid=(m // tm, f // tf),
        in_specs=[pl.BlockSpec((tm, d), lambda i, j: (i, 0)),
                  pl.BlockSpec((tm, d), lambda i, j: (i, 0)),
                  pl.BlockSpec((None, 1, d), lambda i, j: (layer, 0, 0)),
                  pl.BlockSpec((None, 1, d), lambda i, j: (next_layer, 0, 0)),
                  pl.BlockSpec((None, d, tf), lambda i, j: (layer, 0, j)),
                  pl.BlockSpec((None, d, tf), lambda i, j: (layer, 0, j)),
                  pl.BlockSpec((None, tf, d), lambda i, j: (layer, j, 0))],
        out_specs=[pl.BlockSpec((tm, d), lambda i, j: (i, 0)),
                   pl.BlockSpec((tm, d), lambda i, j: (i, 0))],
        out_shape=[jax.ShapeDtypeStruct((m, d), F32), jax.ShapeDtypeStruct((m, d), BF16)],
        scratch_shapes=[pltpu.VMEM((tm, d), F32)],
        compiler_params=_params("parallel", "arbitrary"),
        name="ffn",
    )(x, u, post_g, next_g, wg, wu, wd)


def _matmul_kernel(x_ref, w_ref, o_ref):
    o_ref[...] = _dot(x_ref[...], w_ref[...]).astype(o_ref.dtype)


def _in_proj(u, w, layer):
    m, d = u.shape
    n = w.shape[-1]
    tm, tn = PROJ_TM, _tile_n(n, PROJ_TN_CAP)
    return pl.pallas_call(
        _matmul_kernel,
        grid=(n // tn, m // tm),
        in_specs=[pl.BlockSpec((tm, d), lambda j, i: (i, 0)),
                  pl.BlockSpec((None, d, tn), lambda j, i: (layer, 0, j))],
        out_specs=pl.BlockSpec((tm, tn), lambda j, i: (i, j)),
        out_shape=jax.ShapeDtypeStruct((m, n), BF16),
        compiler_params=_params("parallel", "parallel"),
        name="in_proj",
    )(u, w)


def _matmul_t_kernel(w_ref, x_ref, o_ref):
    o_ref[...] = _dot_nt(w_ref[...], x_ref[...]).astype(o_ref.dtype)


def _in_proj_t(u, w_t, layer):
    m, d = u.shape
    n = w_t.shape[1]
    tm = PROJ_TM
    return pl.pallas_call(
        _matmul_t_kernel,
        grid=(m // tm,),
        in_specs=[pl.BlockSpec((None, n, d), lambda i: (layer, 0, 0)),
                  pl.BlockSpec((tm, d), lambda i: (i, 0))],
        out_specs=pl.BlockSpec((n, tm), lambda i: (0, i)),
        out_shape=jax.ShapeDtypeStruct((n, m), BF16),
        compiler_params=_params("parallel"),
        name="in_proj_t",
    )(w_t, u)


def _softmax_tile_t(s_ref, adds, masks, v_t, m_ref, acc_ref):
    probs, alphas = [], []
    for c in range(s_ref.shape[1] // LANE):
        cols = slice(c * LANE, (c + 1) * LANE)
        s = s_ref[:, cols]
        if masks[c] is not None:
            s = jnp.where(masks[c], s, MASKED)
        m_old = m_ref[:, cols]
        m_tile = jnp.max(s, axis=0, keepdims=True)
        if adds[c] is not None:
            m_tile = m_tile + adds[c]
        m_new = jnp.maximum(m_old, m_tile)
        alpha = jnp.exp2(m_old - m_new)
        p = jnp.exp2(s - (m_new if adds[c] is None else m_new - adds[c]))
        m_ref[:, cols] = m_new
        probs.append(p.astype(BF16))
        alphas.append(alpha)
    p_t = probs[0] if len(probs) == 1 else jnp.concatenate(probs, axis=1)
    alpha = alphas[0] if len(alphas) == 1 else jnp.concatenate(alphas, axis=1)
    v_ones = jnp.concatenate([v_t, jnp.ones((ONES_ROWS, v_t.shape[1]), BF16)], axis=0)
    acc_ref[...] = alpha * acc_ref[...] + _dot(v_ones, p_t)


def _softmax_init(m_ref, acc_ref):
    m_ref[...] = jnp.full(m_ref.shape, NEG, F32)
    acc_ref[...] = jnp.zeros(acc_ref.shape, F32)


def _softmax_out(acc):
    dv = acc.shape[0] - ONES_ROWS
    return acc[:dv] / acc[dv:dv + 1]


def _mla_prep_kernel(cq_ref, ckv_ref, kr_ref, cs_ref, gq_ref, gkv_ref, wq_ref, wuk_ref, wuvt_ref,
                     q_ref, k_ref, vt_ref):
    cs = cs_ref[...]
    qn = _rms(cq_ref[...].astype(F32), gq_ref[...]).astype(BF16)
    q = _dot(qn, wq_ref[...])
    kvn = _rms(ckv_ref[...].astype(F32), gkv_ref[...]).astype(BF16)
    k_nope = _dot(kvn, wuk_ref[...])
    t = kr_ref[...].astype(F32) * cs
    k_pe = (t + pltpu.roll(t, MLA_ROPE, 1)).astype(BF16)
    for h in range(MLA_HEADS):
        lo = h * HEAD_PAD
        q_ref[:, lo:lo + MLA_NOPE] = q[:, lo:lo + MLA_NOPE].astype(BF16)
        q_ref[:, lo + MLA_NOPE:lo + HEAD_PAD] = (q[:, lo + MLA_NOPE:lo + HEAD_PAD] * cs).astype(BF16)
        k_ref[:, lo:lo + MLA_NOPE] = k_nope[:, h * MLA_NOPE:(h + 1) * MLA_NOPE].astype(BF16)
        k_ref[:, lo + MLA_NOPE:lo + HEAD_PAD] = k_pe
    vt_ref[...] = _dot_nt(wuvt_ref[...], kvn).astype(BF16)


def _mla_prep(z, cs, gq, gkv, wq, wuk, wuv_t, layer, zoff):
    m = z.shape[0]
    tm = PREP_TM
    hq = MLA_HEADS * HEAD_PAD
    hv = MLA_HEADS * MLA_V
    const = lambda i: (layer, 0, 0)
    return pl.pallas_call(
        _mla_prep_kernel,
        grid=(m // tm,),
        in_specs=[pl.BlockSpec((tm, MLA_Q_LORA), lambda i: (i, zoff["cq"] // MLA_Q_LORA)),
                  pl.BlockSpec((tm, MLA_KV_LORA), lambda i: (i, zoff["ckv"] // MLA_KV_LORA)),
                  pl.BlockSpec((tm, LANE), lambda i: (i, zoff["kr"] // LANE)),
                  pl.BlockSpec((tm, LANE), lambda i: (i, 0)),
                  pl.BlockSpec((None, 1, MLA_Q_LORA), const),
                  pl.BlockSpec((None, 1, MLA_KV_LORA), const),
                  pl.BlockSpec((None, MLA_Q_LORA, hq), const),
                  pl.BlockSpec((None, MLA_KV_LORA, MLA_HEADS * MLA_NOPE), const),
                  pl.BlockSpec((None, hv, MLA_KV_LORA), const)],
        out_specs=[pl.BlockSpec((tm, hq), lambda i: (i, 0)),
                   pl.BlockSpec((tm, hq), lambda i: (i, 0)),
                   pl.BlockSpec((hv, tm), lambda i: (0, i))],
        out_shape=[jax.ShapeDtypeStruct((m, hq), BF16),
                   jax.ShapeDtypeStruct((m, hq), BF16),
                   jax.ShapeDtypeStruct((hv, m), BF16)],
        compiler_params=_params("parallel"),
        name="mla_prep",
    )(z, z, z, cs, gq, gkv, wq, wuk, wuv_t)


def _by_parity(kt, fn):
    @pl.when(kt % 2 == 0)
    def _():
        fn(0)

    @pl.when(kt % 2 == 1)
    def _():
        fn(1)


def _mla_attn_kernel(q_ref, k_ref, vt_ref, o_ref, s_ref, m_ref, acc_ref, *, tq, tk, heads):
    i = pl.program_id(2)
    nc = tq // LANE
    _softmax_init(m_ref, acc_ref)
    qt = i * tq + lax.broadcasted_iota(jnp.int32, (1, tq), 1)
    kcol = lax.broadcasted_iota(jnp.int32, (tk, 1), 0)
    qk = [slice(h * HEAD_PAD, (h + 1) * HEAD_PAD) for h in range(heads)]
    vd = [slice(h * MLA_V, (h + 1) * MLA_V) for h in range(heads)]
    n_full = (i * tq) // tk
    n_all = (i * tq + tq + tk - 1) // tk

    def logits(kt, buf):
        ks = pl.multiple_of(kt * tk, tk)
        for h in range(heads):
            s_ref[buf, h] = _dot_nt(k_ref[pl.ds(ks, tk), qk[h]], q_ref[:, qk[h]])

    def step(kt, buf, masked):
        ks = pl.multiple_of(kt * tk, tk)
        masks = [None] * nc
        if masked:
            masks = [ks + kcol <= qt[:, c * LANE:(c + 1) * LANE] for c in range(nc)]
        for h in range(heads):
            _softmax_tile_t(s_ref.at[buf, h], [None] * nc, masks, vt_ref[vd[h], pl.ds(ks, tk)],
                            m_ref.at[h], acc_ref.at[h])
        logits(jnp.minimum(kt + 1, n_all - 1), 1 - buf)

    def full_body(kt, c):
        _by_parity(kt, lambda buf: step(kt, buf, False))
        return c

    def diag_body(kt, c):
        _by_parity(kt, lambda buf: step(kt, buf, True))
        return c

    logits(0, 0)
    lax.fori_loop(0, n_full, full_body, 0)
    lax.fori_loop(n_full, n_all, diag_body, 0)
    for h in range(heads):
        o_ref[:, vd[h]] = _softmax_out(acc_ref[h]).T.astype(o_ref.dtype)


def _mla_attn(qp, kp, vt, batch, seq):
    tq, tk, heads = MLA_TQ, MLA_TK, MLA_HEADS_PER_STEP
    nq = seq // tq
    kern = functools.partial(_mla_attn_kernel, tq=tq, tk=tk, heads=heads)
    return pl.pallas_call(
        kern,
        grid=(batch, MLA_HEADS // heads, nq),
        in_specs=[pl.BlockSpec((tq, heads * HEAD_PAD), lambda b, h, i: (b * nq + i, h)),
                  pl.BlockSpec((seq, heads * HEAD_PAD), lambda b, h, i: (b, h)),
                  pl.BlockSpec((heads * MLA_V, seq), lambda b, h, i: (h, b))],
        out_specs=pl.BlockSpec((tq, heads * MLA_V), lambda b, h, i: (b * nq + i, h)),
        out_shape=jax.ShapeDtypeStruct((batch * seq, MLA_HEADS * MLA_V), BF16),
        scratch_shapes=[pltpu.VMEM((2, heads, tk, tq), F32), pltpu.VMEM((heads, 1, tq), F32),
                        pltpu.VMEM((heads, MLA_V + ONES_ROWS, tq), F32)],
        compiler_params=_params("parallel", "parallel", "parallel"),
        name="mla_attn",
    )(qp, kp, vt)


def _compress(a_ref, pe_ref, w1_ref, w2_ref):
    a = a_ref[...].astype(F32)
    half = a.shape[1]
    first = _dot((a + pe_ref[0:1, :]).astype(BF16), w1_ref[0:half, :])
    second = _dot((a + pe_ref[1:2, :]).astype(BF16), w1_ref[half:2 * half, :])
    n = a.shape[0]
    hidden = first + pltpu.roll(second, n - 1, 0)
    return _dot((hidden * jax.nn.sigmoid(hidden)).astype(BF16), w2_ref[...])


def _nsa_cmp_kernel(ak_ref, av_ref, pek_ref, pev_ref, w1k_ref, w2k_ref, w1v_ref, w2v_ref,
                    kc_ref, vct_ref):
    kc_ref[...] = _compress(ak_ref, pek_ref, w1k_ref, w2k_ref).astype(BF16)
    vct_ref[...] = _compress(av_ref, pev_ref, w1v_ref, w2v_ref).T.astype(BF16)


def _nsa_cmp(ak, av, pek, pev, w1k, w2k, w1v, w2v, layer):
    bg, nh, dk16 = ak.shape
    dv16 = av.shape[-1]
    const = lambda i: (layer, 0, 0)
    return pl.pallas_call(
        _nsa_cmp_kernel,
        grid=(bg,),
        in_specs=[pl.BlockSpec((None, nh, dk16), lambda i: (i, 0, 0)),
                  pl.BlockSpec((None, nh, dv16), lambda i: (i, 0, 0)),
                  pl.BlockSpec((None, 2, dk16), const),
                  pl.BlockSpec((None, 2, dv16), const),
                  pl.BlockSpec((None, 2 * dk16, CMP_HIDDEN), const),
                  pl.BlockSpec((None, CMP_HIDDEN, HEAD_PAD), const),
                  pl.BlockSpec((None, 2 * dv16, CMP_HIDDEN), const),
                  pl.BlockSpec((None, CMP_HIDDEN, NSA_DV), const)],
        out_specs=[pl.BlockSpec((None, nh, HEAD_PAD), lambda i: (i, 0, 0)),
                   pl.BlockSpec((None, NSA_DV, nh), lambda i: (i, 0, 0))],
        out_shape=[jax.ShapeDtypeStruct((bg, nh, HEAD_PAD), BF16),
                   jax.ShapeDtypeStruct((bg, NSA_DV, nh), BF16)],
        compiler_params=_params("parallel"),
        name="nsa_cmp",
    )(ak, av, pek, pev, w1k, w2k, w1v, w2v)


def _split3(x):
    a = x.astype(BF16)
    r = x - a.astype(F32)
    b = r.astype(BF16)
    c = (r - b.astype(F32)).astype(BF16)
    return a, b, c


def _lookup(table_row, idx):
    rows, width = idx.shape
    table = jnp.broadcast_to(table_row, (rows, LANE))
    chunks = [jnp.take_along_axis(table, idx[:, c:c + LANE], axis=1, mode="promise_in_bounds")
              for c in range(0, width, LANE)]
    return chunks[0] if len(chunks) == 1 else jnp.concatenate(chunks, axis=1)


def _nsa_attn_kernel(far_ref, q_ref, kc_ref, vct_ref, ks_ref, vst_ref, kw_ref, vwt_ref, ng_ref,
                     prow_ref, pcol_ref, pcmp_ref, tbl_ref, o_ref,
                     s_ref, m_ref, acc_ref, *, tq, tk, seq):
    b, i = pl.program_id(0), pl.program_id(2)
    nq, nk = seq // tq, seq // tk
    J = NSA_GROUP
    n_slc = seq // SLC_LEN
    n_cmp = (seq - CMP_LEN) // CMP_STRIDE + 1
    t0 = pl.multiple_of(i * tq, tq)

    qs = jnp.concatenate([q_ref[:, j * HEAD_PAD:(j + 1) * HEAD_PAD] for j in range(J)], axis=0)
    qpos = prow_ref[:, pl.ds(t0, tq)]
    qt = t0 + lax.broadcasted_iota(jnp.int32, (1, tq), 1)
    tbl = tbl_ref[...]
    far_bias = [tbl[j:j + 1, LANE - 1:LANE] for j in range(J)]
    head = [slice(j * tq, (j + 1) * tq) for j in range(J)]

    blk = lax.broadcasted_iota(jnp.int32, (LANE, 1), 0)
    keep_c = (blk * CMP_STRIDE + (CMP_LEN - 1) <= qt) & (blk < n_cmp)
    idx_c = jnp.clip(qpos - pcmp_ref[...], 0, LANE - 1)
    s_c = _dot_nt(kc_ref[...], qs)
    p_heads = []
    for j in range(J):
        s = jnp.where(keep_c, s_c[:, head[j]] + _lookup(tbl[j:j + 1, :], idx_c), NEG)
        e = jnp.where(keep_c, jnp.exp2(s - jnp.max(s, axis=0, keepdims=True)), 0.0)
        den = jnp.sum(e, axis=0, keepdims=True)
        p_heads.append(e / jnp.where(den > 0.0, den, 1.0))
    o_cmp = _dot(vct_ref[...], jnp.concatenate([p.astype(BF16) for p in p_heads], axis=1))

    rows = 32
    assert n_slc <= rows
    m_row = lax.broadcasted_iota(jnp.int32, (rows, LANE), 0)
    n_col = lax.broadcasted_iota(jnp.int32, (rows, LANE), 1)
    per = SLC_LEN // CMP_STRIDE
    back = (CMP_LEN - 1) // CMP_STRIDE
    overlap = ((n_col >= per * m_row - back) & (n_col <= per * m_row + per - 1)
               & (n_col < n_cmp) & (m_row < n_slc)).astype(BF16)
    imp = sum(_dot(overlap, part) for part in _split3(sum(p_heads)))
    m_blk = lax.broadcasted_iota(jnp.int32, (rows, 1), 0)
    cur = qt // SLC_LEN
    valid = m_blk <= cur
    forced = valid & ((m_blk == 0) | (m_blk >= cur - 1))
    score = jnp.where(forced, FORCED_SCORE, jnp.where(valid, imp, -1.0))
    rank = jnp.zeros((rows, tq), jnp.int32)
    for mp in range(n_slc):
        other = score[mp:mp + 1, :]
        ahead = (other > score) | ((other == score) & (m_blk > mp))
        rank = rank + ahead.astype(jnp.int32)
    sel = (valid & (rank < min(SLC_TOPN, n_slc))).astype(BF16)

    SLC, WIN = 0, 1
    _softmax_init(m_ref, acc_ref)
    sub = tk // LANE

    qchunks = [slice(c, c + LANE) for c in range(0, tq, LANE)]

    def slc_keep(kidx):
        expand = (kidx // SLC_LEN == lax.broadcasted_iota(jnp.int32, (1, rows), 1)).astype(BF16)
        hit = _dot(expand, sel)
        return [(hit[:, c] > 0.5) & (kidx <= qt[:, c]) for c in qchunks]

    def win_keep(kidx):
        diffs = [qt[:, c] - kidx for c in qchunks]
        return [(d >= 0) & (d < WINDOW) for d in diffs]

    k_refs, vt_refs, keeps = (ks_ref, kw_ref), (vst_ref, vwt_ref), (slc_keep, win_keep)
    adds = [far_bias[j] for j in range(J) for _ in qchunks]
    last = (t0 + tq + tk - 1) // tk
    first_win = jnp.maximum(t0 - (WINDOW - 1), 0) // tk

    def logits(kt, buf, slots):
        ks = pl.multiple_of(kt * tk, tk)
        for slot in slots:
            s_ref[buf, slot] = _dot_nt(k_refs[slot][pl.ds(ks, tk), :], qs)

    def near_bias(kt, buf, slots, live=None):
        ks = pl.multiple_of(kt * tk, tk)
        for hb in range(sub):
            kb = kt * sub + hb
            for qc, qcols in enumerate(qchunks):
                near = ((far_ref[((b * nq + i) * len(qchunks) + qc) * (nk * sub) + kb] == 0)
                        & (kb * LANE < t0 + (qc + 1) * LANE))
                if live is not None:
                    near = near & live

                @pl.when(near)
                def _():
                    kpos = pcol_ref[pl.ds(pl.multiple_of(ks + hb * LANE, LANE), LANE), :]
                    idx = jnp.clip(qpos[:, qcols] - kpos, 0, LANE - 1)
                    for j in range(J):
                        delta = _lookup(tbl[j:j + 1, :] - far_bias[j], idx)
                        cols = slice(j * tq + qc * LANE, j * tq + (qc + 1) * LANE)
                        for slot in slots:
                            s_ref[buf, slot, hb * LANE:(hb + 1) * LANE, cols] += delta

    def step(kt, buf, slots):
        ks = pl.multiple_of(kt * tk, tk)
        kidx = ks + lax.broadcasted_iota(jnp.int32, (tk, 1), 0)
        for slot in slots:
            _softmax_tile_t(s_ref.at[buf, slot], adds, keeps[slot](kidx) * J,
                            vt_refs[slot][:, pl.ds(ks, tk)], m_ref.at[slot], acc_ref.at[slot])
        nxt = jnp.minimum(kt + 1, last - 1)
        logits(nxt, 1 - buf, slots)
        near_bias(nxt, 1 - buf, slots, live=kt + 1 < last)

    def slc_body(kt, c):
        _by_parity(kt, lambda buf: step(kt, buf, (SLC,)))
        return c

    def both_body(kt, c):
        _by_parity(kt, lambda buf: step(kt, buf, (SLC, WIN)))
        return c

    def enter_window(buf):
        logits(first_win, buf, (WIN,))
        near_bias(first_win, buf, (WIN,))

    logits(0, 0, (SLC,))
    near_bias(0, 0, (SLC,))
    lax.fori_loop(0, first_win, slc_body, 0)
    _by_parity(first_win, enter_window)
    lax.fori_loop(first_win, last, both_body, 0)
    o_slc = _softmax_out(acc_ref[SLC])
    o_win = _softmax_out(acc_ref[WIN])

    gates = jax.nn.sigmoid(ng_ref[...].astype(F32)).T
    for j in range(J):
        o = (gates[3 * j:3 * j + 1, :] * o_cmp[:, head[j]]
             + gates[3 * j + 1:3 * j + 2, :] * o_slc[:, head[j]]
             + gates[3 * j + 2:3 * j + 3, :] * o_win[:, head[j]])
        o_ref[:, j * NSA_DV:(j + 1) * NSA_DV] = o.T.astype(o_ref.dtype)


def _nsa_attn(far, z, kc, vct, v_t, pos_row, pos_col, pos_cmp, tbl, batch, seq, zoff):
    tq, tk = NSA_TQ, NSA_TK
    nq = seq // tq
    G, J = NSA_KV_HEADS, NSA_GROUP
    R = J * tq
    kern = functools.partial(_nsa_attn_kernel, tq=tq, tk=tk, seq=seq)
    grid_spec = pltpu.PrefetchScalarGridSpec(
        num_scalar_prefetch=1,
        grid=(batch, G, nq),
        in_specs=[
            pl.BlockSpec((tq, J * HEAD_PAD), lambda b, g, i, far: (b * nq + i, g)),
            pl.BlockSpec((None, LANE, HEAD_PAD), lambda b, g, i, far: (b * G + g, 0, 0)),
            pl.BlockSpec((None, NSA_DV, LANE), lambda b, g, i, far: (b * G + g, 0, 0)),
            pl.BlockSpec((seq, HEAD_PAD), lambda b, g, i, far: (b, zoff["ks"] // HEAD_PAD + g)),
            pl.BlockSpec((NSA_DV, seq), lambda b, g, i, far: (g, b)),
            pl.BlockSpec((seq, HEAD_PAD), lambda b, g, i, far: (b, zoff["kw"] // HEAD_PAD + g)),
            pl.BlockSpec((NSA_DV, seq), lambda b, g, i, far: (G + g, b)),
            pl.BlockSpec((tq, LANE), lambda b, g, i, far: (b * nq + i, zoff["ng"] // LANE + g)),
            pl.BlockSpec((None, 1, seq), lambda b, g, i, far: (b, 0, 0)),
            pl.BlockSpec((seq, 1), lambda b, g, i, far: (b, 0)),
            pl.BlockSpec((LANE, 1), lambda b, g, i, far: (b, 0)),
            pl.BlockSpec((None, J, LANE), lambda b, g, i, far: (g, 0, 0)),
        ],
        out_specs=pl.BlockSpec((tq, J * NSA_DV), lambda b, g, i, far: (b * nq + i, g)),
        scratch_shapes=[pltpu.VMEM((2, 2, tk, R), F32), pltpu.VMEM((2, 1, R), F32),
                        pltpu.VMEM((2, NSA_DV + ONES_ROWS, R), F32)],
    )
    return pl.pallas_call(
        kern,
        grid_spec=grid_spec,
        out_shape=jax.ShapeDtypeStruct((batch * seq, NSA_HEADS * NSA_DV), BF16),
        compiler_params=_params("parallel", "parallel", "parallel"),
        name="nsa_attn",
    )(far, z, kc, vct, z, v_t, z, v_t, z, pos_row, pos_col, pos_cmp, tbl)


def _merge_kernel(a_ref, b_ref, ga_ref, gb_ref, h_ref, wa_ref, wb_ref, wo_ref, post_g_ref,
                  next_g_ref, o_ref, un_ref):
    ya = _dot(a_ref[...], wa_ref[...])
    yb = _dot(b_ref[...], wb_ref[...])
    m = (jax.nn.sigmoid(ga_ref[...].astype(F32)) * ya
         + jax.nn.sigmoid(gb_ref[...].astype(F32)) * yb).astype(BF16)
    y = _dot(m, wo_ref[...])
    out = h_ref[...] + _rms(y, post_g_ref[...])
    o_ref[...] = out
    un_ref[...] = _rms(out, next_g_ref[...]).astype(BF16)


def _merge(a, bb, z, h, wa, wb, wo, post_g, next_g, layer, zoff):
    m, d = h.shape
    tm = MERGE_TM
    da, db = a.shape[1], bb.shape[1]
    gblk = zoff["mg"] // d
    const = lambda i: (layer, 0, 0)
    return pl.pallas_call(
        _merge_kernel,
        grid=(m // tm,),
        in_specs=[pl.BlockSpec((tm, da), lambda i: (i, 0)),
                  pl.BlockSpec((tm, db), lambda i: (i, 0)),
                  pl.BlockSpec((tm, d), lambda i: (i, gblk)),
                  pl.BlockSpec((tm, d), lambda i: (i, gblk + 1)),
                  pl.BlockSpec((tm, d), lambda i: (i, 0)),
                  pl.BlockSpec((None, da, d), const, pipeline_mode=pl.Buffered(1)),
                  pl.BlockSpec((None, db, d), const, pipeline_mode=pl.Buffered(1)),
                  pl.BlockSpec((None, d, d), const, pipeline_mode=pl.Buffered(1)),
                  pl.BlockSpec((None, 1, d), const),
                  pl.BlockSpec((None, 1, d), const)],
        out_specs=[pl.BlockSpec((tm, d), lambda i: (i, 0)),
                   pl.BlockSpec((tm, d), lambda i: (i, 0))],
        out_shape=[jax.ShapeDtypeStruct((m, d), F32), jax.ShapeDtypeStruct((m, d), BF16)],
        compiler_params=_params("parallel"),
        name="merge_out",
    )(a, bb, z, z, h, wa, wb, wo, post_g, next_g)


def _z_layout(d_model):
    G = NSA_KV_HEADS
    widths = [("q", NSA_HEADS * HEAD_PAD), ("mg", 2 * d_model), ("cq", MLA_Q_LORA),
              ("ckv", MLA_KV_LORA), ("kc", G * HEAD_PAD), ("ks", G * HEAD_PAD),
              ("kw", G * HEAD_PAD), ("vc", G * NSA_DV), ("kr", LANE), ("ng", G * LANE)]
    off, pos = {}, 0
    for name, w in widths:
        off[name] = pos
        pos += w
    off["used"] = pos
    off["total"] = -(-pos // Z_PAD) * Z_PAD
    assert off["q"] == 0 and off["mg"] % d_model == 0
    assert off["cq"] % MLA_Q_LORA == 0 and off["ckv"] % MLA_KV_LORA == 0
    assert all(off[k] % HEAD_PAD == 0 for k in ("kc", "ks", "kw"))
    return off


def _swap_halves(w):
    half = w.shape[-1] // 2
    return jnp.concatenate([-w[..., half:], w[..., :half]], axis=-1)


def _pad_last(w, width):
    return jnp.pad(w, [(0, 0)] * (w.ndim - 1) + [(0, width - w.shape[-1])])


def _layout_w_in(w_in, zoff):
    L, D, _ = w_in.shape
    G, J = NSA_KV_HEADS, NSA_GROUP
    splits = [MLA_Q_LORA, MLA_KV_LORA, MLA_ROPE, NSA_HEADS * NSA_DK,
              G * NSA_DK, G * NSA_DV, G * NSA_DK, G * NSA_DV, G * NSA_DK, G * NSA_DV,
              NSA_HEADS * 3, 2 * D]
    bounds = [0] + [int(v) for v in np.cumsum(splits)]
    w_bf = w_in.astype(BF16)
    (c_q, c_kv, k_rope, _, k_c, v_c, k_s, v_s, k_w, v_w, nsa_g, merge_g) = [
        w_bf[..., lo:hi] for lo, hi in zip(bounds[:-1], bounds[1:])]
    nsa_q = w_in[..., bounds[3]:bounds[4]] * (NSA_DK ** -0.5 * LOG2E)

    def heads(w, n):
        return _pad_last(w.reshape(L, D, n, NSA_DK), HEAD_PAD).reshape(L, D, n * HEAD_PAD).astype(BF16)

    gates = _pad_last(nsa_g.reshape(L, D, G, J * 3), LANE).reshape(L, D, G * LANE)
    cols = [heads(nsa_q, NSA_HEADS), merge_g, c_q, c_kv, heads(k_c, G),
            heads(k_s, G), heads(k_w, G), v_c, k_rope, _swap_halves(k_rope), gates,
            jnp.zeros((L, D, zoff["total"] - zoff["used"]), BF16)]
    w_z = jnp.concatenate([c.astype(BF16) for c in cols], axis=-1)
    w_vt = jnp.concatenate([v_s, v_w], axis=-1).astype(BF16).transpose(0, 2, 1)
    return w_z, w_vt


def _layout_w_q_up(w):
    L, r, _ = w.shape
    w = w.reshape(L, r, MLA_HEADS, MLA_NOPE + MLA_ROPE) * ((MLA_NOPE + MLA_ROPE) ** -0.5 * LOG2E)
    rope = w[..., MLA_NOPE:]
    return jnp.concatenate([w[..., :MLA_NOPE], rope, _swap_halves(rope)], axis=-1).reshape(
        L, r, MLA_HEADS * HEAD_PAD).astype(BF16)


def kernel(x, positions, rel_bias, ffn1_pre_g, ffn1_post_g, ffn1_w_gate, ffn1_w_up, ffn1_w_down, mix_pre_g, mix_post_g, w_in, mla_q_norm_g, mla_w_q_up, mla_kv_norm_g, mla_w_uk, mla_w_uv, cmp_pe_k, cmp_w1_k, cmp_w2_k, cmp_pe_v, cmp_w1_v, cmp_w2_v, w_branch_mla, w_branch_nsa, w_out, ffn2_pre_g, ffn2_post_g, ffn2_w_gate, ffn2_w_up, ffn2_w_down):
    B, S, D = x.shape
    L = w_in.shape[0]
    M = B * S
    G, J = NSA_KV_HEADS, NSA_GROUP
    zoff = _z_layout(D)
    n_half = S // CMP_STRIDE
    n_cmp = (S - CMP_LEN) // CMP_STRIDE + 1
    assert n_half == LANE and n_cmp <= LANE

    gain = lambda g: g.reshape(L, 1, -1)
    bf = lambda w: w.astype(BF16)
    w_z, w_vt = _layout_w_in(w_in, zoff)
    w_q = _layout_w_q_up(mla_w_q_up)
    w_uk, w_uv_t = bf(mla_w_uk), bf(mla_w_uv).transpose(0, 2, 1)
    pe_k = cmp_pe_k.reshape(L, 2, CMP_STRIDE * NSA_DK)
    pe_v = cmp_pe_v.reshape(L, 2, CMP_STRIDE * NSA_DV)
    w2_k = bf(_pad_last(cmp_w2_k, HEAD_PAD))
    f1 = (bf(ffn1_w_gate), bf(ffn1_w_up), bf(ffn1_w_down))
    f2 = (bf(ffn2_w_gate), bf(ffn2_w_up), bf(ffn2_w_down))
    w1_k, w1_v, w2_v = bf(cmp_w1_k), bf(cmp_w1_v), bf(cmp_w2_v)
    w_a, w_b, w_o = bf(w_branch_mla), bf(w_branch_nsa), bf(w_out)
    g_f1pre, g_f1post, g_mpre, g_mpost = gain(ffn1_pre_g), gain(ffn1_post_g), gain(mix_pre_g), gain(mix_post_g)
    g_f2pre, g_f2post, g_q, g_kv = gain(ffn2_pre_g), gain(ffn2_post_g), gain(mla_q_norm_g), gain(mla_kv_norm_g)

    pos_col = positions.reshape(M, 1)
    pos_row = positions.reshape(B, 1, S)
    pos_cmp = _pad_last(positions[:, CMP_LEN - 1::CMP_STRIDE][:, :n_cmp], LANE).reshape(B * LANE, 1)
    q_min = positions.reshape(B, S // LANE, LANE).min(axis=-1)
    k_max = positions.reshape(B, S // LANE, LANE).max(axis=-1)
    far = (q_min[:, :, None] - k_max[:, None, :] >= FAR_DIST).astype(jnp.int32).reshape(-1)
    tbl = (jnp.take(rel_bias, jnp.asarray(BUCKET_OF_DIST), axis=0).T * LOG2E).reshape(G, J, LANE)
    half = MLA_ROPE // 2
    inv = ROPE_BASE ** (-jnp.arange(half, dtype=F32) * 2.0 / MLA_ROPE)
    cs = _rope_table(pos_col, jnp.concatenate([inv, inv]).reshape(1, MLA_ROPE))

    h = x.reshape(M, D)
    u = _rmsnorm(h, g_f1pre, 0)
    for l in range(L):
        h, u = _ffn(h, u, g_f1post, g_mpre, l, *f1, l)
        z = _in_proj(u, w_z, l)
        v_t = _in_proj_t(u, w_vt, l)
        qp, kp, vt = _mla_prep(z, cs, g_q, g_kv, w_q, w_uk, w_uv_t, l, zoff)
        a = _mla_attn(qp, kp, vt, B, S)

        def half_blocks(off, width, d):
            t = z[:, off:off + G * width].reshape(B, S, G, width)[..., :d]
            return t.transpose(0, 2, 1, 3).reshape(B * G, n_half, CMP_STRIDE * d)

        kc, vct = _nsa_cmp(half_blocks(zoff["kc"], HEAD_PAD, NSA_DK),
                           half_blocks(zoff["vc"], NSA_DV, NSA_DV),
                           pe_k, pe_v, w1_k, w2_k, w1_v, w2_v, l)
        nsa = _nsa_attn(far, z, kc, vct, v_t, pos_row, pos_col, pos_cmp, tbl, B, S, zoff)
        h, u = _merge(a, nsa, z, h, w_a, w_b, w_o, g_mpost, g_f2pre, l, zoff)
        h, u = _ffn(h, u, g_f2post, g_f1pre, min(l + 1, L - 1), *f2, l)
    return h.reshape(B, S, D)
```

```python
import functools
import math

import numpy as np
import jax
import jax.numpy as jnp
from jax import lax
from jax.experimental import pallas as pl
from jax.experimental.pallas import tpu as pltpu

EPS = 1e-6
MLA_HEADS = 8
MLA_Q_LORA = 512
MLA_KV_LORA = 512
MLA_NOPE = 128
MLA_ROPE = 64
MLA_V = 128
ROPE_BASE = 10000.0
NSA_HEADS = 8
NSA_KV_HEADS = 2
NSA_GROUP = NSA_HEADS // NSA_KV_HEADS
NSA_DK = 192
NSA_DV = 128
CMP_LEN = 32
CMP_STRIDE = 16
CMP_HIDDEN = 256
SLC_LEN = 64
SLC_TOPN = 16
WINDOW = 512
FORCED_SCORE = 1e6
REL_BUCKETS = 32
REL_MAX_DIST = 128
NEG = -1e30
MASKED = 2 * NEG
LOG2E = 1.0 / math.log(2.0)

LANE = 128
ONES_ROWS = 16
HEAD_PAD = 256
VMEM_LIMIT = 56 * 1024 * 1024
BF16 = jnp.bfloat16
F32 = jnp.float32

FFN_TM, FFN_TF = 512, 512
PROJ_TM = 1024
PROJ_TN_CAP = 2560
Z_PAD = 5 * LANE
PREP_TM = 512
MLA_TQ, MLA_TK = 512, 256
MLA_HEADS_PER_STEP = 4
NSA_TQ, NSA_TK = 256, 256
MERGE_TM = 512


def _bucket_of_distance():
    n = np.arange(LANE)
    max_exact = REL_BUCKETS // 2
    large = max_exact + (np.log(np.maximum(n, 1) / max_exact) / math.log(REL_MAX_DIST / max_exact)
                         * (REL_BUCKETS - max_exact)).astype(np.int32)
    bucket = np.where(n < max_exact, n, np.minimum(large, REL_BUCKETS - 1)).astype(np.int32)
    assert bucket[-1] == REL_BUCKETS - 1
    return bucket


BUCKET_OF_DIST = _bucket_of_distance()
FAR_DIST = int(np.max(np.nonzero(BUCKET_OF_DIST != REL_BUCKETS - 1)[0])) + 1


def _params(*sem):
    return pltpu.CompilerParams(dimension_semantics=sem, vmem_limit_bytes=VMEM_LIMIT)


def _rms(x, g):
    return x * lax.rsqrt(jnp.mean(x * x, axis=-1, keepdims=True) + EPS) * g


def _dot(a, b):
    return jnp.dot(a, b, preferred_element_type=F32)


def _dot_nt(a, b):
    return lax.dot_general(a, b, (((1,), (1,)), ((), ())), preferred_element_type=F32)


def _tile_n(n, cap):
    best = LANE
    for t in range(LANE, cap + 1, LANE):
        if n % t == 0:
            best = t
    return best


def _rmsnorm_kernel(x_ref, g_ref, o_ref):
    o_ref[...] = _rms(x_ref[...], g_ref[...]).astype(o_ref.dtype)


def _rmsnorm(x, g, layer):
    m, d = x.shape
    tm = 512
    return pl.pallas_call(
        _rmsnorm_kernel,
        grid=(m // tm,),
        in_specs=[pl.BlockSpec((tm, d), lambda i: (i, 0)),
                  pl.BlockSpec((None, 1, d), lambda i: (layer, 0, 0))],
        out_specs=pl.BlockSpec((tm, d), lambda i: (i, 0)),
        out_shape=jax.ShapeDtypeStruct((m, d), BF16),
        compiler_params=_params("parallel"),
        name="rmsnorm",
    )(x, g)


def _rope_table_kernel(pos_ref, inv_ref, o_ref):
    ang = pos_ref[...].astype(F32) * inv_ref[...]
    o_ref[...] = jnp.concatenate([jnp.cos(ang), jnp.sin(ang)], axis=1)


def _rope_table(pos_col, inv):
    m = pos_col.shape[0]
    tm = 512
    return pl.pallas_call(
        _rope_table_kernel,
        grid=(m // tm,),
        in_specs=[pl.BlockSpec((tm, 1), lambda i: (i, 0)),
                  pl.BlockSpec((1, MLA_ROPE), lambda i: (0, 0))],
        out_specs=pl.BlockSpec((tm, 2 * MLA_ROPE), lambda i: (i, 0)),
        out_shape=jax.ShapeDtypeStruct((m, 2 * MLA_ROPE), F32),
        compiler_params=_params("parallel"),
        name="rope_table",
    )(pos_col, inv)


def _ffn_kernel(x_ref, u_ref, post_g_ref, next_g_ref, wg_ref, wu_ref, wd_ref,
                o_ref, un_ref, acc_ref):
    j = pl.program_id(1)

    @pl.when(j == 0)
    def _():
        acc_ref[...] = jnp.zeros(acc_ref.shape, F32)

    u = u_ref[...]
    gate = _dot(u, wg_ref[...])
    up = _dot(u, wu_ref[...])
    hidden = (gate * jax.nn.sigmoid(gate) * up).astype(BF16)
    acc_ref[...] += _dot(hidden, wd_ref[...])

    @pl.when(j == pl.num_programs(1) - 1)
    def _():
        out = x_ref[...] + 0.5 * _rms(acc_ref[...], post_g_ref[...])
        o_ref[...] = out
        un_ref[...] = _rms(out, next_g_ref[...]).astype(BF16)


def _ffn(x, u, post_g, next_g, next_layer, wg, wu, wd, layer):
    m, d = x.shape
    f = wg.shape[-1]
    tm, tf = FFN_TM, _tile_n(f, FFN_TF)
    return pl.pallas_call(
        _ffn_kernel,
        grid=(m // tm, f // tf),
        in_specs=[pl.BlockSpec((tm, d), lambda i, j: (i, 0)),
                  pl.BlockSpec((tm, d), lambda i, j: (i, 0)),
                  pl.BlockSpec((None, 1, d), lambda i, j: (layer, 0, 0)),
                  pl.BlockSpec((None, 1, d), lambda i, j: (next_layer, 0, 0)),
                  pl.BlockSpec((None, d, tf), lambda i, j: (layer, 0, j)),
                  pl.BlockSpec((None, d, tf), lambda i, j: (layer, 0, j)),
                  pl.BlockSpec((None, tf, d), lambda i, j: (layer, j, 0))],
        out_specs=[pl.BlockSpec((tm, d), lambda i, j: (i, 0)),
                   pl.BlockSpec((tm, d), lambda i, j: (i, 0))],
        out_shape=[jax.ShapeDtypeStruct((m, d), F32), jax.ShapeDtypeStruct((m, d), BF16)],
        scratch_shapes=[pltpu.VMEM((tm, d), F32)],
        compiler_params=_params("parallel", "arbitrary"),
        name="ffn",
    )(x, u, post_g, next_g, wg, wu, wd)


def _matmul_kernel(x_ref, w_ref, o_ref):
    o_ref[...] = _dot(x_ref[...], w_ref[...]).astype(o_ref.dtype)


def _in_proj(u, w, layer):
    m, d = u.shape
    n = w.shape[-1]
    tm, tn = PROJ_TM, _tile_n(n, PROJ_TN_CAP)
    return pl.pallas_call(
        _matmul_kernel,
        grid=(n // tn, m // tm),
        in_specs=[pl.BlockSpec((tm, d), lambda j, i: (i, 0)),
                  pl.BlockSpec((None, d, tn), lambda j, i: (layer, 0, j))],
        out_specs=pl.BlockSpec((tm, tn), lambda j, i: (i, j)),
        out_shape=jax.ShapeDtypeStruct((m, n), BF16),
        compiler_params=_params("parallel", "parallel"),
        name="in_proj",
    )(u, w)


def _matmul_t_kernel(w_ref, x_ref, o_ref):
    o_ref[...] = _dot_nt(w_ref[...], x_ref[...]).astype(o_ref.dtype)


def _in_proj_t(u, w_t, layer):
    m, d = u.shape
    n = w_t.shape[1]
    tm = PROJ_TM
    return pl.pallas_call(
        _matmul_t_kernel,
        grid=(m // tm,),
        in_specs=[pl.BlockSpec((None, n, d), lambda i: (layer, 0, 0)),
                  pl.BlockSpec((tm, d), lambda i: (i, 0))],
        out_specs=pl.BlockSpec((n, tm), lambda i: (0, i)),
        out_shape=jax.ShapeDtypeStruct((n, m), BF16),
        compiler_params=_params("parallel"),
        name="in_proj_t",
    )(w_t, u)


def _softmax_tile_t(s_ref, adds, masks, v_t, m_ref, acc_ref):
    probs, alphas = [], []
    for c in range(s_ref.shape[1] // LANE):
        cols = slice(c * LANE, (c + 1) * LANE)
        s = s_ref[:, cols]
        if masks[c] is not None:
            s = jnp.where(masks[c], s, MASKED)
        m_old = m_ref[:, cols]
        m_tile = jnp.max(s, axis=0, keepdims=True)
        if adds[c] is not None:
            m_tile = m_tile + adds[c]
        m_new = jnp.maximum(m_old, m_tile)
        alpha = jnp.exp2(m_old - m_new)
        p = jnp.exp2(s - (m_new if adds[c] is None else m_new - adds[c]))
        m_ref[:, cols] = m_new
        probs.append(p.astype(BF16))
        alphas.append(alpha)
    p_t = probs[0] if len(probs) == 1 else jnp.concatenate(probs, axis=1)
    alpha = alphas[0] if len(alphas) == 1 else jnp.concatenate(alphas, axis=1)
    v_ones = jnp.concatenate([v_t, jnp.ones((ONES_ROWS, v_t.shape[1]), BF16)], axis=0)
    acc_ref[...] = alpha * acc_ref[...] + _dot(v_ones, p_t)


def _softmax_init(m_ref, acc_ref):
    m_ref[...] = jnp.full(m_ref.shape, NEG, F32)
    acc_ref[...] = jnp.zeros(acc_ref.shape, F32)


def _softmax_out(acc):
    dv = acc.shape[0] - ONES_ROWS
    return acc[:dv] / acc[dv:dv + 1]


def _mla_prep_kernel(cq_ref, ckv_ref, kr_ref, cs_ref, gq_ref, gkv_ref, wq_ref, wuk_ref, wuvt_ref,
                     q_ref, k_ref, vt_ref):
    cs = cs_ref[...]
    qn = _rms(cq_ref[...].astype(F32), gq_ref[...]).astype(BF16)
    q = _dot(qn, wq_ref[...])
    kvn = _rms(ckv_ref[...].astype(F32), gkv_ref[...]).astype(BF16)
    k_nope = _dot(kvn, wuk_ref[...])
    t = kr_ref[...].astype(F32) * cs
    k_pe = (t + pltpu.roll(t, MLA_ROPE, 1)).astype(BF16)
    for h in range(MLA_HEADS):
        lo = h * HEAD_PAD
        q_ref[:, lo:lo + MLA_NOPE] = q[:, lo:lo + MLA_NOPE].astype(BF16)
        q_ref[:, lo + MLA_NOPE:lo + HEAD_PAD] = (q[:, lo + MLA_NOPE:lo + HEAD_PAD] * cs).astype(BF16)
        k_ref[:, lo:lo + MLA_NOPE] = k_nope[:, h * MLA_NOPE:(h + 1) * MLA_NOPE].astype(BF16)
        k_ref[:, lo + MLA_NOPE:lo + HEAD_PAD] = k_pe
    vt_ref[...] = _dot_nt(wuvt_ref[...], kvn).astype(BF16)


def _mla_prep(z, cs, gq, gkv, wq, wuk, wuv_t, layer, zoff):
    m = z.shape[0]
    tm = PREP_TM
    hq = MLA_HEADS * HEAD_PAD
    hv = MLA_HEADS * MLA_V
    const = lambda i: (layer, 0, 0)
    return pl.pallas_call(
        _mla_prep_kernel,
        grid=(m // tm,),
        in_specs=[pl.BlockSpec((tm, MLA_Q_LORA), lambda i: (i, zoff["cq"] // MLA_Q_LORA)),
                  pl.BlockSpec((tm, MLA_KV_LORA), lambda i: (i, zoff["ckv"] // MLA_KV_LORA)),
                  pl.BlockSpec((tm, LANE), lambda i: (i, zoff["kr"] // LANE)),
                  pl.BlockSpec((tm, LANE), lambda i: (i, 0)),
                  pl.BlockSpec((None, 1, MLA_Q_LORA), const),
                  pl.BlockSpec((None, 1, MLA_KV_LORA), const),
                  pl.BlockSpec((None, MLA_Q_LORA, hq), const),
                  pl.BlockSpec((None, MLA_KV_LORA, MLA_HEADS * MLA_NOPE), const),
                  pl.BlockSpec((None, hv, MLA_KV_LORA), const)],
        out_specs=[pl.BlockSpec((tm, hq), lambda i: (i, 0)),
                   pl.BlockSpec((tm, hq), lambda i: (i, 0)),
                   pl.BlockSpec((hv, tm), lambda i: (0, i))],
        out_shape=[jax.ShapeDtypeStruct((m, hq), BF16),
                   jax.ShapeDtypeStruct((m, hq), BF16),
                   jax.ShapeDtypeStruct((hv, m), BF16)],
        compiler_params=_params("parallel"),
        name="mla_prep",
    )(z, z, z, cs, gq, gkv, wq, wuk, wuv_t)


def _by_parity(kt, fn):
    @pl.when(kt % 2 == 0)
    def _():
        fn(0)

    @pl.when(kt % 2 == 1)
    def _():
        fn(1)


def _mla_attn_kernel(q_ref, k_ref, vt_ref, o_ref, s_ref, m_ref, acc_ref, *, tq, tk, heads):
    i = pl.program_id(2)
    nc = tq // LANE
    _softmax_init(m_ref, acc_ref)
    qt = i * tq + lax.broadcasted_iota(jnp.int32, (1, tq), 1)
    kcol = lax.broadcasted_iota(jnp.int32, (tk, 1), 0)
    qk = [slice(h * HEAD_PAD, (h + 1) * HEAD_PAD) for h in range(heads)]
    vd = [slice(h * MLA_V, (h + 1) * MLA_V) for h in range(heads)]
    n_full = (i * tq) // tk
    n_all = (i * tq + tq + tk - 1) // tk

    def logits(kt, buf):
        ks = pl.multiple_of(kt * tk, tk)
        for h in range(heads):
            s_ref[buf, h] = _dot_nt(k_ref[pl.ds(ks, tk), qk[h]], q_ref[:, qk[h]])

    def step(kt, buf, masked, prefetch=True):
        ks = pl.multiple_of(kt * tk, tk)
        masks = [None] * nc
        if masked:
            masks = [ks + kcol <= qt[:, c * LANE:(c + 1) * LANE] for c in range(nc)]
        for h in range(heads):
            _softmax_tile_t(s_ref.at[buf, h], [None] * nc, masks, vt_ref[vd[h], pl.ds(ks, tk)],
                            m_ref.at[h], acc_ref.at[h])
        if prefetch:
            logits(kt + 1, 1 - buf)

    def full_body(kt, c):
        _by_parity(kt, lambda buf: step(kt, buf, False))
        return c

    def diag_body(kt, c):
        _by_parity(kt, lambda buf: step(kt, buf, True))
        return c

    logits(0, 0)
    lax.fori_loop(0, n_full, full_body, 0)
    lax.fori_loop(n_full, n_all - 1, diag_body, 0)
    _by_parity(n_all - 1, lambda buf: step(n_all - 1, buf, True, prefetch=False))
    for h in range(heads):
        o_ref[:, vd[h]] = _softmax_out(acc_ref[h]).T.astype(o_ref.dtype)


def _mla_attn(qp, kp, vt, batch, seq):
    tq, tk, heads = MLA_TQ, MLA_TK, MLA_HEADS_PER_STEP
    nq = seq // tq
    kern = functools.partial(_mla_attn_kernel, tq=tq, tk=tk, heads=heads)
    return pl.pallas_call(
        kern,
        grid=(batch, MLA_HEADS // heads, nq),
        in_specs=[pl.BlockSpec((tq, heads * HEAD_PAD), lambda b, h, i: (b * nq + i, h)),
                  pl.BlockSpec((seq, heads * HEAD_PAD), lambda b, h, i: (b, h)),
                  pl.BlockSpec((heads * MLA_V, seq), lambda b, h, i: (h, b))],
        out_specs=pl.BlockSpec((tq, heads * MLA_V), lambda b, h, i: (b * nq + i, h)),
        out_shape=jax.ShapeDtypeStruct((batch * seq, MLA_HEADS * MLA_V), BF16),
        scratch_shapes=[pltpu.VMEM((2, heads, tk, tq), F32), pltpu.VMEM((heads, 1, tq), F32),
                        pltpu.VMEM((heads, MLA_V + ONES_ROWS, tq), F32)],
        compiler_params=_params("parallel", "parallel", "parallel"),
        name="mla_attn",
    )(qp, kp, vt)


def _compress(a_ref, pe_ref, w1_ref, w2_ref):
    a = a_ref[...].astype(F32)
    half = a.shape[1]
    first = _dot((a + pe_ref[0:1, :]).astype(BF16), w1_ref[0:half, :])
    second = _dot((a + pe_ref[1:2, :]).astype(BF16), w1_ref[half:2 * half, :])
    n = a.shape[0]
    hidden = first + pltpu.roll(second, n - 1, 0)
    return _dot((hidden * jax.nn.sigmoid(hidden)).astype(BF16), w2_ref[...])


def _nsa_cmp_kernel(ak_ref, av_ref, pek_ref, pev_ref, w1k_ref, w2k_ref, w1v_ref, w2v_ref,
                    kc_ref, vct_ref):
    kc_ref[...] = _compress(ak_ref, pek_ref, w1k_ref, w2k_ref).astype(BF16)
    vct_ref[...] = _compress(av_ref, pev_ref, w1v_ref, w2v_ref).T.astype(BF16)


def _nsa_cmp(ak, av, pek, pev, w1k, w2k, w1v, w2v, layer):
    bg, nh, dk16 = ak.shape
    dv16 = av.shape[-1]
    const = lambda i: (layer, 0, 0)
    return pl.pallas_call(
        _nsa_cmp_kernel,
        grid=(bg,),
        in_specs=[pl.BlockSpec((None, nh, dk16), lambda i: (i, 0, 0)),
                  pl.BlockSpec((None, nh, dv16), lambda i: (i, 0, 0)),
                  pl.BlockSpec((None, 2, dk16), const),
                  pl.BlockSpec((None, 2, dv16), const),
                  pl.BlockSpec((None, 2 * dk16, CMP_HIDDEN), const),
                  pl.BlockSpec((None, CMP_HIDDEN, HEAD_PAD), const),
                  pl.BlockSpec((None, 2 * dv16, CMP_HIDDEN), const),
                  pl.BlockSpec((None, CMP_HIDDEN, NSA_DV), const)],
        out_specs=[pl.BlockSpec((None, nh, HEAD_PAD), lambda i: (i, 0, 0)),
                   pl.BlockSpec((None, NSA_DV, nh), lambda i: (i, 0, 0))],
        out_shape=[jax.ShapeDtypeStruct((bg, nh, HEAD_PAD), BF16),
                   jax.ShapeDtypeStruct((bg, NSA_DV, nh), BF16)],
        compiler_params=_params("parallel"),
        name="nsa_cmp",
    )(ak, av, pek, pev, w1k, w2k, w1v, w2v)


def _split3(x):
    a = x.astype(BF16)
    r = x - a.astype(F32)
    b = r.astype(BF16)
    c = (r - b.astype(F32)).astype(BF16)
    return a, b, c


def _lookup(table_row, idx):
    rows, width = idx.shape
    table = jnp.broadcast_to(table_row, (rows, LANE))
    chunks = [jnp.take_along_axis(table, idx[:, c:c + LANE], axis=1, mode="promise_in_bounds")
              for c in range(0, width, LANE)]
    return chunks[0] if len(chunks) == 1 else jnp.concatenate(chunks, axis=1)


def _nsa_attn_kernel(far_ref, q_ref, kc_ref, vct_ref, ks_ref, vst_ref, kw_ref, vwt_ref, ng_ref,
                     prow_ref, pcol_ref, pcmp_ref, tbl_ref, o_ref,
                     s_ref, m_ref, acc_ref, *, tq, tk, seq):
    b, i = pl.program_id(0), pl.program_id(2)
    nq, nk = seq // tq, seq // tk
    J = NSA_GROUP
    n_slc = seq // SLC_LEN
    n_cmp = (seq - CMP_LEN) // CMP_STRIDE + 1
    t0 = pl.multiple_of(i * tq, tq)

    qs = jnp.concatenate([q_ref[:, j * HEAD_PAD:(j + 1) * HEAD_PAD] for j in range(J)], axis=0)
    qpos = prow_ref[:, pl.ds(t0, tq)]
    qt = t0 + lax.broadcasted_iota(jnp.int32, (1, tq), 1)
    tbl = tbl_ref[...]
    far_bias = [tbl[j:j + 1, LANE - 1:LANE] for j in range(J)]
    head = [slice(j * tq, (j + 1) * tq) for j in range(J)]

    blk = lax.broadcasted_iota(jnp.int32, (LANE, 1), 0)
    keep_c = (blk * CMP_STRIDE + (CMP_LEN - 1) <= qt) & (blk < n_cmp)
    idx_c = jnp.clip(qpos - pcmp_ref[...], 0, LANE - 1)
    s_c = _dot_nt(kc_ref[...], qs)
    p_heads = []
    for j in range(J):
        s = jnp.where(keep_c, s_c[:, head[j]] + _lookup(tbl[j:j + 1, :], idx_c), NEG)
        e = jnp.where(keep_c, jnp.exp2(s - jnp.max(s, axis=0, keepdims=True)), 0.0)
        den = jnp.sum(e, axis=0, keepdims=True)
        p_heads.append(e / jnp.where(den > 0.0, den, 1.0))
    o_cmp = _dot(vct_ref[...], jnp.concatenate([p.astype(BF16) for p in p_heads], axis=1))

    rows = 32
    assert n_slc <= rows
    m_row = lax.broadcasted_iota(jnp.int32, (rows, LANE), 0)
    n_col = lax.broadcasted_iota(jnp.int32, (rows, LANE), 1)
    per = SLC_LEN // CMP_STRIDE
    back = (CMP_LEN - 1) // CMP_STRIDE
    overlap = ((n_col >= per * m_row - back) & (n_col <= per * m_row + per - 1)
               & (n_col < n_cmp) & (m_row < n_slc)).astype(BF16)
    imp = sum(_dot(overlap, part) for part in _split3(sum(p_heads)))
    m_blk = lax.broadcasted_iota(jnp.int32, (rows, 1), 0)
    cur = qt // SLC_LEN
    valid = m_blk <= cur
    forced = valid & ((m_blk == 0) | (m_blk >= cur - 1))
    score = jnp.where(forced, FORCED_SCORE, jnp.where(valid, imp, -1.0))
    rank = jnp.zeros((rows, tq), jnp.int32)
    for mp in range(n_slc):
        other = score[mp:mp + 1, :]
        ahead = (other > score) | ((other == score) & (m_blk > mp))
        rank = rank + ahead.astype(jnp.int32)
    sel = (valid & (rank < min(SLC_TOPN, n_slc))).astype(BF16)

    SLC, WIN = 0, 1
    _softmax_init(m_ref, acc_ref)
    sub = tk // LANE

    qchunks = [slice(c, c + LANE) for c in range(0, tq, LANE)]

    def slc_keep(kidx):
        expand = (kidx // SLC_LEN == lax.broadcasted_iota(jnp.int32, (1, rows), 1)).astype(BF16)
        hit = _dot(expand, sel)
        return [(hit[:, c] > 0.5) & (kidx <= qt[:, c]) for c in qchunks]

    def win_keep(kidx):
        diffs = [qt[:, c] - kidx for c in qchunks]
        return [(d >= 0) & (d < WINDOW) for d in diffs]

    k_refs, vt_refs, keeps = (ks_ref, kw_ref), (vst_ref, vwt_ref), (slc_keep, win_keep)
    adds = [far_bias[j] for j in range(J) for _ in qchunks]
    last = (t0 + tq + tk - 1) // tk
    first_win = jnp.maximum(t0 - (WINDOW - 1), 0) // tk

    def logits(kt, buf, slots):
        ks = pl.multiple_of(kt * tk, tk)
        for slot in slots:
            s_ref[buf, slot] = _dot_nt(k_refs[slot][pl.ds(ks, tk), :], qs)

    def near_bias(kt, buf, slots, live=None):
        ks = pl.multiple_of(kt * tk, tk)
        for hb in range(sub):
            kb = kt * sub + hb
            for qc, qcols in enumerate(qchunks):
                near = ((far_ref[((b * nq + i) * len(qchunks) + qc) * (nk * sub) + kb] == 0)
                        & (kb * LANE < t0 + (qc + 1) * LANE))
                if live is not None:
                    near = near & live

                @pl.when(near)
                def _():
                    kpos = pcol_ref[pl.ds(pl.multiple_of(ks + hb * LANE, LANE), LANE), :]
                    idx = jnp.clip(qpos[:, qcols] - kpos, 0, LANE - 1)
                    for j in range(J):
                        delta = _lookup(tbl[j:j + 1, :] - far_bias[j], idx)
                        cols = slice(j * tq + qc * LANE, j * tq + (qc + 1) * LANE)
                        for slot in slots:
                            s_ref[buf, slot, hb * LANE:(hb + 1) * LANE, cols] += delta

    def step(kt, buf, slots, prefetch=True):
        ks = pl.multiple_of(kt * tk, tk)
        kidx = ks + lax.broadcasted_iota(jnp.int32, (tk, 1), 0)
        for slot in slots:
            _softmax_tile_t(s_ref.at[buf, slot], adds, keeps[slot](kidx) * J,
                            vt_refs[slot][:, pl.ds(ks, tk)], m_ref.at[slot], acc_ref.at[slot])
        if prefetch:
            logits(kt + 1, 1 - buf, slots)
            near_bias(kt + 1, 1 - buf, slots)

    def slc_body(kt, c):
        _by_parity(kt, lambda buf: step(kt, buf, (SLC,)))
        return c

    def both_body(kt, c):
        _by_parity(kt, lambda buf: step(kt, buf, (SLC, WIN)))
        return c

    def enter_window(buf):
        logits(first_win, buf, (WIN,))
        near_bias(first_win, buf, (WIN,))

    logits(0, 0, (SLC,))
    near_bias(0, 0, (SLC,))
    lax.fori_loop(0, first_win, slc_body, 0)
    _by_parity(first_win, enter_window)
    lax.fori_loop(first_win, last - 1, both_body, 0)
    _by_parity(last - 1, lambda buf: step(last - 1, buf, (SLC, WIN), prefetch=False))
    o_slc = _softmax_out(acc_ref[SLC])
    o_win = _softmax_out(acc_ref[WIN])

    gates = jax.nn.sigmoid(ng_ref[...].astype(F32)).T
    for j in range(J):
        o = (gates[3 * j:3 * j + 1, :] * o_cmp[:, head[j]]
             + gates[3 * j + 1:3 * j + 2, :] * o_slc[:, head[j]]
             + gates[3 * j + 2:3 * j + 3, :] * o_win[:, head[j]])
        o_ref[:, j * NSA_DV:(j + 1) * NSA_DV] = o.T.astype(o_ref.dtype)


def _nsa_attn(far, z, kc, vct, v_t, pos_row, pos_col, pos_cmp, tbl, batch, seq, zoff):
    tq, tk = NSA_TQ, NSA_TK
    nq = seq // tq
    G, J = NSA_KV_HEADS, NSA_GROUP
    R = J * tq
    kern = functools.partial(_nsa_attn_kernel, tq=tq, tk=tk, seq=seq)
    grid_spec = pltpu.PrefetchScalarGridSpec(
        num_scalar_prefetch=1,
        grid=(batch, G, nq),
        in_specs=[
            pl.BlockSpec((tq, J * HEAD_PAD), lambda b, g, i, far: (b * nq + i, g)),
            pl.BlockSpec((None, LANE, HEAD_PAD), lambda b, g, i, far: (b * G + g, 0, 0)),
            pl.BlockSpec((None, NSA_DV, LANE), lambda b, g, i, far: (b * G + g, 0, 0)),
            pl.BlockSpec((seq, HEAD_PAD), lambda b, g, i, far: (b, zoff["ks"] // HEAD_PAD + g)),
            pl.BlockSpec((NSA_DV, seq), lambda b, g, i, far: (g, b)),
            pl.BlockSpec((seq, HEAD_PAD), lambda b, g, i, far: (b, zoff["kw"] // HEAD_PAD + g)),
            pl.BlockSpec((NSA_DV, seq), lambda b, g, i, far: (G + g, b)),
            pl.BlockSpec((tq, LANE), lambda b, g, i, far: (b * nq + i, zoff["ng"] // LANE + g)),
            pl.BlockSpec((None, 1, seq), lambda b, g, i, far: (b, 0, 0)),
            pl.BlockSpec((seq, 1), lambda b, g, i, far: (b, 0)),
            pl.BlockSpec((LANE, 1), lambda b, g, i, far: (b, 0)),
            pl.BlockSpec((None, J, LANE), lambda b, g, i, far: (g, 0, 0)),
        ],
        out_specs=pl.BlockSpec((tq, J * NSA_DV), lambda b, g, i, far: (b * nq + i, g)),
        scratch_shapes=[pltpu.VMEM((2, 2, tk, R), F32), pltpu.VMEM((2, 1, R), F32),
                        pltpu.VMEM((2, NSA_DV + ONES_ROWS, R), F32)],
    )
    return pl.pallas_call(
        kern,
        grid_spec=grid_spec,
        out_shape=jax.ShapeDtypeStruct((batch * seq, NSA_HEADS * NSA_DV), BF16),
        compiler_params=_params("parallel", "parallel", "parallel"),
        name="nsa_attn",
    )(far, z, kc, vct, z, v_t, z, v_t, z, pos_row, pos_col, pos_cmp, tbl)


def _merge_kernel(a_ref, b_ref, ga_ref, gb_ref, h_ref, wa_ref, wb_ref, wo_ref, post_g_ref,
                  next_g_ref, o_ref, un_ref):
    ya = _dot(a_ref[...], wa_ref[...])
    yb = _dot(b_ref[...], wb_ref[...])
    m = (jax.nn.sigmoid(ga_ref[...].astype(F32)) * ya
         + jax.nn.sigmoid(gb_ref[...].astype(F32)) * yb).astype(BF16)
    y = _dot(m, wo_ref[...])
    out = h_ref[...] + _rms(y, post_g_ref[...])
    o_ref[...] = out
    un_ref[...] = _rms(out, next_g_ref[...]).astype(BF16)


def _merge(a, bb, z, h, wa, wb, wo, post_g, next_g, layer, zoff):
    m, d = h.shape
    tm = MERGE_TM
    da, db = a.shape[1], bb.shape[1]
    gblk = zoff["mg"] // d
    const = lambda i: (layer, 0, 0)
    return pl.pallas_call(
        _merge_kernel,
        grid=(m // tm,),
        in_specs=[pl.BlockSpec((tm, da), lambda i: (i, 0)),
                  pl.BlockSpec((tm, db), lambda i: (i, 0)),
                  pl.BlockSpec((tm, d), lambda i: (i, gblk)),
                  pl.BlockSpec((tm, d), lambda i: (i, gblk + 1)),
                  pl.BlockSpec((tm, d), lambda i: (i, 0)),
                  pl.BlockSpec((None, da, d), const, pipeline_mode=pl.Buffered(1)),
                  pl.BlockSpec((None, db, d), const, pipeline_mode=pl.Buffered(1)),
                  pl.BlockSpec((None, d, d), const, pipeline_mode=pl.Buffered(1)),
                  pl.BlockSpec((None, 1, d), const),
                  pl.BlockSpec((None, 1, d), const)],
        out_specs=[pl.BlockSpec((tm, d), lambda i: (i, 0)),
                   pl.BlockSpec((tm, d), lambda i: (i, 0))],
        out_shape=[jax.ShapeDtypeStruct((m, d), F32), jax.ShapeDtypeStruct((m, d), BF16)],
        compiler_params=_params("parallel"),
        name="merge_out",
    )(a, bb, z, z, h, wa, wb, wo, post_g, next_g)


def _z_layout(d_model):
    G = NSA_KV_HEADS
    widths = [("q", NSA_HEADS * HEAD_PAD), ("mg", 2 * d_model), ("cq", MLA_Q_LORA),
              ("ckv", MLA_KV_LORA), ("kc", G * HEAD_PAD), ("ks", G * HEAD_PAD),
              ("kw", G * HEAD_PAD), ("vc", G * NSA_DV), ("kr", LANE), ("ng", G * LANE)]
    off, pos = {}, 0
    for name, w in widths:
        off[name] = pos
        pos += w
    off["used"] = pos
    off["total"] = -(-pos // Z_PAD) * Z_PAD
    assert off["q"] == 0 and off["mg"] % d_model == 0
    assert off["cq"] % MLA_Q_LORA == 0 and off["ckv"] % MLA_KV_LORA == 0
    assert all(off[k] % HEAD_PAD == 0 for k in ("kc", "ks", "kw"))
    return off


def _swap_halves(w):
    half = w.shape[-1] // 2
    return jnp.concatenate([-w[..., half:], w[..., :half]], axis=-1)


def _pad_last(w, width):
    return jnp.pad(w, [(0, 0)] * (w.ndim - 1) + [(0, width - w.shape[-1])])


def _w_in_pieces(d_model, zoff):
    G, J = NSA_KV_HEADS, NSA_GROUP
    splits = [MLA_Q_LORA, MLA_KV_LORA, MLA_ROPE, NSA_HEADS * NSA_DK,
              G * NSA_DK, G * NSA_DV, G * NSA_DK, G * NSA_DV, G * NSA_DK, G * NSA_DV,
              NSA_HEADS * 3, 2 * d_model]
    names = ["cq", "ckv", "kr", "q", "kc", "vc", "ks", "vs", "kw", "vw", "ng", "mg"]
    src = dict(zip(names, [0] + [int(v) for v in np.cumsum(splits)[:-1]]))
    half = MLA_ROPE // 2
    pieces = []
    for h in range(NSA_HEADS):
        pieces.append((zoff["q"] + h * HEAD_PAD, HEAD_PAD,
                       [(src["q"] + h * NSA_DK, NSA_DK, NSA_DK ** -0.5 * LOG2E)]))
    pieces.append((zoff["mg"], 2 * d_model, [(src["mg"], 2 * d_model, None)]))
    pieces.append((zoff["cq"], MLA_Q_LORA, [(src["cq"], MLA_Q_LORA, None)]))
    pieces.append((zoff["ckv"], MLA_KV_LORA, [(src["ckv"], MLA_KV_LORA, None)]))
    for name in ("kc", "ks", "kw"):
        for g in range(G):
            pieces.append((zoff[name] + g * HEAD_PAD, HEAD_PAD, [(src[name] + g * NSA_DK, NSA_DK, None)]))
    pieces.append((zoff["vc"], G * NSA_DV, [(src["vc"], G * NSA_DV, None)]))
    pieces.append((zoff["kr"], LANE, [(src["kr"], MLA_ROPE, None), (src["kr"] + half, half, -1.0),
                                      (src["kr"], half, None)]))
    for g in range(G):
        pieces.append((zoff["ng"] + g * LANE, LANE, [(src["ng"] + g * J * 3, J * 3, None)]))
    if zoff["total"] > zoff["used"]:
        pieces.append((zoff["used"], zoff["total"] - zoff["used"], []))
    return pieces, (src["vs"], src["vw"])


def _w_in_layout_kernel(w_ref, o_ref, *, pieces):
    rows = w_ref.shape[0]
    for dst, width, parts in pieces:
        cols = []
        for src, w, scale in parts:
            x = w_ref[:, src:src + w]
            cols.append(x if scale is None else x * scale)
        used = sum(w for _, w, _ in parts)
        if used < width:
            cols.append(jnp.zeros((rows, width - used), F32))
        x = cols[0] if len(cols) == 1 else jnp.concatenate(cols, axis=1)
        o_ref[:, dst:dst + width] = x.astype(BF16)


def _layout_w_in(w_in, zoff):
    L, D, n_in = w_in.shape
    G = NSA_KV_HEADS
    pieces, (vs, vw) = _w_in_pieces(D, zoff)
    tr = 256
    w_z = pl.pallas_call(
        functools.partial(_w_in_layout_kernel, pieces=pieces),
        grid=(L, D // tr),
        in_specs=[pl.BlockSpec((None, tr, n_in), lambda l, r: (l, r, 0))],
        out_specs=pl.BlockSpec((None, tr, zoff["total"]), lambda l, r: (l, r, 0)),
        out_shape=jax.ShapeDtypeStruct((L, D, zoff["total"]), BF16),
        compiler_params=_params("parallel", "parallel"),
        name="w_in_layout",
    )(w_in)
    w_vt = jnp.concatenate([w_in[..., vs:vs + G * NSA_DV], w_in[..., vw:vw + G * NSA_DV]],
                           axis=-1).astype(BF16).transpose(0, 2, 1)
    return w_z, w_vt


def _layout_w_q_up(w):
    L, r, _ = w.shape
    w = w.reshape(L, r, MLA_HEADS, MLA_NOPE + MLA_ROPE) * ((MLA_NOPE + MLA_ROPE) ** -0.5 * LOG2E)
    rope = w[..., MLA_NOPE:]
    return jnp.concatenate([w[..., :MLA_NOPE], rope, _swap_halves(rope)], axis=-1).reshape(
        L, r, MLA_HEADS * HEAD_PAD).astype(BF16)


def kernel(x, positions, rel_bias, ffn1_pre_g, ffn1_post_g, ffn1_w_gate, ffn1_w_up, ffn1_w_down, mix_pre_g, mix_post_g, w_in, mla_q_norm_g, mla_w_q_up, mla_kv_norm_g, mla_w_uk, mla_w_uv, cmp_pe_k, cmp_w1_k, cmp_w2_k, cmp_pe_v, cmp_w1_v, cmp_w2_v, w_branch_mla, w_branch_nsa, w_out, ffn2_pre_g, ffn2_post_g, ffn2_w_gate, ffn2_w_up, ffn2_w_down):
    B, S, D = x.shape
    L = w_in.shape[0]
    M = B * S
    G, J = NSA_KV_HEADS, NSA_GROUP
    zoff = _z_layout(D)
    n_half = S // CMP_STRIDE
    n_cmp = (S - CMP_LEN) // CMP_STRIDE + 1
    assert n_half == LANE and n_cmp <= LANE

    gain = lambda g: g.reshape(L, 1, -1)
    bf = lambda w: w.astype(BF16)
    w_z, w_vt = _layout_w_in(w_in, zoff)
    w_q = _layout_w_q_up(mla_w_q_up)
    w_uk, w_uv_t = bf(mla_w_uk), bf(mla_w_uv).transpose(0, 2, 1)
    pe_k = cmp_pe_k.reshape(L, 2, CMP_STRIDE * NSA_DK)
    pe_v = cmp_pe_v.reshape(L, 2, CMP_STRIDE * NSA_DV)
    w2_k = bf(_pad_last(cmp_w2_k, HEAD_PAD))
    f1 = (bf(ffn1_w_gate), bf(ffn1_w_up), bf(ffn1_w_down))
    f2 = (bf(ffn2_w_gate), bf(ffn2_w_up), bf(ffn2_w_down))
    w1_k, w1_v, w2_v = bf(cmp_w1_k), bf(cmp_w1_v), bf(cmp_w2_v)
    w_a, w_b, w_o = bf(w_branch_mla), bf(w_branch_nsa), bf(w_out)
    g_f1pre, g_f1post, g_mpre, g_mpost = gain(ffn1_pre_g), gain(ffn1_post_g), gain(mix_pre_g), gain(mix_post_g)
    g_f2pre, g_f2post, g_q, g_kv = gain(ffn2_pre_g), gain(ffn2_post_g), gain(mla_q_norm_g), gain(mla_kv_norm_g)

    pos_col = positions.reshape(M, 1)
    pos_row = positions.reshape(B, 1, S)
    pos_cmp = _pad_last(positions[:, CMP_LEN - 1::CMP_STRIDE][:, :n_cmp], LANE).reshape(B * LANE, 1)
    q_min = positions.reshape(B, S // LANE, LANE).min(axis=-1)
    k_max = positions.reshape(B, S // LANE, LANE).max(axis=-1)
    far = (q_min[:, :, None] - k_max[:, None, :] >= FAR_DIST).astype(jnp.int32).reshape(-1)
    tbl = (jnp.take(rel_bias, jnp.asarray(BUCKET_OF_DIST), axis=0).T * LOG2E).reshape(G, J, LANE)
    half = MLA_ROPE // 2
    inv = ROPE_BASE ** (-jnp.arange(half, dtype=F32) * 2.0 / MLA_ROPE)
    cs = _rope_table(pos_col, jnp.concatenate([inv, inv]).reshape(1, MLA_ROPE))

    h = x.reshape(M, D)
    u = _rmsnorm(h, g_f1pre, 0)
    for l in range(L):
        h, u = _ffn(h, u, g_f1post, g_mpre, l, *f1, l)
        z = _in_proj(u, w_z, l)
        v_t = _in_proj_t(u, w_vt, l)
        qp, kp, vt = _mla_prep(z, cs, g_q, g_kv, w_q, w_uk, w_uv_t, l, zoff)
        a = _mla_attn(qp, kp, vt, B, S)

        def half_blocks(off, width, d):
            t = z[:, off:off + G * width].reshape(B, S, G, width)[..., :d]
            return t.transpose(0, 2, 1, 3).reshape(B * G, n_half, CMP_STRIDE * d)

        kc, vct = _nsa_cmp(half_blocks(zoff["kc"], HEAD_PAD, NSA_DK),
                           half_blocks(zoff["vc"], NSA_DV, NSA_DV),
                           pe_k, pe_v, w1_k, w2_k, w1_v, w2_v, l)
        nsa = _nsa_attn(far, z, kc, vct, v_t, pos_row, pos_col, pos_cmp, tbl, B, S, zoff)
        h, u = _merge(a, nsa, z, h, w_a, w_b, w_o, g_mpost, g_f2pre, l, zoff)
        h, u = _ffn(h, u, g_f2post, g_f1pre, min(l + 1, L - 1), *f2, l)
    return h.reshape(B, S, D)
```

```python
import functools
import math

import numpy as np
import jax
import jax.numpy as jnp
from jax import lax
from jax.experimental import pallas as pl
from jax.experimental.pallas import tpu as pltpu

EPS = 1e-6
MLA_HEADS = 8
MLA_Q_LORA = 512
MLA_KV_LORA = 512
MLA_NOPE = 128
MLA_ROPE = 64
MLA_V = 128
ROPE_BASE = 10000.0
NSA_HEADS = 8
NSA_KV_HEADS = 2
NSA_GROUP = NSA_HEADS // NSA_KV_HEADS
NSA_DK = 192
NSA_DV = 128
CMP_LEN = 32
CMP_STRIDE = 16
CMP_HIDDEN = 256
SLC_LEN = 64
SLC_TOPN = 16
WINDOW = 512
FORCED_SCORE = 1e6
REL_BUCKETS = 32
REL_MAX_DIST = 128
NEG = -1e30
MASKED = 2 * NEG
LOG2E = 1.0 / math.log(2.0)

LANE = 128
ONES_ROWS = 16
HEAD_PAD = 256
VMEM_LIMIT = 56 * 1024 * 1024
BF16 = jnp.bfloat16
F32 = jnp.float32

FFN_TM, FFN_TF = 512, 512
PROJ_TM = 1024
PROJ_TN_CAP = 2560
Z_PAD = 5 * LANE
PREP_TM = 512
MLA_TQ, MLA_TK = 512, 256
MLA_HEADS_PER_STEP = 4
NSA_TQ, NSA_TK = 256, 256
MERGE_TM = 512


def _bucket_of_distance():
    n = np.arange(LANE)
    max_exact = REL_BUCKETS // 2
    large = max_exact + (np.log(np.maximum(n, 1) / max_exact) / math.log(REL_MAX_DIST / max_exact)
                         * (REL_BUCKETS - max_exact)).astype(np.int32)
    bucket = np.where(n < max_exact, n, np.minimum(large, REL_BUCKETS - 1)).astype(np.int32)
    assert bucket[-1] == REL_BUCKETS - 1
    return bucket


BUCKET_OF_DIST = _bucket_of_distance()
FAR_DIST = int(np.max(np.nonzero(BUCKET_OF_DIST != REL_BUCKETS - 1)[0])) + 1


def _params(*sem):
    return pltpu.CompilerParams(dimension_semantics=sem, vmem_limit_bytes=VMEM_LIMIT)


def _rms(x, g):
    return x * lax.rsqrt(jnp.mean(x * x, axis=-1, keepdims=True) + EPS) * g


def _dot(a, b):
    return jnp.dot(a, b, preferred_element_type=F32)


def _dot_nt(a, b):
    return lax.dot_general(a, b, (((1,), (1,)), ((), ())), preferred_element_type=F32)


def _tile_n(n, cap):
    best = LANE
    for t in range(LANE, cap + 1, LANE):
        if n % t == 0:
            best = t
    return best


def _rmsnorm_kernel(x_ref, g_ref, o_ref):
    o_ref[...] = _rms(x_ref[...], g_ref[...]).astype(o_ref.dtype)


def _rmsnorm(x, g, layer):
    m, d = x.shape
    tm = 512
    return pl.pallas_call(
        _rmsnorm_kernel,
        grid=(m // tm,),
        in_specs=[pl.BlockSpec((tm, d), lambda i: (i, 0)),
                  pl.BlockSpec((None, 1, d), lambda i: (layer, 0, 0))],
        out_specs=pl.BlockSpec((tm, d), lambda i: (i, 0)),
        out_shape=jax.ShapeDtypeStruct((m, d), BF16),
        compiler_params=_params("parallel"),
        name="rmsnorm",
    )(x, g)


def _rope_table_kernel(pos_ref, inv_ref, o_ref):
    ang = pos_ref[...].astype(F32) * inv_ref[...]
    o_ref[...] = jnp.concatenate([jnp.cos(ang), jnp.sin(ang)], axis=1)


def _rope_table(pos_col, inv):
    m = pos_col.shape[0]
    tm = 512
    return pl.pallas_call(
        _rope_table_kernel,
        grid=(m // tm,),
        in_specs=[pl.BlockSpec((tm, 1), lambda i: (i, 0)),
                  pl.BlockSpec((1, MLA_ROPE), lambda i: (0, 0))],
        out_specs=pl.BlockSpec((tm, 2 * MLA_ROPE), lambda i: (i, 0)),
        out_shape=jax.ShapeDtypeStruct((m, 2 * MLA_ROPE), F32),
        compiler_params=_params("parallel"),
        name="rope_table",
    )(pos_col, inv)


def _ffn_kernel(x_ref, u_ref, post_g_ref, next_g_ref, wg_ref, wu_ref, wd_ref,
                o_ref, un_ref, acc_ref):
    j = pl.program_id(1)

    @pl.when(j == 0)
    def _():
        acc_ref[...] = jnp.zeros(acc_ref.shape, F32)

    u = u_ref[...]
    gate = _dot(u, wg_ref[...])
    up = _dot(u, wu_ref[...])
    hidden = (gate * jax.nn.sigmoid(gate) * up).astype(BF16)
    acc_ref[...] += _dot(hidden, wd_ref[...])

    @pl.when(j == pl.num_programs(1) - 1)
    def _():
        out = x_ref[...] + 0.5 * _rms(acc_ref[...], post_g_ref[...])
        o_ref[...] = out
        un_ref[...] = _rms(out, next_g_ref[...]).astype(BF16)


def _ffn(x, u, post_g, next_g, next_layer, wg, wu, wd, layer):
    m, d = x.shape
    f = wg.shape[-1]
    tm, tf = FFN_TM, _tile_n(f, FFN_TF)
    return pl.pallas_call(
        _ffn_kernel,
        grid=(m // tm, f // tf),
        in_specs=[pl.BlockSpec((tm, d), lambda i, j: (i, 0)),
                  pl.BlockSpec((tm, d), lambda i, j: (i, 0)),
                  pl.BlockSpec((None, 1, d), lambda i, j: (layer, 0, 0)),
                  pl.BlockSpec((None, 1, d), lambda i, j: (next_layer, 0, 0)),
                  pl.BlockSpec((None, d, tf), lambda i, j: (layer, 0, j)),
                  pl.BlockSpec((None, d, tf), lambda i, j: (layer, 0, j)),
                  pl.BlockSpec((None, tf, d), lambda i, j: (layer, j, 0))],
        out_specs=[pl.BlockSpec((tm, d), lambda i, j: (i, 0)),
                   pl.BlockSpec((tm, d), lambda i, j: (i, 0))],
        out_shape=[jax.ShapeDtypeStruct((m, d), F32), jax.ShapeDtypeStruct((m, d), BF16)],
        scratch_shapes=[pltpu.VMEM((tm, d), F32)],
        compiler_params=_params("parallel", "arbitrary"),
        name="ffn",
    )(x, u, post_g, next_g, wg, wu, wd)


def _matmul_kernel(x_ref, w_ref, o_ref):
    o_ref[...] = _dot(x_ref[...], w_ref[...]).astype(o_ref.dtype)


def _in_proj(u, w, layer):
    m, d = u.shape
    n = w.shape[-1]
    tm, tn = PROJ_TM, _tile_n(n, PROJ_TN_CAP)
    return pl.pallas_call(
        _matmul_kernel,
        grid=(n // tn, m // tm),
        in_specs=[pl.BlockSpec((tm, d), lambda j, i: (i, 0)),
                  pl.BlockSpec((None, d, tn), lambda j, i: (layer, 0, j))],
        out_specs=pl.BlockSpec((tm, tn), lambda j, i: (i, j)),
        out_shape=jax.ShapeDtypeStruct((m, n), BF16),
        compiler_params=_params("parallel", "parallel"),
        name="in_proj",
    )(u, w)


def _matmul_t_kernel(w_ref, x_ref, o_ref):
    o_ref[...] = _dot_nt(w_ref[...], x_ref[...]).astype(o_ref.dtype)


def _in_proj_t(u, w_t, layer):
    m, d = u.shape
    n = w_t.shape[1]
    tm = PROJ_TM
    return pl.pallas_call(
        _matmul_t_kernel,
        grid=(m // tm,),
        in_specs=[pl.BlockSpec((None, n, d), lambda i: (layer, 0, 0)),
                  pl.BlockSpec((tm, d), lambda i: (i, 0))],
        out_specs=pl.BlockSpec((n, tm), lambda i: (0, i)),
        out_shape=jax.ShapeDtypeStruct((n, m), BF16),
        compiler_params=_params("parallel"),
        name="in_proj_t",
    )(w_t, u)


def _softmax_tile_t(s_ref, adds, masks, v_t, m_ref, acc_ref):
    probs, alphas = [], []
    for c in range(s_ref.shape[1] // LANE):
        cols = slice(c * LANE, (c + 1) * LANE)
        s = s_ref[:, cols]
        for mask in masks[c]:
            s = jnp.where(mask, s, MASKED) if mask.dtype == jnp.bool_ else s + mask
        m_old = m_ref[:, cols]
        m_tile = jnp.max(s, axis=0, keepdims=True)
        if adds[c] is not None:
            m_tile = m_tile + adds[c]
        m_new = jnp.maximum(m_old, m_tile)
        alpha = jnp.exp2(m_old - m_new)
        p = jnp.exp2(s - (m_new if adds[c] is None else m_new - adds[c]))
        m_ref[:, cols] = m_new
        probs.append(p.astype(BF16))
        alphas.append(alpha)
    p_t = probs[0] if len(probs) == 1 else jnp.concatenate(probs, axis=1)
    alpha = alphas[0] if len(alphas) == 1 else jnp.concatenate(alphas, axis=1)
    v_ones = jnp.concatenate([v_t, jnp.ones((ONES_ROWS, v_t.shape[1]), BF16)], axis=0)
    acc_ref[...] = alpha * acc_ref[...] + _dot(v_ones, p_t)


def _softmax_init(m_ref, acc_ref):
    m_ref[...] = jnp.full(m_ref.shape, NEG, F32)
    acc_ref[...] = jnp.zeros(acc_ref.shape, F32)


def _softmax_out(acc):
    dv = acc.shape[0] - ONES_ROWS
    return acc[:dv] / acc[dv:dv + 1]


def _mla_prep_kernel(cq_ref, ckv_ref, kr_ref, cs_ref, gq_ref, gkv_ref, wq_ref, wuk_ref, wuvt_ref,
                     q_ref, k_ref, vt_ref):
    cs = cs_ref[...]
    qn = _rms(cq_ref[...].astype(F32), gq_ref[...]).astype(BF16)
    q = _dot(qn, wq_ref[...])
    kvn = _rms(ckv_ref[...].astype(F32), gkv_ref[...]).astype(BF16)
    k_nope = _dot(kvn, wuk_ref[...])
    t = kr_ref[...].astype(F32) * cs
    k_pe = (t + pltpu.roll(t, MLA_ROPE, 1)).astype(BF16)
    for h in range(MLA_HEADS):
        lo = h * HEAD_PAD
        q_ref[:, lo:lo + MLA_NOPE] = q[:, lo:lo + MLA_NOPE].astype(BF16)
        q_ref[:, lo + MLA_NOPE:lo + HEAD_PAD] = (q[:, lo + MLA_NOPE:lo + HEAD_PAD] * cs).astype(BF16)
        k_ref[:, lo:lo + MLA_NOPE] = k_nope[:, h * MLA_NOPE:(h + 1) * MLA_NOPE].astype(BF16)
        k_ref[:, lo + MLA_NOPE:lo + HEAD_PAD] = k_pe
    vt_ref[...] = _dot_nt(wuvt_ref[...], kvn).astype(BF16)


def _mla_prep(z, cs, gq, gkv, wq, wuk, wuv_t, layer, zoff):
    m = z.shape[0]
    tm = PREP_TM
    hq = MLA_HEADS * HEAD_PAD
    hv = MLA_HEADS * MLA_V
    const = lambda i: (layer, 0, 0)
    return pl.pallas_call(
        _mla_prep_kernel,
        grid=(m // tm,),
        in_specs=[pl.BlockSpec((tm, MLA_Q_LORA), lambda i: (i, zoff["cq"] // MLA_Q_LORA)),
                  pl.BlockSpec((tm, MLA_KV_LORA), lambda i: (i, zoff["ckv"] // MLA_KV_LORA)),
                  pl.BlockSpec((tm, LANE), lambda i: (i, zoff["kr"] // LANE)),
                  pl.BlockSpec((tm, LANE), lambda i: (i, 0)),
                  pl.BlockSpec((None, 1, MLA_Q_LORA), const),
                  pl.BlockSpec((None, 1, MLA_KV_LORA), const),
                  pl.BlockSpec((None, MLA_Q_LORA, hq), const),
                  pl.BlockSpec((None, MLA_KV_LORA, MLA_HEADS * MLA_NOPE), const),
                  pl.BlockSpec((None, hv, MLA_KV_LORA), const)],
        out_specs=[pl.BlockSpec((tm, hq), lambda i: (i, 0)),
                   pl.BlockSpec((tm, hq), lambda i: (i, 0)),
                   pl.BlockSpec((hv, tm), lambda i: (0, i))],
        out_shape=[jax.ShapeDtypeStruct((m, hq), BF16),
                   jax.ShapeDtypeStruct((m, hq), BF16),
                   jax.ShapeDtypeStruct((hv, m), BF16)],
        compiler_params=_params("parallel"),
        name="mla_prep",
    )(z, z, z, cs, gq, gkv, wq, wuk, wuv_t)


def _by_parity(kt, fn):
    @pl.when(kt % 2 == 0)
    def _():
        fn(0)

    @pl.when(kt % 2 == 1)
    def _():
        fn(1)


def _mla_attn_kernel(q_ref, k_ref, vt_ref, o_ref, s_ref, m_ref, acc_ref, *, tq, tk, heads):
    i = pl.program_id(2)
    nc = tq // LANE
    _softmax_init(m_ref, acc_ref)
    qt = i * tq + lax.broadcasted_iota(jnp.int32, (1, tq), 1)
    kcol = lax.broadcasted_iota(jnp.int32, (tk, 1), 0)
    qk = [slice(h * HEAD_PAD, (h + 1) * HEAD_PAD) for h in range(heads)]
    vd = [slice(h * MLA_V, (h + 1) * MLA_V) for h in range(heads)]
    n_full = (i * tq) // tk
    n_all = (i * tq + tq + tk - 1) // tk

    def logits(kt, buf):
        ks = pl.multiple_of(kt * tk, tk)
        for h in range(heads):
            s_ref[buf, h] = _dot_nt(k_ref[pl.ds(ks, tk), qk[h]], q_ref[:, qk[h]])

    def step(kt, buf, masked, prefetch=True):
        ks = pl.multiple_of(kt * tk, tk)
        masks = [()] * nc
        if masked:
            masks = [(ks + kcol <= qt[:, c * LANE:(c + 1) * LANE],) for c in range(nc)]
        for h in range(heads):
            _softmax_tile_t(s_ref.at[buf, h], [None] * nc, masks, vt_ref[vd[h], pl.ds(ks, tk)],
                            m_ref.at[h], acc_ref.at[h])
        if prefetch:
            logits(kt + 1, 1 - buf)

    def full_body(kt, c):
        _by_parity(kt, lambda buf: step(kt, buf, False))
        return c

    def diag_body(kt, c):
        _by_parity(kt, lambda buf: step(kt, buf, True))
        return c

    logits(0, 0)
    lax.fori_loop(0, n_full, full_body, 0)
    lax.fori_loop(n_full, n_all - 1, diag_body, 0)
    _by_parity(n_all - 1, lambda buf: step(n_all - 1, buf, True, prefetch=False))
    for h in range(heads):
        o_ref[:, vd[h]] = _softmax_out(acc_ref[h]).T.astype(o_ref.dtype)


def _mla_attn(qp, kp, vt, batch, seq):
    tq, tk, heads = MLA_TQ, MLA_TK, MLA_HEADS_PER_STEP
    nq = seq // tq
    kern = functools.partial(_mla_attn_kernel, tq=tq, tk=tk, heads=heads)
    return pl.pallas_call(
        kern,
        grid=(batch, MLA_HEADS // heads, nq),
        in_specs=[pl.BlockSpec((tq, heads * HEAD_PAD), lambda b, h, i: (b * nq + i, h)),
                  pl.BlockSpec((seq, heads * HEAD_PAD), lambda b, h, i: (b, h)),
                  pl.BlockSpec((heads * MLA_V, seq), lambda b, h, i: (h, b))],
        out_specs=pl.BlockSpec((tq, heads * MLA_V), lambda b, h, i: (b * nq + i, h)),
        out_shape=jax.ShapeDtypeStruct((batch * seq, MLA_HEADS * MLA_V), BF16),
        scratch_shapes=[pltpu.VMEM((2, heads, tk, tq), F32), pltpu.VMEM((heads, 1, tq), F32),
                        pltpu.VMEM((heads, MLA_V + ONES_ROWS, tq), F32)],
        compiler_params=_params("parallel", "parallel", "parallel"),
        name="mla_attn",
    )(qp, kp, vt)


def _compress(a_ref, pe_ref, w1_ref, w2_ref):
    a = a_ref[...].astype(F32)
    half = a.shape[1]
    first = _dot((a + pe_ref[0:1, :]).astype(BF16), w1_ref[0:half, :])
    second = _dot((a + pe_ref[1:2, :]).astype(BF16), w1_ref[half:2 * half, :])
    n = a.shape[0]
    hidden = first + pltpu.roll(second, n - 1, 0)
    return _dot((hidden * jax.nn.sigmoid(hidden)).astype(BF16), w2_ref[...])


def _nsa_cmp_kernel(ak_ref, av_ref, pek_ref, pev_ref, w1k_ref, w2k_ref, w1v_ref, w2v_ref,
                    kc_ref, vct_ref):
    kc_ref[...] = _compress(ak_ref, pek_ref, w1k_ref, w2k_ref).astype(BF16)
    vct_ref[...] = _compress(av_ref, pev_ref, w1v_ref, w2v_ref).T.astype(BF16)


def _nsa_cmp(ak, av, pek, pev, w1k, w2k, w1v, w2v, layer):
    bg, nh, dk16 = ak.shape
    dv16 = av.shape[-1]
    const = lambda i: (layer, 0, 0)
    return pl.pallas_call(
        _nsa_cmp_kernel,
        grid=(bg,),
        in_specs=[pl.BlockSpec((None, nh, dk16), lambda i: (i, 0, 0)),
                  pl.BlockSpec((None, nh, dv16), lambda i: (i, 0, 0)),
                  pl.BlockSpec((None, 2, dk16), const),
                  pl.BlockSpec((None, 2, dv16), const),
                  pl.BlockSpec((None, 2 * dk16, CMP_HIDDEN), const),
                  pl.BlockSpec((None, CMP_HIDDEN, HEAD_PAD), const),
                  pl.BlockSpec((None, 2 * dv16, CMP_HIDDEN), const),
                  pl.BlockSpec((None, CMP_HIDDEN, NSA_DV), const)],
        out_specs=[pl.BlockSpec((None, nh, HEAD_PAD), lambda i: (i, 0, 0)),
                   pl.BlockSpec((None, NSA_DV, nh), lambda i: (i, 0, 0))],
        out_shape=[jax.ShapeDtypeStruct((bg, nh, HEAD_PAD), BF16),
                   jax.ShapeDtypeStruct((bg, NSA_DV, nh), BF16)],
        compiler_params=_params("parallel"),
        name="nsa_cmp",
    )(ak, av, pek, pev, w1k, w2k, w1v, w2v)


def _split3(x):
    a = x.astype(BF16)
    r = x - a.astype(F32)
    b = r.astype(BF16)
    c = (r - b.astype(F32)).astype(BF16)
    return a, b, c


def _lookup(table_row, idx):
    rows, width = idx.shape
    table = jnp.broadcast_to(table_row, (rows, LANE))
    chunks = [jnp.take_along_axis(table, idx[:, c:c + LANE], axis=1, mode="promise_in_bounds")
              for c in range(0, width, LANE)]
    return chunks[0] if len(chunks) == 1 else jnp.concatenate(chunks, axis=1)


def _nsa_attn_kernel(far_ref, q_ref, kc_ref, vct_ref, ks_ref, vst_ref, kw_ref, vwt_ref, ng_ref,
                     prow_ref, pcol_ref, pcmp_ref, tbl_ref, o_ref,
                     s_ref, m_ref, acc_ref, *, tq, tk, seq):
    b, i = pl.program_id(0), pl.program_id(2)
    nq, nk = seq // tq, seq // tk
    J = NSA_GROUP
    n_slc = seq // SLC_LEN
    n_cmp = (seq - CMP_LEN) // CMP_STRIDE + 1
    t0 = pl.multiple_of(i * tq, tq)

    qs = jnp.concatenate([q_ref[:, j * HEAD_PAD:(j + 1) * HEAD_PAD] for j in range(J)], axis=0)
    qpos = prow_ref[:, pl.ds(t0, tq)]
    qt = t0 + lax.broadcasted_iota(jnp.int32, (1, tq), 1)
    tbl = tbl_ref[...]
    far_bias = [tbl[j:j + 1, LANE - 1:LANE] for j in range(J)]
    head = [slice(j * tq, (j + 1) * tq) for j in range(J)]

    blk = lax.broadcasted_iota(jnp.int32, (LANE, 1), 0)
    keep_c = (blk * CMP_STRIDE + (CMP_LEN - 1) <= qt) & (blk < n_cmp)
    idx_c = jnp.clip(qpos - pcmp_ref[...], 0, LANE - 1)
    s_c = _dot_nt(kc_ref[...], qs)
    p_heads = []
    for j in range(J):
        s = jnp.where(keep_c, s_c[:, head[j]] + _lookup(tbl[j:j + 1, :], idx_c), NEG)
        e = jnp.where(keep_c, jnp.exp2(s - jnp.max(s, axis=0, keepdims=True)), 0.0)
        den = jnp.sum(e, axis=0, keepdims=True)
        p_heads.append(e / jnp.where(den > 0.0, den, 1.0))
    o_cmp = _dot(vct_ref[...], jnp.concatenate([p.astype(BF16) for p in p_heads], axis=1))

    rows = 32
    assert n_slc <= rows
    m_row = lax.broadcasted_iota(jnp.int32, (rows, LANE), 0)
    n_col = lax.broadcasted_iota(jnp.int32, (rows, LANE), 1)
    per = SLC_LEN // CMP_STRIDE
    back = (CMP_LEN - 1) // CMP_STRIDE
    overlap = ((n_col >= per * m_row - back) & (n_col <= per * m_row + per - 1)
               & (n_col < n_cmp) & (m_row < n_slc)).astype(BF16)
    imp = sum(_dot(overlap, part) for part in _split3(sum(p_heads)))
    m_blk = lax.broadcasted_iota(jnp.int32, (rows, 1), 0)
    cur = qt // SLC_LEN
    valid = m_blk <= cur
    forced = valid & ((m_blk == 0) | (m_blk >= cur - 1))
    score = jnp.where(forced, FORCED_SCORE, jnp.where(valid, imp, -1.0))
    rank = jnp.zeros((rows, tq), jnp.int32)
    for mp in range(n_slc):
        other = score[mp:mp + 1, :]
        ahead = (other > score) | ((other == score) & (m_blk > mp))
        rank = rank + ahead.astype(jnp.int32)
    selected = valid & (rank < min(SLC_TOPN, n_slc))
    sel_penalty = jnp.where(selected, 0.0, MASKED).astype(BF16)

    SLC, WIN = 0, 1
    _softmax_init(m_ref, acc_ref)
    sub = tk // LANE

    qchunks = [slice(c, c + LANE) for c in range(0, tq, LANE)]

    assert tq == tk

    def slc_masks(kidx, diagonal):
        expand = (kidx // SLC_LEN == lax.broadcasted_iota(jnp.int32, (1, rows), 1)).astype(BF16)
        penalty = _dot(expand, sel_penalty)
        if diagonal:
            return [(penalty[:, c], kidx <= qt[:, c]) for c in qchunks]
        return [(penalty[:, c],) for c in qchunks]

    def win_masks(kidx, diagonal):
        return [(jnp.where(lax.bitcast_convert_type(qt[:, c] - kidx, jnp.uint32) < WINDOW,
                           0.0, MASKED),) for c in qchunks]

    k_refs, vt_refs, masks_of = (ks_ref, kw_ref), (vst_ref, vwt_ref), (slc_masks, win_masks)
    adds = [far_bias[j] for j in range(J) for _ in qchunks]
    last = (t0 + tq + tk - 1) // tk
    first_win = jnp.maximum(t0 - (WINDOW - 1), 0) // tk

    def logits(kt, buf, slots):
        ks = pl.multiple_of(kt * tk, tk)
        for slot in slots:
            s_ref[buf, slot] = _dot_nt(k_refs[slot][pl.ds(ks, tk), :], qs)

    def near_bias(kt, buf, slots):
        ks = pl.multiple_of(kt * tk, tk)
        for hb in range(sub):
            kb = kt * sub + hb
            for qc, qcols in enumerate(qchunks):
                near = ((far_ref[((b * nq + i) * len(qchunks) + qc) * (nk * sub) + kb] == 0)
                        & (kb * LANE < t0 + (qc + 1) * LANE))

                @pl.when(near)
                def _():
                    kpos = pcol_ref[pl.ds(pl.multiple_of(ks + hb * LANE, LANE), LANE), :]
                    idx = jnp.clip(qpos[:, qcols] - kpos, 0, LANE - 1)
                    for j in range(J):
                        delta = _lookup(tbl[j:j + 1, :] - far_bias[j], idx)
                        cols = slice(j * tq + qc * LANE, j * tq + (qc + 1) * LANE)
                        for slot in slots:
                            s_ref[buf, slot, hb * LANE:(hb + 1) * LANE, cols] += delta

    def step(kt, buf, slots, final=False):
        ks = pl.multiple_of(kt * tk, tk)
        kidx = ks + lax.broadcasted_iota(jnp.int32, (tk, 1), 0)
        for slot in slots:
            _softmax_tile_t(s_ref.at[buf, slot], adds, masks_of[slot](kidx, final) * J,
                            vt_refs[slot][:, pl.ds(ks, tk)], m_ref.at[slot], acc_ref.at[slot])
        if not final:
            logits(kt + 1, 1 - buf, slots)
            near_bias(kt + 1, 1 - buf, slots)

    def slc_body(kt, c):
        _by_parity(kt, lambda buf: step(kt, buf, (SLC,)))
        return c

    def both_body(kt, c):
        _by_parity(kt, lambda buf: step(kt, buf, (SLC, WIN)))
        return c

    def enter_window(buf):
        logits(first_win, buf, (WIN,))
        near_bias(first_win, buf, (WIN,))

    logits(0, 0, (SLC,))
    near_bias(0, 0, (SLC,))
    lax.fori_loop(0, first_win, slc_body, 0)
    _by_parity(first_win, enter_window)
    lax.fori_loop(first_win, last - 1, both_body, 0)
    _by_parity(last - 1, lambda buf: step(last - 1, buf, (SLC, WIN), final=True))
    o_slc = _softmax_out(acc_ref[SLC])
    o_win = _softmax_out(acc_ref[WIN])

    gates = jax.nn.sigmoid(ng_ref[...].astype(F32)).T
    for j in range(J):
        o = (gates[3 * j:3 * j + 1, :] * o_cmp[:, head[j]]
             + gates[3 * j + 1:3 * j + 2, :] * o_slc[:, head[j]]
             + gates[3 * j + 2:3 * j + 3, :] * o_win[:, head[j]])
        o_ref[:, j * NSA_DV:(j + 1) * NSA_DV] = o.T.astype(o_ref.dtype)


def _nsa_attn(far, z, kc, vct, v_t, pos_row, pos_col, pos_cmp, tbl, batch, seq, zoff):
    tq, tk = NSA_TQ, NSA_TK
    nq = seq // tq
    G, J = NSA_KV_HEADS, NSA_GROUP
    R = J * tq
    kern = functools.partial(_nsa_attn_kernel, tq=tq, tk=tk, seq=seq)
    grid_spec = pltpu.PrefetchScalarGridSpec(
        num_scalar_prefetch=1,
        grid=(batch, G, nq),
        in_specs=[
            pl.BlockSpec((tq, J * HEAD_PAD), lambda b, g, i, far: (b * nq + i, g)),
            pl.BlockSpec((None, LANE, HEAD_PAD), lambda b, g, i, far: (b * G + g, 0, 0)),
            pl.BlockSpec((None, NSA_DV, LANE), lambda b, g, i, far: (b * G + g, 0, 0)),
            pl.BlockSpec((seq, HEAD_PAD), lambda b, g, i, far: (b, zoff["ks"] // HEAD_PAD + g)),
            pl.BlockSpec((NSA_DV, seq), lambda b, g, i, far: (g, b)),
            pl.BlockSpec((seq, HEAD_PAD), lambda b, g, i, far: (b, zoff["kw"] // HEAD_PAD + g)),
            pl.BlockSpec((NSA_DV, seq), lambda b, g, i, far: (G + g, b)),
            pl.BlockSpec((tq, LANE), lambda b, g, i, far: (b * nq + i, zoff["ng"] // LANE + g)),
            pl.BlockSpec((None, 1, seq), lambda b, g, i, far: (b, 0, 0)),
            pl.BlockSpec((seq, 1), lambda b, g, i, far: (b, 0)),
            pl.BlockSpec((LANE, 1), lambda b, g, i, far: (b, 0)),
            pl.BlockSpec((None, J, LANE), lambda b, g, i, far: (g, 0, 0)),
        ],
        out_specs=pl.BlockSpec((tq, J * NSA_DV), lambda b, g, i, far: (b * nq + i, g)),
        scratch_shapes=[pltpu.VMEM((2, 2, tk, R), F32), pltpu.VMEM((2, 1, R), F32),
                        pltpu.VMEM((2, NSA_DV + ONES_ROWS, R), F32)],
    )
    return pl.pallas_call(
        kern,
        grid_spec=grid_spec,
        out_shape=jax.ShapeDtypeStruct((batch * seq, NSA_HEADS * NSA_DV), BF16),
        compiler_params=_params("parallel", "parallel", "parallel"),
        name="nsa_attn",
    )(far, z, kc, vct, z, v_t, z, v_t, z, pos_row, pos_col, pos_cmp, tbl)


def _merge_kernel(a_ref, b_ref, ga_ref, gb_ref, h_ref, wa_ref, wb_ref, wo_ref, post_g_ref,
                  next_g_ref, o_ref, un_ref):
    ya = _dot(a_ref[...], wa_ref[...])
    yb = _dot(b_ref[...], wb_ref[...])
    m = (jax.nn.sigmoid(ga_ref[...].astype(F32)) * ya
         + jax.nn.sigmoid(gb_ref[...].astype(F32)) * yb).astype(BF16)
    y = _dot(m, wo_ref[...])
    out = h_ref[...] + _rms(y, post_g_ref[...])
    o_ref[...] = out
    un_ref[...] = _rms(out, next_g_ref[...]).astype(BF16)


def _merge(a, bb, z, h, wa, wb, wo, post_g, next_g, layer, zoff):
    m, d = h.shape
    tm = MERGE_TM
    da, db = a.shape[1], bb.shape[1]
    gblk = zoff["mg"] // d
    const = lambda i: (layer, 0, 0)
    return pl.pallas_call(
        _merge_kernel,
        grid=(m // tm,),
        in_specs=[pl.BlockSpec((tm, da), lambda i: (i, 0)),
                  pl.BlockSpec((tm, db), lambda i: (i, 0)),
                  pl.BlockSpec((tm, d), lambda i: (i, gblk)),
                  pl.BlockSpec((tm, d), lambda i: (i, gblk + 1)),
                  pl.BlockSpec((tm, d), lambda i: (i, 0)),
                  pl.BlockSpec((None, da, d), const, pipeline_mode=pl.Buffered(1)),
                  pl.BlockSpec((None, db, d), const, pipeline_mode=pl.Buffered(1)),
                  pl.BlockSpec((None, d, d), const, pipeline_mode=pl.Buffered(1)),
                  pl.BlockSpec((None, 1, d), const),
                  pl.BlockSpec((None, 1, d), const)],
        out_specs=[pl.BlockSpec((tm, d), lambda i: (i, 0)),
                   pl.BlockSpec((tm, d), lambda i: (i, 0))],
        out_shape=[jax.ShapeDtypeStruct((m, d), F32), jax.ShapeDtypeStruct((m, d), BF16)],
        compiler_params=_params("parallel"),
        name="merge_out",
    )(a, bb, z, z, h, wa, wb, wo, post_g, next_g)


def _z_layout(d_model):
    G = NSA_KV_HEADS
    widths = [("q", NSA_HEADS * HEAD_PAD), ("mg", 2 * d_model), ("cq", MLA_Q_LORA),
              ("ckv", MLA_KV_LORA), ("kc", G * HEAD_PAD), ("ks", G * HEAD_PAD),
              ("kw", G * HEAD_PAD), ("vc", G * NSA_DV), ("kr", LANE), ("ng", G * LANE)]
    off, pos = {}, 0
    for name, w in widths:
        off[name] = pos
        pos += w
    off["used"] = pos
    off["total"] = -(-pos // Z_PAD) * Z_PAD
    assert off["q"] == 0 and off["mg"] % d_model == 0
    assert off["cq"] % MLA_Q_LORA == 0 and off["ckv"] % MLA_KV_LORA == 0
    assert all(off[k] % HEAD_PAD == 0 for k in ("kc", "ks", "kw"))
    return off


def _swap_halves(w):
    half = w.shape[-1] // 2
    return jnp.concatenate([-w[..., half:], w[..., :half]], axis=-1)


def _pad_last(w, width):
    return jnp.pad(w, [(0, 0)] * (w.ndim - 1) + [(0, width - w.shape[-1])])


def _w_in_pieces(d_model, zoff):
    G, J = NSA_KV_HEADS, NSA_GROUP
    splits = [MLA_Q_LORA, MLA_KV_LORA, MLA_ROPE, NSA_HEADS * NSA_DK,
              G * NSA_DK, G * NSA_DV, G * NSA_DK, G * NSA_DV, G * NSA_DK, G * NSA_DV,
              NSA_HEADS * 3, 2 * d_model]
    names = ["cq", "ckv", "kr", "q", "kc", "vc", "ks", "vs", "kw", "vw", "ng", "mg"]
    src = dict(zip(names, [0] + [int(v) for v in np.cumsum(splits)[:-1]]))
    half = MLA_ROPE // 2
    pieces = []
    for h in range(NSA_HEADS):
        pieces.append((zoff["q"] + h * HEAD_PAD, HEAD_PAD,
                       [(src["q"] + h * NSA_DK, NSA_DK, NSA_DK ** -0.5 * LOG2E)]))
    pieces.append((zoff["mg"], 2 * d_model, [(src["mg"], 2 * d_model, None)]))
    pieces.append((zoff["cq"], MLA_Q_LORA, [(src["cq"], MLA_Q_LORA, None)]))
    pieces.append((zoff["ckv"], MLA_KV_LORA, [(src["ckv"], MLA_KV_LORA, None)]))
    for name in ("kc", "ks", "kw"):
        for g in range(G):
            pieces.append((zoff[name] + g * HEAD_PAD, HEAD_PAD, [(src[name] + g * NSA_DK, NSA_DK, None)]))
    pieces.append((zoff["vc"], G * NSA_DV, [(src["vc"], G * NSA_DV, None)]))
    pieces.append((zoff["kr"], LANE, [(src["kr"], MLA_ROPE, None), (src["kr"] + half, half, -1.0),
                                      (src["kr"], half, None)]))
    for g in range(G):
        pieces.append((zoff["ng"] + g * LANE, LANE, [(src["ng"] + g * J * 3, J * 3, None)]))
    if zoff["total"] > zoff["used"]:
        pieces.append((zoff["used"], zoff["total"] - zoff["used"], []))
    return pieces, (src["vs"], src["vw"])


def _w_in_layout_kernel(w_ref, o_ref, vt_ref, *, pieces, value_cols):
    rows = w_ref.shape[0]
    values = jnp.concatenate([w_ref[:, c:c + w] for c, w in value_cols], axis=1)
    vt_ref[...] = values.T.astype(BF16)
    for dst, width, parts in pieces:
        cols = []
        for src, w, scale in parts:
            x = w_ref[:, src:src + w]
            cols.append(x if scale is None else x * scale)
        used = sum(w for _, w, _ in parts)
        if used < width:
            cols.append(jnp.zeros((rows, width - used), F32))
        x = cols[0] if len(cols) == 1 else jnp.concatenate(cols, axis=1)
        o_ref[:, dst:dst + width] = x.astype(BF16)


def _layout_w_in(w_in, zoff):
    L, D, n_in = w_in.shape
    G = NSA_KV_HEADS
    pieces, (vs, vw) = _w_in_pieces(D, zoff)
    tr = 256
    nv = 2 * G * NSA_DV
    value_cols = ((vs, G * NSA_DV), (vw, G * NSA_DV))
    return pl.pallas_call(
        functools.partial(_w_in_layout_kernel, pieces=pieces, value_cols=value_cols),
        grid=(L, D // tr),
        in_specs=[pl.BlockSpec((None, tr, n_in), lambda l, r: (l, r, 0))],
        out_specs=[pl.BlockSpec((None, tr, zoff["total"]), lambda l, r: (l, r, 0)),
                   pl.BlockSpec((None, nv, tr), lambda l, r: (l, 0, r))],
        out_shape=[jax.ShapeDtypeStruct((L, D, zoff["total"]), BF16),
                   jax.ShapeDtypeStruct((L, nv, D), BF16)],
        compiler_params=_params("parallel", "parallel"),
        name="w_in_layout",
    )(w_in)


def _layout_w_q_up(w):
    L, r, _ = w.shape
    w = w.reshape(L, r, MLA_HEADS, MLA_NOPE + MLA_ROPE) * ((MLA_NOPE + MLA_ROPE) ** -0.5 * LOG2E)
    rope = w[..., MLA_NOPE:]
    return jnp.concatenate([w[..., :MLA_NOPE], rope, _swap_halves(rope)], axis=-1).reshape(
        L, r, MLA_HEADS * HEAD_PAD).astype(BF16)


def kernel(x, positions, rel_bias, ffn1_pre_g, ffn1_post_g, ffn1_w_gate, ffn1_w_up, ffn1_w_down, mix_pre_g, mix_post_g, w_in, mla_q_norm_g, mla_w_q_up, mla_kv_norm_g, mla_w_uk, mla_w_uv, cmp_pe_k, cmp_w1_k, cmp_w2_k, cmp_pe_v, cmp_w1_v, cmp_w2_v, w_branch_mla, w_branch_nsa, w_out, ffn2_pre_g, ffn2_post_g, ffn2_w_gate, ffn2_w_up, ffn2_w_down):
    B, S, D = x.shape
    L = w_in.shape[0]
    M = B * S
    G, J = NSA_KV_HEADS, NSA_GROUP
    zoff = _z_layout(D)
    n_half = S // CMP_STRIDE
    n_cmp = (S - CMP_LEN) // CMP_STRIDE + 1
    assert n_half == LANE and n_cmp <= LANE

    gain = lambda g: g.reshape(L, 1, -1)
    bf = lambda w: w.astype(BF16)
    w_z, w_vt = _layout_w_in(w_in, zoff)
    w_q = _layout_w_q_up(mla_w_q_up)
    w_uk, w_uv_t = bf(mla_w_uk), bf(mla_w_uv).transpose(0, 2, 1)
    pe_k = cmp_pe_k.reshape(L, 2, CMP_STRIDE * NSA_DK)
    pe_v = cmp_pe_v.reshape(L, 2, CMP_STRIDE * NSA_DV)
    w2_k = bf(_pad_last(cmp_w2_k, HEAD_PAD))
    f1 = (bf(ffn1_w_gate), bf(ffn1_w_up), bf(ffn1_w_down))
    f2 = (bf(ffn2_w_gate), bf(ffn2_w_up), bf(ffn2_w_down))
    w1_k, w1_v, w2_v = bf(cmp_w1_k), bf(cmp_w1_v), bf(cmp_w2_v)
    w_a, w_b, w_o = bf(w_branch_mla), bf(w_branch_nsa), bf(w_out)
    g_f1pre, g_f1post, g_mpre, g_mpost = gain(ffn1_pre_g), gain(ffn1_post_g), gain(mix_pre_g), gain(mix_post_g)
    g_f2pre, g_f2post, g_q, g_kv = gain(ffn2_pre_g), gain(ffn2_post_g), gain(mla_q_norm_g), gain(mla_kv_norm_g)

    pos_col = positions.reshape(M, 1)
    pos_row = positions.reshape(B, 1, S)
    pos_cmp = _pad_last(positions[:, CMP_LEN - 1::CMP_STRIDE][:, :n_cmp], LANE).reshape(B * LANE, 1)
    q_min = positions.reshape(B, S // LANE, LANE).min(axis=-1)
    k_max = positions.reshape(B, S // LANE, LANE).max(axis=-1)
    far = (q_min[:, :, None] - k_max[:, None, :] >= FAR_DIST).astype(jnp.int32).reshape(-1)
    tbl = (jnp.take(rel_bias, jnp.asarray(BUCKET_OF_DIST), axis=0).T * LOG2E).reshape(G, J, LANE)
    half = MLA_ROPE // 2
    inv = ROPE_BASE ** (-jnp.arange(half, dtype=F32) * 2.0 / MLA_ROPE)
    cs = _rope_table(pos_col, jnp.concatenate([inv, inv]).reshape(1, MLA_ROPE))

    h = x.reshape(M, D)
    u = _rmsnorm(h, g_f1pre, 0)
    for l in range(L):
        h, u = _ffn(h, u, g_f1post, g_mpre, l, *f1, l)
        z = _in_proj(u, w_z, l)
        v_t = _in_proj_t(u, w_vt, l)
        qp, kp, vt = _mla_prep(z, cs, g_q, g_kv, w_q, w_uk, w_uv_t, l, zoff)
        a = _mla_attn(qp, kp, vt, B, S)

        def half_blocks(off, width, d):
            t = z[:, off:off + G * width].reshape(B, S, G, width)[..., :d]
            return t.transpose(0, 2, 1, 3).reshape(B * G, n_half, CMP_STRIDE * d)

        kc, vct = _nsa_cmp(half_blocks(zoff["kc"], HEAD_PAD, NSA_DK),
                           half_blocks(zoff["vc"], NSA_DV, NSA_DV),
                           pe_k, pe_v, w1_k, w2_k, w1_v, w2_v, l)
        nsa = _nsa_attn(far, z, kc, vct, v_t, pos_row, pos_col, pos_cmp, tbl, B, S, zoff)
        h, u = _merge(a, nsa, z, h, w_a, w_b, w_o, g_mpost, g_f2pre, l, zoff)
        h, u = _ffn(h, u, g_f2post, g_f1pre, min(l + 1, L - 1), *f2, l)
    return h.reshape(B, S, D)
```

```python
import functools
import math

import numpy as np
import jax
import jax.numpy as jnp
from jax import lax
from jax.experimental import pallas as pl
from jax.experimental.pallas import tpu as pltpu

EPS = 1e-6
MLA_HEADS = 8
MLA_Q_LORA = 512
MLA_KV_LORA = 512
MLA_NOPE = 128
MLA_ROPE = 64
MLA_V = 128
ROPE_BASE = 10000.0
NSA_HEADS = 8
NSA_KV_HEADS = 2
NSA_GROUP = NSA_HEADS // NSA_KV_HEADS
NSA_DK = 192
NSA_DV = 128
CMP_LEN = 32
CMP_STRIDE = 16
CMP_HIDDEN = 256
SLC_LEN = 64
SLC_TOPN = 16
WINDOW = 512
FORCED_SCORE = 1e6
REL_BUCKETS = 32
REL_MAX_DIST = 128
NEG = -1e30
MASKED = 2 * NEG
LOG2E = 1.0 / math.log(2.0)

LANE = 128
ONES_ROWS = 16
HEAD_PAD = 256
VMEM_LIMIT = 56 * 1024 * 1024
BF16 = jnp.bfloat16
F32 = jnp.float32

FFN_TM, FFN_TF = 512, 512
PROJ_TM = 1024
PROJ_TN_CAP = 2560
Z_PAD = 5 * LANE
PREP_TM = 512
MLA_TQ, MLA_TK = 512, 256
MLA_HEADS_PER_STEP = 4
NSA_TQ, NSA_TK = 256, 256
MERGE_TM = 512


def _bucket_of_distance():
    n = np.arange(LANE)
    max_exact = REL_BUCKETS // 2
    large = max_exact + (np.log(np.maximum(n, 1) / max_exact) / math.log(REL_MAX_DIST / max_exact)
                         * (REL_BUCKETS - max_exact)).astype(np.int32)
    bucket = np.where(n < max_exact, n, np.minimum(large, REL_BUCKETS - 1)).astype(np.int32)
    assert bucket[-1] == REL_BUCKETS - 1
    return bucket


BUCKET_OF_DIST = _bucket_of_distance()
FAR_DIST = int(np.max(np.nonzero(BUCKET_OF_DIST != REL_BUCKETS - 1)[0])) + 1


def _params(*sem):
    return pltpu.CompilerParams(dimension_semantics=sem, vmem_limit_bytes=VMEM_LIMIT)


def _rms(x, g):
    return x * lax.rsqrt(jnp.mean(x * x, axis=-1, keepdims=True) + EPS) * g


def _dot(a, b):
    return jnp.dot(a, b, preferred_element_type=F32)


def _dot_nt(a, b):
    return lax.dot_general(a, b, (((1,), (1,)), ((), ())), preferred_element_type=F32)


def _tile_n(n, cap):
    best = LANE
    for t in range(LANE, cap + 1, LANE):
        if n % t == 0:
            best = t
    return best


def _rmsnorm_kernel(x_ref, g_ref, o_ref):
    o_ref[...] = _rms(x_ref[...], g_ref[...]).astype(o_ref.dtype)


def _rmsnorm(x, g, layer):
    m, d = x.shape
    tm = 512
    return pl.pallas_call(
        _rmsnorm_kernel,
        grid=(m // tm,),
        in_specs=[pl.BlockSpec((tm, d), lambda i: (i, 0)),
                  pl.BlockSpec((None, 1, d), lambda i: (layer, 0, 0))],
        out_specs=pl.BlockSpec((tm, d), lambda i: (i, 0)),
        out_shape=jax.ShapeDtypeStruct((m, d), BF16),
        compiler_params=_params("parallel"),
        name="rmsnorm",
    )(x, g)


def _rope_table_kernel(pos_ref, inv_ref, o_ref):
    ang = pos_ref[...].astype(F32) * inv_ref[...]
    o_ref[...] = jnp.concatenate([jnp.cos(ang), jnp.sin(ang)], axis=1)


def _rope_table(pos_col, inv):
    m = pos_col.shape[0]
    tm = 512
    return pl.pallas_call(
        _rope_table_kernel,
        grid=(m // tm,),
        in_specs=[pl.BlockSpec((tm, 1), lambda i: (i, 0)),
                  pl.BlockSpec((1, MLA_ROPE), lambda i: (0, 0))],
        out_specs=pl.BlockSpec((tm, 2 * MLA_ROPE), lambda i: (i, 0)),
        out_shape=jax.ShapeDtypeStruct((m, 2 * MLA_ROPE), F32),
        compiler_params=_params("parallel"),
        name="rope_table",
    )(pos_col, inv)


def _ffn_kernel(x_ref, u_ref, post_g_ref, next_g_ref, wg_ref, wu_ref, wd_ref,
                o_ref, un_ref, acc_ref):
    j = pl.program_id(1)

    @pl.when(j == 0)
    def _():
        acc_ref[...] = jnp.zeros(acc_ref.shape, F32)

    u = u_ref[...]
    gate = _dot(u, wg_ref[...])
    up = _dot(u, wu_ref[...])
    hidden = (gate * jax.nn.sigmoid(gate) * up).astype(BF16)
    acc_ref[...] += _dot(hidden, wd_ref[...])

    @pl.when(j == pl.num_programs(1) - 1)
    def _():
        out = x_ref[...] + 0.5 * _rms(acc_ref[...], post_g_ref[...])
        o_ref[...] = out
        un_ref[...] = _rms(out, next_g_ref[...]).astype(BF16)


def _ffn(x, u, post_g, next_g, next_layer, wg, wu, wd, layer):
    m, d = x.shape
    f = wg.shape[-1]
    tm, tf = FFN_TM, _tile_n(f, FFN_TF)
    return pl.pallas_call(
        _ffn_kernel,
        grid=(m // tm, f // tf),
        in_specs=[pl.BlockSpec((tm, d), lambda i, j: (i, 0)),
                  pl.BlockSpec((tm, d), lambda i, j: (i, 0)),
                  pl.BlockSpec((None, 1, d), lambda i, j: (layer, 0, 0)),
                  pl.BlockSpec((None, 1, d), lambda i, j: (next_layer, 0, 0)),
                  pl.BlockSpec((None, d, tf), lambda i, j: (layer, 0, j)),
                  pl.BlockSpec((None, d, tf), lambda i, j: (layer, 0, j)),
                  pl.BlockSpec((None, tf, d), lambda i, j: (layer, j, 0))],
        out_specs=[pl.BlockSpec((tm, d), lambda i, j: (i, 0)),
                   pl.BlockSpec((tm, d), lambda i, j: (i, 0))],
        out_shape=[jax.ShapeDtypeStruct((m, d), F32), jax.ShapeDtypeStruct((m, d), BF16)],
        scratch_shapes=[pltpu.VMEM((tm, d), F32)],
        compiler_params=_params("parallel", "arbitrary"),
        name="ffn",
    )(x, u, post_g, next_g, wg, wu, wd)


def _matmul_nt_kernel(x_ref, w_ref, o_ref):
    o_ref[...] = _dot_nt(x_ref[...], w_ref[...]).astype(o_ref.dtype)


def _in_proj(u, w_t, layer):
    m, d = u.shape
    n = w_t.shape[1]
    tm, tn = PROJ_TM, _tile_n(n, PROJ_TN_CAP)
    return pl.pallas_call(
        _matmul_nt_kernel,
        grid=(n // tn, m // tm),
        in_specs=[pl.BlockSpec((tm, d), lambda j, i: (i, 0)),
                  pl.BlockSpec((None, tn, d), lambda j, i: (layer, j, 0))],
        out_specs=pl.BlockSpec((tm, tn), lambda j, i: (i, j)),
        out_shape=jax.ShapeDtypeStruct((m, n), BF16),
        compiler_params=_params("parallel", "parallel"),
        name="in_proj",
    )(u, w_t)


def _matmul_t_kernel(w_ref, x_ref, o_ref):
    o_ref[...] = _dot_nt(w_ref[...], x_ref[...]).astype(o_ref.dtype)


def _in_proj_t(u, w_t, layer):
    m, d = u.shape
    n = w_t.shape[1]
    tm = PROJ_TM
    return pl.pallas_call(
        _matmul_t_kernel,
        grid=(m // tm,),
        in_specs=[pl.BlockSpec((None, n, d), lambda i: (layer, 0, 0)),
                  pl.BlockSpec((tm, d), lambda i: (i, 0))],
        out_specs=pl.BlockSpec((n, tm), lambda i: (0, i)),
        out_shape=jax.ShapeDtypeStruct((n, m), BF16),
        compiler_params=_params("parallel"),
        name="in_proj_t",
    )(w_t, u)


def _softmax_tile_t(s_ref, adds, masks, v_t, m_ref, acc_ref):
    probs, alphas = [], []
    for c in range(s_ref.shape[1] // LANE):
        cols = slice(c * LANE, (c + 1) * LANE)
        s = s_ref[:, cols]
        for mask in masks[c]:
            s = jnp.where(mask, s, MASKED) if mask.dtype == jnp.bool_ else s + mask
        m_old = m_ref[:, cols]
        m_tile = jnp.max(s, axis=0, keepdims=True)
        if adds[c] is not None:
            m_tile = m_tile + adds[c]
        m_new = jnp.maximum(m_old, m_tile)
        alpha = jnp.exp2(m_old - m_new)
        p = jnp.exp2(s - (m_new if adds[c] is None else m_new - adds[c]))
        m_ref[:, cols] = m_new
        probs.append(p.astype(BF16))
        alphas.append(alpha)
    p_t = probs[0] if len(probs) == 1 else jnp.concatenate(probs, axis=1)
    alpha = alphas[0] if len(alphas) == 1 else jnp.concatenate(alphas, axis=1)
    v_ones = jnp.concatenate([v_t, jnp.ones((ONES_ROWS, v_t.shape[1]), BF16)], axis=0)
    acc_ref[...] = alpha * acc_ref[...] + _dot(v_ones, p_t)


def _softmax_init(m_ref, acc_ref):
    m_ref[...] = jnp.full(m_ref.shape, NEG, F32)
    acc_ref[...] = jnp.zeros(acc_ref.shape, F32)


def _softmax_out(acc):
    dv = acc.shape[0] - ONES_ROWS
    return acc[:dv] / acc[dv:dv + 1]


def _mla_prep_kernel(cq_ref, ckv_ref, kr_ref, cs_ref, gq_ref, gkv_ref, wq_ref, wuk_ref, wuvt_ref,
                     q_ref, k_ref, vt_ref):
    cs = cs_ref[...]
    qn = _rms(cq_ref[...].astype(F32), gq_ref[...]).astype(BF16)
    q = _dot(qn, wq_ref[...])
    kvn = _rms(ckv_ref[...].astype(F32), gkv_ref[...]).astype(BF16)
    k_nope = _dot(kvn, wuk_ref[...])
    t = kr_ref[...].astype(F32) * cs
    k_pe = (t + pltpu.roll(t, MLA_ROPE, 1)).astype(BF16)
    for h in range(MLA_HEADS):
        lo = h * HEAD_PAD
        q_ref[:, lo:lo + MLA_NOPE] = q[:, lo:lo + MLA_NOPE].astype(BF16)
        q_ref[:, lo + MLA_NOPE:lo + HEAD_PAD] = (q[:, lo + MLA_NOPE:lo + HEAD_PAD] * cs).astype(BF16)
        k_ref[:, lo:lo + MLA_NOPE] = k_nope[:, h * MLA_NOPE:(h + 1) * MLA_NOPE].astype(BF16)
        k_ref[:, lo + MLA_NOPE:lo + HEAD_PAD] = k_pe
    vt_ref[...] = _dot_nt(wuvt_ref[...], kvn).astype(BF16)


def _mla_prep(z, cs, gq, gkv, wq, wuk, wuv_t, layer, zoff):
    m = z.shape[0]
    tm = PREP_TM
    hq = MLA_HEADS * HEAD_PAD
    hv = MLA_HEADS * MLA_V
    const = lambda i: (layer, 0, 0)
    return pl.pallas_call(
        _mla_prep_kernel,
        grid=(m // tm,),
        in_specs=[pl.BlockSpec((tm, MLA_Q_LORA), lambda i: (i, zoff["cq"] // MLA_Q_LORA)),
                  pl.BlockSpec((tm, MLA_KV_LORA), lambda i: (i, zoff["ckv"] // MLA_KV_LORA)),
                  pl.BlockSpec((tm, LANE), lambda i: (i, zoff["kr"] // LANE)),
                  pl.BlockSpec((tm, LANE), lambda i: (i, 0)),
                  pl.BlockSpec((None, 1, MLA_Q_LORA), const),
                  pl.BlockSpec((None, 1, MLA_KV_LORA), const),
                  pl.BlockSpec((None, MLA_Q_LORA, hq), const),
                  pl.BlockSpec((None, MLA_KV_LORA, MLA_HEADS * MLA_NOPE), const),
                  pl.BlockSpec((None, hv, MLA_KV_LORA), const)],
        out_specs=[pl.BlockSpec((tm, hq), lambda i: (i, 0)),
                   pl.BlockSpec((tm, hq), lambda i: (i, 0)),
                   pl.BlockSpec((hv, tm), lambda i: (0, i))],
        out_shape=[jax.ShapeDtypeStruct((m, hq), BF16),
                   jax.ShapeDtypeStruct((m, hq), BF16),
                   jax.ShapeDtypeStruct((hv, m), BF16)],
        compiler_params=_params("parallel"),
        name="mla_prep",
    )(z, z, z, cs, gq, gkv, wq, wuk, wuv_t)


def _by_parity(kt, fn):
    @pl.when(kt % 2 == 0)
    def _():
        fn(0)

    @pl.when(kt % 2 == 1)
    def _():
        fn(1)


def _mla_attn_kernel(q_ref, k_ref, vt_ref, o_ref, s_ref, m_ref, acc_ref, *, tq, tk, heads):
    i = pl.program_id(2)
    nc = tq // LANE
    _softmax_init(m_ref, acc_ref)
    qt = i * tq + lax.broadcasted_iota(jnp.int32, (1, tq), 1)
    kcol = lax.broadcasted_iota(jnp.int32, (tk, 1), 0)
    qk = [slice(h * HEAD_PAD, (h + 1) * HEAD_PAD) for h in range(heads)]
    vd = [slice(h * MLA_V, (h + 1) * MLA_V) for h in range(heads)]
    n_full = (i * tq) // tk
    n_all = (i * tq + tq + tk - 1) // tk

    def logits(kt, buf):
        ks = pl.multiple_of(kt * tk, tk)
        for h in range(heads):
            s_ref[buf, h] = _dot_nt(k_ref[pl.ds(ks, tk), qk[h]], q_ref[:, qk[h]])

    def step(kt, buf, masked, prefetch=True):
        ks = pl.multiple_of(kt * tk, tk)
        masks = [()] * nc
        if masked:
            masks = [(ks + kcol <= qt[:, c * LANE:(c + 1) * LANE],) for c in range(nc)]
        for h in range(heads):
            _softmax_tile_t(s_ref.at[buf, h], [None] * nc, masks, vt_ref[vd[h], pl.ds(ks, tk)],
                            m_ref.at[h], acc_ref.at[h])
        if prefetch:
            logits(kt + 1, 1 - buf)

    def full_body(kt, c):
        _by_parity(kt, lambda buf: step(kt, buf, False))
        return c

    def diag_body(kt, c):
        _by_parity(kt, lambda buf: step(kt, buf, True))
        return c

    logits(0, 0)
    lax.fori_loop(0, n_full, full_body, 0)
    lax.fori_loop(n_full, n_all - 1, diag_body, 0)
    _by_parity(n_all - 1, lambda buf: step(n_all - 1, buf, True, prefetch=False))
    for h in range(heads):
        o_ref[:, vd[h]] = _softmax_out(acc_ref[h]).T.astype(o_ref.dtype)


def _mla_attn(qp, kp, vt, batch, seq):
    tq, tk, heads = MLA_TQ, MLA_TK, MLA_HEADS_PER_STEP
    nq = seq // tq
    kern = functools.partial(_mla_attn_kernel, tq=tq, tk=tk, heads=heads)
    return pl.pallas_call(
        kern,
        grid=(batch, MLA_HEADS // heads, nq),
        in_specs=[pl.BlockSpec((tq, heads * HEAD_PAD), lambda b, h, i: (b * nq + i, h)),
                  pl.BlockSpec((seq, heads * HEAD_PAD), lambda b, h, i: (b, h)),
                  pl.BlockSpec((heads * MLA_V, seq), lambda b, h, i: (h, b))],
        out_specs=pl.BlockSpec((tq, heads * MLA_V), lambda b, h, i: (b * nq + i, h)),
        out_shape=jax.ShapeDtypeStruct((batch * seq, MLA_HEADS * MLA_V), BF16),
        scratch_shapes=[pltpu.VMEM((2, heads, tk, tq), F32), pltpu.VMEM((heads, 1, tq), F32),
                        pltpu.VMEM((heads, MLA_V + ONES_ROWS, tq), F32)],
        compiler_params=_params("parallel", "parallel", "parallel"),
        name="mla_attn",
    )(qp, kp, vt)


def _compress(a_ref, pe_ref, w1_ref, w2_ref):
    a = a_ref[...].astype(F32)
    half = a.shape[1]
    first = _dot((a + pe_ref[0:1, :]).astype(BF16), w1_ref[0:half, :])
    second = _dot((a + pe_ref[1:2, :]).astype(BF16), w1_ref[half:2 * half, :])
    n = a.shape[0]
    hidden = first + pltpu.roll(second, n - 1, 0)
    return _dot((hidden * jax.nn.sigmoid(hidden)).astype(BF16), w2_ref[...])


def _nsa_cmp_kernel(ak_ref, av_ref, pek_ref, pev_ref, w1k_ref, w2k_ref, w1v_ref, w2v_ref,
                    kc_ref, vct_ref):
    kc_ref[...] = _compress(ak_ref, pek_ref, w1k_ref, w2k_ref).astype(BF16)
    vct_ref[...] = _compress(av_ref, pev_ref, w1v_ref, w2v_ref).T.astype(BF16)


def _nsa_cmp(ak, av, pek, pev, w1k, w2k, w1v, w2v, layer):
    bg, nh, dk16 = ak.shape
    dv16 = av.shape[-1]
    const = lambda i: (layer, 0, 0)
    return pl.pallas_call(
        _nsa_cmp_kernel,
        grid=(bg,),
        in_specs=[pl.BlockSpec((None, nh, dk16), lambda i: (i, 0, 0)),
                  pl.BlockSpec((None, nh, dv16), lambda i: (i, 0, 0)),
                  pl.BlockSpec((None, 2, dk16), const),
                  pl.BlockSpec((None, 2, dv16), const),
                  pl.BlockSpec((None, 2 * dk16, CMP_HIDDEN), const),
                  pl.BlockSpec((None, CMP_HIDDEN, HEAD_PAD), const),
                  pl.BlockSpec((None, 2 * dv16, CMP_HIDDEN), const),
                  pl.BlockSpec((None, CMP_HIDDEN, NSA_DV), const)],
        out_specs=[pl.BlockSpec((None, nh, HEAD_PAD), lambda i: (i, 0, 0)),
                   pl.BlockSpec((None, NSA_DV, nh), lambda i: (i, 0, 0))],
        out_shape=[jax.ShapeDtypeStruct((bg, nh, HEAD_PAD), BF16),
                   jax.ShapeDtypeStruct((bg, NSA_DV, nh), BF16)],
        compiler_params=_params("parallel"),
        name="nsa_cmp",
    )(ak, av, pek, pev, w1k, w2k, w1v, w2v)


def _split3(x):
    a = x.astype(BF16)
    r = x - a.astype(F32)
    b = r.astype(BF16)
    c = (r - b.astype(F32)).astype(BF16)
    return a, b, c


def _lookup(table_row, idx):
    rows, width = idx.shape
    table = jnp.broadcast_to(table_row, (rows, LANE))
    chunks = [jnp.take_along_axis(table, idx[:, c:c + LANE], axis=1, mode="promise_in_bounds")
              for c in range(0, width, LANE)]
    return chunks[0] if len(chunks) == 1 else jnp.concatenate(chunks, axis=1)


def _nsa_attn_kernel(far_ref, q_ref, kc_ref, vct_ref, ks_ref, vst_ref, kw_ref, vwt_ref, ng_ref,
                     prow_ref, pcol_ref, pcmp_ref, tbl_ref, o_ref,
                     s_ref, m_ref, acc_ref, *, tq, tk, seq):
    b, i = pl.program_id(0), pl.program_id(2)
    nq, nk = seq // tq, seq // tk
    J = NSA_GROUP
    n_slc = seq // SLC_LEN
    n_cmp = (seq - CMP_LEN) // CMP_STRIDE + 1
    t0 = pl.multiple_of(i * tq, tq)

    qs = jnp.concatenate([q_ref[:, j * HEAD_PAD:(j + 1) * HEAD_PAD] for j in range(J)], axis=0)
    qpos = prow_ref[:, pl.ds(t0, tq)]
    qt = t0 + lax.broadcasted_iota(jnp.int32, (1, tq), 1)
    tbl = tbl_ref[...]
    far_bias = [tbl[j:j + 1, LANE - 1:LANE] for j in range(J)]
    head = [slice(j * tq, (j + 1) * tq) for j in range(J)]

    blk = lax.broadcasted_iota(jnp.int32, (LANE, 1), 0)
    keep_c = (blk * CMP_STRIDE + (CMP_LEN - 1) <= qt) & (blk < n_cmp)
    idx_c = jnp.clip(qpos - pcmp_ref[...], 0, LANE - 1)
    s_c = _dot_nt(kc_ref[...], qs)
    p_heads = []
    for j in range(J):
        s = jnp.where(keep_c, s_c[:, head[j]] + _lookup(tbl[j:j + 1, :], idx_c), NEG)
        e = jnp.where(keep_c, jnp.exp2(s - jnp.max(s, axis=0, keepdims=True)), 0.0)
        den = jnp.sum(e, axis=0, keepdims=True)
        p_heads.append(e / jnp.where(den > 0.0, den, 1.0))
    o_cmp = _dot(vct_ref[...], jnp.concatenate([p.astype(BF16) for p in p_heads], axis=1))

    rows = 32
    assert n_slc <= rows
    m_row = lax.broadcasted_iota(jnp.int32, (rows, LANE), 0)
    n_col = lax.broadcasted_iota(jnp.int32, (rows, LANE), 1)
    per = SLC_LEN // CMP_STRIDE
    back = (CMP_LEN - 1) // CMP_STRIDE
    overlap = ((n_col >= per * m_row - back) & (n_col <= per * m_row + per - 1)
               & (n_col < n_cmp) & (m_row < n_slc)).astype(BF16)
    imp = sum(_dot(overlap, part) for part in _split3(sum(p_heads)))
    m_blk = lax.broadcasted_iota(jnp.int32, (rows, 1), 0)
    cur = qt // SLC_LEN
    valid = m_blk <= cur
    forced = valid & ((m_blk == 0) | (m_blk >= cur - 1))
    score = jnp.where(forced, FORCED_SCORE, jnp.where(valid, imp, -1.0))
    rank = jnp.zeros((rows, tq), jnp.int32)
    for mp in range(n_slc):
        other = score[mp:mp + 1, :]
        ahead = (other > score) | ((other == score) & (m_blk > mp))
        rank = rank + ahead.astype(jnp.int32)
    selected = valid & (rank < min(SLC_TOPN, n_slc))
    sel_penalty = jnp.where(selected, 0.0, MASKED).astype(BF16)

    SLC, WIN = 0, 1
    _softmax_init(m_ref, acc_ref)
    sub = tk // LANE

    qchunks = [slice(c, c + LANE) for c in range(0, tq, LANE)]

    assert tq == tk

    def slc_masks(kidx, diagonal):
        expand = (kidx // SLC_LEN == lax.broadcasted_iota(jnp.int32, (1, rows), 1)).astype(BF16)
        penalty = _dot(expand, sel_penalty)
        if diagonal:
            return [(penalty[:, c], kidx <= qt[:, c]) for c in qchunks]
        return [(penalty[:, c],) for c in qchunks]

    def win_masks(kidx, diagonal):
        return [(jnp.where(lax.bitcast_convert_type(qt[:, c] - kidx, jnp.uint32) < WINDOW,
                           0.0, MASKED),) for c in qchunks]

    k_refs, vt_refs, masks_of = (ks_ref, kw_ref), (vst_ref, vwt_ref), (slc_masks, win_masks)
    adds = [far_bias[j] for j in range(J) for _ in qchunks]
    last = (t0 + tq + tk - 1) // tk
    first_win = jnp.maximum(t0 - (WINDOW - 1), 0) // tk

    def logits(kt, buf, slots):
        ks = pl.multiple_of(kt * tk, tk)
        for slot in slots:
            s_ref[buf, slot] = _dot_nt(k_refs[slot][pl.ds(ks, tk), :], qs)

    def near_bias(kt, buf, slots):
        ks = pl.multiple_of(kt * tk, tk)
        for hb in range(sub):
            kb = kt * sub + hb
            for qc, qcols in enumerate(qchunks):
                near = ((far_ref[((b * nq + i) * len(qchunks) + qc) * (nk * sub) + kb] == 0)
                        & (kb * LANE < t0 + (qc + 1) * LANE))

                @pl.when(near)
                def _():
                    kpos = pcol_ref[pl.ds(pl.multiple_of(ks + hb * LANE, LANE), LANE), :]
                    idx = jnp.clip(qpos[:, qcols] - kpos, 0, LANE - 1)
                    for j in range(J):
                        delta = _lookup(tbl[j:j + 1, :] - far_bias[j], idx)
                        cols = slice(j * tq + qc * LANE, j * tq + (qc + 1) * LANE)
                        for slot in slots:
                            s_ref[buf, slot, hb * LANE:(hb + 1) * LANE, cols] += delta

    def step(kt, buf, slots, final=False):
        ks = pl.multiple_of(kt * tk, tk)
        kidx = ks + lax.broadcasted_iota(jnp.int32, (tk, 1), 0)
        for slot in slots:
            _softmax_tile_t(s_ref.at[buf, slot], adds, masks_of[slot](kidx, final) * J,
                            vt_refs[slot][:, pl.ds(ks, tk)], m_ref.at[slot], acc_ref.at[slot])
        if not final:
            logits(kt + 1, 1 - buf, slots)
            near_bias(kt + 1, 1 - buf, slots)

    def slc_body(kt, c):
        _by_parity(kt, lambda buf: step(kt, buf, (SLC,)))
        return c

    def both_body(kt, c):
        _by_parity(kt, lambda buf: step(kt, buf, (SLC, WIN)))
        return c

    def enter_window(buf):
        logits(first_win, buf, (WIN,))
        near_bias(first_win, buf, (WIN,))

    logits(0, 0, (SLC,))
    near_bias(0, 0, (SLC,))
    lax.fori_loop(0, first_win, slc_body, 0)
    _by_parity(first_win, enter_window)
    lax.fori_loop(first_win, last - 1, both_body, 0)
    _by_parity(last - 1, lambda buf: step(last - 1, buf, (SLC, WIN), final=True))
    o_slc = _softmax_out(acc_ref[SLC])
    o_win = _softmax_out(acc_ref[WIN])

    gates = jax.nn.sigmoid(ng_ref[...].astype(F32)).T
    for j in range(J):
        o = (gates[3 * j:3 * j + 1, :] * o_cmp[:, head[j]]
             + gates[3 * j + 1:3 * j + 2, :] * o_slc[:, head[j]]
             + gates[3 * j + 2:3 * j + 3, :] * o_win[:, head[j]])
        o_ref[:, j * NSA_DV:(j + 1) * NSA_DV] = o.T.astype(o_ref.dtype)


def _nsa_attn(far, z, kc, vct, v_t, pos_row, pos_col, pos_cmp, tbl, batch, seq, zoff):
    tq, tk = NSA_TQ, NSA_TK
    nq = seq // tq
    G, J = NSA_KV_HEADS, NSA_GROUP
    R = J * tq
    kern = functools.partial(_nsa_attn_kernel, tq=tq, tk=tk, seq=seq)
    grid_spec = pltpu.PrefetchScalarGridSpec(
        num_scalar_prefetch=1,
        grid=(batch, G, nq),
        in_specs=[
            pl.BlockSpec((tq, J * HEAD_PAD), lambda b, g, i, far: (b * nq + i, g)),
            pl.BlockSpec((None, LANE, HEAD_PAD), lambda b, g, i, far: (b * G + g, 0, 0)),
            pl.BlockSpec((None, NSA_DV, LANE), lambda b, g, i, far: (b * G + g, 0, 0)),
            pl.BlockSpec((seq, HEAD_PAD), lambda b, g, i, far: (b, zoff["ks"] // HEAD_PAD + g)),
            pl.BlockSpec((NSA_DV, seq), lambda b, g, i, far: (g, b)),
            pl.BlockSpec((seq, HEAD_PAD), lambda b, g, i, far: (b, zoff["kw"] // HEAD_PAD + g)),
            pl.BlockSpec((NSA_DV, seq), lambda b, g, i, far: (G + g, b)),
            pl.BlockSpec((tq, LANE), lambda b, g, i, far: (b * nq + i, zoff["ng"] // LANE + g)),
            pl.BlockSpec((None, 1, seq), lambda b, g, i, far: (b, 0, 0)),
            pl.BlockSpec((seq, 1), lambda b, g, i, far: (b, 0)),
            pl.BlockSpec((LANE, 1), lambda b, g, i, far: (b, 0)),
            pl.BlockSpec((None, J, LANE), lambda b, g, i, far: (g, 0, 0)),
        ],
        out_specs=pl.BlockSpec((tq, J * NSA_DV), lambda b, g, i, far: (b * nq + i, g)),
        scratch_shapes=[pltpu.VMEM((2, 2, tk, R), F32), pltpu.VMEM((2, 1, R), F32),
                        pltpu.VMEM((2, NSA_DV + ONES_ROWS, R), F32)],
    )
    return pl.pallas_call(
        kern,
        grid_spec=grid_spec,
        out_shape=jax.ShapeDtypeStruct((batch * seq, NSA_HEADS * NSA_DV), BF16),
        compiler_params=_params("parallel", "parallel", "parallel"),
        name="nsa_attn",
    )(far, z, kc, vct, z, v_t, z, v_t, z, pos_row, pos_col, pos_cmp, tbl)


def _merge_kernel(a_ref, b_ref, ga_ref, gb_ref, h_ref, wa_ref, wb_ref, wo_ref, post_g_ref,
                  next_g_ref, o_ref, un_ref):
    ya = _dot(a_ref[...], wa_ref[...])
    yb = _dot(b_ref[...], wb_ref[...])
    m = (jax.nn.sigmoid(ga_ref[...].astype(F32)) * ya
         + jax.nn.sigmoid(gb_ref[...].astype(F32)) * yb).astype(BF16)
    y = _dot(m, wo_ref[...])
    out = h_ref[...] + _rms(y, post_g_ref[...])
    o_ref[...] = out
    un_ref[...] = _rms(out, next_g_ref[...]).astype(BF16)


def _merge(a, bb, z, h, wa, wb, wo, post_g, next_g, layer, zoff):
    m, d = h.shape
    tm = MERGE_TM
    da, db = a.shape[1], bb.shape[1]
    gblk = zoff["mg"] // d
    const = lambda i: (layer, 0, 0)
    return pl.pallas_call(
        _merge_kernel,
        grid=(m // tm,),
        in_specs=[pl.BlockSpec((tm, da), lambda i: (i, 0)),
                  pl.BlockSpec((tm, db), lambda i: (i, 0)),
                  pl.BlockSpec((tm, d), lambda i: (i, gblk)),
                  pl.BlockSpec((tm, d), lambda i: (i, gblk + 1)),
                  pl.BlockSpec((tm, d), lambda i: (i, 0)),
                  pl.BlockSpec((None, da, d), const, pipeline_mode=pl.Buffered(1)),
                  pl.BlockSpec((None, db, d), const, pipeline_mode=pl.Buffered(1)),
                  pl.BlockSpec((None, d, d), const, pipeline_mode=pl.Buffered(1)),
                  pl.BlockSpec((None, 1, d), const),
                  pl.BlockSpec((None, 1, d), const)],
        out_specs=[pl.BlockSpec((tm, d), lambda i: (i, 0)),
                   pl.BlockSpec((tm, d), lambda i: (i, 0))],
        out_shape=[jax.ShapeDtypeStruct((m, d), F32), jax.ShapeDtypeStruct((m, d), BF16)],
        compiler_params=_params("parallel"),
        name="merge_out",
    )(a, bb, z, z, h, wa, wb, wo, post_g, next_g)


def _z_layout(d_model):
    G = NSA_KV_HEADS
    widths = [("q", NSA_HEADS * HEAD_PAD), ("mg", 2 * d_model), ("cq", MLA_Q_LORA),
              ("ckv", MLA_KV_LORA), ("kc", G * HEAD_PAD), ("ks", G * HEAD_PAD),
              ("kw", G * HEAD_PAD), ("vc", G * NSA_DV), ("kr", LANE), ("ng", G * LANE)]
    off, pos = {}, 0
    for name, w in widths:
        off[name] = pos
        pos += w
    off["used"] = pos
    off["total"] = -(-pos // Z_PAD) * Z_PAD
    assert off["q"] == 0 and off["mg"] % d_model == 0
    assert off["cq"] % MLA_Q_LORA == 0 and off["ckv"] % MLA_KV_LORA == 0
    assert all(off[k] % HEAD_PAD == 0 for k in ("kc", "ks", "kw"))
    return off


def _swap_halves(w):
    half = w.shape[-1] // 2
    return jnp.concatenate([-w[..., half:], w[..., :half]], axis=-1)


def _pad_last(w, width):
    return jnp.pad(w, [(0, 0)] * (w.ndim - 1) + [(0, width - w.shape[-1])])


def _w_in_pieces(d_model, zoff):
    G, J = NSA_KV_HEADS, NSA_GROUP
    splits = [MLA_Q_LORA, MLA_KV_LORA, MLA_ROPE, NSA_HEADS * NSA_DK,
              G * NSA_DK, G * NSA_DV, G * NSA_DK, G * NSA_DV, G * NSA_DK, G * NSA_DV,
              NSA_HEADS * 3, 2 * d_model]
    names = ["cq", "ckv", "kr", "q", "kc", "vc", "ks", "vs", "kw", "vw", "ng", "mg"]
    src = dict(zip(names, [0] + [int(v) for v in np.cumsum(splits)[:-1]]))
    half = MLA_ROPE // 2
    pieces = []
    for h in range(NSA_HEADS):
        pieces.append((zoff["q"] + h * HEAD_PAD, HEAD_PAD,
                       [(src["q"] + h * NSA_DK, NSA_DK, NSA_DK ** -0.5 * LOG2E)]))
    pieces.append((zoff["mg"], 2 * d_model, [(src["mg"], 2 * d_model, None)]))
    pieces.append((zoff["cq"], MLA_Q_LORA, [(src["cq"], MLA_Q_LORA, None)]))
    pieces.append((zoff["ckv"], MLA_KV_LORA, [(src["ckv"], MLA_KV_LORA, None)]))
    for name in ("kc", "ks", "kw"):
        for g in range(G):
            pieces.append((zoff[name] + g * HEAD_PAD, HEAD_PAD, [(src[name] + g * NSA_DK, NSA_DK, None)]))
    pieces.append((zoff["vc"], G * NSA_DV, [(src["vc"], G * NSA_DV, None)]))
    pieces.append((zoff["kr"], LANE, [(src["kr"], MLA_ROPE, None), (src["kr"] + half, half, -1.0),
                                      (src["kr"], half, None)]))
    for g in range(G):
        pieces.append((zoff["ng"] + g * LANE, LANE, [(src["ng"] + g * J * 3, J * 3, None)]))
    if zoff["total"] > zoff["used"]:
        pieces.append((zoff["used"], zoff["total"] - zoff["used"], []))
    return pieces, (src["vs"], src["vw"])


SUBLANE = 8
COPY_ROWS = 512


def _copy_rows(w_ref, o_ref, src, dst, n, scale):
    for off in range(0, n, COPY_ROWS):
        m = min(COPY_ROWS, n - off)
        x = w_ref[src + off:src + off + m, :]
        o_ref[dst + off:dst + off + m, :] = (x if scale is None else x * scale).astype(BF16)


def _w_in_layout_kernel(w_ref, o_ref, vt_ref, *, pieces, value_rows):
    tc = w_ref.shape[1]
    dst = 0
    for src, n in value_rows:
        _copy_rows(w_ref, vt_ref, src, dst, n, None)
        dst += n
    for dst, width, parts in pieces:
        used = 0
        for src, n, scale in parts:
            if src % SUBLANE == 0 and n % SUBLANE == 0:
                _copy_rows(w_ref, o_ref, src, dst + used, n, scale)
                used += n
            else:
                assert len(parts) == 1 and scale is None
                lo = src - src % SUBLANE
                win = -(-(src % SUBLANE + n) // SUBLANE) * SUBLANE
                x = w_ref[lo:lo + win, :]
                if src % SUBLANE:
                    x = pltpu.roll(x, win - src % SUBLANE, 0)
                x = jnp.where(lax.broadcasted_iota(jnp.int32, (win, tc), 0) < n, x, 0.0)
                x = jnp.concatenate([x, jnp.zeros((width - win, tc), F32)], axis=0)
                o_ref[dst:dst + width, :] = x.astype(BF16)
                used = width
        if used < width:
            o_ref[dst + used:dst + width, :] = jnp.zeros((width - used, tc), BF16)


def _layout_w_in(w_in, zoff):
    L, D, n_in = w_in.shape
    G = NSA_KV_HEADS
    pieces, (vs, vw) = _w_in_pieces(D, zoff)
    tc = 256
    nv = 2 * G * NSA_DV
    value_rows = ((vs, G * NSA_DV), (vw, G * NSA_DV))
    return pl.pallas_call(
        functools.partial(_w_in_layout_kernel, pieces=pieces, value_rows=value_rows),
        grid=(L, D // tc),
        in_specs=[pl.BlockSpec((None, n_in, tc), lambda l, c: (l, 0, c))],
        out_specs=[pl.BlockSpec((None, zoff["total"], tc), lambda l, c: (l, 0, c)),
                   pl.BlockSpec((None, nv, tc), lambda l, c: (l, 0, c))],
        out_shape=[jax.ShapeDtypeStruct((L, zoff["total"], D), BF16),
                   jax.ShapeDtypeStruct((L, nv, D), BF16)],
        compiler_params=_params("parallel", "parallel"),
        name="w_in_layout",
    )(jnp.swapaxes(w_in, 1, 2))


def _layout_w_q_up(w):
    L, r, _ = w.shape
    w = w.reshape(L, r, MLA_HEADS, MLA_NOPE + MLA_ROPE) * ((MLA_NOPE + MLA_ROPE) ** -0.5 * LOG2E)
    rope = w[..., MLA_NOPE:]
    return jnp.concatenate([w[..., :MLA_NOPE], rope, _swap_halves(rope)], axis=-1).reshape(
        L, r, MLA_HEADS * HEAD_PAD).astype(BF16)


def kernel(x, positions, rel_bias, ffn1_pre_g, ffn1_post_g, ffn1_w_gate, ffn1_w_up, ffn1_w_down, mix_pre_g, mix_post_g, w_in, mla_q_norm_g, mla_w_q_up, mla_kv_norm_g, mla_w_uk, mla_w_uv, cmp_pe_k, cmp_w1_k, cmp_w2_k, cmp_pe_v, cmp_w1_v, cmp_w2_v, w_branch_mla, w_branch_nsa, w_out, ffn2_pre_g, ffn2_post_g, ffn2_w_gate, ffn2_w_up, ffn2_w_down):
    B, S, D = x.shape
    L = w_in.shape[0]
    M = B * S
    G, J = NSA_KV_HEADS, NSA_GROUP
    zoff = _z_layout(D)
    n_half = S // CMP_STRIDE
    n_cmp = (S - CMP_LEN) // CMP_STRIDE + 1
    assert n_half == LANE and n_cmp <= LANE

    gain = lambda g: g.reshape(L, 1, -1)
    bf = lambda w: w.astype(BF16)
    w_z, w_vt = _layout_w_in(w_in, zoff)
    w_q = _layout_w_q_up(mla_w_q_up)
    w_uk, w_uv_t = bf(mla_w_uk), bf(mla_w_uv).transpose(0, 2, 1)
    pe_k = cmp_pe_k.reshape(L, 2, CMP_STRIDE * NSA_DK)
    pe_v = cmp_pe_v.reshape(L, 2, CMP_STRIDE * NSA_DV)
    w2_k = bf(_pad_last(cmp_w2_k, HEAD_PAD))
    f1 = (bf(ffn1_w_gate), bf(ffn1_w_up), bf(ffn1_w_down))
    f2 = (bf(ffn2_w_gate), bf(ffn2_w_up), bf(ffn2_w_down))
    w1_k, w1_v, w2_v = bf(cmp_w1_k), bf(cmp_w1_v), bf(cmp_w2_v)
    w_a, w_b, w_o = bf(w_branch_mla), bf(w_branch_nsa), bf(w_out)
    g_f1pre, g_f1post, g_mpre, g_mpost = gain(ffn1_pre_g), gain(ffn1_post_g), gain(mix_pre_g), gain(mix_post_g)
    g_f2pre, g_f2post, g_q, g_kv = gain(ffn2_pre_g), gain(ffn2_post_g), gain(mla_q_norm_g), gain(mla_kv_norm_g)

    pos_col = positions.reshape(M, 1)
    pos_row = positions.reshape(B, 1, S)
    pos_cmp = _pad_last(positions[:, CMP_LEN - 1::CMP_STRIDE][:, :n_cmp], LANE).reshape(B * LANE, 1)
    q_min = positions.reshape(B, S // LANE, LANE).min(axis=-1)
    k_max = positions.reshape(B, S // LANE, LANE).max(axis=-1)
    far = (q_min[:, :, None] - k_max[:, None, :] >= FAR_DIST).astype(jnp.int32).reshape(-1)
    tbl = (jnp.take(rel_bias, jnp.asarray(BUCKET_OF_DIST), axis=0).T * LOG2E).reshape(G, J, LANE)
    half = MLA_ROPE // 2
    inv = ROPE_BASE ** (-jnp.arange(half, dtype=F32) * 2.0 / MLA_ROPE)
    cs = _rope_table(pos_col, jnp.concatenate([inv, inv]).reshape(1, MLA_ROPE))

    h = x.reshape(M, D)
    u = _rmsnorm(h, g_f1pre, 0)
    for l in range(L):
        h, u = _ffn(h, u, g_f1post, g_mpre, l, *f1, l)
        z = _in_proj(u, w_z, l)
        v_t = _in_proj_t(u, w_vt, l)
        qp, kp, vt = _mla_prep(z, cs, g_q, g_kv, w_q, w_uk, w_uv_t, l, zoff)
        a = _mla_attn(qp, kp, vt, B, S)

        def half_blocks(off, width, d):
            t = z[:, off:off + G * width].reshape(B, S, G, width)[..., :d]
            return t.transpose(0, 2, 1, 3).reshape(B * G, n_half, CMP_STRIDE * d)

        kc, vct = _nsa_cmp(half_blocks(zoff["kc"], HEAD_PAD, NSA_DK),
                           half_blocks(zoff["vc"], NSA_DV, NSA_DV),
                           pe_k, pe_v, w1_k, w2_k, w1_v, w2_v, l)
        nsa = _nsa_attn(far, z, kc, vct, v_t, pos_row, pos_col, pos_cmp, tbl, B, S, zoff)
        h, u = _merge(a, nsa, z, h, w_a, w_b, w_o, g_mpost, g_f2pre, l, zoff)
        h, u = _ffn(h, u, g_f2post, g_f1pre, min(l + 1, L - 1), *f2, l)
    return h.reshape(B, S, D)
```

```python
import functools
import math

import numpy as np
import jax
import jax.numpy as jnp
from jax import lax
from jax.experimental import pallas as pl
from jax.experimental.pallas import tpu as pltpu

EPS = 1e-6
MLA_HEADS = 8
MLA_Q_LORA = 512
MLA_KV_LORA = 512
MLA_NOPE = 128
MLA_ROPE = 64
MLA_V = 128
ROPE_BASE = 10000.0
NSA_HEADS = 8
NSA_KV_HEADS = 2
NSA_GROUP = NSA_HEADS // NSA_KV_HEADS
NSA_DK = 192
NSA_DV = 128
CMP_LEN = 32
CMP_STRIDE = 16
CMP_HIDDEN = 256
SLC_LEN = 64
SLC_TOPN = 16
WINDOW = 512
FORCED_SCORE = 1e6
REL_BUCKETS = 32
REL_MAX_DIST = 128
NEG = -1e30
MASKED = 2 * NEG
LOG2E = 1.0 / math.log(2.0)

LANE = 128
ONES_ROWS = 16
HEAD_PAD = 256
VMEM_LIMIT = 56 * 1024 * 1024
BF16 = jnp.bfloat16
F32 = jnp.float32

FFN_TM, FFN_TF = 512, 512
PROJ_TM = 1024
PROJ_TN_CAP = 2560
Z_PAD = 5 * LANE
PREP_TM = 512
MLA_TQ, MLA_TK = 512, 512
MLA_HEADS_PER_STEP = 4
NSA_TQ, NSA_TK = 256, 256
MERGE_TM = 512


def _bucket_of_distance():
    n = np.arange(LANE)
    max_exact = REL_BUCKETS // 2
    large = max_exact + (np.log(np.maximum(n, 1) / max_exact) / math.log(REL_MAX_DIST / max_exact)
                         * (REL_BUCKETS - max_exact)).astype(np.int32)
    bucket = np.where(n < max_exact, n, np.minimum(large, REL_BUCKETS - 1)).astype(np.int32)
    assert bucket[-1] == REL_BUCKETS - 1
    return bucket


BUCKET_OF_DIST = _bucket_of_distance()
FAR_DIST = int(np.max(np.nonzero(BUCKET_OF_DIST != REL_BUCKETS - 1)[0])) + 1


def _params(*sem):
    return pltpu.CompilerParams(dimension_semantics=sem, vmem_limit_bytes=VMEM_LIMIT)


def _rms(x, g):
    return x * lax.rsqrt(jnp.mean(x * x, axis=-1, keepdims=True) + EPS) * g


def _dot(a, b):
    return jnp.dot(a, b, preferred_element_type=F32)


def _dot_nt(a, b):
    return lax.dot_general(a, b, (((1,), (1,)), ((), ())), preferred_element_type=F32)


def _tile_n(n, cap):
    best = LANE
    for t in range(LANE, cap + 1, LANE):
        if n % t == 0:
            best = t
    return best


def _rmsnorm_kernel(x_ref, g_ref, o_ref):
    o_ref[...] = _rms(x_ref[...], g_ref[...]).astype(o_ref.dtype)


def _rmsnorm(x, g, layer):
    m, d = x.shape
    tm = 512
    return pl.pallas_call(
        _rmsnorm_kernel,
        grid=(m // tm,),
        in_specs=[pl.BlockSpec((tm, d), lambda i: (i, 0)),
                  pl.BlockSpec((None, 1, d), lambda i: (layer, 0, 0))],
        out_specs=pl.BlockSpec((tm, d), lambda i: (i, 0)),
        out_shape=jax.ShapeDtypeStruct((m, d), BF16),
        compiler_params=_params("parallel"),
        name="rmsnorm",
    )(x, g)


def _rope_table_kernel(pos_ref, inv_ref, o_ref):
    ang = pos_ref[...].astype(F32) * inv_ref[...]
    o_ref[...] = jnp.concatenate([jnp.cos(ang), jnp.sin(ang)], axis=1)


def _rope_table(pos_col, inv):
    m = pos_col.shape[0]
    tm = 512
    return pl.pallas_call(
        _rope_table_kernel,
        grid=(m // tm,),
        in_specs=[pl.BlockSpec((tm, 1), lambda i: (i, 0)),
                  pl.BlockSpec((1, MLA_ROPE), lambda i: (0, 0))],
        out_specs=pl.BlockSpec((tm, 2 * MLA_ROPE), lambda i: (i, 0)),
        out_shape=jax.ShapeDtypeStruct((m, 2 * MLA_ROPE), F32),
        compiler_params=_params("parallel"),
        name="rope_table",
    )(pos_col, inv)


def _ffn_kernel(x_ref, u_ref, post_g_ref, next_g_ref, wg_ref, wu_ref, wd_ref,
                o_ref, un_ref, acc_ref):
    j = pl.program_id(1)

    @pl.when(j == 0)
    def _():
        acc_ref[...] = jnp.zeros(acc_ref.shape, F32)

    u = u_ref[...]
    gate = _dot(u, wg_ref[...])
    up = _dot(u, wu_ref[...])
    hidden = (gate * jax.nn.sigmoid(gate) * up).astype(BF16)
    acc_ref[...] += _dot(hidden, wd_ref[...])

    @pl.when(j == pl.num_programs(1) - 1)
    def _():
        out = x_ref[...] + 0.5 * _rms(acc_ref[...], post_g_ref[...])
        o_ref[...] = out
        un_ref[...] = _rms(out, next_g_ref[...]).astype(BF16)


def _ffn(x, u, post_g, next_g, next_layer, wg, wu, wd, layer):
    m, d = x.shape
    nf, tf = wg.shape[1], wg.shape[3]
    tm = FFN_TM
    return pl.pallas_call(
        _ffn_kernel,
        grid=(m // tm, nf),
        in_specs=[pl.BlockSpec((tm, d), lambda i, j: (i, 0)),
                  pl.BlockSpec((tm, d), lambda i, j: (i, 0)),
                  pl.BlockSpec((None, 1, d), lambda i, j: (layer, 0, 0)),
                  pl.BlockSpec((None, 1, d), lambda i, j: (next_layer, 0, 0)),
                  pl.BlockSpec((None, None, d, tf), lambda i, j: (layer, j, 0, 0)),
                  pl.BlockSpec((None, None, d, tf), lambda i, j: (layer, j, 0, 0)),
                  pl.BlockSpec((None, tf, d), lambda i, j: (layer, j, 0))],
        out_specs=[pl.BlockSpec((tm, d), lambda i, j: (i, 0)),
                   pl.BlockSpec((tm, d), lambda i, j: (i, 0))],
        out_shape=[jax.ShapeDtypeStruct((m, d), F32), jax.ShapeDtypeStruct((m, d), BF16)],
        scratch_shapes=[pltpu.VMEM((tm, d), F32)],
        compiler_params=_params("parallel", "arbitrary"),
        name="ffn",
    )(x, u, post_g, next_g, wg, wu, wd)


def _matmul_nt_kernel(x_ref, w_ref, o_ref):
    o_ref[...] = _dot_nt(x_ref[...], w_ref[...]).astype(o_ref.dtype)


def _in_proj(u, w_t, layer):
    m, d = u.shape
    n = w_t.shape[1]
    tm, tn = PROJ_TM, _tile_n(n, PROJ_TN_CAP)
    return pl.pallas_call(
        _matmul_nt_kernel,
        grid=(n // tn, m // tm),
        in_specs=[pl.BlockSpec((tm, d), lambda j, i: (i, 0)),
                  pl.BlockSpec((None, tn, d), lambda j, i: (layer, j, 0))],
        out_specs=pl.BlockSpec((tm, tn), lambda j, i: (i, j)),
        out_shape=jax.ShapeDtypeStruct((m, n), BF16),
        compiler_params=_params("parallel", "parallel"),
        name="in_proj",
    )(u, w_t)


def _matmul_t_kernel(w_ref, x_ref, o_ref):
    o_ref[...] = _dot_nt(w_ref[...], x_ref[...]).astype(o_ref.dtype)


def _in_proj_t(u, w_t, layer):
    m, d = u.shape
    n = w_t.shape[1]
    tm = PROJ_TM
    return pl.pallas_call(
        _matmul_t_kernel,
        grid=(m // tm,),
        in_specs=[pl.BlockSpec((None, n, d), lambda i: (layer, 0, 0)),
                  pl.BlockSpec((tm, d), lambda i: (i, 0))],
        out_specs=pl.BlockSpec((n, tm), lambda i: (0, i)),
        out_shape=jax.ShapeDtypeStruct((n, m), BF16),
        compiler_params=_params("parallel"),
        name="in_proj_t",
    )(w_t, u)


def _softmax_tile_t(s_ref, adds, masks, v_t, m_ref, acc_ref):
    probs, alphas = [], []
    for c in range(s_ref.shape[1] // LANE):
        cols = slice(c * LANE, (c + 1) * LANE)
        s = s_ref[:, cols]
        for mask in masks[c]:
            s = jnp.where(mask, s, MASKED) if mask.dtype == jnp.bool_ else s + mask
        m_old = m_ref[:, cols]
        m_tile = jnp.max(s, axis=0, keepdims=True)
        if adds[c] is not None:
            m_tile = m_tile + adds[c]
        m_new = jnp.maximum(m_old, m_tile)
        alpha = jnp.exp2(m_old - m_new)
        p = jnp.exp2(s - (m_new if adds[c] is None else m_new - adds[c]))
        m_ref[:, cols] = m_new
        probs.append(p.astype(BF16))
        alphas.append(alpha)
    p_t = probs[0] if len(probs) == 1 else jnp.concatenate(probs, axis=1)
    alpha = alphas[0] if len(alphas) == 1 else jnp.concatenate(alphas, axis=1)
    v_ones = jnp.concatenate([v_t, jnp.ones((ONES_ROWS, v_t.shape[1]), BF16)], axis=0)
    acc_ref[...] = alpha * acc_ref[...] + _dot(v_ones, p_t)


def _softmax_init(m_ref, acc_ref):
    m_ref[...] = jnp.full(m_ref.shape, NEG, F32)
    acc_ref[...] = jnp.zeros(acc_ref.shape, F32)


def _softmax_out(acc):
    dv = acc.shape[0] - ONES_ROWS
    return acc[:dv] / acc[dv:dv + 1]


def _mla_prep_kernel(cq_ref, ckv_ref, kr_ref, cs_ref, gq_ref, gkv_ref, wq_ref, wuk_ref, wuvt_ref,
                     q_ref, k_ref, vt_ref):
    cs = cs_ref[...]
    qn = _rms(cq_ref[...].astype(F32), gq_ref[...]).astype(BF16)
    q = _dot(qn, wq_ref[...])
    kvn = _rms(ckv_ref[...].astype(F32), gkv_ref[...]).astype(BF16)
    k_nope = _dot(kvn, wuk_ref[...])
    t = kr_ref[...].astype(F32) * cs
    k_pe = (t + pltpu.roll(t, MLA_ROPE, 1)).astype(BF16)
    for h in range(MLA_HEADS):
        lo = h * HEAD_PAD
        q_ref[:, lo:lo + MLA_NOPE] = q[:, lo:lo + MLA_NOPE].astype(BF16)
        q_ref[:, lo + MLA_NOPE:lo + HEAD_PAD] = (q[:, lo + MLA_NOPE:lo + HEAD_PAD] * cs).astype(BF16)
        k_ref[:, lo:lo + MLA_NOPE] = k_nope[:, h * MLA_NOPE:(h + 1) * MLA_NOPE].astype(BF16)
        k_ref[:, lo + MLA_NOPE:lo + HEAD_PAD] = k_pe
    vt_ref[...] = _dot_nt(wuvt_ref[...], kvn).astype(BF16)


def _mla_prep(z, cs, gq, gkv, wq, wuk, wuv_t, layer, zoff):
    m = z.shape[0]
    tm = PREP_TM
    hq = MLA_HEADS * HEAD_PAD
    hv = MLA_HEADS * MLA_V
    const = lambda i: (layer, 0, 0)
    return pl.pallas_call(
        _mla_prep_kernel,
        grid=(m // tm,),
        in_specs=[pl.BlockSpec((tm, MLA_Q_LORA), lambda i: (i, zoff["cq"] // MLA_Q_LORA)),
                  pl.BlockSpec((tm, MLA_KV_LORA), lambda i: (i, zoff["ckv"] // MLA_KV_LORA)),
                  pl.BlockSpec((tm, LANE), lambda i: (i, zoff["kr"] // LANE)),
                  pl.BlockSpec((tm, LANE), lambda i: (i, 0)),
                  pl.BlockSpec((None, 1, MLA_Q_LORA), const),
                  pl.BlockSpec((None, 1, MLA_KV_LORA), const),
                  pl.BlockSpec((None, MLA_Q_LORA, hq), const),
                  pl.BlockSpec((None, MLA_KV_LORA, MLA_HEADS * MLA_NOPE), const),
                  pl.BlockSpec((None, hv, MLA_KV_LORA), const)],
        out_specs=[pl.BlockSpec((tm, hq), lambda i: (i, 0)),
                   pl.BlockSpec((tm, hq), lambda i: (i, 0)),
                   pl.BlockSpec((hv, tm), lambda i: (0, i))],
        out_shape=[jax.ShapeDtypeStruct((m, hq), BF16),
                   jax.ShapeDtypeStruct((m, hq), BF16),
                   jax.ShapeDtypeStruct((hv, m), BF16)],
        compiler_params=_params("parallel"),
        name="mla_prep",
    )(z, z, z, cs, gq, gkv, wq, wuk, wuv_t)


def _by_parity(kt, fn):
    @pl.when(kt % 2 == 0)
    def _():
        fn(0)

    @pl.when(kt % 2 == 1)
    def _():
        fn(1)


def _mla_attn_kernel(q_ref, k_ref, vt_ref, o_ref, s_ref, m_ref, acc_ref, *, tq, tk, heads):
    i = pl.program_id(2)
    nc = tq // LANE
    _softmax_init(m_ref, acc_ref)
    qt = i * tq + lax.broadcasted_iota(jnp.int32, (1, tq), 1)
    kcol = lax.broadcasted_iota(jnp.int32, (tk, 1), 0)
    qk = [slice(h * HEAD_PAD, (h + 1) * HEAD_PAD) for h in range(heads)]
    vd = [slice(h * MLA_V, (h + 1) * MLA_V) for h in range(heads)]
    n_full = (i * tq) // tk
    n_all = (i * tq + tq + tk - 1) // tk

    def logits(kt, buf):
        ks = pl.multiple_of(kt * tk, tk)
        for h in range(heads):
            s_ref[buf, h] = _dot_nt(k_ref[pl.ds(ks, tk), qk[h]], q_ref[:, qk[h]])

    def step(kt, buf, masked, prefetch=True):
        ks = pl.multiple_of(kt * tk, tk)
        masks = [()] * nc
        if masked:
            masks = [(ks + kcol <= qt[:, c * LANE:(c + 1) * LANE],) for c in range(nc)]
        for h in range(heads):
            _softmax_tile_t(s_ref.at[buf, h], [None] * nc, masks, vt_ref[vd[h], pl.ds(ks, tk)],
                            m_ref.at[h], acc_ref.at[h])
        if prefetch:
            logits(kt + 1, 1 - buf)

    def full_body(kt, c):
        _by_parity(kt, lambda buf: step(kt, buf, False))
        return c

    def diag_body(kt, c):
        _by_parity(kt, lambda buf: step(kt, buf, True))
        return c

    logits(0, 0)
    lax.fori_loop(0, n_full, full_body, 0)
    lax.fori_loop(n_full, n_all - 1, diag_body, 0)
    _by_parity(n_all - 1, lambda buf: step(n_all - 1, buf, True, prefetch=False))
    for h in range(heads):
        o_ref[:, vd[h]] = _softmax_out(acc_ref[h]).T.astype(o_ref.dtype)


def _mla_attn(qp, kp, vt, batch, seq):
    tq, tk, heads = MLA_TQ, MLA_TK, MLA_HEADS_PER_STEP
    nq = seq // tq
    kern = functools.partial(_mla_attn_kernel, tq=tq, tk=tk, heads=heads)
    return pl.pallas_call(
        kern,
        grid=(batch, MLA_HEADS // heads, nq),
        in_specs=[pl.BlockSpec((tq, heads * HEAD_PAD), lambda b, h, i: (b * nq + i, h)),
                  pl.BlockSpec((seq, heads * HEAD_PAD), lambda b, h, i: (b, h)),
                  pl.BlockSpec((heads * MLA_V, seq), lambda b, h, i: (h, b))],
        out_specs=pl.BlockSpec((tq, heads * MLA_V), lambda b, h, i: (b * nq + i, h)),
        out_shape=jax.ShapeDtypeStruct((batch * seq, MLA_HEADS * MLA_V), BF16),
        scratch_shapes=[pltpu.VMEM((2, heads, tk, tq), F32), pltpu.VMEM((heads, 1, tq), F32),
                        pltpu.VMEM((heads, MLA_V + ONES_ROWS, tq), F32)],
        compiler_params=_params("parallel", "parallel", "parallel"),
        name="mla_attn",
    )(qp, kp, vt)


def _compress(a_ref, pe_ref, w1_ref, w2_ref):
    a = a_ref[...].astype(F32)
    half = a.shape[1]
    first = _dot((a + pe_ref[0:1, :]).astype(BF16), w1_ref[0:half, :])
    second = _dot((a + pe_ref[1:2, :]).astype(BF16), w1_ref[half:2 * half, :])
    n = a.shape[0]
    hidden = first + pltpu.roll(second, n - 1, 0)
    return _dot((hidden * jax.nn.sigmoid(hidden)).astype(BF16), w2_ref[...])


def _nsa_cmp_kernel(ak_ref, av_ref, pek_ref, pev_ref, w1k_ref, w2k_ref, w1v_ref, w2v_ref,
                    kc_ref, vct_ref):
    kc_ref[...] = _compress(ak_ref, pek_ref, w1k_ref, w2k_ref).astype(BF16)
    vct_ref[...] = _compress(av_ref, pev_ref, w1v_ref, w2v_ref).T.astype(BF16)


def _nsa_cmp(ak, av, pek, pev, w1k, w2k, w1v, w2v, layer):
    bg, nh, dk16 = ak.shape
    dv16 = av.shape[-1]
    const = lambda i: (layer, 0, 0)
    return pl.pallas_call(
        _nsa_cmp_kernel,
        grid=(bg,),
        in_specs=[pl.BlockSpec((None, nh, dk16), lambda i: (i, 0, 0)),
                  pl.BlockSpec((None, nh, dv16), lambda i: (i, 0, 0)),
                  pl.BlockSpec((None, 2, dk16), const),
                  pl.BlockSpec((None, 2, dv16), const),
                  pl.BlockSpec((None, 2 * dk16, CMP_HIDDEN), const),
                  pl.BlockSpec((None, CMP_HIDDEN, HEAD_PAD), const),
                  pl.BlockSpec((None, 2 * dv16, CMP_HIDDEN), const),
                  pl.BlockSpec((None, CMP_HIDDEN, NSA_DV), const)],
        out_specs=[pl.BlockSpec((None, nh, HEAD_PAD), lambda i: (i, 0, 0)),
                   pl.BlockSpec((None, NSA_DV, nh), lambda i: (i, 0, 0))],
        out_shape=[jax.ShapeDtypeStruct((bg, nh, HEAD_PAD), BF16),
                   jax.ShapeDtypeStruct((bg, NSA_DV, nh), BF16)],
        compiler_params=_params("parallel"),
        name="nsa_cmp",
    )(ak, av, pek, pev, w1k, w2k, w1v, w2v)


def _split3(x):
    a = x.astype(BF16)
    r = x - a.astype(F32)
    b = r.astype(BF16)
    c = (r - b.astype(F32)).astype(BF16)
    return a, b, c


def _lookup(table_row, idx):
    rows, width = idx.shape
    table = jnp.broadcast_to(table_row, (rows, LANE))
    chunks = [jnp.take_along_axis(table, idx[:, c:c + LANE], axis=1, mode="promise_in_bounds")
              for c in range(0, width, LANE)]
    return chunks[0] if len(chunks) == 1 else jnp.concatenate(chunks, axis=1)


def _nsa_attn_kernel(far_ref, q_ref, kc_ref, vct_ref, ks_ref, vst_ref, kw_ref, vwt_ref, ng_ref,
                     prow_ref, pcol_ref, pcmp_ref, tbl_ref, o_ref,
                     s_ref, m_ref, acc_ref, *, tq, tk, seq):
    b, i = pl.program_id(0), pl.program_id(2)
    nq, nk = seq // tq, seq // tk
    J = NSA_GROUP
    n_slc = seq // SLC_LEN
    n_cmp = (seq - CMP_LEN) // CMP_STRIDE + 1
    t0 = pl.multiple_of(i * tq, tq)

    qs = jnp.concatenate([q_ref[:, j * HEAD_PAD:(j + 1) * HEAD_PAD] for j in range(J)], axis=0)
    qpos = prow_ref[:, pl.ds(t0, tq)]
    qt = t0 + lax.broadcasted_iota(jnp.int32, (1, tq), 1)
    tbl = tbl_ref[...]
    far_bias = [tbl[j:j + 1, LANE - 1:LANE] for j in range(J)]
    head = [slice(j * tq, (j + 1) * tq) for j in range(J)]

    blk = lax.broadcasted_iota(jnp.int32, (LANE, 1), 0)
    keep_c = (blk * CMP_STRIDE + (CMP_LEN - 1) <= qt) & (blk < n_cmp)
    idx_c = jnp.clip(qpos - pcmp_ref[...], 0, LANE - 1)
    s_c = _dot_nt(kc_ref[...], qs)
    p_heads = []
    for j in range(J):
        s = jnp.where(keep_c, s_c[:, head[j]] + _lookup(tbl[j:j + 1, :], idx_c), NEG)
        e = jnp.where(keep_c, jnp.exp2(s - jnp.max(s, axis=0, keepdims=True)), 0.0)
        den = jnp.sum(e, axis=0, keepdims=True)
        p_heads.append(e / jnp.where(den > 0.0, den, 1.0))
    o_cmp = _dot(vct_ref[...], jnp.concatenate([p.astype(BF16) for p in p_heads], axis=1))

    rows = 32
    assert n_slc <= rows
    m_row = lax.broadcasted_iota(jnp.int32, (rows, LANE), 0)
    n_col = lax.broadcasted_iota(jnp.int32, (rows, LANE), 1)
    per = SLC_LEN // CMP_STRIDE
    back = (CMP_LEN - 1) // CMP_STRIDE
    overlap = ((n_col >= per * m_row - back) & (n_col <= per * m_row + per - 1)
               & (n_col < n_cmp) & (m_row < n_slc)).astype(BF16)
    imp = sum(_dot(overlap, part) for part in _split3(sum(p_heads)))
    m_blk = lax.broadcasted_iota(jnp.int32, (rows, 1), 0)
    cur = qt // SLC_LEN
    valid = m_blk <= cur
    forced = valid & ((m_blk == 0) | (m_blk >= cur - 1))
    score = jnp.where(forced, FORCED_SCORE, jnp.where(valid, imp, -1.0))
    rank = jnp.zeros((rows, tq), jnp.int32)
    for mp in range(n_slc):
        other = score[mp:mp + 1, :]
        ahead = (other > score) | ((other == score) & (m_blk > mp))
        rank = rank + ahead.astype(jnp.int32)
    selected = valid & (rank < min(SLC_TOPN, n_slc))
    sel_penalty = jnp.where(selected, 0.0, MASKED).astype(BF16)

    SLC, WIN = 0, 1
    _softmax_init(m_ref, acc_ref)
    sub = tk // LANE

    qchunks = [slice(c, c + LANE) for c in range(0, tq, LANE)]

    assert tq == tk

    def slc_masks(kidx, diagonal):
        expand = (kidx // SLC_LEN == lax.broadcasted_iota(jnp.int32, (1, rows), 1)).astype(BF16)
        penalty = _dot(expand, sel_penalty)
        if diagonal:
            return [(penalty[:, c], kidx <= qt[:, c]) for c in qchunks]
        return [(penalty[:, c],) for c in qchunks]

    def win_masks(kidx, diagonal):
        return [(jnp.where(lax.bitcast_convert_type(qt[:, c] - kidx, jnp.uint32) < WINDOW,
                           0.0, MASKED),) for c in qchunks]

    k_refs, vt_refs, masks_of = (ks_ref, kw_ref), (vst_ref, vwt_ref), (slc_masks, win_masks)
    adds = [far_bias[j] for j in range(J) for _ in qchunks]
    last = (t0 + tq + tk - 1) // tk
    first_win = jnp.maximum(t0 - (WINDOW - 1), 0) // tk

    def logits(kt, buf, slots):
        ks = pl.multiple_of(kt * tk, tk)
        for slot in slots:
            s_ref[buf, slot] = _dot_nt(k_refs[slot][pl.ds(ks, tk), :], qs)

    def near_bias(kt, buf, slots):
        ks = pl.multiple_of(kt * tk, tk)
        for hb in range(sub):
            kb = kt * sub + hb
            for qc, qcols in enumerate(qchunks):
                near = ((far_ref[((b * nq + i) * len(qchunks) + qc) * (nk * sub) + kb] == 0)
                        & (kb * LANE < t0 + (qc + 1) * LANE))

                @pl.when(near)
                def _():
                    kpos = pcol_ref[pl.ds(pl.multiple_of(ks + hb * LANE, LANE), LANE), :]
                    idx = jnp.clip(qpos[:, qcols] - kpos, 0, LANE - 1)
                    for j in range(J):
                        delta = _lookup(tbl[j:j + 1, :] - far_bias[j], idx)
                        cols = slice(j * tq + qc * LANE, j * tq + (qc + 1) * LANE)
                        for slot in slots:
                            s_ref[buf, slot, hb * LANE:(hb + 1) * LANE, cols] += delta

    def step(kt, buf, slots, final=False):
        ks = pl.multiple_of(kt * tk, tk)
        kidx = ks + lax.broadcasted_iota(jnp.int32, (tk, 1), 0)
        for slot in slots:
            _softmax_tile_t(s_ref.at[buf, slot], adds, masks_of[slot](kidx, final) * J,
                            vt_refs[slot][:, pl.ds(ks, tk)], m_ref.at[slot], acc_ref.at[slot])
        if not final:
            logits(kt + 1, 1 - buf, slots)
            near_bias(kt + 1, 1 - buf, slots)

    def slc_body(kt, c):
        _by_parity(kt, lambda buf: step(kt, buf, (SLC,)))
        return c

    def both_body(kt, c):
        _by_parity(kt, lambda buf: step(kt, buf, (SLC, WIN)))
        return c

    def enter_window(buf):
        logits(first_win, buf, (WIN,))
        near_bias(first_win, buf, (WIN,))

    logits(0, 0, (SLC,))
    near_bias(0, 0, (SLC,))
    lax.fori_loop(0, first_win, slc_body, 0)
    _by_parity(first_win, enter_window)
    lax.fori_loop(first_win, last - 1, both_body, 0)
    _by_parity(last - 1, lambda buf: step(last - 1, buf, (SLC, WIN), final=True))
    o_slc = _softmax_out(acc_ref[SLC])
    o_win = _softmax_out(acc_ref[WIN])

    gates = jax.nn.sigmoid(ng_ref[...].astype(F32)).T
    for j in range(J):
        o = (gates[3 * j:3 * j + 1, :] * o_cmp[:, head[j]]
             + gates[3 * j + 1:3 * j + 2, :] * o_slc[:, head[j]]
             + gates[3 * j + 2:3 * j + 3, :] * o_win[:, head[j]])
        o_ref[:, j * NSA_DV:(j + 1) * NSA_DV] = o.T.astype(o_ref.dtype)


def _nsa_attn(far, z, kc, vct, v_t, pos_row, pos_col, pos_cmp, tbl, batch, seq, zoff):
    tq, tk = NSA_TQ, NSA_TK
    nq = seq // tq
    G, J = NSA_KV_HEADS, NSA_GROUP
    R = J * tq
    kern = functools.partial(_nsa_attn_kernel, tq=tq, tk=tk, seq=seq)
    grid_spec = pltpu.PrefetchScalarGridSpec(
        num_scalar_prefetch=1,
        grid=(batch, G, nq),
        in_specs=[
            pl.BlockSpec((tq, J * HEAD_PAD), lambda b, g, i, far: (b * nq + i, g)),
            pl.BlockSpec((None, LANE, HEAD_PAD), lambda b, g, i, far: (b * G + g, 0, 0)),
            pl.BlockSpec((None, NSA_DV, LANE), lambda b, g, i, far: (b * G + g, 0, 0)),
            pl.BlockSpec((seq, HEAD_PAD), lambda b, g, i, far: (b, zoff["ks"] // HEAD_PAD + g)),
            pl.BlockSpec((NSA_DV, seq), lambda b, g, i, far: (g, b)),
            pl.BlockSpec((seq, HEAD_PAD), lambda b, g, i, far: (b, zoff["kw"] // HEAD_PAD + g)),
            pl.BlockSpec((NSA_DV, seq), lambda b, g, i, far: (G + g, b)),
            pl.BlockSpec((tq, LANE), lambda b, g, i, far: (b * nq + i, zoff["ng"] // LANE + g)),
            pl.BlockSpec((None, 1, seq), lambda b, g, i, far: (b, 0, 0)),
            pl.BlockSpec((seq, 1), lambda b, g, i, far: (b, 0)),
            pl.BlockSpec((LANE, 1), lambda b, g, i, far: (b, 0)),
            pl.BlockSpec((None, J, LANE), lambda b, g, i, far: (g, 0, 0)),
        ],
        out_specs=pl.BlockSpec((tq, J * NSA_DV), lambda b, g, i, far: (b * nq + i, g)),
        scratch_shapes=[pltpu.VMEM((2, 2, tk, R), F32), pltpu.VMEM((2, 1, R), F32),
                        pltpu.VMEM((2, NSA_DV + ONES_ROWS, R), F32)],
    )
    return pl.pallas_call(
        kern,
        grid_spec=grid_spec,
        out_shape=jax.ShapeDtypeStruct((batch * seq, NSA_HEADS * NSA_DV), BF16),
        compiler_params=_params("parallel", "parallel", "parallel"),
        name="nsa_attn",
    )(far, z, kc, vct, z, v_t, z, v_t, z, pos_row, pos_col, pos_cmp, tbl)


def _merge_kernel(a_ref, b_ref, ga_ref, gb_ref, h_ref, wa_ref, wb_ref, wo_ref, post_g_ref,
                  next_g_ref, o_ref, un_ref):
    ya = _dot(a_ref[...], wa_ref[...])
    yb = _dot(b_ref[...], wb_ref[...])
    m = (jax.nn.sigmoid(ga_ref[...].astype(F32)) * ya
         + jax.nn.sigmoid(gb_ref[...].astype(F32)) * yb).astype(BF16)
    y = _dot(m, wo_ref[...])
    out = h_ref[...] + _rms(y, post_g_ref[...])
    o_ref[...] = out
    un_ref[...] = _rms(out, next_g_ref[...]).astype(BF16)


def _merge(a, bb, z, h, wa, wb, wo, post_g, next_g, layer, zoff):
    m, d = h.shape
    tm = MERGE_TM
    da, db = a.shape[1], bb.shape[1]
    gblk = zoff["mg"] // d
    const = lambda i: (layer, 0, 0)
    return pl.pallas_call(
        _merge_kernel,
        grid=(m // tm,),
        in_specs=[pl.BlockSpec((tm, da), lambda i: (i, 0)),
                  pl.BlockSpec((tm, db), lambda i: (i, 0)),
                  pl.BlockSpec((tm, d), lambda i: (i, gblk)),
                  pl.BlockSpec((tm, d), lambda i: (i, gblk + 1)),
                  pl.BlockSpec((tm, d), lambda i: (i, 0)),
                  pl.BlockSpec((None, da, d), const, pipeline_mode=pl.Buffered(1)),
                  pl.BlockSpec((None, db, d), const, pipeline_mode=pl.Buffered(1)),
                  pl.BlockSpec((None, d, d), const, pipeline_mode=pl.Buffered(1)),
                  pl.BlockSpec((None, 1, d), const),
                  pl.BlockSpec((None, 1, d), const)],
        out_specs=[pl.BlockSpec((tm, d), lambda i: (i, 0)),
                   pl.BlockSpec((tm, d), lambda i: (i, 0))],
        out_shape=[jax.ShapeDtypeStruct((m, d), F32), jax.ShapeDtypeStruct((m, d), BF16)],
        compiler_params=_params("parallel"),
        name="merge_out",
    )(a, bb, z, z, h, wa, wb, wo, post_g, next_g)


def _z_layout(d_model):
    G = NSA_KV_HEADS
    widths = [("q", NSA_HEADS * HEAD_PAD), ("mg", 2 * d_model), ("cq", MLA_Q_LORA),
              ("ckv", MLA_KV_LORA), ("kc", G * HEAD_PAD), ("ks", G * HEAD_PAD),
              ("kw", G * HEAD_PAD), ("vc", G * NSA_DV), ("kr", LANE), ("ng", G * LANE)]
    off, pos = {}, 0
    for name, w in widths:
        off[name] = pos
        pos += w
    off["used"] = pos
    off["total"] = -(-pos // Z_PAD) * Z_PAD
    assert off["q"] == 0 and off["mg"] % d_model == 0
    assert off["cq"] % MLA_Q_LORA == 0 and off["ckv"] % MLA_KV_LORA == 0
    assert all(off[k] % HEAD_PAD == 0 for k in ("kc", "ks", "kw"))
    return off


def _swap_halves(w):
    half = w.shape[-1] // 2
    return jnp.concatenate([-w[..., half:], w[..., :half]], axis=-1)


def _pad_last(w, width):
    return jnp.pad(w, [(0, 0)] * (w.ndim - 1) + [(0, width - w.shape[-1])])


def _w_in_pieces(d_model, zoff):
    G, J = NSA_KV_HEADS, NSA_GROUP
    splits = [MLA_Q_LORA, MLA_KV_LORA, MLA_ROPE, NSA_HEADS * NSA_DK,
              G * NSA_DK, G * NSA_DV, G * NSA_DK, G * NSA_DV, G * NSA_DK, G * NSA_DV,
              NSA_HEADS * 3, 2 * d_model]
    names = ["cq", "ckv", "kr", "q", "kc", "vc", "ks", "vs", "kw", "vw", "ng", "mg"]
    src = dict(zip(names, [0] + [int(v) for v in np.cumsum(splits)[:-1]]))
    half = MLA_ROPE // 2
    pieces = []
    for h in range(NSA_HEADS):
        pieces.append((zoff["q"] + h * HEAD_PAD, HEAD_PAD,
                       [(src["q"] + h * NSA_DK, NSA_DK, NSA_DK ** -0.5 * LOG2E)]))
    pieces.append((zoff["mg"], 2 * d_model, [(src["mg"], 2 * d_model, None)]))
    pieces.append((zoff["cq"], MLA_Q_LORA, [(src["cq"], MLA_Q_LORA, None)]))
    pieces.append((zoff["ckv"], MLA_KV_LORA, [(src["ckv"], MLA_KV_LORA, None)]))
    for name in ("kc", "ks", "kw"):
        for g in range(G):
            pieces.append((zoff[name] + g * HEAD_PAD, HEAD_PAD, [(src[name] + g * NSA_DK, NSA_DK, None)]))
    pieces.append((zoff["vc"], G * NSA_DV, [(src["vc"], G * NSA_DV, None)]))
    pieces.append((zoff["kr"], LANE, [(src["kr"], MLA_ROPE, None), (src["kr"] + half, half, -1.0),
                                      (src["kr"], half, None)]))
    for g in range(G):
        pieces.append((zoff["ng"] + g * LANE, LANE, [(src["ng"] + g * J * 3, J * 3, None)]))
    if zoff["total"] > zoff["used"]:
        pieces.append((zoff["used"], zoff["total"] - zoff["used"], []))
    return pieces, (src["vs"], src["vw"])


SUBLANE = 8
COPY_ROWS = 512


def _copy_rows(w_ref, o_ref, src, dst, n, scale):
    for off in range(0, n, COPY_ROWS):
        m = min(COPY_ROWS, n - off)
        x = w_ref[src + off:src + off + m, :]
        o_ref[dst + off:dst + off + m, :] = (x if scale is None else x * scale).astype(BF16)


def _w_in_layout_kernel(w_ref, o_ref, vt_ref, *, pieces, value_rows):
    tc = w_ref.shape[1]
    dst = 0
    for src, n in value_rows:
        _copy_rows(w_ref, vt_ref, src, dst, n, None)
        dst += n
    for dst, width, parts in pieces:
        used = 0
        for src, n, scale in parts:
            if src % SUBLANE == 0 and n % SUBLANE == 0:
                _copy_rows(w_ref, o_ref, src, dst + used, n, scale)
                used += n
            else:
                assert len(parts) == 1 and scale is None
                lo = src - src % SUBLANE
                win = -(-(src % SUBLANE + n) // SUBLANE) * SUBLANE
                x = w_ref[lo:lo + win, :]
                if src % SUBLANE:
                    x = pltpu.roll(x, win - src % SUBLANE, 0)
                x = jnp.where(lax.broadcasted_iota(jnp.int32, (win, tc), 0) < n, x, 0.0)
                x = jnp.concatenate([x, jnp.zeros((width - win, tc), F32)], axis=0)
                o_ref[dst:dst + width, :] = x.astype(BF16)
                used = width
        if used < width:
            o_ref[dst + used:dst + width, :] = jnp.zeros((width - used, tc), BF16)


def _layout_w_in(w_in, zoff):
    L, D, n_in = w_in.shape
    G = NSA_KV_HEADS
    pieces, (vs, vw) = _w_in_pieces(D, zoff)
    tc = 256
    nv = 2 * G * NSA_DV
    value_rows = ((vs, G * NSA_DV), (vw, G * NSA_DV))
    return pl.pallas_call(
        functools.partial(_w_in_layout_kernel, pieces=pieces, value_rows=value_rows),
        grid=(L, D // tc),
        in_specs=[pl.BlockSpec((None, n_in, tc), lambda l, c: (l, 0, c))],
        out_specs=[pl.BlockSpec((None, zoff["total"], tc), lambda l, c: (l, 0, c)),
                   pl.BlockSpec((None, nv, tc), lambda l, c: (l, 0, c))],
        out_shape=[jax.ShapeDtypeStruct((L, zoff["total"], D), BF16),
                   jax.ShapeDtypeStruct((L, nv, D), BF16)],
        compiler_params=_params("parallel", "parallel"),
        name="w_in_layout",
    )(jnp.swapaxes(w_in, 1, 2))


def _layout_w_q_up(w):
    L, r, _ = w.shape
    w = w.reshape(L, r, MLA_HEADS, MLA_NOPE + MLA_ROPE) * ((MLA_NOPE + MLA_ROPE) ** -0.5 * LOG2E)
    rope = w[..., MLA_NOPE:]
    return jnp.concatenate([w[..., :MLA_NOPE], rope, _swap_halves(rope)], axis=-1).reshape(
        L, r, MLA_HEADS * HEAD_PAD).astype(BF16)


def kernel(x, positions, rel_bias, ffn1_pre_g, ffn1_post_g, ffn1_w_gate, ffn1_w_up, ffn1_w_down, mix_pre_g, mix_post_g, w_in, mla_q_norm_g, mla_w_q_up, mla_kv_norm_g, mla_w_uk, mla_w_uv, cmp_pe_k, cmp_w1_k, cmp_w2_k, cmp_pe_v, cmp_w1_v, cmp_w2_v, w_branch_mla, w_branch_nsa, w_out, ffn2_pre_g, ffn2_post_g, ffn2_w_gate, ffn2_w_up, ffn2_w_down):
    B, S, D = x.shape
    L = w_in.shape[0]
    M = B * S
    G, J = NSA_KV_HEADS, NSA_GROUP
    zoff = _z_layout(D)
    n_half = S // CMP_STRIDE
    n_cmp = (S - CMP_LEN) // CMP_STRIDE + 1
    assert n_half == LANE and n_cmp <= LANE

    gain = lambda g: g.reshape(L, 1, -1)
    bf = lambda w: w.astype(BF16)
    w_z, w_vt = _layout_w_in(w_in, zoff)
    w_q = _layout_w_q_up(mla_w_q_up)
    w_uk, w_uv_t = bf(mla_w_uk), bf(mla_w_uv).transpose(0, 2, 1)
    pe_k = cmp_pe_k.reshape(L, 2, CMP_STRIDE * NSA_DK)
    pe_v = cmp_pe_v.reshape(L, 2, CMP_STRIDE * NSA_DV)
    w2_k = bf(_pad_last(cmp_w2_k, HEAD_PAD))
    tf = _tile_n(ffn1_w_gate.shape[-1], FFN_TF)

    def tiles(w):
        return bf(w).reshape(L, D, -1, tf).transpose(0, 2, 1, 3)

    f1 = (tiles(ffn1_w_gate), tiles(ffn1_w_up), bf(ffn1_w_down))
    f2 = (tiles(ffn2_w_gate), tiles(ffn2_w_up), bf(ffn2_w_down))
    w1_k, w1_v, w2_v = bf(cmp_w1_k), bf(cmp_w1_v), bf(cmp_w2_v)
    w_a, w_b, w_o = bf(w_branch_mla), bf(w_branch_nsa), bf(w_out)
    g_f1pre, g_f1post, g_mpre, g_mpost = gain(ffn1_pre_g), gain(ffn1_post_g), gain(mix_pre_g), gain(mix_post_g)
    g_f2pre, g_f2post, g_q, g_kv = gain(ffn2_pre_g), gain(ffn2_post_g), gain(mla_q_norm_g), gain(mla_kv_norm_g)

    pos_col = positions.reshape(M, 1)
    pos_row = positions.reshape(B, 1, S)
    pos_cmp = _pad_last(positions[:, CMP_LEN - 1::CMP_STRIDE][:, :n_cmp], LANE).reshape(B * LANE, 1)
    q_min = positions.reshape(B, S // LANE, LANE).min(axis=-1)
    k_max = positions.reshape(B, S // LANE, LANE).max(axis=-1)
    far = (q_min[:, :, None] - k_max[:, None, :] >= FAR_DIST).astype(jnp.int32).reshape(-1)
    tbl = (jnp.take(rel_bias, jnp.asarray(BUCKET_OF_DIST), axis=0).T * LOG2E).reshape(G, J, LANE)
    half = MLA_ROPE // 2
    inv = ROPE_BASE ** (-jnp.arange(half, dtype=F32) * 2.0 / MLA_ROPE)
    cs = _rope_table(pos_col, jnp.concatenate([inv, inv]).reshape(1, MLA_ROPE))

    h = x.reshape(M, D)
    u = _rmsnorm(h, g_f1pre, 0)
    for l in range(L):
        h, u = _ffn(h, u, g_f1post, g_mpre, l, *f1, l)
        z = _in_proj(u, w_z, l)
        v_t = _in_proj_t(u, w_vt, l)
        qp, kp, vt = _mla_prep(z, cs, g_q, g_kv, w_q, w_uk, w_uv_t, l, zoff)
        a = _mla_attn(qp, kp, vt, B, S)

        def half_blocks(off, width, d):
            t = z[:, off:off + G * width].reshape(B, S, G, width)[..., :d]
            return t.transpose(0, 2, 1, 3).reshape(B * G, n_half, CMP_STRIDE * d)

        kc, vct = _nsa_cmp(half_blocks(zoff["kc"], HEAD_PAD, NSA_DK),
                           half_blocks(zoff["vc"], NSA_DV, NSA_DV),
                           pe_k, pe_v, w1_k, w2_k, w1_v, w2_v, l)
        nsa = _nsa_attn(far, z, kc, vct, v_t, pos_row, pos_col, pos_cmp, tbl, B, S, zoff)
        h, u = _merge(a, nsa, z, h, w_a, w_b, w_o, g_mpost, g_f2pre, l, zoff)
        h, u = _ffn(h, u, g_f2post, g_f1pre, min(l + 1, L - 1), *f2, l)
    return h.reshape(B, S, D)
```

```python
import functools
import math

import numpy as np
import jax
import jax.numpy as jnp
from jax import lax
from jax.experimental import pallas as pl
from jax.experimental.pallas import tpu as pltpu

EPS = 1e-6
MLA_HEADS = 8
MLA_Q_LORA = 512
MLA_KV_LORA = 512
MLA_NOPE = 128
MLA_ROPE = 64
MLA_V = 128
ROPE_BASE = 10000.0
NSA_HEADS = 8
NSA_KV_HEADS = 2
NSA_GROUP = NSA_HEADS // NSA_KV_HEADS
NSA_DK = 192
NSA_DV = 128
CMP_LEN = 32
CMP_STRIDE = 16
CMP_HIDDEN = 256
SLC_LEN = 64
SLC_TOPN = 16
WINDOW = 512
FORCED_SCORE = 1e6
REL_BUCKETS = 32
REL_MAX_DIST = 128
NEG = -1e30
MASKED = 2 * NEG
LOG2E = 1.0 / math.log(2.0)

LANE = 128
ONES_ROWS = 16
HEAD_PAD = 256
VMEM_LIMIT = 56 * 1024 * 1024
BF16 = jnp.bfloat16
F32 = jnp.float32

FFN_TM, FFN_TF = 512, 512
PROJ_TM = 1024
PROJ_TN_CAP = 2560
Z_PAD = 5 * LANE
PREP_TM = 512
MLA_TQ, MLA_TK = 512, 512
MLA_HEADS_PER_STEP = 4
NSA_TQ, NSA_TK = 256, 256
MERGE_TM = 512


def _bucket_of_distance():
    n = np.arange(LANE)
    max_exact = REL_BUCKETS // 2
    large = max_exact + (np.log(np.maximum(n, 1) / max_exact) / math.log(REL_MAX_DIST / max_exact)
                         * (REL_BUCKETS - max_exact)).astype(np.int32)
    bucket = np.where(n < max_exact, n, np.minimum(large, REL_BUCKETS - 1)).astype(np.int32)
    assert bucket[-1] == REL_BUCKETS - 1
    return bucket


BUCKET_OF_DIST = _bucket_of_distance()
FAR_DIST = int(np.max(np.nonzero(BUCKET_OF_DIST != REL_BUCKETS - 1)[0])) + 1


def _params(*sem):
    return pltpu.CompilerParams(dimension_semantics=sem, vmem_limit_bytes=VMEM_LIMIT)


def _rms(x, g):
    return x * lax.rsqrt(jnp.mean(x * x, axis=-1, keepdims=True) + EPS) * g


def _dot(a, b):
    return jnp.dot(a, b, preferred_element_type=F32)


def _dot_nt(a, b):
    return lax.dot_general(a, b, (((1,), (1,)), ((), ())), preferred_element_type=F32)


def _tile_n(n, cap):
    best = LANE
    for t in range(LANE, cap + 1, LANE):
        if n % t == 0:
            best = t
    return best


def _rmsnorm_kernel(x_ref, g_ref, o_ref):
    o_ref[...] = _rms(x_ref[...], g_ref[...]).astype(o_ref.dtype)


def _rmsnorm(x, g, layer):
    m, d = x.shape
    tm = 512
    return pl.pallas_call(
        _rmsnorm_kernel,
        grid=(m // tm,),
        in_specs=[pl.BlockSpec((tm, d), lambda i: (i, 0)),
                  pl.BlockSpec((None, 1, d), lambda i: (layer, 0, 0))],
        out_specs=pl.BlockSpec((tm, d), lambda i: (i, 0)),
        out_shape=jax.ShapeDtypeStruct((m, d), BF16),
        compiler_params=_params("parallel"),
        name="rmsnorm",
    )(x, g)


def _rope_table_kernel(pos_ref, inv_ref, o_ref):
    ang = pos_ref[...].astype(F32) * inv_ref[...]
    o_ref[...] = jnp.concatenate([jnp.cos(ang), jnp.sin(ang)], axis=1)


def _rope_table(pos_col, inv):
    m = pos_col.shape[0]
    tm = 512
    return pl.pallas_call(
        _rope_table_kernel,
        grid=(m // tm,),
        in_specs=[pl.BlockSpec((tm, 1), lambda i: (i, 0)),
                  pl.BlockSpec((1, MLA_ROPE), lambda i: (0, 0))],
        out_specs=pl.BlockSpec((tm, 2 * MLA_ROPE), lambda i: (i, 0)),
        out_shape=jax.ShapeDtypeStruct((m, 2 * MLA_ROPE), F32),
        compiler_params=_params("parallel"),
        name="rope_table",
    )(pos_col, inv)


def _ffn_kernel(x_ref, u_ref, post_g_ref, next_g_ref, wg_ref, wu_ref, wd_ref,
                o_ref, un_ref, acc_ref):
    j = pl.program_id(1)

    @pl.when(j == 0)
    def _():
        acc_ref[...] = jnp.zeros(acc_ref.shape, F32)

    u = u_ref[...]
    gate = _dot(u, wg_ref[...])
    up = _dot(u, wu_ref[...])
    hidden = (gate * jax.nn.sigmoid(gate) * up).astype(BF16)
    acc_ref[...] += _dot(hidden, wd_ref[...])

    @pl.when(j == pl.num_programs(1) - 1)
    def _():
        out = x_ref[...] + 0.5 * _rms(acc_ref[...], post_g_ref[...])
        o_ref[...] = out
        un_ref[...] = _rms(out, next_g_ref[...]).astype(BF16)


def _ffn(x, u, post_g, next_g, next_layer, wg, wu, wd, layer):
    m, d = x.shape
    f = wg.shape[-1]
    tm, tf = FFN_TM, _tile_n(f, FFN_TF)
    return pl.pallas_call(
        _ffn_kernel,
        grid=(m // tm, f // tf),
        in_specs=[pl.BlockSpec((tm, d), lambda i, j: (i, 0)),
                  pl.BlockSpec((tm, d), lambda i, j: (i, 0)),
                  pl.BlockSpec((None, 1, d), lambda i, j: (layer, 0, 0)),
                  pl.BlockSpec((None, 1, d), lambda i, j: (next_layer, 0, 0)),
                  pl.BlockSpec((None, d, tf), lambda i, j: (layer, 0, j)),
                  pl.BlockSpec((None, d, tf), lambda i, j: (layer, 0, j)),
                  pl.BlockSpec((None, tf, d), lambda i, j: (layer, j, 0))],
        out_specs=[pl.BlockSpec((tm, d), lambda i, j: (i, 0)),
                   pl.BlockSpec((tm, d), lambda i, j: (i, 0))],
        out_shape=[jax.ShapeDtypeStruct((m, d), F32), jax.ShapeDtypeStruct((m, d), BF16)],
        scratch_shapes=[pltpu.VMEM((tm, d), F32)],
        compiler_params=_params("parallel", "arbitrary"),
        name="ffn",
    )(x, u, post_g, next_g, wg, wu, wd)


def _matmul_nt_kernel(x_ref, w_ref, o_ref):
    o_ref[...] = _dot_nt(x_ref[...], w_ref[...]).astype(o_ref.dtype)


def _in_proj(u, w_t, layer):
    m, d = u.shape
    n = w_t.shape[1]
    tm, tn = PROJ_TM, _tile_n(n, PROJ_TN_CAP)
    return pl.pallas_call(
        _matmul_nt_kernel,
        grid=(n // tn, m // tm),
        in_specs=[pl.BlockSpec((tm, d), lambda j, i: (i, 0)),
                  pl.BlockSpec((None, tn, d), lambda j, i: (layer, j, 0))],
        out_specs=pl.BlockSpec((tm, tn), lambda j, i: (i, j)),
        out_shape=jax.ShapeDtypeStruct((m, n), BF16),
        compiler_params=_params("parallel", "parallel"),
        name="in_proj",
    )(u, w_t)


def _matmul_t_kernel(w_ref, x_ref, o_ref):
    o_ref[...] = _dot_nt(w_ref[...], x_ref[...]).astype(o_ref.dtype)


def _in_proj_t(u, w_t, layer):
    m, d = u.shape
    n = w_t.shape[1]
    tm = PROJ_TM
    return pl.pallas_call(
        _matmul_t_kernel,
        grid=(m // tm,),
        in_specs=[pl.BlockSpec((None, n, d), lambda i: (layer, 0, 0)),
                  pl.BlockSpec((tm, d), lambda i: (i, 0))],
        out_specs=pl.BlockSpec((n, tm), lambda i: (0, i)),
        out_shape=jax.ShapeDtypeStruct((n, m), BF16),
        compiler_params=_params("parallel"),
        name="in_proj_t",
    )(w_t, u)


def _softmax_tile_t(s_ref, adds, masks, v_t, m_ref, acc_ref):
    probs, alphas = [], []
    for c in range(s_ref.shape[1] // LANE):
        cols = slice(c * LANE, (c + 1) * LANE)
        s = s_ref[:, cols]
        for mask in masks[c]:
            s = jnp.where(mask, s, MASKED) if mask.dtype == jnp.bool_ else s + mask
        m_old = m_ref[:, cols]
        m_tile = jnp.max(s, axis=0, keepdims=True)
        if adds[c] is not None:
            m_tile = m_tile + adds[c]
        m_new = jnp.maximum(m_old, m_tile)
        alpha = jnp.exp2(m_old - m_new)
        p = jnp.exp2(s - (m_new if adds[c] is None else m_new - adds[c]))
        m_ref[:, cols] = m_new
        probs.append(p.astype(BF16))
        alphas.append(alpha)
    p_t = probs[0] if len(probs) == 1 else jnp.concatenate(probs, axis=1)
    alpha = alphas[0] if len(alphas) == 1 else jnp.concatenate(alphas, axis=1)
    v_ones = jnp.concatenate([v_t, jnp.ones((ONES_ROWS, v_t.shape[1]), BF16)], axis=0)
    acc_ref[...] = alpha * acc_ref[...] + _dot(v_ones, p_t)


def _softmax_init(m_ref, acc_ref):
    m_ref[...] = jnp.full(m_ref.shape, NEG, F32)
    acc_ref[...] = jnp.zeros(acc_ref.shape, F32)


def _softmax_out(acc):
    dv = acc.shape[0] - ONES_ROWS
    return acc[:dv] / acc[dv:dv + 1]


def _mla_prep_kernel(cq_ref, ckv_ref, kr_ref, cs_ref, gq_ref, gkv_ref, wq_ref, wuk_ref, wuvt_ref,
                     q_ref, k_ref, vt_ref):
    cs = cs_ref[...]
    qn = _rms(cq_ref[...].astype(F32), gq_ref[...]).astype(BF16)
    q = _dot(qn, wq_ref[...])
    kvn = _rms(ckv_ref[...].astype(F32), gkv_ref[...]).astype(BF16)
    k_nope = _dot(kvn, wuk_ref[...])
    t = kr_ref[...].astype(F32) * cs
    k_pe = (t + pltpu.roll(t, MLA_ROPE, 1)).astype(BF16)
    for h in range(MLA_HEADS):
        lo = h * HEAD_PAD
        q_ref[:, lo:lo + MLA_NOPE] = q[:, lo:lo + MLA_NOPE].astype(BF16)
        q_ref[:, lo + MLA_NOPE:lo + HEAD_PAD] = (q[:, lo + MLA_NOPE:lo + HEAD_PAD] * cs).astype(BF16)
        k_ref[:, lo:lo + MLA_NOPE] = k_nope[:, h * MLA_NOPE:(h + 1) * MLA_NOPE].astype(BF16)
        k_ref[:, lo + MLA_NOPE:lo + HEAD_PAD] = k_pe
    vt_ref[...] = _dot_nt(wuvt_ref[...], kvn).astype(BF16)


def _mla_prep(z, cs, gq, gkv, wq, wuk, wuv_t, layer, zoff):
    m = z.shape[0]
    tm = PREP_TM
    hq = MLA_HEADS * HEAD_PAD
    hv = MLA_HEADS * MLA_V
    const = lambda i: (layer, 0, 0)
    return pl.pallas_call(
        _mla_prep_kernel,
        grid=(m // tm,),
        in_specs=[pl.BlockSpec((tm, MLA_Q_LORA), lambda i: (i, zoff["cq"] // MLA_Q_LORA)),
                  pl.BlockSpec((tm, MLA_KV_LORA), lambda i: (i, zoff["ckv"] // MLA_KV_LORA)),
                  pl.BlockSpec((tm, LANE), lambda i: (i, zoff["kr"] // LANE)),
                  pl.BlockSpec((tm, LANE), lambda i: (i, 0)),
                  pl.BlockSpec((None, 1, MLA_Q_LORA), const),
                  pl.BlockSpec((None, 1, MLA_KV_LORA), const),
                  pl.BlockSpec((None, MLA_Q_LORA, hq), const),
                  pl.BlockSpec((None, MLA_KV_LORA, MLA_HEADS * MLA_NOPE), const),
                  pl.BlockSpec((None, hv, MLA_KV_LORA), const)],
        out_specs=[pl.BlockSpec((tm, hq), lambda i: (i, 0)),
                   pl.BlockSpec((tm, hq), lambda i: (i, 0)),
                   pl.BlockSpec((hv, tm), lambda i: (0, i))],
        out_shape=[jax.ShapeDtypeStruct((m, hq), BF16),
                   jax.ShapeDtypeStruct((m, hq), BF16),
                   jax.ShapeDtypeStruct((hv, m), BF16)],
        compiler_params=_params("parallel"),
        name="mla_prep",
    )(z, z, z, cs, gq, gkv, wq, wuk, wuv_t)


def _by_parity(kt, fn):
    @pl.when(kt % 2 == 0)
    def _():
        fn(0)

    @pl.when(kt % 2 == 1)
    def _():
        fn(1)


def _mla_attn_kernel(q_ref, k_ref, vt_ref, o_ref, s_ref, m_ref, acc_ref, *, tq, tk, heads):
    i = pl.program_id(2)
    nc = tq // LANE
    _softmax_init(m_ref, acc_ref)
    qt = i * tq + lax.broadcasted_iota(jnp.int32, (1, tq), 1)
    kcol = lax.broadcasted_iota(jnp.int32, (tk, 1), 0)
    qk = [slice(h * HEAD_PAD, (h + 1) * HEAD_PAD) for h in range(heads)]
    vd = [slice(h * MLA_V, (h + 1) * MLA_V) for h in range(heads)]
    n_full = (i * tq) // tk
    n_all = (i * tq + tq + tk - 1) // tk

    def logits(kt, buf):
        ks = pl.multiple_of(kt * tk, tk)
        for h in range(heads):
            s_ref[buf, h] = _dot_nt(k_ref[pl.ds(ks, tk), qk[h]], q_ref[:, qk[h]])

    def step(kt, buf, masked, prefetch=True):
        ks = pl.multiple_of(kt * tk, tk)
        masks = [()] * nc
        if masked:
            masks = [(ks + kcol <= qt[:, c * LANE:(c + 1) * LANE],) for c in range(nc)]
        for h in range(heads):
            _softmax_tile_t(s_ref.at[buf, h], [None] * nc, masks, vt_ref[vd[h], pl.ds(ks, tk)],
                            m_ref.at[h], acc_ref.at[h])
        if prefetch:
            logits(kt + 1, 1 - buf)

    def full_body(kt, c):
        _by_parity(kt, lambda buf: step(kt, buf, False))
        return c

    def diag_body(kt, c):
        _by_parity(kt, lambda buf: step(kt, buf, True))
        return c

    logits(0, 0)
    lax.fori_loop(0, n_full, full_body, 0)
    lax.fori_loop(n_full, n_all - 1, diag_body, 0)
    _by_parity(n_all - 1, lambda buf: step(n_all - 1, buf, True, prefetch=False))
    for h in range(heads):
        o_ref[:, vd[h]] = _softmax_out(acc_ref[h]).T.astype(o_ref.dtype)


def _mla_attn(qp, kp, vt, batch, seq):
    tq, tk, heads = MLA_TQ, MLA_TK, MLA_HEADS_PER_STEP
    nq = seq // tq
    kern = functools.partial(_mla_attn_kernel, tq=tq, tk=tk, heads=heads)
    return pl.pallas_call(
        kern,
        grid=(batch, MLA_HEADS // heads, nq),
        in_specs=[pl.BlockSpec((tq, heads * HEAD_PAD), lambda b, h, i: (b * nq + i, h)),
                  pl.BlockSpec((seq, heads * HEAD_PAD), lambda b, h, i: (b, h)),
                  pl.BlockSpec((heads * MLA_V, seq), lambda b, h, i: (h, b))],
        out_specs=pl.BlockSpec((tq, heads * MLA_V), lambda b, h, i: (b * nq + i, h)),
        out_shape=jax.ShapeDtypeStruct((batch * seq, MLA_HEADS * MLA_V), BF16),
        scratch_shapes=[pltpu.VMEM((2, heads, tk, tq), F32), pltpu.VMEM((heads, 1, tq), F32),
                        pltpu.VMEM((heads, MLA_V + ONES_ROWS, tq), F32)],
        compiler_params=_params("parallel", "parallel", "parallel"),
        name="mla_attn",
    )(qp, kp, vt)


def _compress(z_ref, zf_ref, pe_ref, w1_ref, w2_ref):
    chunks, s, _ = zf_ref.shape
    for c in range(chunks):
        zf_ref[c] = z_ref[:, c * LANE:(c + 1) * LANE].astype(F32)
    d = chunks * LANE
    n = s // CMP_STRIDE
    first = jnp.zeros((n, CMP_HIDDEN), F32)
    second = jnp.zeros((n, CMP_HIDDEN), F32)
    for l in range(CMP_STRIDE):
        toks = [zf_ref[c, pl.ds(l, n, stride=CMP_STRIDE), :] for c in range(chunks)]
        tok = toks[0] if chunks == 1 else jnp.concatenate(toks, axis=1)
        lo, hi = l, CMP_STRIDE + l
        first += _dot((tok + pe_ref[lo:lo + 1, :]).astype(BF16), w1_ref[lo * d:(lo + 1) * d, :])
        second += _dot((tok + pe_ref[hi:hi + 1, :]).astype(BF16), w1_ref[hi * d:(hi + 1) * d, :])
    hidden = first + pltpu.roll(second, n - 1, 0)
    return _dot((hidden * jax.nn.sigmoid(hidden)).astype(BF16), w2_ref[...])


def _nsa_cmp_kernel(zk_ref, zv_ref, pek_ref, pev_ref, w1k_ref, w2k_ref, w1v_ref, w2v_ref,
                    kc_ref, vct_ref, kf_ref, vf_ref):
    kc_ref[...] = _compress(zk_ref, kf_ref, pek_ref, w1k_ref, w2k_ref).astype(BF16)
    vct_ref[...] = _compress(zv_ref, vf_ref, pev_ref, w1v_ref, w2v_ref).T.astype(BF16)


def _nsa_cmp(z, pek, pev, w1k, w2k, w1v, w2v, layer, batch, seq, zoff):
    G = NSA_KV_HEADS
    nh = seq // CMP_STRIDE
    const = lambda i: (layer, 0, 0)
    return pl.pallas_call(
        _nsa_cmp_kernel,
        grid=(batch * G,),
        in_specs=[pl.BlockSpec((seq, HEAD_PAD), lambda i: (i // G, zoff["kc"] // HEAD_PAD + i % G)),
                  pl.BlockSpec((seq, NSA_DV), lambda i: (i // G, zoff["vc"] // NSA_DV + i % G)),
                  pl.BlockSpec((None, CMP_LEN, HEAD_PAD), const),
                  pl.BlockSpec((None, CMP_LEN, NSA_DV), const),
                  pl.BlockSpec((None, CMP_LEN * HEAD_PAD, CMP_HIDDEN), const),
                  pl.BlockSpec((None, CMP_HIDDEN, HEAD_PAD), const),
                  pl.BlockSpec((None, CMP_LEN * NSA_DV, CMP_HIDDEN), const),
                  pl.BlockSpec((None, CMP_HIDDEN, NSA_DV), const)],
        out_specs=[pl.BlockSpec((None, nh, HEAD_PAD), lambda i: (i, 0, 0)),
                   pl.BlockSpec((None, NSA_DV, nh), lambda i: (i, 0, 0))],
        out_shape=[jax.ShapeDtypeStruct((batch * G, nh, HEAD_PAD), BF16),
                   jax.ShapeDtypeStruct((batch * G, NSA_DV, nh), BF16)],
        scratch_shapes=[pltpu.VMEM((HEAD_PAD // LANE, seq, LANE), F32),
                        pltpu.VMEM((NSA_DV // LANE, seq, LANE), F32)],
        compiler_params=_params("parallel"),
        name="nsa_cmp",
    )(z, z, pek, pev, w1k, w2k, w1v, w2v)


def _split3(x):
    a = x.astype(BF16)
    r = x - a.astype(F32)
    b = r.astype(BF16)
    c = (r - b.astype(F32)).astype(BF16)
    return a, b, c


def _lookup(table_row, idx):
    rows, width = idx.shape
    table = jnp.broadcast_to(table_row, (rows, LANE))
    chunks = [jnp.take_along_axis(table, idx[:, c:c + LANE], axis=1, mode="promise_in_bounds")
              for c in range(0, width, LANE)]
    return chunks[0] if len(chunks) == 1 else jnp.concatenate(chunks, axis=1)


def _nsa_attn_kernel(far_ref, q_ref, kc_ref, vct_ref, ks_ref, vst_ref, kw_ref, vwt_ref, ng_ref,
                     prow_ref, pcol_ref, pcmp_ref, tbl_ref, o_ref,
                     s_ref, m_ref, acc_ref, *, tq, tk, seq):
    b, i = pl.program_id(0), pl.program_id(2)
    nq, nk = seq // tq, seq // tk
    J = NSA_GROUP
    n_slc = seq // SLC_LEN
    n_cmp = (seq - CMP_LEN) // CMP_STRIDE + 1
    t0 = pl.multiple_of(i * tq, tq)

    qs = jnp.concatenate([q_ref[:, j * HEAD_PAD:(j + 1) * HEAD_PAD] for j in range(J)], axis=0)
    qpos = prow_ref[:, pl.ds(t0, tq)]
    qt = t0 + lax.broadcasted_iota(jnp.int32, (1, tq), 1)
    tbl = tbl_ref[...]
    far_bias = [tbl[j:j + 1, LANE - 1:LANE] for j in range(J)]
    head = [slice(j * tq, (j + 1) * tq) for j in range(J)]

    blk = lax.broadcasted_iota(jnp.int32, (LANE, 1), 0)
    keep_c = (blk * CMP_STRIDE + (CMP_LEN - 1) <= qt) & (blk < n_cmp)
    idx_c = jnp.clip(qpos - pcmp_ref[...], 0, LANE - 1)
    s_c = _dot_nt(kc_ref[...], qs)
    p_heads = []
    for j in range(J):
        s = jnp.where(keep_c, s_c[:, head[j]] + _lookup(tbl[j:j + 1, :], idx_c), NEG)
        e = jnp.where(keep_c, jnp.exp2(s - jnp.max(s, axis=0, keepdims=True)), 0.0)
        den = jnp.sum(e, axis=0, keepdims=True)
        p_heads.append(e / jnp.where(den > 0.0, den, 1.0))
    o_cmp = _dot(vct_ref[...], jnp.concatenate([p.astype(BF16) for p in p_heads], axis=1))

    rows = 32
    assert n_slc <= rows
    m_row = lax.broadcasted_iota(jnp.int32, (rows, LANE), 0)
    n_col = lax.broadcasted_iota(jnp.int32, (rows, LANE), 1)
    per = SLC_LEN // CMP_STRIDE
    back = (CMP_LEN - 1) // CMP_STRIDE
    overlap = ((n_col >= per * m_row - back) & (n_col <= per * m_row + per - 1)
               & (n_col < n_cmp) & (m_row < n_slc)).astype(BF16)
    imp = sum(_dot(overlap, part) for part in _split3(sum(p_heads)))
    m_blk = lax.broadcasted_iota(jnp.int32, (rows, 1), 0)
    cur = qt // SLC_LEN
    valid = m_blk <= cur
    forced = valid & ((m_blk == 0) | (m_blk >= cur - 1))
    score = jnp.where(forced, FORCED_SCORE, jnp.where(valid, imp, -1.0))
    rank = jnp.zeros((rows, tq), jnp.int32)
    for mp in range(n_slc):
        other = score[mp:mp + 1, :]
        ahead = (other > score) | ((other == score) & (m_blk > mp))
        rank = rank + ahead.astype(jnp.int32)
    selected = valid & (rank < min(SLC_TOPN, n_slc))
    sel_penalty = jnp.where(selected, 0.0, MASKED).astype(BF16)

    SLC, WIN = 0, 1
    _softmax_init(m_ref, acc_ref)
    sub = tk // LANE

    qchunks = [slice(c, c + LANE) for c in range(0, tq, LANE)]

    assert tq == tk

    def slc_masks(kidx, diagonal):
        expand = (kidx // SLC_LEN == lax.broadcasted_iota(jnp.int32, (1, rows), 1)).astype(BF16)
        penalty = _dot(expand, sel_penalty)
        if diagonal:
            return [(penalty[:, c], kidx <= qt[:, c]) for c in qchunks]
        return [(penalty[:, c],) for c in qchunks]

    def win_masks(kidx, diagonal):
        return [(jnp.where(lax.bitcast_convert_type(qt[:, c] - kidx, jnp.uint32) < WINDOW,
                           0.0, MASKED),) for c in qchunks]

    k_refs, vt_refs, masks_of = (ks_ref, kw_ref), (vst_ref, vwt_ref), (slc_masks, win_masks)
    adds = [far_bias[j] for j in range(J) for _ in qchunks]
    last = (t0 + tq + tk - 1) // tk
    first_win = jnp.maximum(t0 - (WINDOW - 1), 0) // tk

    def logits(kt, buf, slots):
        ks = pl.multiple_of(kt * tk, tk)
        for slot in slots:
            s_ref[buf, slot] = _dot_nt(k_refs[slot][pl.ds(ks, tk), :], qs)

    def near_bias(kt, buf, slots):
        ks = pl.multiple_of(kt * tk, tk)
        for hb in range(sub):
            kb = kt * sub + hb
            for qc, qcols in enumerate(qchunks):
                near = ((far_ref[((b * nq + i) * len(qchunks) + qc) * (nk * sub) + kb] == 0)
                        & (kb * LANE < t0 + (qc + 1) * LANE))

                @pl.when(near)
                def _():
                    kpos = pcol_ref[pl.ds(pl.multiple_of(ks + hb * LANE, LANE), LANE), :]
                    idx = jnp.clip(qpos[:, qcols] - kpos, 0, LANE - 1)
                    for j in range(J):
                        delta = _lookup(tbl[j:j + 1, :] - far_bias[j], idx)
                        cols = slice(j * tq + qc * LANE, j * tq + (qc + 1) * LANE)
                        for slot in slots:
                            s_ref[buf, slot, hb * LANE:(hb + 1) * LANE, cols] += delta

    def step(kt, buf, slots, final=False):
        ks = pl.multiple_of(kt * tk, tk)
        kidx = ks + lax.broadcasted_iota(jnp.int32, (tk, 1), 0)
        for slot in slots:
            _softmax_tile_t(s_ref.at[buf, slot], adds, masks_of[slot](kidx, final) * J,
                            vt_refs[slot][:, pl.ds(ks, tk)], m_ref.at[slot], acc_ref.at[slot])
        if not final:
            logits(kt + 1, 1 - buf, slots)
            near_bias(kt + 1, 1 - buf, slots)

    def slc_body(kt, c):
        _by_parity(kt, lambda buf: step(kt, buf, (SLC,)))
        return c

    def both_body(kt, c):
        _by_parity(kt, lambda buf: step(kt, buf, (SLC, WIN)))
        return c

    def enter_window(buf):
        logits(first_win, buf, (WIN,))
        near_bias(first_win, buf, (WIN,))

    logits(0, 0, (SLC,))
    near_bias(0, 0, (SLC,))
    lax.fori_loop(0, first_win, slc_body, 0)
    _by_parity(first_win, enter_window)
    lax.fori_loop(first_win, last - 1, both_body, 0)
    _by_parity(last - 1, lambda buf: step(last - 1, buf, (SLC, WIN), final=True))
    o_slc = _softmax_out(acc_ref[SLC])
    o_win = _softmax_out(acc_ref[WIN])

    gates = jax.nn.sigmoid(ng_ref[...].astype(F32)).T
    for j in range(J):
        o = (gates[3 * j:3 * j + 1, :] * o_cmp[:, head[j]]
             + gates[3 * j + 1:3 * j + 2, :] * o_slc[:, head[j]]
             + gates[3 * j + 2:3 * j + 3, :] * o_win[:, head[j]])
        o_ref[:, j * NSA_DV:(j + 1) * NSA_DV] = o.T.astype(o_ref.dtype)


def _nsa_attn(far, z, kc, vct, v_t, pos_row, pos_col, pos_cmp, tbl, batch, seq, zoff):
    tq, tk = NSA_TQ, NSA_TK
    nq = seq // tq
    G, J = NSA_KV_HEADS, NSA_GROUP
    R = J * tq
    kern = functools.partial(_nsa_attn_kernel, tq=tq, tk=tk, seq=seq)
    grid_spec = pltpu.PrefetchScalarGridSpec(
        num_scalar_prefetch=1,
        grid=(batch, G, nq),
        in_specs=[
            pl.BlockSpec((tq, J * HEAD_PAD), lambda b, g, i, far: (b * nq + i, g)),
            pl.BlockSpec((None, LANE, HEAD_PAD), lambda b, g, i, far: (b * G + g, 0, 0)),
            pl.BlockSpec((None, NSA_DV, LANE), lambda b, g, i, far: (b * G + g, 0, 0)),
            pl.BlockSpec((seq, HEAD_PAD), lambda b, g, i, far: (b, zoff["ks"] // HEAD_PAD + g)),
            pl.BlockSpec((NSA_DV, seq), lambda b, g, i, far: (g, b)),
            pl.BlockSpec((seq, HEAD_PAD), lambda b, g, i, far: (b, zoff["kw"] // HEAD_PAD + g)),
            pl.BlockSpec((NSA_DV, seq), lambda b, g, i, far: (G + g, b)),
            pl.BlockSpec((tq, LANE), lambda b, g, i, far: (b * nq + i, zoff["ng"] // LANE + g)),
            pl.BlockSpec((None, 1, seq), lambda b, g, i, far: (b, 0, 0)),
            pl.BlockSpec((seq, 1), lambda b, g, i, far: (b, 0)),
            pl.BlockSpec((LANE, 1), lambda b, g, i, far: (b, 0)),
            pl.BlockSpec((None, J, LANE), lambda b, g, i, far: (g, 0, 0)),
        ],
        out_specs=pl.BlockSpec((tq, J * NSA_DV), lambda b, g, i, far: (b * nq + i, g)),
        scratch_shapes=[pltpu.VMEM((2, 2, tk, R), F32), pltpu.VMEM((2, 1, R), F32),
                        pltpu.VMEM((2, NSA_DV + ONES_ROWS, R), F32)],
    )
    return pl.pallas_call(
        kern,
        grid_spec=grid_spec,
        out_shape=jax.ShapeDtypeStruct((batch * seq, NSA_HEADS * NSA_DV), BF16),
        compiler_params=_params("parallel", "parallel", "parallel"),
        name="nsa_attn",
    )(far, z, kc, vct, z, v_t, z, v_t, z, pos_row, pos_col, pos_cmp, tbl)


def _merge_kernel(a_ref, b_ref, ga_ref, gb_ref, h_ref, wa_ref, wb_ref, wo_ref, post_g_ref,
                  next_g_ref, o_ref, un_ref):
    ya = _dot(a_ref[...], wa_ref[...])
    yb = _dot(b_ref[...], wb_ref[...])
    m = (jax.nn.sigmoid(ga_ref[...].astype(F32)) * ya
         + jax.nn.sigmoid(gb_ref[...].astype(F32)) * yb).astype(BF16)
    y = _dot(m, wo_ref[...])
    out = h_ref[...] + _rms(y, post_g_ref[...])
    o_ref[...] = out
    un_ref[...] = _rms(out, next_g_ref[...]).astype(BF16)


def _merge(a, bb, z, h, wa, wb, wo, post_g, next_g, layer, zoff):
    m, d = h.shape
    tm = MERGE_TM
    da, db = a.shape[1], bb.shape[1]
    gblk = zoff["mg"] // d
    const = lambda i: (layer, 0, 0)
    return pl.pallas_call(
        _merge_kernel,
        grid=(m // tm,),
        in_specs=[pl.BlockSpec((tm, da), lambda i: (i, 0)),
                  pl.BlockSpec((tm, db), lambda i: (i, 0)),
                  pl.BlockSpec((tm, d), lambda i: (i, gblk)),
                  pl.BlockSpec((tm, d), lambda i: (i, gblk + 1)),
                  pl.BlockSpec((tm, d), lambda i: (i, 0)),
                  pl.BlockSpec((None, da, d), const, pipeline_mode=pl.Buffered(1)),
                  pl.BlockSpec((None, db, d), const, pipeline_mode=pl.Buffered(1)),
                  pl.BlockSpec((None, d, d), const, pipeline_mode=pl.Buffered(1)),
                  pl.BlockSpec((None, 1, d), const),
                  pl.BlockSpec((None, 1, d), const)],
        out_specs=[pl.BlockSpec((tm, d), lambda i: (i, 0)),
                   pl.BlockSpec((tm, d), lambda i: (i, 0))],
        out_shape=[jax.ShapeDtypeStruct((m, d), F32), jax.ShapeDtypeStruct((m, d), BF16)],
        compiler_params=_params("parallel"),
        name="merge_out",
    )(a, bb, z, z, h, wa, wb, wo, post_g, next_g)


def _z_layout(d_model):
    G = NSA_KV_HEADS
    widths = [("q", NSA_HEADS * HEAD_PAD), ("mg", 2 * d_model), ("cq", MLA_Q_LORA),
              ("ckv", MLA_KV_LORA), ("kc", G * HEAD_PAD), ("ks", G * HEAD_PAD),
              ("kw", G * HEAD_PAD), ("vc", G * NSA_DV), ("kr", LANE), ("ng", G * LANE)]
    off, pos = {}, 0
    for name, w in widths:
        off[name] = pos
        pos += w
    off["used"] = pos
    off["total"] = -(-pos // Z_PAD) * Z_PAD
    assert off["q"] == 0 and off["mg"] % d_model == 0
    assert off["cq"] % MLA_Q_LORA == 0 and off["ckv"] % MLA_KV_LORA == 0
    assert all(off[k] % HEAD_PAD == 0 for k in ("kc", "ks", "kw"))
    return off


def _swap_halves(w):
    half = w.shape[-1] // 2
    return jnp.concatenate([-w[..., half:], w[..., :half]], axis=-1)


def _pad_last(w, width):
    return jnp.pad(w, [(0, 0)] * (w.ndim - 1) + [(0, width - w.shape[-1])])


def _w_in_pieces(d_model, zoff):
    G, J = NSA_KV_HEADS, NSA_GROUP
    splits = [MLA_Q_LORA, MLA_KV_LORA, MLA_ROPE, NSA_HEADS * NSA_DK,
              G * NSA_DK, G * NSA_DV, G * NSA_DK, G * NSA_DV, G * NSA_DK, G * NSA_DV,
              NSA_HEADS * 3, 2 * d_model]
    names = ["cq", "ckv", "kr", "q", "kc", "vc", "ks", "vs", "kw", "vw", "ng", "mg"]
    src = dict(zip(names, [0] + [int(v) for v in np.cumsum(splits)[:-1]]))
    half = MLA_ROPE // 2
    pieces = []
    for h in range(NSA_HEADS):
        pieces.append((zoff["q"] + h * HEAD_PAD, HEAD_PAD,
                       [(src["q"] + h * NSA_DK, NSA_DK, NSA_DK ** -0.5 * LOG2E)]))
    pieces.append((zoff["mg"], 2 * d_model, [(src["mg"], 2 * d_model, None)]))
    pieces.append((zoff["cq"], MLA_Q_LORA, [(src["cq"], MLA_Q_LORA, None)]))
    pieces.append((zoff["ckv"], MLA_KV_LORA, [(src["ckv"], MLA_KV_LORA, None)]))
    for name in ("kc", "ks", "kw"):
        for g in range(G):
            pieces.append((zoff[name] + g * HEAD_PAD, HEAD_PAD, [(src[name] + g * NSA_DK, NSA_DK, None)]))
    pieces.append((zoff["vc"], G * NSA_DV, [(src["vc"], G * NSA_DV, None)]))
    pieces.append((zoff["kr"], LANE, [(src["kr"], MLA_ROPE, None), (src["kr"] + half, half, -1.0),
                                      (src["kr"], half, None)]))
    for g in range(G):
        pieces.append((zoff["ng"] + g * LANE, LANE, [(src["ng"] + g * J * 3, J * 3, None)]))
    if zoff["total"] > zoff["used"]:
        pieces.append((zoff["used"], zoff["total"] - zoff["used"], []))
    return pieces, (src["vs"], src["vw"])


SUBLANE = 8
COPY_ROWS = 512


def _copy_rows(w_ref, o_ref, src, dst, n, scale):
    for off in range(0, n, COPY_ROWS):
        m = min(COPY_ROWS, n - off)
        x = w_ref[src + off:src + off + m, :]
        o_ref[dst + off:dst + off + m, :] = (x if scale is None else x * scale).astype(BF16)


def _w_in_layout_kernel(w_ref, o_ref, vt_ref, *, pieces, value_rows):
    tc = w_ref.shape[1]
    dst = 0
    for src, n in value_rows:
        _copy_rows(w_ref, vt_ref, src, dst, n, None)
        dst += n
    for dst, width, parts in pieces:
        used = 0
        for src, n, scale in parts:
            if src % SUBLANE == 0 and n % SUBLANE == 0:
                _copy_rows(w_ref, o_ref, src, dst + used, n, scale)
                used += n
            else:
                assert len(parts) == 1 and scale is None
                lo = src - src % SUBLANE
                win = -(-(src % SUBLANE + n) // SUBLANE) * SUBLANE
                x = w_ref[lo:lo + win, :]
                if src % SUBLANE:
                    x = pltpu.roll(x, win - src % SUBLANE, 0)
                x = jnp.where(lax.broadcasted_iota(jnp.int32, (win, tc), 0) < n, x, 0.0)
                x = jnp.concatenate([x, jnp.zeros((width - win, tc), F32)], axis=0)
                o_ref[dst:dst + width, :] = x.astype(BF16)
                used = width
        if used < width:
            o_ref[dst + used:dst + width, :] = jnp.zeros((width - used, tc), BF16)


def _layout_w_in(w_in, zoff):
    L, D, n_in = w_in.shape
    G = NSA_KV_HEADS
    pieces, (vs, vw) = _w_in_pieces(D, zoff)
    tc = 256
    nv = 2 * G * NSA_DV
    value_rows = ((vs, G * NSA_DV), (vw, G * NSA_DV))
    return pl.pallas_call(
        functools.partial(_w_in_layout_kernel, pieces=pieces, value_rows=value_rows),
        grid=(L, D // tc),
        in_specs=[pl.BlockSpec((None, n_in, tc), lambda l, c: (l, 0, c))],
        out_specs=[pl.BlockSpec((None, zoff["total"], tc), lambda l, c: (l, 0, c)),
                   pl.BlockSpec((None, nv, tc), lambda l, c: (l, 0, c))],
        out_shape=[jax.ShapeDtypeStruct((L, zoff["total"], D), BF16),
                   jax.ShapeDtypeStruct((L, nv, D), BF16)],
        compiler_params=_params("parallel", "parallel"),
        name="w_in_layout",
    )(jnp.swapaxes(w_in, 1, 2))


def _layout_w_q_up(w):
    L, r, _ = w.shape
    w = w.reshape(L, r, MLA_HEADS, MLA_NOPE + MLA_ROPE) * ((MLA_NOPE + MLA_ROPE) ** -0.5 * LOG2E)
    rope = w[..., MLA_NOPE:]
    return jnp.concatenate([w[..., :MLA_NOPE], rope, _swap_halves(rope)], axis=-1).reshape(
        L, r, MLA_HEADS * HEAD_PAD).astype(BF16)


def kernel(x, positions, rel_bias, ffn1_pre_g, ffn1_post_g, ffn1_w_gate, ffn1_w_up, ffn1_w_down, mix_pre_g, mix_post_g, w_in, mla_q_norm_g, mla_w_q_up, mla_kv_norm_g, mla_w_uk, mla_w_uv, cmp_pe_k, cmp_w1_k, cmp_w2_k, cmp_pe_v, cmp_w1_v, cmp_w2_v, w_branch_mla, w_branch_nsa, w_out, ffn2_pre_g, ffn2_post_g, ffn2_w_gate, ffn2_w_up, ffn2_w_down):
    B, S, D = x.shape
    L = w_in.shape[0]
    M = B * S
    G, J = NSA_KV_HEADS, NSA_GROUP
    zoff = _z_layout(D)
    n_half = S // CMP_STRIDE
    n_cmp = (S - CMP_LEN) // CMP_STRIDE + 1
    assert n_half == LANE and n_cmp <= LANE

    gain = lambda g: g.reshape(L, 1, -1)
    bf = lambda w: w.astype(BF16)
    w_z, w_vt = _layout_w_in(w_in, zoff)
    w_q = _layout_w_q_up(mla_w_q_up)
    w_uk, w_uv_t = bf(mla_w_uk), bf(mla_w_uv).transpose(0, 2, 1)
    pe_k = _pad_last(cmp_pe_k, HEAD_PAD)
    w1_k = bf(jnp.pad(cmp_w1_k.reshape(L, CMP_LEN, NSA_DK, CMP_HIDDEN),
                      ((0, 0), (0, 0), (0, HEAD_PAD - NSA_DK), (0, 0)))
              ).reshape(L, CMP_LEN * HEAD_PAD, CMP_HIDDEN)
    w2_k = bf(_pad_last(cmp_w2_k, HEAD_PAD))
    f1 = (bf(ffn1_w_gate), bf(ffn1_w_up), bf(ffn1_w_down))
    f2 = (bf(ffn2_w_gate), bf(ffn2_w_up), bf(ffn2_w_down))
    w1_v, w2_v = bf(cmp_w1_v), bf(cmp_w2_v)
    w_a, w_b, w_o = bf(w_branch_mla), bf(w_branch_nsa), bf(w_out)
    g_f1pre, g_f1post, g_mpre, g_mpost = gain(ffn1_pre_g), gain(ffn1_post_g), gain(mix_pre_g), gain(mix_post_g)
    g_f2pre, g_f2post, g_q, g_kv = gain(ffn2_pre_g), gain(ffn2_post_g), gain(mla_q_norm_g), gain(mla_kv_norm_g)

    pos_col = positions.reshape(M, 1)
    pos_row = positions.reshape(B, 1, S)
    pos_cmp = _pad_last(positions[:, CMP_LEN - 1::CMP_STRIDE][:, :n_cmp], LANE).reshape(B * LANE, 1)
    q_min = positions.reshape(B, S // LANE, LANE).min(axis=-1)
    k_max = positions.reshape(B, S // LANE, LANE).max(axis=-1)
    far = (q_min[:, :, None] - k_max[:, None, :] >= FAR_DIST).astype(jnp.int32).reshape(-1)
    tbl = (jnp.take(rel_bias, jnp.asarray(BUCKET_OF_DIST), axis=0).T * LOG2E).reshape(G, J, LANE)
    half = MLA_ROPE // 2
    inv = ROPE_BASE ** (-jnp.arange(half, dtype=F32) * 2.0 / MLA_ROPE)
    cs = _rope_table(pos_col, jnp.concatenate([inv, inv]).reshape(1, MLA_ROPE))

    h = x.reshape(M, D)
    u = _rmsnorm(h, g_f1pre, 0)
    for l in range(L):
        h, u = _ffn(h, u, g_f1post, g_mpre, l, *f1, l)
        z = _in_proj(u, w_z, l)
        v_t = _in_proj_t(u, w_vt, l)
        qp, kp, vt = _mla_prep(z, cs, g_q, g_kv, w_q, w_uk, w_uv_t, l, zoff)
        a = _mla_attn(qp, kp, vt, B, S)

        kc, vct = _nsa_cmp(z, pe_k, cmp_pe_v, w1_k, w2_k, w1_v, w2_v, l, B, S, zoff)
        nsa = _nsa_attn(far, z, kc, vct, v_t, pos_row, pos_col, pos_cmp, tbl, B, S, zoff)
        h, u = _merge(a, nsa, z, h, w_a, w_b, w_o, g_mpost, g_f2pre, l, zoff)
        h, u = _ffn(h, u, g_f2post, g_f1pre, min(l + 1, L - 1), *f2, l)
    return h.reshape(B, S, D)
```

```python
import functools
import math

import numpy as np
import jax
import jax.numpy as jnp
from jax import lax
from jax.experimental import pallas as pl
from jax.experimental.pallas import tpu as pltpu

EPS = 1e-6
MLA_HEADS = 8
MLA_Q_LORA = 512
MLA_KV_LORA = 512
MLA_NOPE = 128
MLA_ROPE = 64
MLA_V = 128
ROPE_BASE = 10000.0
NSA_HEADS = 8
NSA_KV_HEADS = 2
NSA_GROUP = NSA_HEADS // NSA_KV_HEADS
NSA_DK = 192
NSA_DV = 128
CMP_LEN = 32
CMP_STRIDE = 16
CMP_HIDDEN = 256
SLC_LEN = 64
SLC_TOPN = 16
WINDOW = 512
FORCED_SCORE = 1e6
REL_BUCKETS = 32
REL_MAX_DIST = 128
NEG = -1e30
MASKED = 2 * NEG
LOG2E = 1.0 / math.log(2.0)

LANE = 128
ONES_ROWS = 16
HEAD_PAD = 256
VMEM_LIMIT = 56 * 1024 * 1024
BF16 = jnp.bfloat16
F32 = jnp.float32

FFN_TM, FFN_TF = 512, 512
PROJ_TM = 1024
PROJ_TN_CAP = 2560
Z_PAD = 5 * LANE
PREP_TM = 512
MLA_TQ, MLA_TK = 512, 512
MLA_HEADS_PER_STEP = 8
NSA_TQ, NSA_TK = 256, 256
MERGE_TM = 512


def _bucket_of_distance():
    n = np.arange(LANE)
    max_exact = REL_BUCKETS // 2
    large = max_exact + (np.log(np.maximum(n, 1) / max_exact) / math.log(REL_MAX_DIST / max_exact)
                         * (REL_BUCKETS - max_exact)).astype(np.int32)
    bucket = np.where(n < max_exact, n, np.minimum(large, REL_BUCKETS - 1)).astype(np.int32)
    assert bucket[-1] == REL_BUCKETS - 1
    return bucket


BUCKET_OF_DIST = _bucket_of_distance()
FAR_DIST = int(np.max(np.nonzero(BUCKET_OF_DIST != REL_BUCKETS - 1)[0])) + 1


def _params(*sem):
    return pltpu.CompilerParams(dimension_semantics=sem, vmem_limit_bytes=VMEM_LIMIT)


def _rms(x, g):
    return x * lax.rsqrt(jnp.mean(x * x, axis=-1, keepdims=True) + EPS) * g


def _dot(a, b):
    return jnp.dot(a, b, preferred_element_type=F32)


def _dot_nt(a, b):
    return lax.dot_general(a, b, (((1,), (1,)), ((), ())), preferred_element_type=F32)


def _tile_n(n, cap):
    best = LANE
    for t in range(LANE, cap + 1, LANE):
        if n % t == 0:
            best = t
    return best


def _rmsnorm_kernel(x_ref, g_ref, o_ref):
    o_ref[...] = _rms(x_ref[...], g_ref[...]).astype(o_ref.dtype)


def _rmsnorm(x, g, layer):
    m, d = x.shape
    tm = 512
    return pl.pallas_call(
        _rmsnorm_kernel,
        grid=(m // tm,),
        in_specs=[pl.BlockSpec((tm, d), lambda i: (i, 0)),
                  pl.BlockSpec((None, 1, d), lambda i: (layer, 0, 0))],
        out_specs=pl.BlockSpec((tm, d), lambda i: (i, 0)),
        out_shape=jax.ShapeDtypeStruct((m, d), BF16),
        compiler_params=_params("parallel"),
        name="rmsnorm",
    )(x, g)


def _rope_table_kernel(pos_ref, inv_ref, o_ref):
    ang = pos_ref[...].astype(F32) * inv_ref[...]
    o_ref[...] = jnp.concatenate([jnp.cos(ang), jnp.sin(ang)], axis=1)


def _rope_table(pos_col, inv):
    m = pos_col.shape[0]
    tm = 512
    return pl.pallas_call(
        _rope_table_kernel,
        grid=(m // tm,),
        in_specs=[pl.BlockSpec((tm, 1), lambda i: (i, 0)),
                  pl.BlockSpec((1, MLA_ROPE), lambda i: (0, 0))],
        out_specs=pl.BlockSpec((tm, 2 * MLA_ROPE), lambda i: (i, 0)),
        out_shape=jax.ShapeDtypeStruct((m, 2 * MLA_ROPE), F32),
        compiler_params=_params("parallel"),
        name="rope_table",
    )(pos_col, inv)


def _ffn_kernel(x_ref, u_ref, post_g_ref, next_g_ref, wg_ref, wu_ref, wd_ref,
                o_ref, un_ref, acc_ref):
    j = pl.program_id(1)

    @pl.when(j == 0)
    def _():
        acc_ref[...] = jnp.zeros(acc_ref.shape, F32)

    u = u_ref[...]
    gate = _dot(u, wg_ref[...])
    up = _dot(u, wu_ref[...])
    hidden = (gate * jax.nn.sigmoid(gate) * up).astype(BF16)
    acc_ref[...] += _dot(hidden, wd_ref[...])

    @pl.when(j == pl.num_programs(1) - 1)
    def _():
        out = x_ref[...] + 0.5 * _rms(acc_ref[...], post_g_ref[...])
        o_ref[...] = out
        un_ref[...] = _rms(out, next_g_ref[...]).astype(BF16)


def _ffn(x, u, post_g, next_g, next_layer, wg, wu, wd, layer):
    m, d = x.shape
    f = wg.shape[-1]
    tm, tf = FFN_TM, _tile_n(f, FFN_TF)
    return pl.pallas_call(
        _ffn_kernel,
        grid=(m // tm, f // tf),
        in_specs=[pl.BlockSpec((tm, d), lambda i, j: (i, 0)),
                  pl.BlockSpec((tm, d), lambda i, j: (i, 0)),
                  pl.BlockSpec((None, 1, d), lambda i, j: (layer, 0, 0)),
                  pl.BlockSpec((None, 1, d), lambda i, j: (next_layer, 0, 0)),
                  pl.BlockSpec((None, d, tf), lambda i, j: (layer, 0, j)),
                  pl.BlockSpec((None, d, tf), lambda i, j: (layer, 0, j)),
                  pl.BlockSpec((None, tf, d), lambda i, j: (layer, j, 0))],
        out_specs=[pl.BlockSpec((tm, d), lambda i, j: (i, 0)),
                   pl.BlockSpec((tm, d), lambda i, j: (i, 0))],
        out_shape=[jax.ShapeDtypeStruct((m, d), F32), jax.ShapeDtypeStruct((m, d), BF16)],
        scratch_shapes=[pltpu.VMEM((tm, d), F32)],
        compiler_params=_params("parallel", "arbitrary"),
        name="ffn",
    )(x, u, post_g, next_g, wg, wu, wd)


def _matmul_nt_kernel(x_ref, w_ref, o_ref):
    o_ref[...] = _dot_nt(x_ref[...], w_ref[...]).astype(o_ref.dtype)


def _in_proj(u, w_t, layer):
    m, d = u.shape
    n = w_t.shape[1]
    tm, tn = PROJ_TM, _tile_n(n, PROJ_TN_CAP)
    return pl.pallas_call(
        _matmul_nt_kernel,
        grid=(n // tn, m // tm),
        in_specs=[pl.BlockSpec((tm, d), lambda j, i: (i, 0)),
                  pl.BlockSpec((None, tn, d), lambda j, i: (layer, j, 0))],
        out_specs=pl.BlockSpec((tm, tn), lambda j, i: (i, j)),
        out_shape=jax.ShapeDtypeStruct((m, n), BF16),
        compiler_params=_params("parallel", "parallel"),
        name="in_proj",
    )(u, w_t)


def _matmul_t_kernel(w_ref, x_ref, o_ref):
    o_ref[...] = _dot_nt(w_ref[...], x_ref[...]).astype(o_ref.dtype)


def _in_proj_t(u, w_t, layer):
    m, d = u.shape
    n = w_t.shape[1]
    tm = PROJ_TM
    return pl.pallas_call(
        _matmul_t_kernel,
        grid=(m // tm,),
        in_specs=[pl.BlockSpec((None, n, d), lambda i: (layer, 0, 0)),
                  pl.BlockSpec((tm, d), lambda i: (i, 0))],
        out_specs=pl.BlockSpec((n, tm), lambda i: (0, i)),
        out_shape=jax.ShapeDtypeStruct((n, m), BF16),
        compiler_params=_params("parallel"),
        name="in_proj_t",
    )(w_t, u)


def _softmax_tile_t(s_ref, adds, masks, v_t, m_ref, acc_ref):
    probs, alphas = [], []
    for c in range(s_ref.shape[1] // LANE):
        cols = slice(c * LANE, (c + 1) * LANE)
        s = s_ref[:, cols]
        for mask in masks[c]:
            s = jnp.where(mask, s, MASKED) if mask.dtype == jnp.bool_ else s + mask
        m_old = m_ref[:, cols]
        m_tile = jnp.max(s, axis=0, keepdims=True)
        if adds[c] is not None:
            m_tile = m_tile + adds[c]
        m_new = jnp.maximum(m_old, m_tile)
        alpha = jnp.exp2(m_old - m_new)
        p = jnp.exp2(s - (m_new if adds[c] is None else m_new - adds[c]))
        m_ref[:, cols] = m_new
        probs.append(p.astype(BF16))
        alphas.append(alpha)
    p_t = probs[0] if len(probs) == 1 else jnp.concatenate(probs, axis=1)
    alpha = alphas[0] if len(alphas) == 1 else jnp.concatenate(alphas, axis=1)
    v_ones = jnp.concatenate([v_t, jnp.ones((ONES_ROWS, v_t.shape[1]), BF16)], axis=0)
    acc_ref[...] = alpha * acc_ref[...] + _dot(v_ones, p_t)


def _softmax_init(m_ref, acc_ref):
    m_ref[...] = jnp.full(m_ref.shape, NEG, F32)
    acc_ref[...] = jnp.zeros(acc_ref.shape, F32)


def _softmax_out(acc):
    dv = acc.shape[0] - ONES_ROWS
    return acc[:dv] / acc[dv:dv + 1]


def _mla_prep_kernel(cq_ref, ckv_ref, kr_ref, cs_ref, gq_ref, gkv_ref, wq_ref, wuk_ref, wuvt_ref,
                     q_ref, k_ref, vt_ref):
    cs = cs_ref[...]
    qn = _rms(cq_ref[...].astype(F32), gq_ref[...]).astype(BF16)
    q = _dot(qn, wq_ref[...])
    kvn = _rms(ckv_ref[...].astype(F32), gkv_ref[...]).astype(BF16)
    k_nope = _dot(kvn, wuk_ref[...])
    t = kr_ref[...].astype(F32) * cs
    k_pe = (t + pltpu.roll(t, MLA_ROPE, 1)).astype(BF16)
    for h in range(MLA_HEADS):
        lo = h * HEAD_PAD
        q_ref[:, lo:lo + MLA_NOPE] = q[:, lo:lo + MLA_NOPE].astype(BF16)
        q_ref[:, lo + MLA_NOPE:lo + HEAD_PAD] = (q[:, lo + MLA_NOPE:lo + HEAD_PAD] * cs).astype(BF16)
        k_ref[:, lo:lo + MLA_NOPE] = k_nope[:, h * MLA_NOPE:(h + 1) * MLA_NOPE].astype(BF16)
        k_ref[:, lo + MLA_NOPE:lo + HEAD_PAD] = k_pe
    vt_ref[...] = _dot_nt(wuvt_ref[...], kvn).astype(BF16)


def _mla_prep(z, cs, gq, gkv, wq, wuk, wuv_t, layer, zoff):
    m = z.shape[0]
    tm = PREP_TM
    hq = MLA_HEADS * HEAD_PAD
    hv = MLA_HEADS * MLA_V
    const = lambda i: (layer, 0, 0)
    return pl.pallas_call(
        _mla_prep_kernel,
        grid=(m // tm,),
        in_specs=[pl.BlockSpec((tm, MLA_Q_LORA), lambda i: (i, zoff["cq"] // MLA_Q_LORA)),
                  pl.BlockSpec((tm, MLA_KV_LORA), lambda i: (i, zoff["ckv"] // MLA_KV_LORA)),
                  pl.BlockSpec((tm, LANE), lambda i: (i, zoff["kr"] // LANE)),
                  pl.BlockSpec((tm, LANE), lambda i: (i, 0)),
                  pl.BlockSpec((None, 1, MLA_Q_LORA), const),
                  pl.BlockSpec((None, 1, MLA_KV_LORA), const),
                  pl.BlockSpec((None, MLA_Q_LORA, hq), const),
                  pl.BlockSpec((None, MLA_KV_LORA, MLA_HEADS * MLA_NOPE), const),
                  pl.BlockSpec((None, hv, MLA_KV_LORA), const)],
        out_specs=[pl.BlockSpec((tm, hq), lambda i: (i, 0)),
                   pl.BlockSpec((tm, hq), lambda i: (i, 0)),
                   pl.BlockSpec((hv, tm), lambda i: (0, i))],
        out_shape=[jax.ShapeDtypeStruct((m, hq), BF16),
                   jax.ShapeDtypeStruct((m, hq), BF16),
                   jax.ShapeDtypeStruct((hv, m), BF16)],
        compiler_params=_params("parallel"),
        name="mla_prep",
    )(z, z, z, cs, gq, gkv, wq, wuk, wuv_t)


def _by_parity(kt, fn):
    @pl.when(kt % 2 == 0)
    def _():
        fn(0)

    @pl.when(kt % 2 == 1)
    def _():
        fn(1)


def _mla_attn_kernel(q_ref, k_ref, vt_ref, o_ref, s_ref, m_ref, acc_ref, *, tq, tk, heads):
    i = pl.program_id(2)
    nc = tq // LANE
    _softmax_init(m_ref, acc_ref)
    qt = i * tq + lax.broadcasted_iota(jnp.int32, (1, tq), 1)
    kcol = lax.broadcasted_iota(jnp.int32, (tk, 1), 0)
    qk = [slice(h * HEAD_PAD, (h + 1) * HEAD_PAD) for h in range(heads)]
    vd = [slice(h * MLA_V, (h + 1) * MLA_V) for h in range(heads)]
    n_full = (i * tq) // tk
    n_all = (i * tq + tq + tk - 1) // tk

    def logits(kt, buf):
        ks = pl.multiple_of(kt * tk, tk)
        for h in range(heads):
            s_ref[buf, h] = _dot_nt(k_ref[pl.ds(ks, tk), qk[h]], q_ref[:, qk[h]])

    def step(kt, buf, masked, prefetch=True):
        ks = pl.multiple_of(kt * tk, tk)
        masks = [()] * nc
        if masked:
            masks = [(ks + kcol <= qt[:, c * LANE:(c + 1) * LANE],) for c in range(nc)]
        for h in range(heads):
            _softmax_tile_t(s_ref.at[buf, h], [None] * nc, masks, vt_ref[vd[h], pl.ds(ks, tk)],
                            m_ref.at[h], acc_ref.at[h])
        if prefetch:
            logits(kt + 1, 1 - buf)

    def full_body(kt, c):
        _by_parity(kt, lambda buf: step(kt, buf, False))
        return c

    def diag_body(kt, c):
        _by_parity(kt, lambda buf: step(kt, buf, True))
        return c

    logits(0, 0)
    lax.fori_loop(0, n_full, full_body, 0)
    lax.fori_loop(n_full, n_all - 1, diag_body, 0)
    _by_parity(n_all - 1, lambda buf: step(n_all - 1, buf, True, prefetch=False))
    for h in range(heads):
        o_ref[:, vd[h]] = _softmax_out(acc_ref[h]).T.astype(o_ref.dtype)


def _mla_attn(qp, kp, vt, batch, seq):
    tq, tk, heads = MLA_TQ, MLA_TK, MLA_HEADS_PER_STEP
    nq = seq // tq
    kern = functools.partial(_mla_attn_kernel, tq=tq, tk=tk, heads=heads)
    return pl.pallas_call(
        kern,
        grid=(batch, MLA_HEADS // heads, nq),
        in_specs=[pl.BlockSpec((tq, heads * HEAD_PAD), lambda b, h, i: (b * nq + i, h)),
                  pl.BlockSpec((seq, heads * HEAD_PAD), lambda b, h, i: (b, h)),
                  pl.BlockSpec((heads * MLA_V, seq), lambda b, h, i: (h, b))],
        out_specs=pl.BlockSpec((tq, heads * MLA_V), lambda b, h, i: (b * nq + i, h)),
        out_shape=jax.ShapeDtypeStruct((batch * seq, MLA_HEADS * MLA_V), BF16),
        scratch_shapes=[pltpu.VMEM((2, heads, tk, tq), F32), pltpu.VMEM((heads, 1, tq), F32),
                        pltpu.VMEM((heads, MLA_V + ONES_ROWS, tq), F32)],
        compiler_params=_params("parallel", "parallel", "parallel"),
        name="mla_attn",
    )(qp, kp, vt)


def _compress(z_ref, zf_ref, pe_ref, w1_ref, w2_ref):
    chunks, s, _ = zf_ref.shape
    for c in range(chunks):
        zf_ref[c] = z_ref[:, c * LANE:(c + 1) * LANE].astype(F32)
    d = chunks * LANE
    n = s // CMP_STRIDE
    first = jnp.zeros((n, CMP_HIDDEN), F32)
    second = jnp.zeros((n, CMP_HIDDEN), F32)
    for l in range(CMP_STRIDE):
        toks = [zf_ref[c, pl.ds(l, n, stride=CMP_STRIDE), :] for c in range(chunks)]
        tok = toks[0] if chunks == 1 else jnp.concatenate(toks, axis=1)
        lo, hi = l, CMP_STRIDE + l
        first += _dot((tok + pe_ref[lo:lo + 1, :]).astype(BF16), w1_ref[lo * d:(lo + 1) * d, :])
        second += _dot((tok + pe_ref[hi:hi + 1, :]).astype(BF16), w1_ref[hi * d:(hi + 1) * d, :])
    hidden = first + pltpu.roll(second, n - 1, 0)
    return _dot((hidden * jax.nn.sigmoid(hidden)).astype(BF16), w2_ref[...])


def _nsa_cmp_kernel(zk_ref, zv_ref, pek_ref, pev_ref, w1k_ref, w2k_ref, w1v_ref, w2v_ref,
                    kc_ref, vct_ref, kf_ref, vf_ref):
    kc_ref[...] = _compress(zk_ref, kf_ref, pek_ref, w1k_ref, w2k_ref).astype(BF16)
    vct_ref[...] = _compress(zv_ref, vf_ref, pev_ref, w1v_ref, w2v_ref).T.astype(BF16)


def _nsa_cmp(z, pek, pev, w1k, w2k, w1v, w2v, layer, batch, seq, zoff):
    G = NSA_KV_HEADS
    nh = seq // CMP_STRIDE
    const = lambda i: (layer, 0, 0)
    return pl.pallas_call(
        _nsa_cmp_kernel,
        grid=(batch * G,),
        in_specs=[pl.BlockSpec((seq, HEAD_PAD), lambda i: (i // G, zoff["kc"] // HEAD_PAD + i % G)),
                  pl.BlockSpec((seq, NSA_DV), lambda i: (i // G, zoff["vc"] // NSA_DV + i % G)),
                  pl.BlockSpec((None, CMP_LEN, HEAD_PAD), const),
                  pl.BlockSpec((None, CMP_LEN, NSA_DV), const),
                  pl.BlockSpec((None, CMP_LEN * HEAD_PAD, CMP_HIDDEN), const),
                  pl.BlockSpec((None, CMP_HIDDEN, HEAD_PAD), const),
                  pl.BlockSpec((None, CMP_LEN * NSA_DV, CMP_HIDDEN), const),
                  pl.BlockSpec((None, CMP_HIDDEN, NSA_DV), const)],
        out_specs=[pl.BlockSpec((None, nh, HEAD_PAD), lambda i: (i, 0, 0)),
                   pl.BlockSpec((None, NSA_DV, nh), lambda i: (i, 0, 0))],
        out_shape=[jax.ShapeDtypeStruct((batch * G, nh, HEAD_PAD), BF16),
                   jax.ShapeDtypeStruct((batch * G, NSA_DV, nh), BF16)],
        scratch_shapes=[pltpu.VMEM((HEAD_PAD // LANE, seq, LANE), F32),
                        pltpu.VMEM((NSA_DV // LANE, seq, LANE), F32)],
        compiler_params=_params("parallel"),
        name="nsa_cmp",
    )(z, z, pek, pev, w1k, w2k, w1v, w2v)


def _split3(x):
    a = x.astype(BF16)
    r = x - a.astype(F32)
    b = r.astype(BF16)
    c = (r - b.astype(F32)).astype(BF16)
    return a, b, c


def _lookup(table_row, idx):
    rows, width = idx.shape
    table = jnp.broadcast_to(table_row, (rows, LANE))
    chunks = [jnp.take_along_axis(table, idx[:, c:c + LANE], axis=1, mode="promise_in_bounds")
              for c in range(0, width, LANE)]
    return chunks[0] if len(chunks) == 1 else jnp.concatenate(chunks, axis=1)


def _nsa_attn_kernel(far_ref, q_ref, kc_ref, vct_ref, ks_ref, vst_ref, kw_ref, vwt_ref, ng_ref,
                     prow_ref, pcol_ref, pcmp_ref, tbl_ref, o_ref,
                     s_ref, m_ref, acc_ref, *, tq, tk, seq):
    b, i = pl.program_id(0), pl.program_id(2)
    nq, nk = seq // tq, seq // tk
    J = NSA_GROUP
    n_slc = seq // SLC_LEN
    n_cmp = (seq - CMP_LEN) // CMP_STRIDE + 1
    t0 = pl.multiple_of(i * tq, tq)

    qs = jnp.concatenate([q_ref[:, j * HEAD_PAD:(j + 1) * HEAD_PAD] for j in range(J)], axis=0)
    qpos = prow_ref[:, pl.ds(t0, tq)]
    qt = t0 + lax.broadcasted_iota(jnp.int32, (1, tq), 1)
    tbl = tbl_ref[...]
    far_bias = [tbl[j:j + 1, LANE - 1:LANE] for j in range(J)]
    head = [slice(j * tq, (j + 1) * tq) for j in range(J)]

    blk = lax.broadcasted_iota(jnp.int32, (LANE, 1), 0)
    keep_c = (blk * CMP_STRIDE + (CMP_LEN - 1) <= qt) & (blk < n_cmp)
    idx_c = jnp.clip(qpos - pcmp_ref[...], 0, LANE - 1)
    s_c = _dot_nt(kc_ref[...], qs)
    p_heads = []
    for j in range(J):
        s = jnp.where(keep_c, s_c[:, head[j]] + _lookup(tbl[j:j + 1, :], idx_c), NEG)
        e = jnp.where(keep_c, jnp.exp2(s - jnp.max(s, axis=0, keepdims=True)), 0.0)
        den = jnp.sum(e, axis=0, keepdims=True)
        p_heads.append(e / jnp.where(den > 0.0, den, 1.0))
    o_cmp = _dot(vct_ref[...], jnp.concatenate([p.astype(BF16) for p in p_heads], axis=1))

    rows = 32
    assert n_slc <= rows
    m_row = lax.broadcasted_iota(jnp.int32, (rows, LANE), 0)
    n_col = lax.broadcasted_iota(jnp.int32, (rows, LANE), 1)
    per = SLC_LEN // CMP_STRIDE
    back = (CMP_LEN - 1) // CMP_STRIDE
    overlap = ((n_col >= per * m_row - back) & (n_col <= per * m_row + per - 1)
               & (n_col < n_cmp) & (m_row < n_slc)).astype(BF16)
    imp = sum(_dot(overlap, part) for part in _split3(sum(p_heads)))
    m_blk = lax.broadcasted_iota(jnp.int32, (rows, 1), 0)
    cur = qt // SLC_LEN
    valid = m_blk <= cur
    forced = valid & ((m_blk == 0) | (m_blk >= cur - 1))
    score = jnp.where(forced, FORCED_SCORE, jnp.where(valid, imp, -1.0))
    rank = jnp.zeros((rows, tq), jnp.int32)
    for mp in range(n_slc):
        other = score[mp:mp + 1, :]
        ahead = (other > score) | ((other == score) & (m_blk > mp))
        rank = rank + ahead.astype(jnp.int32)
    selected = valid & (rank < min(SLC_TOPN, n_slc))
    sel_penalty = jnp.where(selected, 0.0, MASKED).astype(BF16)

    SLC, WIN, SLC_ODD = 0, 1, 2
    _softmax_init(m_ref, acc_ref)
    sub = tk // LANE

    qchunks = [slice(c, c + LANE) for c in range(0, tq, LANE)]

    assert tq == tk

    def slc_masks(kidx, diagonal):
        expand = (kidx // SLC_LEN == lax.broadcasted_iota(jnp.int32, (1, rows), 1)).astype(BF16)
        penalty = _dot(expand, sel_penalty)
        if diagonal:
            return [(penalty[:, c], kidx <= qt[:, c]) for c in qchunks]
        return [(penalty[:, c],) for c in qchunks]

    def win_masks(kidx, diagonal):
        return [(jnp.where(lax.bitcast_convert_type(qt[:, c] - kidx, jnp.uint32) < WINDOW,
                           0.0, MASKED),) for c in qchunks]

    k_refs, vt_refs, masks_of = (ks_ref, kw_ref), (vst_ref, vwt_ref), (slc_masks, win_masks)
    adds = [far_bias[j] for j in range(J) for _ in qchunks]
    last = (t0 + tq + tk - 1) // tk
    first_win = jnp.maximum(t0 - (WINDOW - 1), 0) // tk

    def logits(kt, buf, slot, branch):
        ks = pl.multiple_of(kt * tk, tk)
        s_ref[buf, slot] = _dot_nt(k_refs[branch][pl.ds(ks, tk), :], qs)

    def near_bias(kt, buf, slots):
        ks = pl.multiple_of(kt * tk, tk)
        for hb in range(sub):
            kb = kt * sub + hb
            for qc, qcols in enumerate(qchunks):
                near = ((far_ref[((b * nq + i) * len(qchunks) + qc) * (nk * sub) + kb] == 0)
                        & (kb * LANE < t0 + (qc + 1) * LANE))

                @pl.when(near)
                def _():
                    kpos = pcol_ref[pl.ds(pl.multiple_of(ks + hb * LANE, LANE), LANE), :]
                    idx = jnp.clip(qpos[:, qcols] - kpos, 0, LANE - 1)
                    for j in range(J):
                        delta = _lookup(tbl[j:j + 1, :] - far_bias[j], idx)
                        cols = slice(j * tq + qc * LANE, j * tq + (qc + 1) * LANE)
                        for slot in slots:
                            s_ref[buf, slot, hb * LANE:(hb + 1) * LANE, cols] += delta

    def attend(kt, buf, slot, branch, state, final=False):
        ks = pl.multiple_of(kt * tk, tk)
        kidx = ks + lax.broadcasted_iota(jnp.int32, (tk, 1), 0)
        _softmax_tile_t(s_ref.at[buf, slot], adds, masks_of[branch](kidx, final) * J,
                        vt_refs[branch][:, pl.ds(ks, tk)], m_ref.at[state], acc_ref.at[state])

    def pair_body(p, c):
        kt = 2 * p
        logits(kt, 0, 0, SLC)
        logits(kt + 1, 0, 1, SLC)
        near_bias(kt, 0, (0,))
        near_bias(kt + 1, 0, (1,))
        attend(kt, 0, 0, SLC, SLC)
        attend(kt + 1, 0, 1, SLC, SLC_ODD)
        return c

    lax.fori_loop(0, first_win // 2, pair_body, 0)

    @pl.when(first_win % 2 == 1)
    def _():
        logits(first_win - 1, 0, 0, SLC)
        near_bias(first_win - 1, 0, (0,))
        attend(first_win - 1, 0, 0, SLC, SLC)

    def fill(kt, buf):
        logits(kt, buf, 0, SLC)
        logits(kt, buf, 1, WIN)
        near_bias(kt, buf, (0, 1))

    def step(kt, buf, final=False):
        attend(kt, buf, 0, SLC, SLC, final)
        attend(kt, buf, 1, WIN, WIN, final)
        if not final:
            fill(kt + 1, 1 - buf)

    def both_body(kt, c):
        _by_parity(kt, lambda buf: step(kt, buf))
        return c

    _by_parity(first_win, lambda buf: fill(first_win, buf))
    lax.fori_loop(first_win, last - 1, both_body, 0)
    _by_parity(last - 1, lambda buf: step(last - 1, buf, final=True))
    m_even, m_odd = m_ref[SLC], m_ref[SLC_ODD]
    m_slc = jnp.maximum(m_even, m_odd)
    o_slc = _softmax_out(acc_ref[SLC] * jnp.exp2(m_even - m_slc)
                         + acc_ref[SLC_ODD] * jnp.exp2(m_odd - m_slc))
    o_win = _softmax_out(acc_ref[WIN])

    gates = jax.nn.sigmoid(ng_ref[...].astype(F32)).T
    for j in range(J):
        o = (gates[3 * j:3 * j + 1, :] * o_cmp[:, head[j]]
             + gates[3 * j + 1:3 * j + 2, :] * o_slc[:, head[j]]
             + gates[3 * j + 2:3 * j + 3, :] * o_win[:, head[j]])
        o_ref[:, j * NSA_DV:(j + 1) * NSA_DV] = o.T.astype(o_ref.dtype)


def _nsa_attn(far, z, kc, vct, v_t, pos_row, pos_col, pos_cmp, tbl, batch, seq, zoff):
    tq, tk = NSA_TQ, NSA_TK
    nq = seq // tq
    G, J = NSA_KV_HEADS, NSA_GROUP
    R = J * tq
    kern = functools.partial(_nsa_attn_kernel, tq=tq, tk=tk, seq=seq)
    grid_spec = pltpu.PrefetchScalarGridSpec(
        num_scalar_prefetch=1,
        grid=(batch, G, nq),
        in_specs=[
            pl.BlockSpec((tq, J * HEAD_PAD), lambda b, g, i, far: (b * nq + i, g)),
            pl.BlockSpec((None, LANE, HEAD_PAD), lambda b, g, i, far: (b * G + g, 0, 0)),
            pl.BlockSpec((None, NSA_DV, LANE), lambda b, g, i, far: (b * G + g, 0, 0)),
            pl.BlockSpec((seq, HEAD_PAD), lambda b, g, i, far: (b, zoff["ks"] // HEAD_PAD + g)),
            pl.BlockSpec((NSA_DV, seq), lambda b, g, i, far: (g, b)),
            pl.BlockSpec((seq, HEAD_PAD), lambda b, g, i, far: (b, zoff["kw"] // HEAD_PAD + g)),
            pl.BlockSpec((NSA_DV, seq), lambda b, g, i, far: (G + g, b)),
            pl.BlockSpec((tq, LANE), lambda b, g, i, far: (b * nq + i, zoff["ng"] // LANE + g)),
            pl.BlockSpec((None, 1, seq), lambda b, g, i, far: (b, 0, 0)),
            pl.BlockSpec((seq, 1), lambda b, g, i, far: (b, 0)),
            pl.BlockSpec((LANE, 1), lambda b, g, i, far: (b, 0)),
            pl.BlockSpec((None, J, LANE), lambda b, g, i, far: (g, 0, 0)),
        ],
        out_specs=pl.BlockSpec((tq, J * NSA_DV), lambda b, g, i, far: (b * nq + i, g)),
        scratch_shapes=[pltpu.VMEM((2, 2, tk, R), F32), pltpu.VMEM((3, 1, R), F32),
                        pltpu.VMEM((3, NSA_DV + ONES_ROWS, R), F32)],
    )
    return pl.pallas_call(
        kern,
        grid_spec=grid_spec,
        out_shape=jax.ShapeDtypeStruct((batch * seq, NSA_HEADS * NSA_DV), BF16),
        compiler_params=_params("parallel", "parallel", "parallel"),
        name="nsa_attn",
    )(far, z, kc, vct, z, v_t, z, v_t, z, pos_row, pos_col, pos_cmp, tbl)


def _merge_kernel(a_ref, b_ref, ga_ref, gb_ref, h_ref, wa_ref, wb_ref, wo_ref, post_g_ref,
                  next_g_ref, o_ref, un_ref):
    ya = _dot(a_ref[...], wa_ref[...])
    yb = _dot(b_ref[...], wb_ref[...])
    m = (jax.nn.sigmoid(ga_ref[...].astype(F32)) * ya
         + jax.nn.sigmoid(gb_ref[...].astype(F32)) * yb).astype(BF16)
    y = _dot(m, wo_ref[...])
    out = h_ref[...] + _rms(y, post_g_ref[...])
    o_ref[...] = out
    un_ref[...] = _rms(out, next_g_ref[...]).astype(BF16)


def _merge(a, bb, z, h, wa, wb, wo, post_g, next_g, layer, zoff):
    m, d = h.shape
    tm = MERGE_TM
    da, db = a.shape[1], bb.shape[1]
    gblk = zoff["mg"] // d
    const = lambda i: (layer, 0, 0)
    return pl.pallas_call(
        _merge_kernel,
        grid=(m // tm,),
        in_specs=[pl.BlockSpec((tm, da), lambda i: (i, 0)),
                  pl.BlockSpec((tm, db), lambda i: (i, 0)),
                  pl.BlockSpec((tm, d), lambda i: (i, gblk)),
                  pl.BlockSpec((tm, d), lambda i: (i, gblk + 1)),
                  pl.BlockSpec((tm, d), lambda i: (i, 0)),
                  pl.BlockSpec((None, da, d), const, pipeline_mode=pl.Buffered(1)),
                  pl.BlockSpec((None, db, d), const, pipeline_mode=pl.Buffered(1)),
                  pl.BlockSpec((None, d, d), const, pipeline_mode=pl.Buffered(1)),
                  pl.BlockSpec((None, 1, d), const),
                  pl.BlockSpec((None, 1, d), const)],
        out_specs=[pl.BlockSpec((tm, d), lambda i: (i, 0)),
                   pl.BlockSpec((tm, d), lambda i: (i, 0))],
        out_shape=[jax.ShapeDtypeStruct((m, d), F32), jax.ShapeDtypeStruct((m, d), BF16)],
        compiler_params=_params("parallel"),
        name="merge_out",
    )(a, bb, z, z, h, wa, wb, wo, post_g, next_g)


def _z_layout(d_model):
    G = NSA_KV_HEADS
    widths = [("q", NSA_HEADS * HEAD_PAD), ("mg", 2 * d_model), ("cq", MLA_Q_LORA),
              ("ckv", MLA_KV_LORA), ("kc", G * HEAD_PAD), ("ks", G * HEAD_PAD),
              ("kw", G * HEAD_PAD), ("vc", G * NSA_DV), ("kr", LANE), ("ng", G * LANE)]
    off, pos = {}, 0
    for name, w in widths:
        off[name] = pos
        pos += w
    off["used"] = pos
    off["total"] = -(-pos // Z_PAD) * Z_PAD
    assert off["q"] == 0 and off["mg"] % d_model == 0
    assert off["cq"] % MLA_Q_LORA == 0 and off["ckv"] % MLA_KV_LORA == 0
    assert all(off[k] % HEAD_PAD == 0 for k in ("kc", "ks", "kw"))
    return off


def _swap_halves(w):
    half = w.shape[-1] // 2
    return jnp.concatenate([-w[..., half:], w[..., :half]], axis=-1)


def _pad_last(w, width):
    return jnp.pad(w, [(0, 0)] * (w.ndim - 1) + [(0, width - w.shape[-1])])


def _w_in_pieces(d_model, zoff):
    G, J = NSA_KV_HEADS, NSA_GROUP
    splits = [MLA_Q_LORA, MLA_KV_LORA, MLA_ROPE, NSA_HEADS * NSA_DK,
              G * NSA_DK, G * NSA_DV, G * NSA_DK, G * NSA_DV, G * NSA_DK, G * NSA_DV,
              NSA_HEADS * 3, 2 * d_model]
    names = ["cq", "ckv", "kr", "q", "kc", "vc", "ks", "vs", "kw", "vw", "ng", "mg"]
    src = dict(zip(names, [0] + [int(v) for v in np.cumsum(splits)[:-1]]))
    half = MLA_ROPE // 2
    pieces = []
    for h in range(NSA_HEADS):
        pieces.append((zoff["q"] + h * HEAD_PAD, HEAD_PAD,
                       [(src["q"] + h * NSA_DK, NSA_DK, NSA_DK ** -0.5 * LOG2E)]))
    pieces.append((zoff["mg"], 2 * d_model, [(src["mg"], 2 * d_model, None)]))
    pieces.append((zoff["cq"], MLA_Q_LORA, [(src["cq"], MLA_Q_LORA, None)]))
    pieces.append((zoff["ckv"], MLA_KV_LORA, [(src["ckv"], MLA_KV_LORA, None)]))
    for name in ("kc", "ks", "kw"):
        for g in range(G):
            pieces.append((zoff[name] + g * HEAD_PAD, HEAD_PAD, [(src[name] + g * NSA_DK, NSA_DK, None)]))
    pieces.append((zoff["vc"], G * NSA_DV, [(src["vc"], G * NSA_DV, None)]))
    pieces.append((zoff["kr"], LANE, [(src["kr"], MLA_ROPE, None), (src["kr"] + half, half, -1.0),
                                      (src["kr"], half, None)]))
    for g in range(G):
        pieces.append((zoff["ng"] + g * LANE, LANE, [(src["ng"] + g * J * 3, J * 3, None)]))
    if zoff["total"] > zoff["used"]:
        pieces.append((zoff["used"], zoff["total"] - zoff["used"], []))
    return pieces, (src["vs"], src["vw"])


SUBLANE = 8
COPY_ROWS = 512


def _copy_rows(w_ref, o_ref, src, dst, n, scale):
    for off in range(0, n, COPY_ROWS):
        m = min(COPY_ROWS, n - off)
        x = w_ref[src + off:src + off + m, :]
        o_ref[dst + off:dst + off + m, :] = (x if scale is None else x * scale).astype(BF16)


def _w_in_layout_kernel(w_ref, o_ref, vt_ref, *, pieces, value_rows):
    tc = w_ref.shape[1]
    dst = 0
    for src, n in value_rows:
        _copy_rows(w_ref, vt_ref, src, dst, n, None)
        dst += n
    for dst, width, parts in pieces:
        used = 0
        for src, n, scale in parts:
            if src % SUBLANE == 0 and n % SUBLANE == 0:
                _copy_rows(w_ref, o_ref, src, dst + used, n, scale)
                used += n
            else:
                assert len(parts) == 1 and scale is None
                lo = src - src % SUBLANE
                win = -(-(src % SUBLANE + n) // SUBLANE) * SUBLANE
                x = w_ref[lo:lo + win, :]
                if src % SUBLANE:
                    x = pltpu.roll(x, win - src % SUBLANE, 0)
                x = jnp.where(lax.broadcasted_iota(jnp.int32, (win, tc), 0) < n, x, 0.0)
                x = jnp.concatenate([x, jnp.zeros((width - win, tc), F32)], axis=0)
                o_ref[dst:dst + width, :] = x.astype(BF16)
                used = width
        if used < width:
            o_ref[dst + used:dst + width, :] = jnp.zeros((width - used, tc), BF16)


def _layout_w_in(w_in, zoff):
    L, D, n_in = w_in.shape
    G = NSA_KV_HEADS
    pieces, (vs, vw) = _w_in_pieces(D, zoff)
    tc = 256
    nv = 2 * G * NSA_DV
    value_rows = ((vs, G * NSA_DV), (vw, G * NSA_DV))
    return pl.pallas_call(
        functools.partial(_w_in_layout_kernel, pieces=pieces, value_rows=value_rows),
        grid=(L, D // tc),
        in_specs=[pl.BlockSpec((None, n_in, tc), lambda l, c: (l, 0, c))],
        out_specs=[pl.BlockSpec((None, zoff["total"], tc), lambda l, c: (l, 0, c)),
                   pl.BlockSpec((None, nv, tc), lambda l, c: (l, 0, c))],
        out_shape=[jax.ShapeDtypeStruct((L, zoff["total"], D), BF16),
                   jax.ShapeDtypeStruct((L, nv, D), BF16)],
        compiler_params=_params("parallel", "parallel"),
        name="w_in_layout",
    )(jnp.swapaxes(w_in, 1, 2))


def _layout_w_q_up(w):
    L, r, _ = w.shape
    w = w.reshape(L, r, MLA_HEADS, MLA_NOPE + MLA_ROPE) * ((MLA_NOPE + MLA_ROPE) ** -0.5 * LOG2E)
    rope = w[..., MLA_NOPE:]
    return jnp.concatenate([w[..., :MLA_NOPE], rope, _swap_halves(rope)], axis=-1).reshape(
        L, r, MLA_HEADS * HEAD_PAD).astype(BF16)


def kernel(x, positions, rel_bias, ffn1_pre_g, ffn1_post_g, ffn1_w_gate, ffn1_w_up, ffn1_w_down, mix_pre_g, mix_post_g, w_in, mla_q_norm_g, mla_w_q_up, mla_kv_norm_g, mla_w_uk, mla_w_uv, cmp_pe_k, cmp_w1_k, cmp_w2_k, cmp_pe_v, cmp_w1_v, cmp_w2_v, w_branch_mla, w_branch_nsa, w_out, ffn2_pre_g, ffn2_post_g, ffn2_w_gate, ffn2_w_up, ffn2_w_down):
    B, S, D = x.shape
    L = w_in.shape[0]
    M = B * S
    G, J = NSA_KV_HEADS, NSA_GROUP
    zoff = _z_layout(D)
    n_half = S // CMP_STRIDE
    n_cmp = (S - CMP_LEN) // CMP_STRIDE + 1
    assert n_half == LANE and n_cmp <= LANE

    gain = lambda g: g.reshape(L, 1, -1)
    bf = lambda w: w.astype(BF16)
    w_z, w_vt = _layout_w_in(w_in, zoff)
    w_q = _layout_w_q_up(mla_w_q_up)
    w_uk, w_uv_t = bf(mla_w_uk), bf(mla_w_uv).transpose(0, 2, 1)
    pe_k = _pad_last(cmp_pe_k, HEAD_PAD)
    w1_k = bf(jnp.pad(cmp_w1_k.reshape(L, CMP_LEN, NSA_DK, CMP_HIDDEN),
                      ((0, 0), (0, 0), (0, HEAD_PAD - NSA_DK), (0, 0)))
              ).reshape(L, CMP_LEN * HEAD_PAD, CMP_HIDDEN)
    w2_k = bf(_pad_last(cmp_w2_k, HEAD_PAD))
    f1 = (bf(ffn1_w_gate), bf(ffn1_w_up), bf(ffn1_w_down))
    f2 = (bf(ffn2_w_gate), bf(ffn2_w_up), bf(ffn2_w_down))
    w1_v, w2_v = bf(cmp_w1_v), bf(cmp_w2_v)
    w_a, w_b, w_o = bf(w_branch_mla), bf(w_branch_nsa), bf(w_out)
    g_f1pre, g_f1post, g_mpre, g_mpost = gain(ffn1_pre_g), gain(ffn1_post_g), gain(mix_pre_g), gain(mix_post_g)
    g_f2pre, g_f2post, g_q, g_kv = gain(ffn2_pre_g), gain(ffn2_post_g), gain(mla_q_norm_g), gain(mla_kv_norm_g)

    pos_col = positions.reshape(M, 1)
    pos_row = positions.reshape(B, 1, S)
    pos_cmp = _pad_last(positions[:, CMP_LEN - 1::CMP_STRIDE][:, :n_cmp], LANE).reshape(B * LANE, 1)
    q_min = positions.reshape(B, S // LANE, LANE).min(axis=-1)
    k_max = positions.reshape(B, S // LANE, LANE).max(axis=-1)
    far = (q_min[:, :, None] - k_max[:, None, :] >= FAR_DIST).astype(jnp.int32).reshape(-1)
    tbl = (jnp.take(rel_bias, jnp.asarray(BUCKET_OF_DIST), axis=0).T * LOG2E).reshape(G, J, LANE)
    half = MLA_ROPE // 2
    inv = ROPE_BASE ** (-jnp.arange(half, dtype=F32) * 2.0 / MLA_ROPE)
    cs = _rope_table(pos_col, jnp.concatenate([inv, inv]).reshape(1, MLA_ROPE))

    h = x.reshape(M, D)
    u = _rmsnorm(h, g_f1pre, 0)
    for l in range(L):
        h, u = _ffn(h, u, g_f1post, g_mpre, l, *f1, l)
        z = _in_proj(u, w_z, l)
        v_t = _in_proj_t(u, w_vt, l)
        qp, kp, vt = _mla_prep(z, cs, g_q, g_kv, w_q, w_uk, w_uv_t, l, zoff)
        a = _mla_attn(qp, kp, vt, B, S)

        kc, vct = _nsa_cmp(z, pe_k, cmp_pe_v, w1_k, w2_k, w1_v, w2_v, l, B, S, zoff)
        nsa = _nsa_attn(far, z, kc, vct, v_t, pos_row, pos_col, pos_cmp, tbl, B, S, zoff)
        h, u = _merge(a, nsa, z, h, w_a, w_b, w_o, g_mpost, g_f2pre, l, zoff)
        h, u = _ffn(h, u, g_f2post, g_f1pre, min(l + 1, L - 1), *f2, l)
    return h.reshape(B, S, D)
```

```python
import functools
import math

import numpy as np
import jax
import jax.numpy as jnp
from jax import lax
from jax.experimental import pallas as pl
from jax.experimental.pallas import tpu as pltpu

EPS = 1e-6
MLA_HEADS = 8
MLA_Q_LORA = 512
MLA_KV_LORA = 512
MLA_NOPE = 128
MLA_ROPE = 64
MLA_V = 128
ROPE_BASE = 10000.0
NSA_HEADS = 8
NSA_KV_HEADS = 2
NSA_GROUP = NSA_HEADS // NSA_KV_HEADS
NSA_DK = 192
NSA_DV = 128
CMP_LEN = 32
CMP_STRIDE = 16
CMP_HIDDEN = 256
SLC_LEN = 64
SLC_TOPN = 16
WINDOW = 512
FORCED_SCORE = 1e6
REL_BUCKETS = 32
REL_MAX_DIST = 128
NEG = -1e30
MASKED = 2 * NEG
LOG2E = 1.0 / math.log(2.0)

LANE = 128
ONES_ROWS = 16
HEAD_PAD = 256
VMEM_LIMIT = 56 * 1024 * 1024
BF16 = jnp.bfloat16
F32 = jnp.float32

FFN_TM, FFN_TF = 512, 512
PROJ_TM = 1024
PROJ_TN_CAP = 2560
Z_PAD = 5 * LANE
PREP_TM = 512
MLA_TQ, MLA_TK = 512, 512
MLA_HEADS_PER_STEP = 8
NSA_TQ, NSA_TK = 256, 256
MERGE_TM = 512


def _bucket_of_distance():
    n = np.arange(LANE)
    max_exact = REL_BUCKETS // 2
    large = max_exact + (np.log(np.maximum(n, 1) / max_exact) / math.log(REL_MAX_DIST / max_exact)
                         * (REL_BUCKETS - max_exact)).astype(np.int32)
    bucket = np.where(n < max_exact, n, np.minimum(large, REL_BUCKETS - 1)).astype(np.int32)
    assert bucket[-1] == REL_BUCKETS - 1
    return bucket


BUCKET_OF_DIST = _bucket_of_distance()
FAR_DIST = int(np.max(np.nonzero(BUCKET_OF_DIST != REL_BUCKETS - 1)[0])) + 1


def _params(*sem):
    return pltpu.CompilerParams(dimension_semantics=sem, vmem_limit_bytes=VMEM_LIMIT)


def _rms(x, g):
    return x * lax.rsqrt(jnp.mean(x * x, axis=-1, keepdims=True) + EPS) * g


def _dot(a, b):
    return jnp.dot(a, b, preferred_element_type=F32)


def _dot_nt(a, b):
    return lax.dot_general(a, b, (((1,), (1,)), ((), ())), preferred_element_type=F32)


def _tile_n(n, cap):
    best = LANE
    for t in range(LANE, cap + 1, LANE):
        if n % t == 0:
            best = t
    return best


def _rmsnorm_kernel(x_ref, g_ref, o_ref):
    o_ref[...] = _rms(x_ref[...], g_ref[...]).astype(o_ref.dtype)


def _rmsnorm(x, g, layer):
    m, d = x.shape
    tm = 512
    return pl.pallas_call(
        _rmsnorm_kernel,
        grid=(m // tm,),
        in_specs=[pl.BlockSpec((tm, d), lambda i: (i, 0)),
                  pl.BlockSpec((None, 1, d), lambda i: (layer, 0, 0))],
        out_specs=pl.BlockSpec((tm, d), lambda i: (i, 0)),
        out_shape=jax.ShapeDtypeStruct((m, d), BF16),
        compiler_params=_params("parallel"),
        name="rmsnorm",
    )(x, g)


def _rope_table_kernel(pos_ref, inv_ref, o_ref):
    ang = pos_ref[...].astype(F32) * inv_ref[...]
    o_ref[...] = jnp.concatenate([jnp.cos(ang), jnp.sin(ang)], axis=1)


def _rope_table(pos_col, inv):
    m = pos_col.shape[0]
    tm = 512
    return pl.pallas_call(
        _rope_table_kernel,
        grid=(m // tm,),
        in_specs=[pl.BlockSpec((tm, 1), lambda i: (i, 0)),
                  pl.BlockSpec((1, MLA_ROPE), lambda i: (0, 0))],
        out_specs=pl.BlockSpec((tm, 2 * MLA_ROPE), lambda i: (i, 0)),
        out_shape=jax.ShapeDtypeStruct((m, 2 * MLA_ROPE), F32),
        compiler_params=_params("parallel"),
        name="rope_table",
    )(pos_col, inv)


def _ffn_kernel(x_ref, u_ref, post_g_ref, next_g_ref, wg_ref, wu_ref, wd_ref,
                o_ref, un_ref, acc_ref):
    j = pl.program_id(1)

    @pl.when(j == 0)
    def _():
        acc_ref[...] = jnp.zeros(acc_ref.shape, F32)

    u = u_ref[...]
    gate = _dot(u, wg_ref[...])
    up = _dot(u, wu_ref[...])
    hidden = (gate * jax.nn.sigmoid(gate) * up).astype(BF16)
    acc_ref[...] += _dot(hidden, wd_ref[...])

    @pl.when(j == pl.num_programs(1) - 1)
    def _():
        out = x_ref[...] + _rms(acc_ref[...], post_g_ref[...])
        o_ref[...] = out
        un_ref[...] = _rms(out, next_g_ref[...]).astype(BF16)


def _ffn(x, u, post_g, next_g, next_layer, wg, wu, wd, layer):
    m, d = x.shape
    f = wg.shape[-1]
    tm, tf = FFN_TM, _tile_n(f, FFN_TF)
    return pl.pallas_call(
        _ffn_kernel,
        grid=(m // tm, f // tf),
        in_specs=[pl.BlockSpec((tm, d), lambda i, j: (i, 0)),
                  pl.BlockSpec((tm, d), lambda i, j: (i, 0)),
                  pl.BlockSpec((None, 1, d), lambda i, j: (layer, 0, 0)),
                  pl.BlockSpec((None, 1, d), lambda i, j: (next_layer, 0, 0)),
                  pl.BlockSpec((None, d, tf), lambda i, j: (layer, 0, j)),
                  pl.BlockSpec((None, d, tf), lambda i, j: (layer, 0, j)),
                  pl.BlockSpec((None, tf, d), lambda i, j: (layer, j, 0))],
        out_specs=[pl.BlockSpec((tm, d), lambda i, j: (i, 0)),
                   pl.BlockSpec((tm, d), lambda i, j: (i, 0))],
        out_shape=[jax.ShapeDtypeStruct((m, d), F32), jax.ShapeDtypeStruct((m, d), BF16)],
        scratch_shapes=[pltpu.VMEM((tm, d), F32)],
        compiler_params=_params("parallel", "arbitrary"),
        name="ffn",
    )(x, u, post_g, next_g, wg, wu, wd)


def _matmul_nt_kernel(x_ref, w_ref, o_ref):
    o_ref[...] = _dot_nt(x_ref[...], w_ref[...]).astype(o_ref.dtype)


def _in_proj(u, w_t, layer):
    m, d = u.shape
    n = w_t.shape[1]
    tm, tn = PROJ_TM, _tile_n(n, PROJ_TN_CAP)
    return pl.pallas_call(
        _matmul_nt_kernel,
        grid=(n // tn, m // tm),
        in_specs=[pl.BlockSpec((tm, d), lambda j, i: (i, 0)),
                  pl.BlockSpec((None, tn, d), lambda j, i: (layer, j, 0))],
        out_specs=pl.BlockSpec((tm, tn), lambda j, i: (i, j)),
        out_shape=jax.ShapeDtypeStruct((m, n), BF16),
        compiler_params=_params("parallel", "parallel"),
        name="in_proj",
    )(u, w_t)


def _matmul_t_kernel(w_ref, x_ref, o_ref):
    o_ref[...] = _dot_nt(w_ref[...], x_ref[...]).astype(o_ref.dtype)


def _in_proj_t(u, w_t, layer):
    m, d = u.shape
    n = w_t.shape[1]
    tm = PROJ_TM
    return pl.pallas_call(
        _matmul_t_kernel,
        grid=(m // tm,),
        in_specs=[pl.BlockSpec((None, n, d), lambda i: (layer, 0, 0)),
                  pl.BlockSpec((tm, d), lambda i: (i, 0))],
        out_specs=pl.BlockSpec((n, tm), lambda i: (0, i)),
        out_shape=jax.ShapeDtypeStruct((n, m), BF16),
        compiler_params=_params("parallel"),
        name="in_proj_t",
    )(w_t, u)


def _softmax_tile_t(s_ref, adds, masks, v_t, m_ref, acc_ref):
    probs, alphas = [], []
    for c in range(s_ref.shape[1] // LANE):
        cols = slice(c * LANE, (c + 1) * LANE)
        s = s_ref[:, cols]
        for mask in masks[c]:
            s = jnp.where(mask, s, MASKED) if mask.dtype == jnp.bool_ else s + mask
        m_old = m_ref[:, cols]
        m_tile = jnp.max(s, axis=0, keepdims=True)
        if adds[c] is not None:
            m_tile = m_tile + adds[c]
        m_new = jnp.maximum(m_old, m_tile)
        alpha = jnp.exp2(m_old - m_new)
        p = jnp.exp2(s - (m_new if adds[c] is None else m_new - adds[c]))
        m_ref[:, cols] = m_new
        probs.append(p.astype(BF16))
        alphas.append(alpha)
    p_t = probs[0] if len(probs) == 1 else jnp.concatenate(probs, axis=1)
    alpha = alphas[0] if len(alphas) == 1 else jnp.concatenate(alphas, axis=1)
    v_ones = jnp.concatenate([v_t, jnp.ones((ONES_ROWS, v_t.shape[1]), BF16)], axis=0)
    acc_ref[...] = alpha * acc_ref[...] + _dot(v_ones, p_t)


def _softmax_init(m_ref, acc_ref):
    m_ref[...] = jnp.full(m_ref.shape, NEG, F32)
    acc_ref[...] = jnp.zeros(acc_ref.shape, F32)


def _softmax_out(acc):
    dv = acc.shape[0] - ONES_ROWS
    return acc[:dv] / acc[dv:dv + 1]


def _mla_prep_kernel(cq_ref, ckv_ref, kr_ref, cs_ref, gq_ref, gkv_ref, wq_ref, wuk_ref, wuvt_ref,
                     q_ref, k_ref, vt_ref):
    cs = cs_ref[...]
    qn = _rms(cq_ref[...].astype(F32), gq_ref[...]).astype(BF16)
    q = _dot(qn, wq_ref[...])
    kvn = _rms(ckv_ref[...].astype(F32), gkv_ref[...]).astype(BF16)
    k_nope = _dot(kvn, wuk_ref[...])
    t = kr_ref[...].astype(F32) * cs
    k_pe = (t + pltpu.roll(t, MLA_ROPE, 1)).astype(BF16)
    for h in range(MLA_HEADS):
        lo = h * HEAD_PAD
        q_ref[:, lo:lo + MLA_NOPE] = q[:, lo:lo + MLA_NOPE].astype(BF16)
        q_ref[:, lo + MLA_NOPE:lo + HEAD_PAD] = (q[:, lo + MLA_NOPE:lo + HEAD_PAD] * cs).astype(BF16)
        k_ref[:, lo:lo + MLA_NOPE] = k_nope[:, h * MLA_NOPE:(h + 1) * MLA_NOPE].astype(BF16)
        k_ref[:, lo + MLA_NOPE:lo + HEAD_PAD] = k_pe
    vt_ref[...] = _dot_nt(wuvt_ref[...], kvn).astype(BF16)


def _mla_prep(z, cs, gq, gkv, wq, wuk, wuv_t, layer, zoff):
    m = z.shape[0]
    tm = PREP_TM
    hq = MLA_HEADS * HEAD_PAD
    hv = MLA_HEADS * MLA_V
    const = lambda i: (layer, 0, 0)
    return pl.pallas_call(
        _mla_prep_kernel,
        grid=(m // tm,),
        in_specs=[pl.BlockSpec((tm, MLA_Q_LORA), lambda i: (i, zoff["cq"] // MLA_Q_LORA)),
                  pl.BlockSpec((tm, MLA_KV_LORA), lambda i: (i, zoff["ckv"] // MLA_KV_LORA)),
                  pl.BlockSpec((tm, LANE), lambda i: (i, zoff["kr"] // LANE)),
                  pl.BlockSpec((tm, LANE), lambda i: (i, 0)),
                  pl.BlockSpec((None, 1, MLA_Q_LORA), const),
                  pl.BlockSpec((None, 1, MLA_KV_LORA), const),
                  pl.BlockSpec((None, MLA_Q_LORA, hq), const),
                  pl.BlockSpec((None, MLA_KV_LORA, MLA_HEADS * MLA_NOPE), const),
                  pl.BlockSpec((None, hv, MLA_KV_LORA), const)],
        out_specs=[pl.BlockSpec((tm, hq), lambda i: (i, 0)),
                   pl.BlockSpec((tm, hq), lambda i: (i, 0)),
                   pl.BlockSpec((hv, tm), lambda i: (0, i))],
        out_shape=[jax.ShapeDtypeStruct((m, hq), BF16),
                   jax.ShapeDtypeStruct((m, hq), BF16),
                   jax.ShapeDtypeStruct((hv, m), BF16)],
        compiler_params=_params("parallel"),
        name="mla_prep",
    )(z, z, z, cs, gq, gkv, wq, wuk, wuv_t)


def _by_parity(kt, fn):
    @pl.when(kt % 2 == 0)
    def _():
        fn(0)

    @pl.when(kt % 2 == 1)
    def _():
        fn(1)


def _mla_attn_kernel(q_ref, k_ref, vt_ref, o_ref, s_ref, m_ref, acc_ref, *, tq, tk, heads):
    i = pl.program_id(2)
    nc = tq // LANE
    _softmax_init(m_ref, acc_ref)
    qt = i * tq + lax.broadcasted_iota(jnp.int32, (1, tq), 1)
    kcol = lax.broadcasted_iota(jnp.int32, (tk, 1), 0)
    qk = [slice(h * HEAD_PAD, (h + 1) * HEAD_PAD) for h in range(heads)]
    vd = [slice(h * MLA_V, (h + 1) * MLA_V) for h in range(heads)]
    n_full = (i * tq) // tk
    n_all = (i * tq + tq + tk - 1) // tk

    def logits(kt, buf):
        ks = pl.multiple_of(kt * tk, tk)
        for h in range(heads):
            s_ref[buf, h] = _dot_nt(k_ref[pl.ds(ks, tk), qk[h]], q_ref[:, qk[h]])

    def step(kt, buf, masked, prefetch=True):
        ks = pl.multiple_of(kt * tk, tk)
        masks = [()] * nc
        if masked:
            masks = [(ks + kcol <= qt[:, c * LANE:(c + 1) * LANE],) for c in range(nc)]
        for h in range(heads):
            _softmax_tile_t(s_ref.at[buf, h], [None] * nc, masks, vt_ref[vd[h], pl.ds(ks, tk)],
                            m_ref.at[h], acc_ref.at[h])
        if prefetch:
            logits(kt + 1, 1 - buf)

    def full_body(kt, c):
        _by_parity(kt, lambda buf: step(kt, buf, False))
        return c

    def diag_body(kt, c):
        _by_parity(kt, lambda buf: step(kt, buf, True))
        return c

    logits(0, 0)
    lax.fori_loop(0, n_full, full_body, 0)
    lax.fori_loop(n_full, n_all - 1, diag_body, 0)
    _by_parity(n_all - 1, lambda buf: step(n_all - 1, buf, True, prefetch=False))
    for h in range(heads):
        o_ref[:, vd[h]] = _softmax_out(acc_ref[h]).T.astype(o_ref.dtype)


def _mla_attn(qp, kp, vt, batch, seq):
    tq, tk, heads = MLA_TQ, MLA_TK, MLA_HEADS_PER_STEP
    nq = seq // tq
    kern = functools.partial(_mla_attn_kernel, tq=tq, tk=tk, heads=heads)
    return pl.pallas_call(
        kern,
        grid=(batch, MLA_HEADS // heads, nq),
        in_specs=[pl.BlockSpec((tq, heads * HEAD_PAD), lambda b, h, i: (b * nq + i, h)),
                  pl.BlockSpec((seq, heads * HEAD_PAD), lambda b, h, i: (b, h)),
                  pl.BlockSpec((heads * MLA_V, seq), lambda b, h, i: (h, b))],
        out_specs=pl.BlockSpec((tq, heads * MLA_V), lambda b, h, i: (b * nq + i, h)),
        out_shape=jax.ShapeDtypeStruct((batch * seq, MLA_HEADS * MLA_V), BF16),
        scratch_shapes=[pltpu.VMEM((2, heads, tk, tq), F32), pltpu.VMEM((heads, 1, tq), F32),
                        pltpu.VMEM((heads, MLA_V + ONES_ROWS, tq), F32)],
        compiler_params=_params("parallel", "parallel", "parallel"),
        name="mla_attn",
    )(qp, kp, vt)


def _compress(z_ref, zf_ref, pe_ref, w1_ref, w2_ref):
    chunks, s, _ = zf_ref.shape
    for c in range(chunks):
        zf_ref[c] = z_ref[:, c * LANE:(c + 1) * LANE].astype(F32)
    d = chunks * LANE
    n = s // CMP_STRIDE
    first = jnp.zeros((n, CMP_HIDDEN), F32)
    second = jnp.zeros((n, CMP_HIDDEN), F32)
    for l in range(CMP_STRIDE):
        toks = [zf_ref[c, pl.ds(l, n, stride=CMP_STRIDE), :] for c in range(chunks)]
        tok = toks[0] if chunks == 1 else jnp.concatenate(toks, axis=1)
        lo, hi = l, CMP_STRIDE + l
        first += _dot((tok + pe_ref[lo:lo + 1, :]).astype(BF16), w1_ref[lo * d:(lo + 1) * d, :])
        second += _dot((tok + pe_ref[hi:hi + 1, :]).astype(BF16), w1_ref[hi * d:(hi + 1) * d, :])
    hidden = first + pltpu.roll(second, n - 1, 0)
    return _dot((hidden * jax.nn.sigmoid(hidden)).astype(BF16), w2_ref[...])


def _nsa_cmp_kernel(zk_ref, zv_ref, pek_ref, pev_ref, w1k_ref, w2k_ref, w1v_ref, w2v_ref,
                    kc_ref, vct_ref, kf_ref, vf_ref):
    kc_ref[...] = _compress(zk_ref, kf_ref, pek_ref, w1k_ref, w2k_ref).astype(BF16)
    vct_ref[...] = _compress(zv_ref, vf_ref, pev_ref, w1v_ref, w2v_ref).T.astype(BF16)


def _nsa_cmp(z, pek, pev, w1k, w2k, w1v, w2v, layer, batch, seq, zoff):
    G = NSA_KV_HEADS
    nh = seq // CMP_STRIDE
    const = lambda i: (layer, 0, 0)
    return pl.pallas_call(
        _nsa_cmp_kernel,
        grid=(batch * G,),
        in_specs=[pl.BlockSpec((seq, HEAD_PAD), lambda i: (i // G, zoff["kc"] // HEAD_PAD + i % G)),
                  pl.BlockSpec((seq, NSA_DV), lambda i: (i // G, zoff["vc"] // NSA_DV + i % G)),
                  pl.BlockSpec((None, CMP_LEN, HEAD_PAD), const),
                  pl.BlockSpec((None, CMP_LEN, NSA_DV), const),
                  pl.BlockSpec((None, CMP_LEN * HEAD_PAD, CMP_HIDDEN), const),
                  pl.BlockSpec((None, CMP_HIDDEN, HEAD_PAD), const),
                  pl.BlockSpec((None, CMP_LEN * NSA_DV, CMP_HIDDEN), const),
                  pl.BlockSpec((None, CMP_HIDDEN, NSA_DV), const)],
        out_specs=[pl.BlockSpec((None, nh, HEAD_PAD), lambda i: (i, 0, 0)),
                   pl.BlockSpec((None, NSA_DV, nh), lambda i: (i, 0, 0))],
        out_shape=[jax.ShapeDtypeStruct((batch * G, nh, HEAD_PAD), BF16),
                   jax.ShapeDtypeStruct((batch * G, NSA_DV, nh), BF16)],
        scratch_shapes=[pltpu.VMEM((HEAD_PAD // LANE, seq, LANE), F32),
                        pltpu.VMEM((NSA_DV // LANE, seq, LANE), F32)],
        compiler_params=_params("parallel"),
        name="nsa_cmp",
    )(z, z, pek, pev, w1k, w2k, w1v, w2v)


def _split3(x):
    a = x.astype(BF16)
    r = x - a.astype(F32)
    b = r.astype(BF16)
    c = (r - b.astype(F32)).astype(BF16)
    return a, b, c


def _lookup(table_row, idx):
    rows, width = idx.shape
    table = jnp.broadcast_to(table_row, (rows, LANE))
    chunks = [jnp.take_along_axis(table, idx[:, c:c + LANE], axis=1, mode="promise_in_bounds")
              for c in range(0, width, LANE)]
    return chunks[0] if len(chunks) == 1 else jnp.concatenate(chunks, axis=1)


def _nsa_attn_kernel(far_ref, q_ref, kc_ref, vct_ref, ks_ref, vst_ref, kw_ref, vwt_ref, ng_ref,
                     prow_ref, pcol_ref, pcmp_ref, tbl_ref, o_ref,
                     s_ref, m_ref, acc_ref, *, tq, tk, seq):
    b, i = pl.program_id(0), pl.program_id(2)
    nq, nk = seq // tq, seq // tk
    J = NSA_GROUP
    n_slc = seq // SLC_LEN
    n_cmp = (seq - CMP_LEN) // CMP_STRIDE + 1
    t0 = pl.multiple_of(i * tq, tq)

    qs = jnp.concatenate([q_ref[:, j * HEAD_PAD:(j + 1) * HEAD_PAD] for j in range(J)], axis=0)
    qpos = prow_ref[:, pl.ds(t0, tq)]
    qt = t0 + lax.broadcasted_iota(jnp.int32, (1, tq), 1)
    tbl = tbl_ref[...]
    far_bias = [tbl[j:j + 1, LANE - 1:LANE] for j in range(J)]
    head = [slice(j * tq, (j + 1) * tq) for j in range(J)]

    blk = lax.broadcasted_iota(jnp.int32, (LANE, 1), 0)
    keep_c = (blk * CMP_STRIDE + (CMP_LEN - 1) <= qt) & (blk < n_cmp)
    idx_c = jnp.clip(qpos - pcmp_ref[...], 0, LANE - 1)
    s_c = _dot_nt(kc_ref[...], qs)
    p_heads = []
    for j in range(J):
        s = jnp.where(keep_c, s_c[:, head[j]] + _lookup(tbl[j:j + 1, :], idx_c), NEG)
        e = jnp.where(keep_c, jnp.exp2(s - jnp.max(s, axis=0, keepdims=True)), 0.0)
        den = jnp.sum(e, axis=0, keepdims=True)
        p_heads.append(e / jnp.where(den > 0.0, den, 1.0))
    o_cmp = _dot(vct_ref[...], jnp.concatenate([p.astype(BF16) for p in p_heads], axis=1))

    rows = 32
    assert n_slc <= rows
    m_row = lax.broadcasted_iota(jnp.int32, (rows, LANE), 0)
    n_col = lax.broadcasted_iota(jnp.int32, (rows, LANE), 1)
    per = SLC_LEN // CMP_STRIDE
    back = (CMP_LEN - 1) // CMP_STRIDE
    overlap = ((n_col >= per * m_row - back) & (n_col <= per * m_row + per - 1)
               & (n_col < n_cmp) & (m_row < n_slc)).astype(BF16)
    imp = sum(_dot(overlap, part) for part in _split3(sum(p_heads)))
    m_blk = lax.broadcasted_iota(jnp.int32, (rows, 1), 0)
    cur = qt // SLC_LEN
    valid = m_blk <= cur
    forced = valid & ((m_blk == 0) | (m_blk >= cur - 1))
    score = jnp.where(forced, FORCED_SCORE, jnp.where(valid, imp, -1.0))
    rank = jnp.zeros((rows, tq), jnp.int32)
    for mp in range(n_slc):
        other = score[mp:mp + 1, :]
        ahead = (other > score) | ((other == score) & (m_blk > mp))
        rank = rank + ahead.astype(jnp.int32)
    selected = valid & (rank < min(SLC_TOPN, n_slc))
    sel_penalty = jnp.where(selected, 0.0, MASKED).astype(BF16)

    SLC, WIN = 0, 1
    _softmax_init(m_ref, acc_ref)
    sub = tk // LANE

    qchunks = [slice(c, c + LANE) for c in range(0, tq, LANE)]

    assert tq == tk

    def slc_masks(kidx, diagonal):
        expand = (kidx // SLC_LEN == lax.broadcasted_iota(jnp.int32, (1, rows), 1)).astype(BF16)
        penalty = _dot(expand, sel_penalty)
        if diagonal:
            return [(penalty[:, c], kidx <= qt[:, c]) for c in qchunks]
        return [(penalty[:, c],) for c in qchunks]

    def win_masks(kidx, diagonal):
        return [(jnp.where(lax.bitcast_convert_type(qt[:, c] - kidx, jnp.uint32) < WINDOW,
                           0.0, MASKED),) for c in qchunks]

    k_refs, vt_refs, masks_of = (ks_ref, kw_ref), (vst_ref, vwt_ref), (slc_masks, win_masks)
    adds = [far_bias[j] for j in range(J) for _ in qchunks]
    last = (t0 + tq + tk - 1) // tk
    first_win = jnp.maximum(t0 - (WINDOW - 1), 0) // tk

    def logits(kt, buf, slots):
        ks = pl.multiple_of(kt * tk, tk)
        for slot in slots:
            s_ref[buf, slot] = _dot_nt(k_refs[slot][pl.ds(ks, tk), :], qs)

    def near_bias(kt, buf, slots):
        ks = pl.multiple_of(kt * tk, tk)
        for hb in range(sub):
            kb = kt * sub + hb
            for qc, qcols in enumerate(qchunks):
                near = ((far_ref[((b * nq + i) * len(qchunks) + qc) * (nk * sub) + kb] == 0)
                        & (kb * LANE < t0 + (qc + 1) * LANE))

                @pl.when(near)
                def _():
                    kpos = pcol_ref[pl.ds(pl.multiple_of(ks + hb * LANE, LANE), LANE), :]
                    idx = jnp.clip(qpos[:, qcols] - kpos, 0, LANE - 1)
                    for j in range(J):
                        delta = _lookup(tbl[j:j + 1, :] - far_bias[j], idx)
                        cols = slice(j * tq + qc * LANE, j * tq + (qc + 1) * LANE)
                        for slot in slots:
                            s_ref[buf, slot, hb * LANE:(hb + 1) * LANE, cols] += delta

    def step(kt, buf, slots, final=False):
        ks = pl.multiple_of(kt * tk, tk)
        kidx = ks + lax.broadcasted_iota(jnp.int32, (tk, 1), 0)
        for slot in slots:
            _softmax_tile_t(s_ref.at[buf, slot], adds, masks_of[slot](kidx, final) * J,
                            vt_refs[slot][:, pl.ds(ks, tk)], m_ref.at[slot], acc_ref.at[slot])
        if not final:
            logits(kt + 1, 1 - buf, slots)
            near_bias(kt + 1, 1 - buf, slots)

    def slc_body(kt, c):
        _by_parity(kt, lambda buf: step(kt, buf, (SLC,)))
        return c

    def both_body(kt, c):
        _by_parity(kt, lambda buf: step(kt, buf, (SLC, WIN)))
        return c

    def enter_window(buf):
        logits(first_win, buf, (WIN,))
        near_bias(first_win, buf, (WIN,))

    logits(0, 0, (SLC,))
    near_bias(0, 0, (SLC,))
    lax.fori_loop(0, first_win, slc_body, 0)
    _by_parity(first_win, enter_window)
    lax.fori_loop(first_win, last - 1, both_body, 0)
    _by_parity(last - 1, lambda buf: step(last - 1, buf, (SLC, WIN), final=True))
    o_slc = _softmax_out(acc_ref[SLC])
    o_win = _softmax_out(acc_ref[WIN])

    gates = jax.nn.sigmoid(ng_ref[...].astype(F32)).T
    for j in range(J):
        o = (gates[3 * j:3 * j + 1, :] * o_cmp[:, head[j]]
             + gates[3 * j + 1:3 * j + 2, :] * o_slc[:, head[j]]
             + gates[3 * j + 2:3 * j + 3, :] * o_win[:, head[j]])
        o_ref[:, j * NSA_DV:(j + 1) * NSA_DV] = o.T.astype(o_ref.dtype)


def _nsa_attn(far, z, kc, vct, v_t, pos_row, pos_col, pos_cmp, tbl, batch, seq, zoff):
    tq, tk = NSA_TQ, NSA_TK
    nq = seq // tq
    G, J = NSA_KV_HEADS, NSA_GROUP
    R = J * tq
    kern = functools.partial(_nsa_attn_kernel, tq=tq, tk=tk, seq=seq)
    grid_spec = pltpu.PrefetchScalarGridSpec(
        num_scalar_prefetch=1,
        grid=(batch, G, nq),
        in_specs=[
            pl.BlockSpec((tq, J * HEAD_PAD), lambda b, g, i, far: (b * nq + i, g)),
            pl.BlockSpec((None, LANE, HEAD_PAD), lambda b, g, i, far: (b * G + g, 0, 0)),
            pl.BlockSpec((None, NSA_DV, LANE), lambda b, g, i, far: (b * G + g, 0, 0)),
            pl.BlockSpec((seq, HEAD_PAD), lambda b, g, i, far: (b, zoff["ks"] // HEAD_PAD + g)),
            pl.BlockSpec((NSA_DV, seq), lambda b, g, i, far: (g, b)),
            pl.BlockSpec((seq, HEAD_PAD), lambda b, g, i, far: (b, zoff["kw"] // HEAD_PAD + g)),
            pl.BlockSpec((NSA_DV, seq), lambda b, g, i, far: (G + g, b)),
            pl.BlockSpec((tq, LANE), lambda b, g, i, far: (b * nq + i, zoff["ng"] // LANE + g)),
            pl.BlockSpec((None, 1, seq), lambda b, g, i, far: (b, 0, 0)),
            pl.BlockSpec((seq, 1), lambda b, g, i, far: (b, 0)),
            pl.BlockSpec((LANE, 1), lambda b, g, i, far: (b, 0)),
            pl.BlockSpec((None, J, LANE), lambda b, g, i, far: (g, 0, 0)),
        ],
        out_specs=pl.BlockSpec((tq, J * NSA_DV), lambda b, g, i, far: (b * nq + i, g)),
        scratch_shapes=[pltpu.VMEM((2, 2, tk, R), F32), pltpu.VMEM((2, 1, R), F32),
                        pltpu.VMEM((2, NSA_DV + ONES_ROWS, R), F32)],
    )
    return pl.pallas_call(
        kern,
        grid_spec=grid_spec,
        out_shape=jax.ShapeDtypeStruct((batch * seq, NSA_HEADS * NSA_DV), BF16),
        compiler_params=_params("parallel", "parallel", "parallel"),
        name="nsa_attn",
    )(far, z, kc, vct, z, v_t, z, v_t, z, pos_row, pos_col, pos_cmp, tbl)


def _merge_kernel(a_ref, b_ref, ga_ref, gb_ref, h_ref, wa_ref, wb_ref, wo_ref, post_g_ref,
                  next_g_ref, o_ref, un_ref):
    ya = _dot(a_ref[...], wa_ref[...])
    yb = _dot(b_ref[...], wb_ref[...])
    m = (jax.nn.sigmoid(ga_ref[...].astype(F32)) * ya
         + jax.nn.sigmoid(gb_ref[...].astype(F32)) * yb).astype(BF16)
    y = _dot(m, wo_ref[...])
    out = h_ref[...] + _rms(y, post_g_ref[...])
    o_ref[...] = out
    un_ref[...] = _rms(out, next_g_ref[...]).astype(BF16)


def _merge(a, bb, z, h, wa, wb, wo, post_g, next_g, layer, zoff):
    m, d = h.shape
    tm = MERGE_TM
    da, db = a.shape[1], bb.shape[1]
    gblk = zoff["mg"] // d
    const = lambda i: (layer, 0, 0)
    return pl.pallas_call(
        _merge_kernel,
        grid=(m // tm,),
        in_specs=[pl.BlockSpec((tm, da), lambda i: (i, 0)),
                  pl.BlockSpec((tm, db), lambda i: (i, 0)),
                  pl.BlockSpec((tm, d), lambda i: (i, gblk)),
                  pl.BlockSpec((tm, d), lambda i: (i, gblk + 1)),
                  pl.BlockSpec((tm, d), lambda i: (i, 0)),
                  pl.BlockSpec((None, da, d), const, pipeline_mode=pl.Buffered(1)),
                  pl.BlockSpec((None, db, d), const, pipeline_mode=pl.Buffered(1)),
                  pl.BlockSpec((None, d, d), const, pipeline_mode=pl.Buffered(1)),
                  pl.BlockSpec((None, 1, d), const),
                  pl.BlockSpec((None, 1, d), const)],
        out_specs=[pl.BlockSpec((tm, d), lambda i: (i, 0)),
                   pl.BlockSpec((tm, d), lambda i: (i, 0))],
        out_shape=[jax.ShapeDtypeStruct((m, d), F32), jax.ShapeDtypeStruct((m, d), BF16)],
        compiler_params=_params("parallel"),
        name="merge_out",
    )(a, bb, z, z, h, wa, wb, wo, post_g, next_g)


def _z_layout(d_model):
    G = NSA_KV_HEADS
    widths = [("q", NSA_HEADS * HEAD_PAD), ("mg", 2 * d_model), ("cq", MLA_Q_LORA),
              ("ckv", MLA_KV_LORA), ("kc", G * HEAD_PAD), ("ks", G * HEAD_PAD),
              ("kw", G * HEAD_PAD), ("vc", G * NSA_DV), ("kr", LANE), ("ng", G * LANE)]
    off, pos = {}, 0
    for name, w in widths:
        off[name] = pos
        pos += w
    off["used"] = pos
    off["total"] = -(-pos // Z_PAD) * Z_PAD
    assert off["q"] == 0 and off["mg"] % d_model == 0
    assert off["cq"] % MLA_Q_LORA == 0 and off["ckv"] % MLA_KV_LORA == 0
    assert all(off[k] % HEAD_PAD == 0 for k in ("kc", "ks", "kw"))
    return off


def _swap_halves(w):
    half = w.shape[-1] // 2
    return jnp.concatenate([-w[..., half:], w[..., :half]], axis=-1)


def _pad_last(w, width):
    return jnp.pad(w, [(0, 0)] * (w.ndim - 1) + [(0, width - w.shape[-1])])


def _w_in_pieces(d_model, zoff):
    G, J = NSA_KV_HEADS, NSA_GROUP
    splits = [MLA_Q_LORA, MLA_KV_LORA, MLA_ROPE, NSA_HEADS * NSA_DK,
              G * NSA_DK, G * NSA_DV, G * NSA_DK, G * NSA_DV, G * NSA_DK, G * NSA_DV,
              NSA_HEADS * 3, 2 * d_model]
    names = ["cq", "ckv", "kr", "q", "kc", "vc", "ks", "vs", "kw", "vw", "ng", "mg"]
    src = dict(zip(names, [0] + [int(v) for v in np.cumsum(splits)[:-1]]))
    half = MLA_ROPE // 2
    pieces = []
    for h in range(NSA_HEADS):
        pieces.append((zoff["q"] + h * HEAD_PAD, HEAD_PAD,
                       [(src["q"] + h * NSA_DK, NSA_DK, NSA_DK ** -0.5 * LOG2E)]))
    pieces.append((zoff["mg"], 2 * d_model, [(src["mg"], 2 * d_model, None)]))
    pieces.append((zoff["cq"], MLA_Q_LORA, [(src["cq"], MLA_Q_LORA, None)]))
    pieces.append((zoff["ckv"], MLA_KV_LORA, [(src["ckv"], MLA_KV_LORA, None)]))
    for name in ("kc", "ks", "kw"):
        for g in range(G):
            pieces.append((zoff[name] + g * HEAD_PAD, HEAD_PAD, [(src[name] + g * NSA_DK, NSA_DK, None)]))
    pieces.append((zoff["vc"], G * NSA_DV, [(src["vc"], G * NSA_DV, None)]))
    pieces.append((zoff["kr"], LANE, [(src["kr"], MLA_ROPE, None), (src["kr"] + half, half, -1.0),
                                      (src["kr"], half, None)]))
    for g in range(G):
        pieces.append((zoff["ng"] + g * LANE, LANE, [(src["ng"] + g * J * 3, J * 3, None)]))
    if zoff["total"] > zoff["used"]:
        pieces.append((zoff["used"], zoff["total"] - zoff["used"], []))
    return pieces, (src["vs"], src["vw"])


SUBLANE = 8
COPY_ROWS = 512


def _copy_rows(w_ref, o_ref, src, dst, n, scale):
    for off in range(0, n, COPY_ROWS):
        m = min(COPY_ROWS, n - off)
        x = w_ref[src + off:src + off + m, :]
        o_ref[dst + off:dst + off + m, :] = (x if scale is None else x * scale).astype(BF16)


def _w_in_layout_kernel(w_ref, o_ref, vt_ref, *, pieces, value_rows):
    tc = w_ref.shape[1]
    dst = 0
    for src, n in value_rows:
        _copy_rows(w_ref, vt_ref, src, dst, n, None)
        dst += n
    for dst, width, parts in pieces:
        used = 0
        for src, n, scale in parts:
            if src % SUBLANE == 0 and n % SUBLANE == 0:
                _copy_rows(w_ref, o_ref, src, dst + used, n, scale)
                used += n
            else:
                assert len(parts) == 1 and scale is None
                lo = src - src % SUBLANE
                win = -(-(src % SUBLANE + n) // SUBLANE) * SUBLANE
                x = w_ref[lo:lo + win, :]
                if src % SUBLANE:
                    x = pltpu.roll(x, win - src % SUBLANE, 0)
                x = jnp.where(lax.broadcasted_iota(jnp.int32, (win, tc), 0) < n, x, 0.0)
                x = jnp.concatenate([x, jnp.zeros((width - win, tc), F32)], axis=0)
                o_ref[dst:dst + width, :] = x.astype(BF16)
                used = width
        if used < width:
            o_ref[dst + used:dst + width, :] = jnp.zeros((width - used, tc), BF16)


def _layout_w_in(w_in, zoff):
    L, D, n_in = w_in.shape
    G = NSA_KV_HEADS
    pieces, (vs, vw) = _w_in_pieces(D, zoff)
    tc = 256
    nv = 2 * G * NSA_DV
    value_rows = ((vs, G * NSA_DV), (vw, G * NSA_DV))
    return pl.pallas_call(
        functools.partial(_w_in_layout_kernel, pieces=pieces, value_rows=value_rows),
        grid=(L, D // tc),
        in_specs=[pl.BlockSpec((None, n_in, tc), lambda l, c: (l, 0, c))],
        out_specs=[pl.BlockSpec((None, zoff["total"], tc), lambda l, c: (l, 0, c)),
                   pl.BlockSpec((None, nv, tc), lambda l, c: (l, 0, c))],
        out_shape=[jax.ShapeDtypeStruct((L, zoff["total"], D), BF16),
                   jax.ShapeDtypeStruct((L, nv, D), BF16)],
        compiler_params=_params("parallel", "parallel"),
        name="w_in_layout",
    )(jnp.swapaxes(w_in, 1, 2))


def _layout_w_q_up(w):
    L, r, _ = w.shape
    w = w.reshape(L, r, MLA_HEADS, MLA_NOPE + MLA_ROPE) * ((MLA_NOPE + MLA_ROPE) ** -0.5 * LOG2E)
    rope = w[..., MLA_NOPE:]
    return jnp.concatenate([w[..., :MLA_NOPE], rope, _swap_halves(rope)], axis=-1).reshape(
        L, r, MLA_HEADS * HEAD_PAD).astype(BF16)


def kernel(x, positions, rel_bias, ffn1_pre_g, ffn1_post_g, ffn1_w_gate, ffn1_w_up, ffn1_w_down, mix_pre_g, mix_post_g, w_in, mla_q_norm_g, mla_w_q_up, mla_kv_norm_g, mla_w_uk, mla_w_uv, cmp_pe_k, cmp_w1_k, cmp_w2_k, cmp_pe_v, cmp_w1_v, cmp_w2_v, w_branch_mla, w_branch_nsa, w_out, ffn2_pre_g, ffn2_post_g, ffn2_w_gate, ffn2_w_up, ffn2_w_down):
    B, S, D = x.shape
    L = w_in.shape[0]
    M = B * S
    G, J = NSA_KV_HEADS, NSA_GROUP
    zoff = _z_layout(D)
    n_half = S // CMP_STRIDE
    n_cmp = (S - CMP_LEN) // CMP_STRIDE + 1
    assert n_half == LANE and n_cmp <= LANE

    gain = lambda g: g.reshape(L, 1, -1)
    bf = lambda w: w.astype(BF16)
    w_z, w_vt = _layout_w_in(w_in, zoff)
    w_q = _layout_w_q_up(mla_w_q_up)
    w_uk, w_uv_t = bf(mla_w_uk), bf(mla_w_uv).transpose(0, 2, 1)
    pe_k = _pad_last(cmp_pe_k, HEAD_PAD)
    w1_k = bf(jnp.pad(cmp_w1_k.reshape(L, CMP_LEN, NSA_DK, CMP_HIDDEN),
                      ((0, 0), (0, 0), (0, HEAD_PAD - NSA_DK), (0, 0)))
              ).reshape(L, CMP_LEN * HEAD_PAD, CMP_HIDDEN)
    w2_k = bf(_pad_last(cmp_w2_k, HEAD_PAD))
    f1 = (bf(ffn1_w_gate), bf(ffn1_w_up), bf(ffn1_w_down))
    f2 = (bf(ffn2_w_gate), bf(ffn2_w_up), bf(ffn2_w_down))
    w1_v, w2_v = bf(cmp_w1_v), bf(cmp_w2_v)
    w_a, w_b, w_o = bf(w_branch_mla), bf(w_branch_nsa), bf(w_out)
    g_f1pre, g_f1post, g_mpre, g_mpost = gain(ffn1_pre_g), gain(ffn1_post_g), gain(mix_pre_g), gain(mix_post_g)
    g_f2pre, g_f2post, g_q, g_kv = gain(ffn2_pre_g), gain(ffn2_post_g), gain(mla_q_norm_g), gain(mla_kv_norm_g)

    pos_col = positions.reshape(M, 1)
    pos_row = positions.reshape(B, 1, S)
    pos_cmp = _pad_last(positions[:, CMP_LEN - 1::CMP_STRIDE][:, :n_cmp], LANE).reshape(B * LANE, 1)
    q_min = positions.reshape(B, S // LANE, LANE).min(axis=-1)
    k_max = positions.reshape(B, S // LANE, LANE).max(axis=-1)
    far = (q_min[:, :, None] - k_max[:, None, :] >= FAR_DIST).astype(jnp.int32).reshape(-1)
    tbl = (jnp.take(rel_bias, jnp.asarray(BUCKET_OF_DIST), axis=0).T * LOG2E).reshape(G, J, LANE)
    half = MLA_ROPE // 2
    inv = ROPE_BASE ** (-jnp.arange(half, dtype=F32) * 2.0 / MLA_ROPE)
    cs = _rope_table(pos_col, jnp.concatenate([inv, inv]).reshape(1, MLA_ROPE))

    h = x.reshape(M, D)
    u = _rmsnorm(h, g_f1pre, 0)
    for l in range(L):
        h, u = _ffn(h, u, 0.5 * g_f1post, g_mpre, l, *f1, l)
        z = _in_proj(u, w_z, l)
        v_t = _in_proj_t(u, w_vt, l)
        qp, kp, vt = _mla_prep(z, cs, g_q, g_kv, w_q, w_uk, w_uv_t, l, zoff)
        a = _mla_attn(qp, kp, vt, B, S)

        kc, vct = _nsa_cmp(z, pe_k, cmp_pe_v, w1_k, w2_k, w1_v, w2_v, l, B, S, zoff)
        nsa = _nsa_attn(far, z, kc, vct, v_t, pos_row, pos_col, pos_cmp, tbl, B, S, zoff)
        h, u = _merge(a, nsa, z, h, w_a, w_b, w_o, g_mpost, g_f2pre, l, zoff)
        h, u = _ffn(h, u, 0.5 * g_f2post, g_f1pre, min(l + 1, L - 1), *f2, l)
    return h.reshape(B, S, D)
```

```python
import functools
import math

import numpy as np
import jax
import jax.numpy as jnp
from jax import lax
from jax.experimental import pallas as pl
from jax.experimental.pallas import tpu as pltpu

EPS = 1e-6
MLA_HEADS = 8
MLA_Q_LORA = 512
MLA_KV_LORA = 512
MLA_NOPE = 128
MLA_ROPE = 64
MLA_V = 128
ROPE_BASE = 10000.0
NSA_HEADS = 8
NSA_KV_HEADS = 2
NSA_GROUP = NSA_HEADS // NSA_KV_HEADS
NSA_DK = 192
NSA_DV = 128
CMP_LEN = 32
CMP_STRIDE = 16
CMP_HIDDEN = 256
SLC_LEN = 64
SLC_TOPN = 16
WINDOW = 512
FORCED_SCORE = 1e6
REL_BUCKETS = 32
REL_MAX_DIST = 128
NEG = -1e30
MASKED = 2 * NEG
LOG2E = 1.0 / math.log(2.0)

LANE = 128
ONES_ROWS = 16
HEAD_PAD = 256
VMEM_LIMIT = 56 * 1024 * 1024
BF16 = jnp.bfloat16
F32 = jnp.float32

FFN_TM, FFN_TF = 512, 512
PROJ_TM = 1024
PROJ_TN_CAP = 2560
PREP_TM = 512
MLA_TQ, MLA_TK = 512, 512
MLA_HEADS_PER_STEP = 8
NSA_TQ, NSA_TK = 256, 256
MERGE_TM = 512


def _bucket_of_distance():
    n = np.arange(LANE)
    max_exact = REL_BUCKETS // 2
    large = max_exact + (np.log(np.maximum(n, 1) / max_exact) / math.log(REL_MAX_DIST / max_exact)
                         * (REL_BUCKETS - max_exact)).astype(np.int32)
    bucket = np.where(n < max_exact, n, np.minimum(large, REL_BUCKETS - 1)).astype(np.int32)
    assert bucket[-1] == REL_BUCKETS - 1
    return bucket


BUCKET_OF_DIST = _bucket_of_distance()
FAR_DIST = int(np.max(np.nonzero(BUCKET_OF_DIST != REL_BUCKETS - 1)[0])) + 1


def _params(*sem):
    return pltpu.CompilerParams(dimension_semantics=sem, vmem_limit_bytes=VMEM_LIMIT)


def _rms(x, g):
    return x * lax.rsqrt(jnp.mean(x * x, axis=-1, keepdims=True) + EPS) * g


def _dot(a, b):
    return jnp.dot(a, b, preferred_element_type=F32)


def _dot_nt(a, b):
    return lax.dot_general(a, b, (((1,), (1,)), ((), ())), preferred_element_type=F32)


def _tile_n(n, cap):
    best = LANE
    for t in range(LANE, cap + 1, LANE):
        if n % t == 0:
            best = t
    return best


def _rmsnorm_kernel(x_ref, g_ref, o_ref):
    o_ref[...] = _rms(x_ref[...], g_ref[...]).astype(o_ref.dtype)


def _rmsnorm(x, g, layer):
    m, d = x.shape
    tm = 512
    return pl.pallas_call(
        _rmsnorm_kernel,
        grid=(m // tm,),
        in_specs=[pl.BlockSpec((tm, d), lambda i: (i, 0)),
                  pl.BlockSpec((None, 1, d), lambda i: (layer, 0, 0))],
        out_specs=pl.BlockSpec((tm, d), lambda i: (i, 0)),
        out_shape=jax.ShapeDtypeStruct((m, d), BF16),
        compiler_params=_params("parallel"),
        name="rmsnorm",
    )(x, g)


def _rope_table_kernel(pos_ref, inv_ref, o_ref):
    ang = pos_ref[...].astype(F32) * inv_ref[...]
    o_ref[...] = jnp.concatenate([jnp.cos(ang), jnp.sin(ang)], axis=1)


def _rope_table(pos_col, inv):
    m = pos_col.shape[0]
    tm = 512
    return pl.pallas_call(
        _rope_table_kernel,
        grid=(m // tm,),
        in_specs=[pl.BlockSpec((tm, 1), lambda i: (i, 0)),
                  pl.BlockSpec((1, MLA_ROPE), lambda i: (0, 0))],
        out_specs=pl.BlockSpec((tm, 2 * MLA_ROPE), lambda i: (i, 0)),
        out_shape=jax.ShapeDtypeStruct((m, 2 * MLA_ROPE), F32),
        compiler_params=_params("parallel"),
        name="rope_table",
    )(pos_col, inv)


def _ffn_kernel(x_ref, u_ref, post_g_ref, next_g_ref, wg_ref, wu_ref, wd_ref,
                o_ref, un_ref, acc_ref):
    j = pl.program_id(1)

    @pl.when(j == 0)
    def _():
        acc_ref[...] = jnp.zeros(acc_ref.shape, F32)

    u = u_ref[...]
    gate = _dot(u, wg_ref[...])
    up = _dot(u, wu_ref[...])
    hidden = (gate * jax.nn.sigmoid(gate) * up).astype(BF16)
    acc_ref[...] += _dot(hidden, wd_ref[...])

    @pl.when(j == pl.num_programs(1) - 1)
    def _():
        out = x_ref[...] + _rms(acc_ref[...], post_g_ref[...])
        o_ref[...] = out
        un_ref[...] = _rms(out, next_g_ref[...]).astype(BF16)


def _ffn(x, u, post_g, next_g, next_layer, wg, wu, wd, layer):
    m, d = x.shape
    f = wg.shape[-1]
    tm, tf = FFN_TM, _tile_n(f, FFN_TF)
    return pl.pallas_call(
        _ffn_kernel,
        grid=(m // tm, f // tf),
        in_specs=[pl.BlockSpec((tm, d), lambda i, j: (i, 0)),
                  pl.BlockSpec((tm, d), lambda i, j: (i, 0)),
                  pl.BlockSpec((None, 1, d), lambda i, j: (layer, 0, 0)),
                  pl.BlockSpec((None, 1, d), lambda i, j: (next_layer, 0, 0)),
                  pl.BlockSpec((None, d, tf), lambda i, j: (layer, 0, j)),
                  pl.BlockSpec((None, d, tf), lambda i, j: (layer, 0, j)),
                  pl.BlockSpec((None, tf, d), lambda i, j: (layer, j, 0))],
        out_specs=[pl.BlockSpec((tm, d), lambda i, j: (i, 0)),
                   pl.BlockSpec((tm, d), lambda i, j: (i, 0))],
        out_shape=[jax.ShapeDtypeStruct((m, d), F32), jax.ShapeDtypeStruct((m, d), BF16)],
        scratch_shapes=[pltpu.VMEM((tm, d), F32)],
        compiler_params=_params("parallel", "arbitrary"),
        name="ffn",
    )(x, u, post_g, next_g, wg, wu, wd)


def _matmul_nt_kernel(x_ref, w_ref, o_ref):
    o_ref[...] = _dot_nt(x_ref[...], w_ref[...]).astype(o_ref.dtype)


def _in_proj(u, w_t, layer):
    m, d = u.shape
    n = w_t.shape[1]
    tm, tn = PROJ_TM, _tile_n(n, PROJ_TN_CAP)
    return pl.pallas_call(
        _matmul_nt_kernel,
        grid=(n // tn, m // tm),
        in_specs=[pl.BlockSpec((tm, d), lambda j, i: (i, 0)),
                  pl.BlockSpec((None, tn, d), lambda j, i: (layer, j, 0))],
        out_specs=pl.BlockSpec((tm, tn), lambda j, i: (i, j)),
        out_shape=jax.ShapeDtypeStruct((m, n), BF16),
        compiler_params=_params("parallel", "parallel"),
        name="in_proj",
    )(u, w_t)


def _matmul_t_kernel(w_ref, x_ref, o_ref):
    o_ref[...] = _dot_nt(w_ref[...], x_ref[...]).astype(o_ref.dtype)


def _in_proj_t(u, w_t, layer):
    m, d = u.shape
    n = w_t.shape[1]
    tm = PROJ_TM
    return pl.pallas_call(
        _matmul_t_kernel,
        grid=(m // tm,),
        in_specs=[pl.BlockSpec((None, n, d), lambda i: (layer, 0, 0)),
                  pl.BlockSpec((tm, d), lambda i: (i, 0))],
        out_specs=pl.BlockSpec((n, tm), lambda i: (0, i)),
        out_shape=jax.ShapeDtypeStruct((n, m), BF16),
        compiler_params=_params("parallel"),
        name="in_proj_t",
    )(w_t, u)


def _softmax_tile_t(s_ref, adds, masks, v_t, m_ref, acc_ref):
    probs, alphas = [], []
    for c in range(s_ref.shape[1] // LANE):
        cols = slice(c * LANE, (c + 1) * LANE)
        s = s_ref[:, cols]
        for mask in masks[c]:
            s = jnp.where(mask, s, MASKED) if mask.dtype == jnp.bool_ else s + mask
        m_old = m_ref[:, cols]
        m_tile = jnp.max(s, axis=0, keepdims=True)
        if adds[c] is not None:
            m_tile = m_tile + adds[c]
        m_new = jnp.maximum(m_old, m_tile)
        alpha = jnp.exp2(m_old - m_new)
        p = jnp.exp2(s - (m_new if adds[c] is None else m_new - adds[c]))
        m_ref[:, cols] = m_new
        probs.append(p.astype(BF16))
        alphas.append(alpha)
    p_t = probs[0] if len(probs) == 1 else jnp.concatenate(probs, axis=1)
    alpha = alphas[0] if len(alphas) == 1 else jnp.concatenate(alphas, axis=1)
    v_ones = jnp.concatenate([v_t, jnp.ones((ONES_ROWS, v_t.shape[1]), BF16)], axis=0)
    acc_ref[...] = alpha * acc_ref[...] + _dot(v_ones, p_t)


def _softmax_init(m_ref, acc_ref):
    m_ref[...] = jnp.full(m_ref.shape, NEG, F32)
    acc_ref[...] = jnp.zeros(acc_ref.shape, F32)


def _softmax_out(acc):
    dv = acc.shape[0] - ONES_ROWS
    return acc[:dv] / acc[dv:dv + 1]


def _mla_prep_kernel(cq_ref, ckv_ref, kr_ref, cs_ref, gq_ref, gkv_ref, wq_ref, wuk_ref, wuvt_ref,
                     q_ref, k_ref, vt_ref):
    cs = cs_ref[...]
    qn = _rms(cq_ref[...].astype(F32), gq_ref[...]).astype(BF16)
    q = _dot(qn, wq_ref[...])
    kvn = _rms(ckv_ref[...].astype(F32), gkv_ref[...]).astype(BF16)
    k_nope = _dot(kvn, wuk_ref[...])
    t = kr_ref[...].astype(F32) * cs
    k_pe = (t + pltpu.roll(t, MLA_ROPE, 1)).astype(BF16)
    for h in range(MLA_HEADS):
        lo = h * HEAD_PAD
        q_ref[:, lo:lo + MLA_NOPE] = q[:, lo:lo + MLA_NOPE].astype(BF16)
        q_ref[:, lo + MLA_NOPE:lo + HEAD_PAD] = (q[:, lo + MLA_NOPE:lo + HEAD_PAD] * cs).astype(BF16)
        k_ref[:, lo:lo + MLA_NOPE] = k_nope[:, h * MLA_NOPE:(h + 1) * MLA_NOPE].astype(BF16)
        k_ref[:, lo + MLA_NOPE:lo + HEAD_PAD] = k_pe
    vt_ref[...] = _dot_nt(wuvt_ref[...], kvn).astype(BF16)


def _mla_prep(z, cs, gq, gkv, wq, wuk, wuv_t, layer, zoff):
    m = z.shape[0]
    tm = PREP_TM
    hq = MLA_HEADS * HEAD_PAD
    hv = MLA_HEADS * MLA_V
    const = lambda i: (layer, 0, 0)
    return pl.pallas_call(
        _mla_prep_kernel,
        grid=(m // tm,),
        in_specs=[pl.BlockSpec((tm, MLA_Q_LORA), lambda i: (i, zoff["cq"] // MLA_Q_LORA)),
                  pl.BlockSpec((tm, MLA_KV_LORA), lambda i: (i, zoff["ckv"] // MLA_KV_LORA)),
                  pl.BlockSpec((tm, LANE), lambda i: (i, zoff["kr"] // LANE)),
                  pl.BlockSpec((tm, LANE), lambda i: (i, 0)),
                  pl.BlockSpec((None, 1, MLA_Q_LORA), const),
                  pl.BlockSpec((None, 1, MLA_KV_LORA), const),
                  pl.BlockSpec((None, MLA_Q_LORA, hq), const),
                  pl.BlockSpec((None, MLA_KV_LORA, MLA_HEADS * MLA_NOPE), const),
                  pl.BlockSpec((None, hv, MLA_KV_LORA), const)],
        out_specs=[pl.BlockSpec((tm, hq), lambda i: (i, 0)),
                   pl.BlockSpec((tm, hq), lambda i: (i, 0)),
                   pl.BlockSpec((hv, tm), lambda i: (0, i))],
        out_shape=[jax.ShapeDtypeStruct((m, hq), BF16),
                   jax.ShapeDtypeStruct((m, hq), BF16),
                   jax.ShapeDtypeStruct((hv, m), BF16)],
        compiler_params=_params("parallel"),
        name="mla_prep",
    )(z, z, z, cs, gq, gkv, wq, wuk, wuv_t)


def _by_parity(kt, fn):
    @pl.when(kt % 2 == 0)
    def _():
        fn(0)

    @pl.when(kt % 2 == 1)
    def _():
        fn(1)


def _mla_attn_kernel(q_ref, k_ref, vt_ref, o_ref, s_ref, m_ref, acc_ref, *, tq, tk, heads):
    i = pl.program_id(2)
    nc = tq // LANE
    _softmax_init(m_ref, acc_ref)
    qt = i * tq + lax.broadcasted_iota(jnp.int32, (1, tq), 1)
    kcol = lax.broadcasted_iota(jnp.int32, (tk, 1), 0)
    qk = [slice(h * HEAD_PAD, (h + 1) * HEAD_PAD) for h in range(heads)]
    vd = [slice(h * MLA_V, (h + 1) * MLA_V) for h in range(heads)]
    n_full = (i * tq) // tk
    n_all = (i * tq + tq + tk - 1) // tk

    def logits(kt, buf):
        ks = pl.multiple_of(kt * tk, tk)
        for h in range(heads):
            s_ref[buf, h] = _dot_nt(k_ref[pl.ds(ks, tk), qk[h]], q_ref[:, qk[h]])

    def step(kt, buf, masked, prefetch=True):
        ks = pl.multiple_of(kt * tk, tk)
        masks = [()] * nc
        if masked:
            masks = [(ks + kcol <= qt[:, c * LANE:(c + 1) * LANE],) for c in range(nc)]
        for h in range(heads):
            _softmax_tile_t(s_ref.at[buf, h], [None] * nc, masks, vt_ref[vd[h], pl.ds(ks, tk)],
                            m_ref.at[h], acc_ref.at[h])
        if prefetch:
            logits(kt + 1, 1 - buf)

    def full_body(kt, c):
        _by_parity(kt, lambda buf: step(kt, buf, False))
        return c

    def diag_body(kt, c):
        _by_parity(kt, lambda buf: step(kt, buf, True))
        return c

    logits(0, 0)
    lax.fori_loop(0, n_full, full_body, 0)
    lax.fori_loop(n_full, n_all - 1, diag_body, 0)
    _by_parity(n_all - 1, lambda buf: step(n_all - 1, buf, True, prefetch=False))
    for h in range(heads):
        o_ref[:, vd[h]] = _softmax_out(acc_ref[h]).T.astype(o_ref.dtype)


def _mla_attn(qp, kp, vt, batch, seq):
    tq, tk, heads = MLA_TQ, MLA_TK, MLA_HEADS_PER_STEP
    nq = seq // tq
    kern = functools.partial(_mla_attn_kernel, tq=tq, tk=tk, heads=heads)
    return pl.pallas_call(
        kern,
        grid=(batch, MLA_HEADS // heads, nq),
        in_specs=[pl.BlockSpec((tq, heads * HEAD_PAD), lambda b, h, i: (b * nq + i, h)),
                  pl.BlockSpec((seq, heads * HEAD_PAD), lambda b, h, i: (b, h)),
                  pl.BlockSpec((heads * MLA_V, seq), lambda b, h, i: (h, b))],
        out_specs=pl.BlockSpec((tq, heads * MLA_V), lambda b, h, i: (b * nq + i, h)),
        out_shape=jax.ShapeDtypeStruct((batch * seq, MLA_HEADS * MLA_V), BF16),
        scratch_shapes=[pltpu.VMEM((2, heads, tk, tq), F32), pltpu.VMEM((heads, 1, tq), F32),
                        pltpu.VMEM((heads, MLA_V + ONES_ROWS, tq), F32)],
        compiler_params=_params("parallel", "parallel", "parallel"),
        name="mla_attn",
    )(qp, kp, vt)


def _compress(z_ref, zf_ref, pe_ref, w1_ref, w2_ref):
    chunks, s, _ = zf_ref.shape
    for c in range(chunks):
        zf_ref[c] = z_ref[:, c * LANE:(c + 1) * LANE].astype(F32)
    d = chunks * LANE
    n = s // CMP_STRIDE
    first = jnp.zeros((n, CMP_HIDDEN), F32)
    second = jnp.zeros((n, CMP_HIDDEN), F32)
    for l in range(CMP_STRIDE):
        toks = [zf_ref[c, pl.ds(l, n, stride=CMP_STRIDE), :] for c in range(chunks)]
        tok = toks[0] if chunks == 1 else jnp.concatenate(toks, axis=1)
        lo, hi = l, CMP_STRIDE + l
        first += _dot((tok + pe_ref[lo:lo + 1, :]).astype(BF16), w1_ref[lo * d:(lo + 1) * d, :])
        second += _dot((tok + pe_ref[hi:hi + 1, :]).astype(BF16), w1_ref[hi * d:(hi + 1) * d, :])
    hidden = first + pltpu.roll(second, n - 1, 0)
    return _dot((hidden * jax.nn.sigmoid(hidden)).astype(BF16), w2_ref[...])


def _nsa_cmp_kernel(zk_ref, zv_ref, pek_ref, pev_ref, w1k_ref, w2k_ref, w1v_ref, w2v_ref,
                    kc_ref, vct_ref, kf_ref, vf_ref):
    kc_ref[...] = _compress(zk_ref, kf_ref, pek_ref, w1k_ref, w2k_ref).astype(BF16)
    vct_ref[...] = _compress(zv_ref, vf_ref, pev_ref, w1v_ref, w2v_ref).T.astype(BF16)


def _nsa_cmp(z, pek, pev, w1k, w2k, w1v, w2v, layer, batch, seq, zoff):
    G = NSA_KV_HEADS
    nh = seq // CMP_STRIDE
    const = lambda i: (layer, 0, 0)
    return pl.pallas_call(
        _nsa_cmp_kernel,
        grid=(batch * G,),
        in_specs=[pl.BlockSpec((seq, HEAD_PAD), lambda i: (i // G, zoff["kc"] // HEAD_PAD + i % G)),
                  pl.BlockSpec((seq, NSA_DV), lambda i: (i // G, zoff["vc"] // NSA_DV + i % G)),
                  pl.BlockSpec((None, CMP_LEN, HEAD_PAD), const),
                  pl.BlockSpec((None, CMP_LEN, NSA_DV), const),
                  pl.BlockSpec((None, CMP_LEN * HEAD_PAD, CMP_HIDDEN), const),
                  pl.BlockSpec((None, CMP_HIDDEN, HEAD_PAD), const),
                  pl.BlockSpec((None, CMP_LEN * NSA_DV, CMP_HIDDEN), const),
                  pl.BlockSpec((None, CMP_HIDDEN, NSA_DV), const)],
        out_specs=[pl.BlockSpec((None, nh, HEAD_PAD), lambda i: (i, 0, 0)),
                   pl.BlockSpec((None, NSA_DV, nh), lambda i: (i, 0, 0))],
        out_shape=[jax.ShapeDtypeStruct((batch * G, nh, HEAD_PAD), BF16),
                   jax.ShapeDtypeStruct((batch * G, NSA_DV, nh), BF16)],
        scratch_shapes=[pltpu.VMEM((HEAD_PAD // LANE, seq, LANE), F32),
                        pltpu.VMEM((NSA_DV // LANE, seq, LANE), F32)],
        compiler_params=_params("parallel"),
        name="nsa_cmp",
    )(z, z, pek, pev, w1k, w2k, w1v, w2v)


def _split3(x):
    a = x.astype(BF16)
    r = x - a.astype(F32)
    b = r.astype(BF16)
    c = (r - b.astype(F32)).astype(BF16)
    return a, b, c


def _lookup(table_row, idx):
    rows, width = idx.shape
    table = jnp.broadcast_to(table_row, (rows, LANE))
    chunks = [jnp.take_along_axis(table, idx[:, c:c + LANE], axis=1, mode="promise_in_bounds")
              for c in range(0, width, LANE)]
    return chunks[0] if len(chunks) == 1 else jnp.concatenate(chunks, axis=1)


def _nsa_attn_kernel(far_ref, q_ref, kc_ref, vct_ref, ks_ref, vst_ref, kw_ref, vwt_ref, ng_ref,
                     prow_ref, pcol_ref, pcmp_ref, tbl_ref, o_ref,
                     s_ref, m_ref, acc_ref, gate_ref, *, tq, tk, seq):
    b, i = pl.program_id(0), pl.program_id(2)
    nq, nk = seq // tq, seq // tk
    J = NSA_GROUP
    n_slc = seq // SLC_LEN
    n_cmp = (seq - CMP_LEN) // CMP_STRIDE + 1
    t0 = pl.multiple_of(i * tq, tq)

    qs = jnp.concatenate([q_ref[:, j * HEAD_PAD:(j + 1) * HEAD_PAD] for j in range(J)], axis=0)
    qpos = prow_ref[:, pl.ds(t0, tq)]
    qt = t0 + lax.broadcasted_iota(jnp.int32, (1, tq), 1)
    tbl = tbl_ref[...]
    far_bias = [tbl[j:j + 1, LANE - 1:LANE] for j in range(J)]
    head = [slice(j * tq, (j + 1) * tq) for j in range(J)]

    blk = lax.broadcasted_iota(jnp.int32, (LANE, 1), 0)
    keep_c = (blk * CMP_STRIDE + (CMP_LEN - 1) <= qt) & (blk < n_cmp)
    idx_c = jnp.clip(qpos - pcmp_ref[...], 0, LANE - 1)
    s_c = _dot_nt(kc_ref[...], qs)
    p_heads = []
    for j in range(J):
        s = jnp.where(keep_c, s_c[:, head[j]] + _lookup(tbl[j:j + 1, :], idx_c), NEG)
        e = jnp.where(keep_c, jnp.exp2(s - jnp.max(s, axis=0, keepdims=True)), 0.0)
        den = jnp.sum(e, axis=0, keepdims=True)
        p_heads.append(e / jnp.where(den > 0.0, den, 1.0))
    o_cmp = _dot(vct_ref[...], jnp.concatenate([p.astype(BF16) for p in p_heads], axis=1))

    rows = 32
    assert n_slc <= rows
    m_row = lax.broadcasted_iota(jnp.int32, (rows, LANE), 0)
    n_col = lax.broadcasted_iota(jnp.int32, (rows, LANE), 1)
    per = SLC_LEN // CMP_STRIDE
    back = (CMP_LEN - 1) // CMP_STRIDE
    overlap = ((n_col >= per * m_row - back) & (n_col <= per * m_row + per - 1)
               & (n_col < n_cmp) & (m_row < n_slc)).astype(BF16)
    imp = sum(_dot(overlap, part) for part in _split3(sum(p_heads)))
    m_blk = lax.broadcasted_iota(jnp.int32, (rows, 1), 0)
    cur = qt // SLC_LEN
    valid = m_blk <= cur
    forced = valid & ((m_blk == 0) | (m_blk >= cur - 1))
    score = jnp.where(forced, FORCED_SCORE, jnp.where(valid, imp, -1.0))
    rank = jnp.zeros((rows, tq), jnp.int32)
    for mp in range(n_slc):
        other = score[mp:mp + 1, :]
        ahead = (other > score) | ((other == score) & (m_blk > mp))
        rank = rank + ahead.astype(jnp.int32)
    selected = valid & (rank < min(SLC_TOPN, n_slc))
    sel_penalty = jnp.where(selected, 0.0, MASKED).astype(BF16)

    SLC, WIN = 0, 1
    _softmax_init(m_ref, acc_ref)
    sub = tk // LANE

    qchunks = [slice(c, c + LANE) for c in range(0, tq, LANE)]

    assert tq == tk

    def slc_masks(kidx, diagonal):
        expand = (kidx // SLC_LEN == lax.broadcasted_iota(jnp.int32, (1, rows), 1)).astype(BF16)
        penalty = _dot(expand, sel_penalty)
        if diagonal:
            return [(penalty[:, c], kidx <= qt[:, c]) for c in qchunks]
        return [(penalty[:, c],) for c in qchunks]

    def win_masks(kidx, diagonal):
        return [(jnp.where(lax.bitcast_convert_type(qt[:, c] - kidx, jnp.uint32) < WINDOW,
                           0.0, MASKED),) for c in qchunks]

    k_refs, vt_refs, masks_of = (ks_ref, kw_ref), (vst_ref, vwt_ref), (slc_masks, win_masks)
    adds = [far_bias[j] for j in range(J) for _ in qchunks]
    last = (t0 + tq + tk - 1) // tk
    first_win = jnp.maximum(t0 - (WINDOW - 1), 0) // tk

    def logits(kt, buf, slots):
        ks = pl.multiple_of(kt * tk, tk)
        for slot in slots:
            s_ref[buf, slot] = _dot_nt(k_refs[slot][pl.ds(ks, tk), :], qs)

    def near_bias(kt, buf, slots):
        ks = pl.multiple_of(kt * tk, tk)
        for hb in range(sub):
            kb = kt * sub + hb
            for qc, qcols in enumerate(qchunks):
                near = ((far_ref[((b * nq + i) * len(qchunks) + qc) * (nk * sub) + kb] == 0)
                        & (kb * LANE < t0 + (qc + 1) * LANE))

                @pl.when(near)
                def _():
                    kpos = pcol_ref[pl.ds(pl.multiple_of(ks + hb * LANE, LANE), LANE), :]
                    idx = jnp.clip(qpos[:, qcols] - kpos, 0, LANE - 1)
                    for j in range(J):
                        delta = _lookup(tbl[j:j + 1, :] - far_bias[j], idx)
                        cols = slice(j * tq + qc * LANE, j * tq + (qc + 1) * LANE)
                        for slot in slots:
                            s_ref[buf, slot, hb * LANE:(hb + 1) * LANE, cols] += delta

    def step(kt, buf, slots, final=False):
        ks = pl.multiple_of(kt * tk, tk)
        kidx = ks + lax.broadcasted_iota(jnp.int32, (tk, 1), 0)
        for slot in slots:
            _softmax_tile_t(s_ref.at[buf, slot], adds, masks_of[slot](kidx, final) * J,
                            vt_refs[slot][:, pl.ds(ks, tk)], m_ref.at[slot], acc_ref.at[slot])
        if not final:
            logits(kt + 1, 1 - buf, slots)
            near_bias(kt + 1, 1 - buf, slots)

    def slc_body(kt, c):
        _by_parity(kt, lambda buf: step(kt, buf, (SLC,)))
        return c

    def both_body(kt, c):
        _by_parity(kt, lambda buf: step(kt, buf, (SLC, WIN)))
        return c

    def enter_window(buf):
        logits(first_win, buf, (WIN,))
        near_bias(first_win, buf, (WIN,))

    logits(0, 0, (SLC,))
    near_bias(0, 0, (SLC,))
    lax.fori_loop(0, first_win, slc_body, 0)
    _by_parity(first_win, enter_window)
    lax.fori_loop(first_win, last - 1, both_body, 0)
    _by_parity(last - 1, lambda buf: step(last - 1, buf, (SLC, WIN), final=True))
    o_slc = _softmax_out(acc_ref[SLC])
    o_win = _softmax_out(acc_ref[WIN])

    gate_ref[...] = jax.nn.sigmoid(ng_ref[...].astype(F32)).T
    first_gate = pl.program_id(1) * (J * 3)
    for j in range(J):
        gate = [gate_ref[pl.ds(first_gate + 3 * j + c, 1), :] for c in range(3)]
        o = (gate[0] * o_cmp[:, head[j]] + gate[1] * o_slc[:, head[j]]
             + gate[2] * o_win[:, head[j]])
        o_ref[:, j * NSA_DV:(j + 1) * NSA_DV] = o.T.astype(o_ref.dtype)


def _nsa_attn(far, z, kc, vct, v_t, pos_row, pos_col, pos_cmp, tbl, batch, seq, zoff):
    tq, tk = NSA_TQ, NSA_TK
    nq = seq // tq
    G, J = NSA_KV_HEADS, NSA_GROUP
    R = J * tq
    kern = functools.partial(_nsa_attn_kernel, tq=tq, tk=tk, seq=seq)
    grid_spec = pltpu.PrefetchScalarGridSpec(
        num_scalar_prefetch=1,
        grid=(batch, G, nq),
        in_specs=[
            pl.BlockSpec((tq, J * HEAD_PAD), lambda b, g, i, far: (b * nq + i, g)),
            pl.BlockSpec((None, LANE, HEAD_PAD), lambda b, g, i, far: (b * G + g, 0, 0)),
            pl.BlockSpec((None, NSA_DV, LANE), lambda b, g, i, far: (b * G + g, 0, 0)),
            pl.BlockSpec((seq, HEAD_PAD), lambda b, g, i, far: (b, zoff["ks"] // HEAD_PAD + g)),
            pl.BlockSpec((NSA_DV, seq), lambda b, g, i, far: (g, b)),
            pl.BlockSpec((seq, HEAD_PAD), lambda b, g, i, far: (b, zoff["kw"] // HEAD_PAD + g)),
            pl.BlockSpec((NSA_DV, seq), lambda b, g, i, far: (G + g, b)),
            pl.BlockSpec((tq, LANE), lambda b, g, i, far: (b * nq + i, zoff["ng"] // LANE)),
            pl.BlockSpec((None, 1, seq), lambda b, g, i, far: (b, 0, 0)),
            pl.BlockSpec((seq, 1), lambda b, g, i, far: (b, 0)),
            pl.BlockSpec((LANE, 1), lambda b, g, i, far: (b, 0)),
            pl.BlockSpec((None, J, LANE), lambda b, g, i, far: (g, 0, 0)),
        ],
        out_specs=pl.BlockSpec((tq, J * NSA_DV), lambda b, g, i, far: (b * nq + i, g)),
        scratch_shapes=[pltpu.VMEM((2, 2, tk, R), F32), pltpu.VMEM((2, 1, R), F32),
                        pltpu.VMEM((2, NSA_DV + ONES_ROWS, R), F32),
                        pltpu.VMEM((LANE, tq), F32)],
    )
    return pl.pallas_call(
        kern,
        grid_spec=grid_spec,
        out_shape=jax.ShapeDtypeStruct((batch * seq, NSA_HEADS * NSA_DV), BF16),
        compiler_params=_params("parallel", "parallel", "parallel"),
        name="nsa_attn",
    )(far, z, kc, vct, z, v_t, z, v_t, z, pos_row, pos_col, pos_cmp, tbl)


def _merge_kernel(a_ref, b_ref, ga_ref, gb_ref, h_ref, wa_ref, wb_ref, wo_ref, post_g_ref,
                  next_g_ref, o_ref, un_ref):
    ya = _dot(a_ref[...], wa_ref[...])
    yb = _dot(b_ref[...], wb_ref[...])
    m = (jax.nn.sigmoid(ga_ref[...].astype(F32)) * ya
         + jax.nn.sigmoid(gb_ref[...].astype(F32)) * yb).astype(BF16)
    y = _dot(m, wo_ref[...])
    out = h_ref[...] + _rms(y, post_g_ref[...])
    o_ref[...] = out
    un_ref[...] = _rms(out, next_g_ref[...]).astype(BF16)


def _merge(a, bb, z, h, wa, wb, wo, post_g, next_g, layer, zoff):
    m, d = h.shape
    tm = MERGE_TM
    da, db = a.shape[1], bb.shape[1]
    gblk = zoff["mg"] // d
    const = lambda i: (layer, 0, 0)
    return pl.pallas_call(
        _merge_kernel,
        grid=(m // tm,),
        in_specs=[pl.BlockSpec((tm, da), lambda i: (i, 0)),
                  pl.BlockSpec((tm, db), lambda i: (i, 0)),
                  pl.BlockSpec((tm, d), lambda i: (i, gblk)),
                  pl.BlockSpec((tm, d), lambda i: (i, gblk + 1)),
                  pl.BlockSpec((tm, d), lambda i: (i, 0)),
                  pl.BlockSpec((None, da, d), const, pipeline_mode=pl.Buffered(1)),
                  pl.BlockSpec((None, db, d), const, pipeline_mode=pl.Buffered(1)),
                  pl.BlockSpec((None, d, d), const, pipeline_mode=pl.Buffered(1)),
                  pl.BlockSpec((None, 1, d), const),
                  pl.BlockSpec((None, 1, d), const)],
        out_specs=[pl.BlockSpec((tm, d), lambda i: (i, 0)),
                   pl.BlockSpec((tm, d), lambda i: (i, 0))],
        out_shape=[jax.ShapeDtypeStruct((m, d), F32), jax.ShapeDtypeStruct((m, d), BF16)],
        compiler_params=_params("parallel"),
        name="merge_out",
    )(a, bb, z, z, h, wa, wb, wo, post_g, next_g)


def _z_layout(d_model):
    G = NSA_KV_HEADS
    widths = [("q", NSA_HEADS * HEAD_PAD), ("mg", 2 * d_model), ("cq", MLA_Q_LORA),
              ("ckv", MLA_KV_LORA), ("kc", G * HEAD_PAD), ("ks", G * HEAD_PAD),
              ("kw", G * HEAD_PAD), ("vc", G * NSA_DV), ("kr", LANE), ("ng", LANE)]
    off, pos = {}, 0
    for name, w in widths:
        off[name] = pos
        pos += w
    off["total"] = pos
    assert off["q"] == 0 and off["mg"] % d_model == 0
    assert off["cq"] % MLA_Q_LORA == 0 and off["ckv"] % MLA_KV_LORA == 0
    assert all(off[k] % HEAD_PAD == 0 for k in ("kc", "ks", "kw"))
    return off


def _swap_halves(w):
    half = w.shape[-1] // 2
    return jnp.concatenate([-w[..., half:], w[..., :half]], axis=-1)


def _pad_last(w, width):
    return jnp.pad(w, [(0, 0)] * (w.ndim - 1) + [(0, width - w.shape[-1])])


def _w_in_pieces(d_model, zoff):
    G, J = NSA_KV_HEADS, NSA_GROUP
    splits = [MLA_Q_LORA, MLA_KV_LORA, MLA_ROPE, NSA_HEADS * NSA_DK,
              G * NSA_DK, G * NSA_DV, G * NSA_DK, G * NSA_DV, G * NSA_DK, G * NSA_DV,
              NSA_HEADS * 3, 2 * d_model]
    names = ["cq", "ckv", "kr", "q", "kc", "vc", "ks", "vs", "kw", "vw", "ng", "mg"]
    src = dict(zip(names, [0] + [int(v) for v in np.cumsum(splits)[:-1]]))
    half = MLA_ROPE // 2
    pieces = []
    for h in range(NSA_HEADS):
        pieces.append((zoff["q"] + h * HEAD_PAD, HEAD_PAD,
                       [(src["q"] + h * NSA_DK, NSA_DK, NSA_DK ** -0.5 * LOG2E)]))
    pieces.append((zoff["mg"], 2 * d_model, [(src["mg"], 2 * d_model, None)]))
    pieces.append((zoff["cq"], MLA_Q_LORA, [(src["cq"], MLA_Q_LORA, None)]))
    pieces.append((zoff["ckv"], MLA_KV_LORA, [(src["ckv"], MLA_KV_LORA, None)]))
    for name in ("kc", "ks", "kw"):
        for g in range(G):
            pieces.append((zoff[name] + g * HEAD_PAD, HEAD_PAD, [(src[name] + g * NSA_DK, NSA_DK, None)]))
    pieces.append((zoff["vc"], G * NSA_DV, [(src["vc"], G * NSA_DV, None)]))
    pieces.append((zoff["kr"], LANE, [(src["kr"], MLA_ROPE, None), (src["kr"] + half, half, -1.0),
                                      (src["kr"], half, None)]))
    pieces.append((zoff["ng"], LANE, [(src["ng"], NSA_HEADS * 3, None)]))
    return pieces, (src["vs"], src["vw"])


COPY_ROWS = 512


def _copy_rows(w_ref, o_ref, src, dst, n, scale):
    for off in range(0, n, COPY_ROWS):
        m = min(COPY_ROWS, n - off)
        x = w_ref[src + off:src + off + m, :]
        o_ref[dst + off:dst + off + m, :] = (x if scale is None else x * scale).astype(BF16)


def _w_in_layout_kernel(w_ref, o_ref, vt_ref, *, pieces, value_rows):
    tc = w_ref.shape[1]
    dst = 0
    for src, n in value_rows:
        _copy_rows(w_ref, vt_ref, src, dst, n, None)
        dst += n
    for dst, width, parts in pieces:
        used = sum(n for _, n, _ in parts)
        if all(n % ONES_ROWS == 0 for _, n, _ in parts):
            at = dst
            for src, n, scale in parts:
                _copy_rows(w_ref, o_ref, src, at, n, scale)
                at += n
            if used < width:
                o_ref[at:dst + width, :] = jnp.zeros((width - used, tc), BF16)
        else:
            rows = [w_ref[src:src + n, :] if scale is None else w_ref[src:src + n, :] * scale
                    for src, n, scale in parts]
            rows.append(jnp.zeros((width - used, tc), F32))
            o_ref[dst:dst + width, :] = jnp.concatenate(rows, axis=0).astype(BF16)


def _layout_w_in(w_in, zoff):
    L, D, n_in = w_in.shape
    G = NSA_KV_HEADS
    pieces, (vs, vw) = _w_in_pieces(D, zoff)
    tc = 256
    nv = 2 * G * NSA_DV
    value_rows = ((vs, G * NSA_DV), (vw, G * NSA_DV))
    return pl.pallas_call(
        functools.partial(_w_in_layout_kernel, pieces=pieces, value_rows=value_rows),
        grid=(L, D // tc),
        in_specs=[pl.BlockSpec((None, n_in, tc), lambda l, c: (l, 0, c))],
        out_specs=[pl.BlockSpec((None, zoff["total"], tc), lambda l, c: (l, 0, c)),
                   pl.BlockSpec((None, nv, tc), lambda l, c: (l, 0, c))],
        out_shape=[jax.ShapeDtypeStruct((L, zoff["total"], D), BF16),
                   jax.ShapeDtypeStruct((L, nv, D), BF16)],
        compiler_params=_params("parallel", "parallel"),
        name="w_in_layout",
    )(jnp.swapaxes(w_in, 1, 2))


def _layout_w_q_up(w):
    L, r, _ = w.shape
    w = w.reshape(L, r, MLA_HEADS, MLA_NOPE + MLA_ROPE) * ((MLA_NOPE + MLA_ROPE) ** -0.5 * LOG2E)
    rope = w[..., MLA_NOPE:]
    return jnp.concatenate([w[..., :MLA_NOPE], rope, _swap_halves(rope)], axis=-1).reshape(
        L, r, MLA_HEADS * HEAD_PAD).astype(BF16)


def kernel(x, positions, rel_bias, ffn1_pre_g, ffn1_post_g, ffn1_w_gate, ffn1_w_up, ffn1_w_down, mix_pre_g, mix_post_g, w_in, mla_q_norm_g, mla_w_q_up, mla_kv_norm_g, mla_w_uk, mla_w_uv, cmp_pe_k, cmp_w1_k, cmp_w2_k, cmp_pe_v, cmp_w1_v, cmp_w2_v, w_branch_mla, w_branch_nsa, w_out, ffn2_pre_g, ffn2_post_g, ffn2_w_gate, ffn2_w_up, ffn2_w_down):
    B, S, D = x.shape
    L = w_in.shape[0]
    M = B * S
    G, J = NSA_KV_HEADS, NSA_GROUP
    zoff = _z_layout(D)
    n_half = S // CMP_STRIDE
    n_cmp = (S - CMP_LEN) // CMP_STRIDE + 1
    assert n_half == LANE and n_cmp <= LANE

    gain = lambda g: g.reshape(L, 1, -1)
    bf = lambda w: w.astype(BF16)
    w_z, w_vt = _layout_w_in(w_in, zoff)
    w_q = _layout_w_q_up(mla_w_q_up)
    w_uk, w_uv_t = bf(mla_w_uk), bf(mla_w_uv).transpose(0, 2, 1)
    pe_k = _pad_last(cmp_pe_k, HEAD_PAD)
    w1_k = bf(jnp.pad(cmp_w1_k.reshape(L, CMP_LEN, NSA_DK, CMP_HIDDEN),
                      ((0, 0), (0, 0), (0, HEAD_PAD - NSA_DK), (0, 0)))
              ).reshape(L, CMP_LEN * HEAD_PAD, CMP_HIDDEN)
    w2_k = bf(_pad_last(cmp_w2_k, HEAD_PAD))
    f1 = (bf(ffn1_w_gate), bf(ffn1_w_up), bf(ffn1_w_down))
    f2 = (bf(ffn2_w_gate), bf(ffn2_w_up), bf(ffn2_w_down))
    w1_v, w2_v = bf(cmp_w1_v), bf(cmp_w2_v)
    w_a, w_b, w_o = bf(w_branch_mla), bf(w_branch_nsa), bf(w_out)
    g_f1pre, g_f1post, g_mpre, g_mpost = gain(ffn1_pre_g), gain(ffn1_post_g), gain(mix_pre_g), gain(mix_post_g)
    g_f2pre, g_f2post, g_q, g_kv = gain(ffn2_pre_g), gain(ffn2_post_g), gain(mla_q_norm_g), gain(mla_kv_norm_g)

    pos_col = positions.reshape(M, 1)
    pos_row = positions.reshape(B, 1, S)
    pos_cmp = _pad_last(positions[:, CMP_LEN - 1::CMP_STRIDE][:, :n_cmp], LANE).reshape(B * LANE, 1)
    q_min = positions.reshape(B, S // LANE, LANE).min(axis=-1)
    k_max = positions.reshape(B, S // LANE, LANE).max(axis=-1)
    far = (q_min[:, :, None] - k_max[:, None, :] >= FAR_DIST).astype(jnp.int32).reshape(-1)
    tbl = (jnp.take(rel_bias, jnp.asarray(BUCKET_OF_DIST), axis=0).T * LOG2E).reshape(G, J, LANE)
    half = MLA_ROPE // 2
    inv = ROPE_BASE ** (-jnp.arange(half, dtype=F32) * 2.0 / MLA_ROPE)
    cs = _rope_table(pos_col, jnp.concatenate([inv, inv]).reshape(1, MLA_ROPE))

    h = x.reshape(M, D)
    u = _rmsnorm(h, g_f1pre, 0)
    for l in range(L):
        h, u = _ffn(h, u, 0.5 * g_f1post, g_mpre, l, *f1, l)
        z = _in_proj(u, w_z, l)
        v_t = _in_proj_t(u, w_vt, l)
        qp, kp, vt = _mla_prep(z, cs, g_q, g_kv, w_q, w_uk, w_uv_t, l, zoff)
        a = _mla_attn(qp, kp, vt, B, S)

        kc, vct = _nsa_cmp(z, pe_k, cmp_pe_v, w1_k, w2_k, w1_v, w2_v, l, B, S, zoff)
        nsa = _nsa_attn(far, z, kc, vct, v_t, pos_row, pos_col, pos_cmp, tbl, B, S, zoff)
        h, u = _merge(a, nsa, z, h, w_a, w_b, w_o, g_mpost, g_f2pre, l, zoff)
        h, u = _ffn(h, u, 0.5 * g_f2post, g_f1pre, min(l + 1, L - 1), *f2, l)
    return h.reshape(B, S, D)
```

```python
import functools
import math

import numpy as np
import jax
import jax.numpy as jnp
from jax import lax
from jax.experimental import pallas as pl
from jax.experimental.pallas import tpu as pltpu

EPS = 1e-6
MLA_HEADS = 8
MLA_Q_LORA = 512
MLA_KV_LORA = 512
MLA_NOPE = 128
MLA_ROPE = 64
MLA_V = 128
ROPE_BASE = 10000.0
NSA_HEADS = 8
NSA_KV_HEADS = 2
NSA_GROUP = NSA_HEADS // NSA_KV_HEADS
NSA_DK = 192
NSA_DV = 128
CMP_LEN = 32
CMP_STRIDE = 16
CMP_HIDDEN = 256
SLC_LEN = 64
SLC_TOPN = 16
WINDOW = 512
FORCED_SCORE = 1e6
REL_BUCKETS = 32
REL_MAX_DIST = 128
NEG = -1e30
MASKED = 2 * NEG
LOG2E = 1.0 / math.log(2.0)

LANE = 128
ONES_ROWS = 16
HEAD_PAD = 256
VMEM_LIMIT = 56 * 1024 * 1024
BF16 = jnp.bfloat16
F32 = jnp.float32

FFN_TM, FFN_TF = 512, 512
PROJ_TM = 1024
PROJ_TN_CAP = 2560
PREP_TM = 512
MLA_TQ, MLA_TK = 512, 512
MLA_HEADS_PER_STEP = 8
NSA_TQ, NSA_TK = 256, 256
MERGE_TM = 512


def _bucket_of_distance():
    n = np.arange(LANE)
    max_exact = REL_BUCKETS // 2
    large = max_exact + (np.log(np.maximum(n, 1) / max_exact) / math.log(REL_MAX_DIST / max_exact)
                         * (REL_BUCKETS - max_exact)).astype(np.int32)
    bucket = np.where(n < max_exact, n, np.minimum(large, REL_BUCKETS - 1)).astype(np.int32)
    assert bucket[-1] == REL_BUCKETS - 1
    return bucket


BUCKET_OF_DIST = _bucket_of_distance()
FAR_DIST = int(np.max(np.nonzero(BUCKET_OF_DIST != REL_BUCKETS - 1)[0])) + 1


def _params(*sem):
    return pltpu.CompilerParams(dimension_semantics=sem, vmem_limit_bytes=VMEM_LIMIT)


def _rms(x, g):
    return x * lax.rsqrt(jnp.mean(x * x, axis=-1, keepdims=True) + EPS) * g


def _dot(a, b):
    return jnp.dot(a, b, preferred_element_type=F32)


def _dot_nt(a, b):
    return lax.dot_general(a, b, (((1,), (1,)), ((), ())), preferred_element_type=F32)


def _tile_n(n, cap):
    best = LANE
    for t in range(LANE, cap + 1, LANE):
        if n % t == 0:
            best = t
    return best


def _rmsnorm_kernel(x_ref, g_ref, o_ref):
    o_ref[...] = _rms(x_ref[...], g_ref[...]).astype(o_ref.dtype)


def _rmsnorm(x, g, layer):
    m, d = x.shape
    tm = 512
    return pl.pallas_call(
        _rmsnorm_kernel,
        grid=(m // tm,),
        in_specs=[pl.BlockSpec((tm, d), lambda i: (i, 0)),
                  pl.BlockSpec((None, 1, d), lambda i: (layer, 0, 0))],
        out_specs=pl.BlockSpec((tm, d), lambda i: (i, 0)),
        out_shape=jax.ShapeDtypeStruct((m, d), BF16),
        compiler_params=_params("parallel"),
        name="rmsnorm",
    )(x, g)


def _rope_table_kernel(pos_ref, inv_ref, o_ref):
    ang = pos_ref[...].astype(F32) * inv_ref[...]
    o_ref[...] = jnp.concatenate([jnp.cos(ang), jnp.sin(ang)], axis=1)


def _rope_table(pos_col, inv):
    m = pos_col.shape[0]
    tm = 512
    return pl.pallas_call(
        _rope_table_kernel,
        grid=(m // tm,),
        in_specs=[pl.BlockSpec((tm, 1), lambda i: (i, 0)),
                  pl.BlockSpec((1, MLA_ROPE), lambda i: (0, 0))],
        out_specs=pl.BlockSpec((tm, 2 * MLA_ROPE), lambda i: (i, 0)),
        out_shape=jax.ShapeDtypeStruct((m, 2 * MLA_ROPE), F32),
        compiler_params=_params("parallel"),
        name="rope_table",
    )(pos_col, inv)


def _ffn_kernel(x_ref, u_ref, post_g_ref, next_g_ref, wg_ref, wu_ref, wd_ref,
                o_ref, un_ref, acc_ref):
    j = pl.program_id(1)

    @pl.when(j == 0)
    def _():
        acc_ref[...] = jnp.zeros(acc_ref.shape, F32)

    u = u_ref[...]
    gate = _dot(u, wg_ref[...])
    up = _dot(u, wu_ref[...])
    hidden = (gate * jax.nn.sigmoid(gate) * up).astype(BF16)
    acc_ref[...] += _dot(hidden, wd_ref[...].astype(BF16))

    @pl.when(j == pl.num_programs(1) - 1)
    def _():
        out = x_ref[...] + _rms(acc_ref[...], post_g_ref[...])
        o_ref[...] = out
        un_ref[...] = _rms(out, next_g_ref[...]).astype(BF16)


def _ffn(x, u, post_g, next_g, next_layer, wg, wu, wd, layer):
    m, d = x.shape
    f = wg.shape[-1]
    tm, tf = FFN_TM, _tile_n(f, FFN_TF)
    return pl.pallas_call(
        _ffn_kernel,
        grid=(m // tm, f // tf),
        in_specs=[pl.BlockSpec((tm, d), lambda i, j: (i, 0)),
                  pl.BlockSpec((tm, d), lambda i, j: (i, 0)),
                  pl.BlockSpec((None, 1, d), lambda i, j: (layer, 0, 0)),
                  pl.BlockSpec((None, 1, d), lambda i, j: (next_layer, 0, 0)),
                  pl.BlockSpec((None, d, tf), lambda i, j: (layer, 0, j)),
                  pl.BlockSpec((None, d, tf), lambda i, j: (layer, 0, j)),
                  pl.BlockSpec((None, tf, d), lambda i, j: (layer, j, 0))],
        out_specs=[pl.BlockSpec((tm, d), lambda i, j: (i, 0)),
                   pl.BlockSpec((tm, d), lambda i, j: (i, 0))],
        out_shape=[jax.ShapeDtypeStruct((m, d), F32), jax.ShapeDtypeStruct((m, d), BF16)],
        scratch_shapes=[pltpu.VMEM((tm, d), F32)],
        compiler_params=_params("parallel", "arbitrary"),
        name="ffn",
    )(x, u, post_g, next_g, wg, wu, wd)


def _matmul_nt_kernel(x_ref, w_ref, o_ref):
    o_ref[...] = _dot_nt(x_ref[...], w_ref[...]).astype(o_ref.dtype)


def _in_proj(u, w_t, layer):
    m, d = u.shape
    n = w_t.shape[1]
    tm, tn = PROJ_TM, _tile_n(n, PROJ_TN_CAP)
    return pl.pallas_call(
        _matmul_nt_kernel,
        grid=(n // tn, m // tm),
        in_specs=[pl.BlockSpec((tm, d), lambda j, i: (i, 0)),
                  pl.BlockSpec((None, tn, d), lambda j, i: (layer, j, 0))],
        out_specs=pl.BlockSpec((tm, tn), lambda j, i: (i, j)),
        out_shape=jax.ShapeDtypeStruct((m, n), BF16),
        compiler_params=_params("parallel", "parallel"),
        name="in_proj",
    )(u, w_t)


def _matmul_t_kernel(w_ref, x_ref, o_ref):
    o_ref[...] = _dot_nt(w_ref[...], x_ref[...]).astype(o_ref.dtype)


def _in_proj_t(u, w_t, layer):
    m, d = u.shape
    n = w_t.shape[1]
    tm = PROJ_TM
    return pl.pallas_call(
        _matmul_t_kernel,
        grid=(m // tm,),
        in_specs=[pl.BlockSpec((None, n, d), lambda i: (layer, 0, 0)),
                  pl.BlockSpec((tm, d), lambda i: (i, 0))],
        out_specs=pl.BlockSpec((n, tm), lambda i: (0, i)),
        out_shape=jax.ShapeDtypeStruct((n, m), BF16),
        compiler_params=_params("parallel"),
        name="in_proj_t",
    )(w_t, u)


def _softmax_tile_t(s_ref, adds, masks, v_t, m_ref, acc_ref):
    probs, alphas = [], []
    for c in range(s_ref.shape[1] // LANE):
        cols = slice(c * LANE, (c + 1) * LANE)
        s = s_ref[:, cols]
        for mask in masks[c]:
            s = jnp.where(mask, s, MASKED) if mask.dtype == jnp.bool_ else s + mask
        m_old = m_ref[:, cols]
        m_tile = jnp.max(s, axis=0, keepdims=True)
        if adds[c] is not None:
            m_tile = m_tile + adds[c]
        m_new = jnp.maximum(m_old, m_tile)
        alpha = jnp.exp2(m_old - m_new)
        p = jnp.exp2(s - (m_new if adds[c] is None else m_new - adds[c]))
        m_ref[:, cols] = m_new
        probs.append(p.astype(BF16))
        alphas.append(alpha)
    p_t = probs[0] if len(probs) == 1 else jnp.concatenate(probs, axis=1)
    alpha = alphas[0] if len(alphas) == 1 else jnp.concatenate(alphas, axis=1)
    v_ones = jnp.concatenate([v_t, jnp.ones((ONES_ROWS, v_t.shape[1]), BF16)], axis=0)
    acc_ref[...] = alpha * acc_ref[...] + _dot(v_ones, p_t)


def _softmax_init(m_ref, acc_ref):
    m_ref[...] = jnp.full(m_ref.shape, NEG, F32)
    acc_ref[...] = jnp.zeros(acc_ref.shape, F32)


def _softmax_out(acc):
    dv = acc.shape[0] - ONES_ROWS
    return acc[:dv] / acc[dv:dv + 1]


def _mla_prep_kernel(cq_ref, ckv_ref, kr_ref, cs_ref, gq_ref, gkv_ref, wq_ref, wuk_ref, wuvt_ref,
                     q_ref, k_ref, vt_ref):
    cs = cs_ref[...]
    qn = _rms(cq_ref[...].astype(F32), gq_ref[...]).astype(BF16)
    q = _dot(qn, wq_ref[...])
    kvn = _rms(ckv_ref[...].astype(F32), gkv_ref[...]).astype(BF16)
    k_nope = _dot(kvn, wuk_ref[...])
    t = kr_ref[...].astype(F32) * cs
    k_pe = (t + pltpu.roll(t, MLA_ROPE, 1)).astype(BF16)
    for h in range(MLA_HEADS):
        lo = h * HEAD_PAD
        q_ref[:, lo:lo + MLA_NOPE] = q[:, lo:lo + MLA_NOPE].astype(BF16)
        q_ref[:, lo + MLA_NOPE:lo + HEAD_PAD] = (q[:, lo + MLA_NOPE:lo + HEAD_PAD] * cs).astype(BF16)
        k_ref[:, lo:lo + MLA_NOPE] = k_nope[:, h * MLA_NOPE:(h + 1) * MLA_NOPE].astype(BF16)
        k_ref[:, lo + MLA_NOPE:lo + HEAD_PAD] = k_pe
    vt_ref[...] = _dot_nt(wuvt_ref[...], kvn).astype(BF16)


def _mla_prep(z, cs, gq, gkv, wq, wuk, wuv_t, layer, zoff):
    m = z.shape[0]
    tm = PREP_TM
    hq = MLA_HEADS * HEAD_PAD
    hv = MLA_HEADS * MLA_V
    const = lambda i: (layer, 0, 0)
    return pl.pallas_call(
        _mla_prep_kernel,
        grid=(m // tm,),
        in_specs=[pl.BlockSpec((tm, MLA_Q_LORA), lambda i: (i, zoff["cq"] // MLA_Q_LORA)),
                  pl.BlockSpec((tm, MLA_KV_LORA), lambda i: (i, zoff["ckv"] // MLA_KV_LORA)),
                  pl.BlockSpec((tm, LANE), lambda i: (i, zoff["kr"] // LANE)),
                  pl.BlockSpec((tm, LANE), lambda i: (i, 0)),
                  pl.BlockSpec((None, 1, MLA_Q_LORA), const),
                  pl.BlockSpec((None, 1, MLA_KV_LORA), const),
                  pl.BlockSpec((None, MLA_Q_LORA, hq), const),
                  pl.BlockSpec((None, MLA_KV_LORA, MLA_HEADS * MLA_NOPE), const),
                  pl.BlockSpec((None, hv, MLA_KV_LORA), const)],
        out_specs=[pl.BlockSpec((tm, hq), lambda i: (i, 0)),
                   pl.BlockSpec((tm, hq), lambda i: (i, 0)),
                   pl.BlockSpec((hv, tm), lambda i: (0, i))],
        out_shape=[jax.ShapeDtypeStruct((m, hq), BF16),
                   jax.ShapeDtypeStruct((m, hq), BF16),
                   jax.ShapeDtypeStruct((hv, m), BF16)],
        compiler_params=_params("parallel"),
        name="mla_prep",
    )(z, z, z, cs, gq, gkv, wq, wuk, wuv_t)


def _by_parity(kt, fn):
    @pl.when(kt % 2 == 0)
    def _():
        fn(0)

    @pl.when(kt % 2 == 1)
    def _():
        fn(1)


def _mla_attn_kernel(q_ref, k_ref, vt_ref, o_ref, s_ref, m_ref, acc_ref, *, tq, tk, heads):
    i = pl.program_id(2)
    nc = tq // LANE
    _softmax_init(m_ref, acc_ref)
    qt = i * tq + lax.broadcasted_iota(jnp.int32, (1, tq), 1)
    kcol = lax.broadcasted_iota(jnp.int32, (tk, 1), 0)
    qk = [slice(h * HEAD_PAD, (h + 1) * HEAD_PAD) for h in range(heads)]
    vd = [slice(h * MLA_V, (h + 1) * MLA_V) for h in range(heads)]
    n_full = (i * tq) // tk
    n_all = (i * tq + tq + tk - 1) // tk

    def logits(kt, buf):
        ks = pl.multiple_of(kt * tk, tk)
        for h in range(heads):
            s_ref[buf, h] = _dot_nt(k_ref[pl.ds(ks, tk), qk[h]], q_ref[:, qk[h]])

    def step(kt, buf, masked, prefetch=True):
        ks = pl.multiple_of(kt * tk, tk)
        masks = [()] * nc
        if masked:
            masks = [(ks + kcol <= qt[:, c * LANE:(c + 1) * LANE],) for c in range(nc)]
        for h in range(heads):
            _softmax_tile_t(s_ref.at[buf, h], [None] * nc, masks, vt_ref[vd[h], pl.ds(ks, tk)],
                            m_ref.at[h], acc_ref.at[h])
        if prefetch:
            logits(kt + 1, 1 - buf)

    def full_body(kt, c):
        _by_parity(kt, lambda buf: step(kt, buf, False))
        return c

    def diag_body(kt, c):
        _by_parity(kt, lambda buf: step(kt, buf, True))
        return c

    logits(0, 0)
    lax.fori_loop(0, n_full, full_body, 0)
    lax.fori_loop(n_full, n_all - 1, diag_body, 0)
    _by_parity(n_all - 1, lambda buf: step(n_all - 1, buf, True, prefetch=False))
    for h in range(heads):
        o_ref[:, vd[h]] = _softmax_out(acc_ref[h]).T.astype(o_ref.dtype)


def _mla_attn(qp, kp, vt, batch, seq):
    tq, tk, heads = MLA_TQ, MLA_TK, MLA_HEADS_PER_STEP
    nq = seq // tq
    kern = functools.partial(_mla_attn_kernel, tq=tq, tk=tk, heads=heads)
    return pl.pallas_call(
        kern,
        grid=(batch, MLA_HEADS // heads, nq),
        in_specs=[pl.BlockSpec((tq, heads * HEAD_PAD), lambda b, h, i: (b * nq + i, h)),
                  pl.BlockSpec((seq, heads * HEAD_PAD), lambda b, h, i: (b, h)),
                  pl.BlockSpec((heads * MLA_V, seq), lambda b, h, i: (h, b))],
        out_specs=pl.BlockSpec((tq, heads * MLA_V), lambda b, h, i: (b * nq + i, h)),
        out_shape=jax.ShapeDtypeStruct((batch * seq, MLA_HEADS * MLA_V), BF16),
        scratch_shapes=[pltpu.VMEM((2, heads, tk, tq), F32), pltpu.VMEM((heads, 1, tq), F32),
                        pltpu.VMEM((heads, MLA_V + ONES_ROWS, tq), F32)],
        compiler_params=_params("parallel", "parallel", "parallel"),
        name="mla_attn",
    )(qp, kp, vt)


def _compress(z_ref, zf_ref, pe_ref, w1_ref, w2_ref):
    chunks, s, _ = zf_ref.shape
    for c in range(chunks):
        zf_ref[c] = z_ref[:, c * LANE:(c + 1) * LANE].astype(F32)
    d = chunks * LANE
    n = s // CMP_STRIDE
    first = jnp.zeros((n, CMP_HIDDEN), F32)
    second = jnp.zeros((n, CMP_HIDDEN), F32)
    for l in range(CMP_STRIDE):
        toks = [zf_ref[c, pl.ds(l, n, stride=CMP_STRIDE), :] for c in range(chunks)]
        tok = toks[0] if chunks == 1 else jnp.concatenate(toks, axis=1)
        lo, hi = l, CMP_STRIDE + l
        first += _dot((tok + pe_ref[lo:lo + 1, :]).astype(BF16), w1_ref[lo * d:(lo + 1) * d, :])
        second += _dot((tok + pe_ref[hi:hi + 1, :]).astype(BF16), w1_ref[hi * d:(hi + 1) * d, :])
    hidden = first + pltpu.roll(second, n - 1, 0)
    return _dot((hidden * jax.nn.sigmoid(hidden)).astype(BF16), w2_ref[...])


def _nsa_cmp_kernel(zk_ref, zv_ref, pek_ref, pev_ref, w1k_ref, w2k_ref, w1v_ref, w2v_ref,
                    kc_ref, vct_ref, kf_ref, vf_ref):
    kc_ref[...] = _compress(zk_ref, kf_ref, pek_ref, w1k_ref, w2k_ref).astype(BF16)
    vct_ref[...] = _compress(zv_ref, vf_ref, pev_ref, w1v_ref, w2v_ref).T.astype(BF16)


def _nsa_cmp(z, pek, pev, w1k, w2k, w1v, w2v, layer, batch, seq, zoff):
    G = NSA_KV_HEADS
    nh = seq // CMP_STRIDE
    const = lambda i: (layer, 0, 0)
    return pl.pallas_call(
        _nsa_cmp_kernel,
        grid=(batch * G,),
        in_specs=[pl.BlockSpec((seq, HEAD_PAD), lambda i: (i // G, zoff["kc"] // HEAD_PAD + i % G)),
                  pl.BlockSpec((seq, NSA_DV), lambda i: (i // G, zoff["vc"] // NSA_DV + i % G)),
                  pl.BlockSpec((None, CMP_LEN, HEAD_PAD), const),
                  pl.BlockSpec((None, CMP_LEN, NSA_DV), const),
                  pl.BlockSpec((None, CMP_LEN * HEAD_PAD, CMP_HIDDEN), const),
                  pl.BlockSpec((None, CMP_HIDDEN, HEAD_PAD), const),
                  pl.BlockSpec((None, CMP_LEN * NSA_DV, CMP_HIDDEN), const),
                  pl.BlockSpec((None, CMP_HIDDEN, NSA_DV), const)],
        out_specs=[pl.BlockSpec((None, nh, HEAD_PAD), lambda i: (i, 0, 0)),
                   pl.BlockSpec((None, NSA_DV, nh), lambda i: (i, 0, 0))],
        out_shape=[jax.ShapeDtypeStruct((batch * G, nh, HEAD_PAD), BF16),
                   jax.ShapeDtypeStruct((batch * G, NSA_DV, nh), BF16)],
        scratch_shapes=[pltpu.VMEM((HEAD_PAD // LANE, seq, LANE), F32),
                        pltpu.VMEM((NSA_DV // LANE, seq, LANE), F32)],
        compiler_params=_params("parallel"),
        name="nsa_cmp",
    )(z, z, pek, pev, w1k, w2k, w1v, w2v)


def _split3(x):
    a = x.astype(BF16)
    r = x - a.astype(F32)
    b = r.astype(BF16)
    c = (r - b.astype(F32)).astype(BF16)
    return a, b, c


def _lookup(table_row, idx):
    rows, width = idx.shape
    table = jnp.broadcast_to(table_row, (rows, LANE))
    chunks = [jnp.take_along_axis(table, idx[:, c:c + LANE], axis=1, mode="promise_in_bounds")
              for c in range(0, width, LANE)]
    return chunks[0] if len(chunks) == 1 else jnp.concatenate(chunks, axis=1)


def _nsa_attn_kernel(far_ref, q_ref, kc_ref, vct_ref, ks_ref, vst_ref, kw_ref, vwt_ref, ng_ref,
                     prow_ref, pcol_ref, pcmp_ref, tbl_ref, o_ref,
                     s_ref, m_ref, acc_ref, gate_ref, *, tq, tk, seq):
    b, i = pl.program_id(0), pl.program_id(2)
    nq, nk = seq // tq, seq // tk
    J = NSA_GROUP
    n_slc = seq // SLC_LEN
    n_cmp = (seq - CMP_LEN) // CMP_STRIDE + 1
    t0 = pl.multiple_of(i * tq, tq)

    qs = jnp.concatenate([q_ref[:, j * HEAD_PAD:(j + 1) * HEAD_PAD] for j in range(J)], axis=0)
    qpos = prow_ref[:, pl.ds(t0, tq)]
    qt = t0 + lax.broadcasted_iota(jnp.int32, (1, tq), 1)
    tbl = tbl_ref[...]
    far_bias = [tbl[j:j + 1, LANE - 1:LANE] for j in range(J)]
    head = [slice(j * tq, (j + 1) * tq) for j in range(J)]

    blk = lax.broadcasted_iota(jnp.int32, (LANE, 1), 0)
    keep_c = (blk * CMP_STRIDE + (CMP_LEN - 1) <= qt) & (blk < n_cmp)
    idx_c = jnp.clip(qpos - pcmp_ref[...], 0, LANE - 1)
    s_c = _dot_nt(kc_ref[...], qs)
    p_heads = []
    for j in range(J):
        s = jnp.where(keep_c, s_c[:, head[j]] + _lookup(tbl[j:j + 1, :], idx_c), NEG)
        e = jnp.where(keep_c, jnp.exp2(s - jnp.max(s, axis=0, keepdims=True)), 0.0)
        den = jnp.sum(e, axis=0, keepdims=True)
        p_heads.append(e / jnp.where(den > 0.0, den, 1.0))
    o_cmp = _dot(vct_ref[...], jnp.concatenate([p.astype(BF16) for p in p_heads], axis=1))

    rows = 32
    assert n_slc <= rows
    m_row = lax.broadcasted_iota(jnp.int32, (rows, LANE), 0)
    n_col = lax.broadcasted_iota(jnp.int32, (rows, LANE), 1)
    per = SLC_LEN // CMP_STRIDE
    back = (CMP_LEN - 1) // CMP_STRIDE
    overlap = ((n_col >= per * m_row - back) & (n_col <= per * m_row + per - 1)
               & (n_col < n_cmp) & (m_row < n_slc)).astype(BF16)
    imp = sum(_dot(overlap, part) for part in _split3(sum(p_heads)))
    m_blk = lax.broadcasted_iota(jnp.int32, (rows, 1), 0)
    cur = qt // SLC_LEN
    valid = m_blk <= cur
    forced = valid & ((m_blk == 0) | (m_blk >= cur - 1))
    score = jnp.where(forced, FORCED_SCORE, jnp.where(valid, imp, -1.0))
    rank = jnp.zeros((rows, tq), jnp.int32)
    for mp in range(n_slc):
        other = score[mp:mp + 1, :]
        ahead = (other > score) | ((other == score) & (m_blk > mp))
        rank = rank + ahead.astype(jnp.int32)
    selected = valid & (rank < min(SLC_TOPN, n_slc))
    sel_penalty = jnp.where(selected, 0.0, MASKED).astype(BF16)

    SLC, WIN = 0, 1
    _softmax_init(m_ref, acc_ref)
    sub = tk // LANE

    qchunks = [slice(c, c + LANE) for c in range(0, tq, LANE)]

    assert tq == tk

    def slc_masks(kidx, diagonal):
        expand = (kidx // SLC_LEN == lax.broadcasted_iota(jnp.int32, (1, rows), 1)).astype(BF16)
        penalty = _dot(expand, sel_penalty)
        if diagonal:
            return [(penalty[:, c], kidx <= qt[:, c]) for c in qchunks]
        return [(penalty[:, c],) for c in qchunks]

    def win_masks(kidx, diagonal):
        return [(jnp.where(lax.bitcast_convert_type(qt[:, c] - kidx, jnp.uint32) < WINDOW,
                           0.0, MASKED),) for c in qchunks]

    k_refs, vt_refs, masks_of = (ks_ref, kw_ref), (vst_ref, vwt_ref), (slc_masks, win_masks)
    adds = [far_bias[j] for j in range(J) for _ in qchunks]
    last = (t0 + tq + tk - 1) // tk
    first_win = jnp.maximum(t0 - (WINDOW - 1), 0) // tk

    def logits(kt, buf, slots):
        ks = pl.multiple_of(kt * tk, tk)
        for slot in slots:
            s_ref[buf, slot] = _dot_nt(k_refs[slot][pl.ds(ks, tk), :], qs)

    def near_bias(kt, buf, slots):
        ks = pl.multiple_of(kt * tk, tk)
        for hb in range(sub):
            kb = kt * sub + hb
            for qc, qcols in enumerate(qchunks):
                near = ((far_ref[((b * nq + i) * len(qchunks) + qc) * (nk * sub) + kb] == 0)
                        & (kb * LANE < t0 + (qc + 1) * LANE))

                @pl.when(near)
                def _():
                    kpos = pcol_ref[pl.ds(pl.multiple_of(ks + hb * LANE, LANE), LANE), :]
                    idx = jnp.clip(qpos[:, qcols] - kpos, 0, LANE - 1)
                    for j in range(J):
                        delta = _lookup(tbl[j:j + 1, :] - far_bias[j], idx)
                        cols = slice(j * tq + qc * LANE, j * tq + (qc + 1) * LANE)
                        for slot in slots:
                            s_ref[buf, slot, hb * LANE:(hb + 1) * LANE, cols] += delta

    def step(kt, buf, slots, final=False):
        ks = pl.multiple_of(kt * tk, tk)
        kidx = ks + lax.broadcasted_iota(jnp.int32, (tk, 1), 0)
        for slot in slots:
            _softmax_tile_t(s_ref.at[buf, slot], adds, masks_of[slot](kidx, final) * J,
                            vt_refs[slot][:, pl.ds(ks, tk)], m_ref.at[slot], acc_ref.at[slot])
        if not final:
            logits(kt + 1, 1 - buf, slots)
            near_bias(kt + 1, 1 - buf, slots)

    def slc_body(kt, c):
        _by_parity(kt, lambda buf: step(kt, buf, (SLC,)))
        return c

    def both_body(kt, c):
        _by_parity(kt, lambda buf: step(kt, buf, (SLC, WIN)))
        return c

    def enter_window(buf):
        logits(first_win, buf, (WIN,))
        near_bias(first_win, buf, (WIN,))

    logits(0, 0, (SLC,))
    near_bias(0, 0, (SLC,))
    lax.fori_loop(0, first_win, slc_body, 0)
    _by_parity(first_win, enter_window)
    lax.fori_loop(first_win, last - 1, both_body, 0)
    _by_parity(last - 1, lambda buf: step(last - 1, buf, (SLC, WIN), final=True))
    o_slc = _softmax_out(acc_ref[SLC])
    o_win = _softmax_out(acc_ref[WIN])

    gate_ref[...] = jax.nn.sigmoid(ng_ref[...].astype(F32)).T
    first_gate = pl.program_id(1) * (J * 3)
    for j in range(J):
        gate = [gate_ref[pl.ds(first_gate + 3 * j + c, 1), :] for c in range(3)]
        o = (gate[0] * o_cmp[:, head[j]] + gate[1] * o_slc[:, head[j]]
             + gate[2] * o_win[:, head[j]])
        o_ref[:, j * NSA_DV:(j + 1) * NSA_DV] = o.T.astype(o_ref.dtype)


def _nsa_attn(far, z, kc, vct, v_t, pos_row, pos_col, pos_cmp, tbl, batch, seq, zoff):
    tq, tk = NSA_TQ, NSA_TK
    nq = seq // tq
    G, J = NSA_KV_HEADS, NSA_GROUP
    R = J * tq
    kern = functools.partial(_nsa_attn_kernel, tq=tq, tk=tk, seq=seq)
    grid_spec = pltpu.PrefetchScalarGridSpec(
        num_scalar_prefetch=1,
        grid=(batch, G, nq),
        in_specs=[
            pl.BlockSpec((tq, J * HEAD_PAD), lambda b, g, i, far: (b * nq + i, g)),
            pl.BlockSpec((None, LANE, HEAD_PAD), lambda b, g, i, far: (b * G + g, 0, 0)),
            pl.BlockSpec((None, NSA_DV, LANE), lambda b, g, i, far: (b * G + g, 0, 0)),
            pl.BlockSpec((seq, HEAD_PAD), lambda b, g, i, far: (b, zoff["ks"] // HEAD_PAD + g)),
            pl.BlockSpec((NSA_DV, seq), lambda b, g, i, far: (g, b)),
            pl.BlockSpec((seq, HEAD_PAD), lambda b, g, i, far: (b, zoff["kw"] // HEAD_PAD + g)),
            pl.BlockSpec((NSA_DV, seq), lambda b, g, i, far: (G + g, b)),
            pl.BlockSpec((tq, LANE), lambda b, g, i, far: (b * nq + i, zoff["ng"] // LANE)),
            pl.BlockSpec((None, 1, seq), lambda b, g, i, far: (b, 0, 0)),
            pl.BlockSpec((seq, 1), lambda b, g, i, far: (b, 0)),
            pl.BlockSpec((LANE, 1), lambda b, g, i, far: (b, 0)),
            pl.BlockSpec((None, J, LANE), lambda b, g, i, far: (g, 0, 0)),
        ],
        out_specs=pl.BlockSpec((tq, J * NSA_DV), lambda b, g, i, far: (b * nq + i, g)),
        scratch_shapes=[pltpu.VMEM((2, 2, tk, R), F32), pltpu.VMEM((2, 1, R), F32),
                        pltpu.VMEM((2, NSA_DV + ONES_ROWS, R), F32),
                        pltpu.VMEM((LANE, tq), F32)],
    )
    return pl.pallas_call(
        kern,
        grid_spec=grid_spec,
        out_shape=jax.ShapeDtypeStruct((batch * seq, NSA_HEADS * NSA_DV), BF16),
        compiler_params=_params("parallel", "parallel", "parallel"),
        name="nsa_attn",
    )(far, z, kc, vct, z, v_t, z, v_t, z, pos_row, pos_col, pos_cmp, tbl)


def _merge_kernel(a_ref, b_ref, ga_ref, gb_ref, h_ref, wa_ref, wb_ref, wo_ref, post_g_ref,
                  next_g_ref, o_ref, un_ref):
    ya = _dot(a_ref[...], wa_ref[...])
    yb = _dot(b_ref[...], wb_ref[...])
    m = (jax.nn.sigmoid(ga_ref[...].astype(F32)) * ya
         + jax.nn.sigmoid(gb_ref[...].astype(F32)) * yb).astype(BF16)
    y = _dot(m, wo_ref[...])
    out = h_ref[...] + _rms(y, post_g_ref[...])
    o_ref[...] = out
    un_ref[...] = _rms(out, next_g_ref[...]).astype(BF16)


def _merge(a, bb, z, h, wa, wb, wo, post_g, next_g, layer, zoff):
    m, d = h.shape
    tm = MERGE_TM
    da, db = a.shape[1], bb.shape[1]
    gblk = zoff["mg"] // d
    const = lambda i: (layer, 0, 0)
    return pl.pallas_call(
        _merge_kernel,
        grid=(m // tm,),
        in_specs=[pl.BlockSpec((tm, da), lambda i: (i, 0)),
                  pl.BlockSpec((tm, db), lambda i: (i, 0)),
                  pl.BlockSpec((tm, d), lambda i: (i, gblk)),
                  pl.BlockSpec((tm, d), lambda i: (i, gblk + 1)),
                  pl.BlockSpec((tm, d), lambda i: (i, 0)),
                  pl.BlockSpec((None, da, d), const, pipeline_mode=pl.Buffered(1)),
                  pl.BlockSpec((None, db, d), const, pipeline_mode=pl.Buffered(1)),
                  pl.BlockSpec((None, d, d), const, pipeline_mode=pl.Buffered(1)),
                  pl.BlockSpec((None, 1, d), const),
                  pl.BlockSpec((None, 1, d), const)],
        out_specs=[pl.BlockSpec((tm, d), lambda i: (i, 0)),
                   pl.BlockSpec((tm, d), lambda i: (i, 0))],
        out_shape=[jax.ShapeDtypeStruct((m, d), F32), jax.ShapeDtypeStruct((m, d), BF16)],
        compiler_params=_params("parallel"),
        name="merge_out",
    )(a, bb, z, z, h, wa, wb, wo, post_g, next_g)


def _z_layout(d_model):
    G = NSA_KV_HEADS
    widths = [("q", NSA_HEADS * HEAD_PAD), ("mg", 2 * d_model), ("cq", MLA_Q_LORA),
              ("ckv", MLA_KV_LORA), ("kc", G * HEAD_PAD), ("ks", G * HEAD_PAD),
              ("kw", G * HEAD_PAD), ("vc", G * NSA_DV), ("kr", LANE), ("ng", LANE)]
    off, pos = {}, 0
    for name, w in widths:
        off[name] = pos
        pos += w
    off["total"] = pos
    assert off["q"] == 0 and off["mg"] % d_model == 0
    assert off["cq"] % MLA_Q_LORA == 0 and off["ckv"] % MLA_KV_LORA == 0
    assert all(off[k] % HEAD_PAD == 0 for k in ("kc", "ks", "kw"))
    return off


def _swap_halves(w):
    half = w.shape[-1] // 2
    return jnp.concatenate([-w[..., half:], w[..., :half]], axis=-1)


def _pad_last(w, width):
    return jnp.pad(w, [(0, 0)] * (w.ndim - 1) + [(0, width - w.shape[-1])])


def _w_in_pieces(d_model, zoff):
    G, J = NSA_KV_HEADS, NSA_GROUP
    splits = [MLA_Q_LORA, MLA_KV_LORA, MLA_ROPE, NSA_HEADS * NSA_DK,
              G * NSA_DK, G * NSA_DV, G * NSA_DK, G * NSA_DV, G * NSA_DK, G * NSA_DV,
              NSA_HEADS * 3, 2 * d_model]
    names = ["cq", "ckv", "kr", "q", "kc", "vc", "ks", "vs", "kw", "vw", "ng", "mg"]
    src = dict(zip(names, [0] + [int(v) for v in np.cumsum(splits)[:-1]]))
    half = MLA_ROPE // 2
    pieces = []
    for h in range(NSA_HEADS):
        pieces.append((zoff["q"] + h * HEAD_PAD, HEAD_PAD,
                       [(src["q"] + h * NSA_DK, NSA_DK, NSA_DK ** -0.5 * LOG2E)]))
    pieces.append((zoff["mg"], 2 * d_model, [(src["mg"], 2 * d_model, None)]))
    pieces.append((zoff["cq"], MLA_Q_LORA, [(src["cq"], MLA_Q_LORA, None)]))
    pieces.append((zoff["ckv"], MLA_KV_LORA, [(src["ckv"], MLA_KV_LORA, None)]))
    for name in ("kc", "ks", "kw"):
        for g in range(G):
            pieces.append((zoff[name] + g * HEAD_PAD, HEAD_PAD, [(src[name] + g * NSA_DK, NSA_DK, None)]))
    pieces.append((zoff["vc"], G * NSA_DV, [(src["vc"], G * NSA_DV, None)]))
    pieces.append((zoff["kr"], LANE, [(src["kr"], MLA_ROPE, None), (src["kr"] + half, half, -1.0),
                                      (src["kr"], half, None)]))
    pieces.append((zoff["ng"], LANE, [(src["ng"], NSA_HEADS * 3, None)]))
    return pieces, (src["vs"], src["vw"])


COPY_ROWS = 512


def _copy_rows(w_ref, o_ref, src, dst, n, scale):
    for off in range(0, n, COPY_ROWS):
        m = min(COPY_ROWS, n - off)
        x = w_ref[src + off:src + off + m, :]
        o_ref[dst + off:dst + off + m, :] = (x if scale is None else x * scale).astype(BF16)


def _w_in_layout_kernel(w_ref, o_ref, vt_ref, *, pieces, value_rows):
    tc = w_ref.shape[1]
    dst = 0
    for src, n in value_rows:
        _copy_rows(w_ref, vt_ref, src, dst, n, None)
        dst += n
    for dst, width, parts in pieces:
        used = sum(n for _, n, _ in parts)
        if all(n % ONES_ROWS == 0 for _, n, _ in parts):
            at = dst
            for src, n, scale in parts:
                _copy_rows(w_ref, o_ref, src, at, n, scale)
                at += n
            if used < width:
                o_ref[at:dst + width, :] = jnp.zeros((width - used, tc), BF16)
        else:
            rows = [w_ref[src:src + n, :] if scale is None else w_ref[src:src + n, :] * scale
                    for src, n, scale in parts]
            rows.append(jnp.zeros((width - used, tc), F32))
            o_ref[dst:dst + width, :] = jnp.concatenate(rows, axis=0).astype(BF16)


def _layout_w_in(w_in, zoff):
    L, D, n_in = w_in.shape
    G = NSA_KV_HEADS
    pieces, (vs, vw) = _w_in_pieces(D, zoff)
    tc = 256
    nv = 2 * G * NSA_DV
    value_rows = ((vs, G * NSA_DV), (vw, G * NSA_DV))
    return pl.pallas_call(
        functools.partial(_w_in_layout_kernel, pieces=pieces, value_rows=value_rows),
        grid=(L, D // tc),
        in_specs=[pl.BlockSpec((None, n_in, tc), lambda l, c: (l, 0, c))],
        out_specs=[pl.BlockSpec((None, zoff["total"], tc), lambda l, c: (l, 0, c)),
                   pl.BlockSpec((None, nv, tc), lambda l, c: (l, 0, c))],
        out_shape=[jax.ShapeDtypeStruct((L, zoff["total"], D), BF16),
                   jax.ShapeDtypeStruct((L, nv, D), BF16)],
        compiler_params=_params("parallel", "parallel"),
        name="w_in_layout",
    )(jnp.swapaxes(w_in, 1, 2))


def _layout_w_q_up(w):
    L, r, _ = w.shape
    w = w.reshape(L, r, MLA_HEADS, MLA_NOPE + MLA_ROPE) * ((MLA_NOPE + MLA_ROPE) ** -0.5 * LOG2E)
    rope = w[..., MLA_NOPE:]
    return jnp.concatenate([w[..., :MLA_NOPE], rope, _swap_halves(rope)], axis=-1).reshape(
        L, r, MLA_HEADS * HEAD_PAD).astype(BF16)


def kernel(x, positions, rel_bias, ffn1_pre_g, ffn1_post_g, ffn1_w_gate, ffn1_w_up, ffn1_w_down, mix_pre_g, mix_post_g, w_in, mla_q_norm_g, mla_w_q_up, mla_kv_norm_g, mla_w_uk, mla_w_uv, cmp_pe_k, cmp_w1_k, cmp_w2_k, cmp_pe_v, cmp_w1_v, cmp_w2_v, w_branch_mla, w_branch_nsa, w_out, ffn2_pre_g, ffn2_post_g, ffn2_w_gate, ffn2_w_up, ffn2_w_down):
    B, S, D = x.shape
    L = w_in.shape[0]
    M = B * S
    G, J = NSA_KV_HEADS, NSA_GROUP
    zoff = _z_layout(D)
    n_half = S // CMP_STRIDE
    n_cmp = (S - CMP_LEN) // CMP_STRIDE + 1
    assert n_half == LANE and n_cmp <= LANE

    gain = lambda g: g.reshape(L, 1, -1)
    bf = lambda w: w.astype(BF16)
    w_z, w_vt = _layout_w_in(w_in, zoff)
    w_q = _layout_w_q_up(mla_w_q_up)
    w_uk, w_uv_t = bf(mla_w_uk), bf(mla_w_uv).transpose(0, 2, 1)
    pe_k = _pad_last(cmp_pe_k, HEAD_PAD)
    w1_k = bf(jnp.pad(cmp_w1_k.reshape(L, CMP_LEN, NSA_DK, CMP_HIDDEN),
                      ((0, 0), (0, 0), (0, HEAD_PAD - NSA_DK), (0, 0)))
              ).reshape(L, CMP_LEN * HEAD_PAD, CMP_HIDDEN)
    w2_k = bf(_pad_last(cmp_w2_k, HEAD_PAD))
    f1 = (bf(ffn1_w_gate), bf(ffn1_w_up), ffn1_w_down)
    f2 = (bf(ffn2_w_gate), bf(ffn2_w_up), ffn2_w_down)
    w1_v, w2_v = bf(cmp_w1_v), bf(cmp_w2_v)
    w_a, w_b, w_o = bf(w_branch_mla), bf(w_branch_nsa), bf(w_out)
    g_f1pre, g_f1post, g_mpre, g_mpost = gain(ffn1_pre_g), gain(ffn1_post_g), gain(mix_pre_g), gain(mix_post_g)
    g_f2pre, g_f2post, g_q, g_kv = gain(ffn2_pre_g), gain(ffn2_post_g), gain(mla_q_norm_g), gain(mla_kv_norm_g)

    pos_col = positions.reshape(M, 1)
    pos_row = positions.reshape(B, 1, S)
    pos_cmp = _pad_last(positions[:, CMP_LEN - 1::CMP_STRIDE][:, :n_cmp], LANE).reshape(B * LANE, 1)
    q_min = positions.reshape(B, S // LANE, LANE).min(axis=-1)
    k_max = positions.reshape(B, S // LANE, LANE).max(axis=-1)
    far = (q_min[:, :, None] - k_max[:, None, :] >= FAR_DIST).astype(jnp.int32).reshape(-1)
    tbl = (jnp.take(rel_bias, jnp.asarray(BUCKET_OF_DIST), axis=0).T * LOG2E).reshape(G, J, LANE)
    half = MLA_ROPE // 2
    inv = ROPE_BASE ** (-jnp.arange(half, dtype=F32) * 2.0 / MLA_ROPE)
    cs = _rope_table(pos_col, jnp.concatenate([inv, inv]).reshape(1, MLA_ROPE))

    h = x.reshape(M, D)
    u = _rmsnorm(h, g_f1pre, 0)
    for l in range(L):
        h, u = _ffn(h, u, 0.5 * g_f1post, g_mpre, l, *f1, l)
        z = _in_proj(u, w_z, l)
        v_t = _in_proj_t(u, w_vt, l)
        qp, kp, vt = _mla_prep(z, cs, g_q, g_kv, w_q, w_uk, w_uv_t, l, zoff)
        a = _mla_attn(qp, kp, vt, B, S)

        kc, vct = _nsa_cmp(z, pe_k, cmp_pe_v, w1_k, w2_k, w1_v, w2_v, l, B, S, zoff)
        nsa = _nsa_attn(far, z, kc, vct, v_t, pos_row, pos_col, pos_cmp, tbl, B, S, zoff)
        h, u = _merge(a, nsa, z, h, w_a, w_b, w_o, g_mpost, g_f2pre, l, zoff)
        h, u = _ffn(h, u, 0.5 * g_f2post, g_f1pre, min(l + 1, L - 1), *f2, l)
    return h.reshape(B, S, D)
```

```python
import functools
import math

import numpy as np
import jax
import jax.numpy as jnp
from jax import lax
from jax.experimental import pallas as pl
from jax.experimental.pallas import tpu as pltpu

EPS = 1e-6
MLA_HEADS = 8
MLA_Q_LORA = 512
MLA_KV_LORA = 512
MLA_NOPE = 128
MLA_ROPE = 64
MLA_V = 128
ROPE_BASE = 10000.0
NSA_HEADS = 8
NSA_KV_HEADS = 2
NSA_GROUP = NSA_HEADS // NSA_KV_HEADS
NSA_DK = 192
NSA_DV = 128
CMP_LEN = 32
CMP_STRIDE = 16
CMP_HIDDEN = 256
SLC_LEN = 64
SLC_TOPN = 16
WINDOW = 512
FORCED_SCORE = 1e6
REL_BUCKETS = 32
REL_MAX_DIST = 128
NEG = -1e30
MASKED = 2 * NEG
LOG2E = 1.0 / math.log(2.0)

LANE = 128
ONES_ROWS = 16
HEAD_PAD = 256
VMEM_LIMIT = 56 * 1024 * 1024
BF16 = jnp.bfloat16
F32 = jnp.float32

FFN_TM, FFN_TF = 512, 512
PROJ_TM = 1024
PROJ_TN_CAP = 2560
PREP_TM = 512
MLA_TQ, MLA_TK = 512, 512
MLA_HEADS_PER_STEP = 8
NSA_TQ, NSA_TK = 256, 256
MERGE_TM = 512


def _bucket_of_distance():
    n = np.arange(LANE)
    max_exact = REL_BUCKETS // 2
    large = max_exact + (np.log(np.maximum(n, 1) / max_exact) / math.log(REL_MAX_DIST / max_exact)
                         * (REL_BUCKETS - max_exact)).astype(np.int32)
    bucket = np.where(n < max_exact, n, np.minimum(large, REL_BUCKETS - 1)).astype(np.int32)
    assert bucket[-1] == REL_BUCKETS - 1
    return bucket


BUCKET_OF_DIST = _bucket_of_distance()
FAR_DIST = int(np.max(np.nonzero(BUCKET_OF_DIST != REL_BUCKETS - 1)[0])) + 1


def _params(*sem):
    return pltpu.CompilerParams(dimension_semantics=sem, vmem_limit_bytes=VMEM_LIMIT)


def _rms(x, g):
    return x * lax.rsqrt(jnp.mean(x * x, axis=-1, keepdims=True) + EPS) * g


def _dot(a, b):
    return jnp.dot(a, b, preferred_element_type=F32)


def _dot_nt(a, b):
    return lax.dot_general(a, b, (((1,), (1,)), ((), ())), preferred_element_type=F32)


def _tile_n(n, cap):
    best = LANE
    for t in range(LANE, cap + 1, LANE):
        if n % t == 0:
            best = t
    return best


def _rmsnorm_kernel(x_ref, g_ref, o_ref):
    o_ref[...] = _rms(x_ref[...], g_ref[...]).astype(o_ref.dtype)


def _rmsnorm(x, g, layer):
    m, d = x.shape
    tm = 512
    return pl.pallas_call(
        _rmsnorm_kernel,
        grid=(m // tm,),
        in_specs=[pl.BlockSpec((tm, d), lambda i: (i, 0)),
                  pl.BlockSpec((None, 1, d), lambda i: (layer, 0, 0))],
        out_specs=pl.BlockSpec((tm, d), lambda i: (i, 0)),
        out_shape=jax.ShapeDtypeStruct((m, d), BF16),
        compiler_params=_params("parallel"),
        name="rmsnorm",
    )(x, g)


def _rope_table_kernel(pos_ref, inv_ref, o_ref):
    ang = pos_ref[...].astype(F32) * inv_ref[...]
    o_ref[...] = jnp.concatenate([jnp.cos(ang), jnp.sin(ang)], axis=1)


def _rope_table(pos_col, inv):
    m = pos_col.shape[0]
    tm = 512
    return pl.pallas_call(
        _rope_table_kernel,
        grid=(m // tm,),
        in_specs=[pl.BlockSpec((tm, 1), lambda i: (i, 0)),
                  pl.BlockSpec((1, MLA_ROPE), lambda i: (0, 0))],
        out_specs=pl.BlockSpec((tm, 2 * MLA_ROPE), lambda i: (i, 0)),
        out_shape=jax.ShapeDtypeStruct((m, 2 * MLA_ROPE), F32),
        compiler_params=_params("parallel"),
        name="rope_table",
    )(pos_col, inv)


def _ffn_kernel(x_ref, u_ref, post_g_ref, next_g_ref, wg_ref, wu_ref, wd_ref,
                o_ref, un_ref, acc_ref):
    j = pl.program_id(1)

    @pl.when(j == 0)
    def _():
        acc_ref[...] = jnp.zeros(acc_ref.shape, F32)

    u = u_ref[...]
    gate = _dot(u, wg_ref[...])
    up = _dot(u, wu_ref[...])
    hidden = (gate * jax.nn.sigmoid(gate) * up).astype(BF16)
    acc_ref[...] += _dot(hidden, wd_ref[...].astype(BF16))

    @pl.when(j == pl.num_programs(1) - 1)
    def _():
        out = x_ref[...] + _rms(acc_ref[...], post_g_ref[...])
        o_ref[...] = out
        un_ref[...] = _rms(out, next_g_ref[...]).astype(BF16)


def _ffn(x, u, post_g, next_g, next_layer, wg, wu, wd, layer):
    m, d = x.shape
    f = wg.shape[-1]
    tm, tf = FFN_TM, _tile_n(f, FFN_TF)
    return pl.pallas_call(
        _ffn_kernel,
        grid=(m // tm, f // tf),
        in_specs=[pl.BlockSpec((tm, d), lambda i, j: (i, 0)),
                  pl.BlockSpec((tm, d), lambda i, j: (i, 0)),
                  pl.BlockSpec((None, 1, d), lambda i, j: (layer, 0, 0)),
                  pl.BlockSpec((None, 1, d), lambda i, j: (next_layer, 0, 0)),
                  pl.BlockSpec((None, d, tf), lambda i, j: (layer, 0, j)),
                  pl.BlockSpec((None, d, tf), lambda i, j: (layer, 0, j)),
                  pl.BlockSpec((None, tf, d), lambda i, j: (layer, j, 0))],
        out_specs=[pl.BlockSpec((tm, d), lambda i, j: (i, 0)),
                   pl.BlockSpec((tm, d), lambda i, j: (i, 0))],
        out_shape=[jax.ShapeDtypeStruct((m, d), F32), jax.ShapeDtypeStruct((m, d), BF16)],
        scratch_shapes=[pltpu.VMEM((tm, d), F32)],
        compiler_params=_params("parallel", "arbitrary"),
        name="ffn",
    )(x, u, post_g, next_g, wg, wu, wd)


def _matmul_nt_kernel(x_ref, w_ref, o_ref):
    o_ref[...] = _dot_nt(x_ref[...], w_ref[...]).astype(o_ref.dtype)


def _in_proj(u, w_t, layer):
    m, d = u.shape
    n = w_t.shape[1]
    tm, tn = PROJ_TM, _tile_n(n, PROJ_TN_CAP)
    return pl.pallas_call(
        _matmul_nt_kernel,
        grid=(n // tn, m // tm),
        in_specs=[pl.BlockSpec((tm, d), lambda j, i: (i, 0)),
                  pl.BlockSpec((None, tn, d), lambda j, i: (layer, j, 0))],
        out_specs=pl.BlockSpec((tm, tn), lambda j, i: (i, j)),
        out_shape=jax.ShapeDtypeStruct((m, n), BF16),
        compiler_params=_params("parallel", "parallel"),
        name="in_proj",
    )(u, w_t)


def _matmul_t_kernel(w_ref, x_ref, o_ref):
    o_ref[...] = _dot_nt(w_ref[...], x_ref[...]).astype(o_ref.dtype)


def _in_proj_t(u, w_t, layer):
    m, d = u.shape
    n = w_t.shape[1]
    tm = PROJ_TM
    return pl.pallas_call(
        _matmul_t_kernel,
        grid=(m // tm,),
        in_specs=[pl.BlockSpec((None, n, d), lambda i: (layer, 0, 0)),
                  pl.BlockSpec((tm, d), lambda i: (i, 0))],
        out_specs=pl.BlockSpec((n, tm), lambda i: (0, i)),
        out_shape=jax.ShapeDtypeStruct((n, m), BF16),
        compiler_params=_params("parallel"),
        name="in_proj_t",
    )(w_t, u)


def _softmax_tile_t(s_ref, adds, masks, v_t, m_ref, acc_ref, interleave=()):
    probs, alphas = [], []
    n_chunks = s_ref.shape[1] // LANE
    for c in range(n_chunks):
        if interleave and c % (n_chunks // len(interleave)) == 0:
            interleave[c // (n_chunks // len(interleave))]()
        cols = slice(c * LANE, (c + 1) * LANE)
        s = s_ref[:, cols]
        for mask in masks[c]:
            s = jnp.where(mask, s, MASKED) if mask.dtype == jnp.bool_ else s + mask
        m_old = m_ref[:, cols]
        m_tile = jnp.max(s, axis=0, keepdims=True)
        if adds[c] is not None:
            m_tile = m_tile + adds[c]
        m_new = jnp.maximum(m_old, m_tile)
        alpha = jnp.exp2(m_old - m_new)
        p = jnp.exp2(s - (m_new if adds[c] is None else m_new - adds[c]))
        m_ref[:, cols] = m_new
        probs.append(p.astype(BF16))
        alphas.append(alpha)
    p_t = probs[0] if len(probs) == 1 else jnp.concatenate(probs, axis=1)
    alpha = alphas[0] if len(alphas) == 1 else jnp.concatenate(alphas, axis=1)
    v_ones = jnp.concatenate([v_t, jnp.ones((ONES_ROWS, v_t.shape[1]), BF16)], axis=0)
    acc_ref[...] = alpha * acc_ref[...] + _dot(v_ones, p_t)


def _softmax_init(m_ref, acc_ref):
    m_ref[...] = jnp.full(m_ref.shape, NEG, F32)
    acc_ref[...] = jnp.zeros(acc_ref.shape, F32)


def _softmax_out(acc):
    dv = acc.shape[0] - ONES_ROWS
    return acc[:dv] / acc[dv:dv + 1]


def _mla_prep_kernel(cq_ref, ckv_ref, kr_ref, cs_ref, gq_ref, gkv_ref, wq_ref, wuk_ref, wuvt_ref,
                     q_ref, k_ref, vt_ref):
    cs = cs_ref[...]
    qn = _rms(cq_ref[...].astype(F32), gq_ref[...]).astype(BF16)
    q = _dot(qn, wq_ref[...])
    kvn = _rms(ckv_ref[...].astype(F32), gkv_ref[...]).astype(BF16)
    k_nope = _dot(kvn, wuk_ref[...])
    t = kr_ref[...].astype(F32) * cs
    k_pe = (t + pltpu.roll(t, MLA_ROPE, 1)).astype(BF16)
    for h in range(MLA_HEADS):
        lo = h * HEAD_PAD
        q_ref[:, lo:lo + MLA_NOPE] = q[:, lo:lo + MLA_NOPE].astype(BF16)
        q_ref[:, lo + MLA_NOPE:lo + HEAD_PAD] = (q[:, lo + MLA_NOPE:lo + HEAD_PAD] * cs).astype(BF16)
        k_ref[:, lo:lo + MLA_NOPE] = k_nope[:, h * MLA_NOPE:(h + 1) * MLA_NOPE].astype(BF16)
        k_ref[:, lo + MLA_NOPE:lo + HEAD_PAD] = k_pe
    vt_ref[...] = _dot_nt(wuvt_ref[...], kvn).astype(BF16)


def _mla_prep(z, cs, gq, gkv, wq, wuk, wuv_t, layer, zoff):
    m = z.shape[0]
    tm = PREP_TM
    hq = MLA_HEADS * HEAD_PAD
    hv = MLA_HEADS * MLA_V
    const = lambda i: (layer, 0, 0)
    return pl.pallas_call(
        _mla_prep_kernel,
        grid=(m // tm,),
        in_specs=[pl.BlockSpec((tm, MLA_Q_LORA), lambda i: (i, zoff["cq"] // MLA_Q_LORA)),
                  pl.BlockSpec((tm, MLA_KV_LORA), lambda i: (i, zoff["ckv"] // MLA_KV_LORA)),
                  pl.BlockSpec((tm, LANE), lambda i: (i, zoff["kr"] // LANE)),
                  pl.BlockSpec((tm, LANE), lambda i: (i, 0)),
                  pl.BlockSpec((None, 1, MLA_Q_LORA), const),
                  pl.BlockSpec((None, 1, MLA_KV_LORA), const),
                  pl.BlockSpec((None, MLA_Q_LORA, hq), const),
                  pl.BlockSpec((None, MLA_KV_LORA, MLA_HEADS * MLA_NOPE), const),
                  pl.BlockSpec((None, hv, MLA_KV_LORA), const)],
        out_specs=[pl.BlockSpec((tm, hq), lambda i: (i, 0)),
                   pl.BlockSpec((tm, hq), lambda i: (i, 0)),
                   pl.BlockSpec((hv, tm), lambda i: (0, i))],
        out_shape=[jax.ShapeDtypeStruct((m, hq), BF16),
                   jax.ShapeDtypeStruct((m, hq), BF16),
                   jax.ShapeDtypeStruct((hv, m), BF16)],
        compiler_params=_params("parallel"),
        name="mla_prep",
    )(z, z, z, cs, gq, gkv, wq, wuk, wuv_t)


def _by_parity(kt, fn):
    @pl.when(kt % 2 == 0)
    def _():
        fn(0)

    @pl.when(kt % 2 == 1)
    def _():
        fn(1)


def _mla_attn_kernel(q_ref, k_ref, vt_ref, o_ref, s_ref, m_ref, acc_ref, *, tq, tk, heads):
    i = pl.program_id(2)
    nc = tq // LANE
    _softmax_init(m_ref, acc_ref)
    qt = i * tq + lax.broadcasted_iota(jnp.int32, (1, tq), 1)
    kcol = lax.broadcasted_iota(jnp.int32, (tk, 1), 0)
    qk = [slice(h * HEAD_PAD, (h + 1) * HEAD_PAD) for h in range(heads)]
    vd = [slice(h * MLA_V, (h + 1) * MLA_V) for h in range(heads)]
    n_full = (i * tq) // tk
    n_all = (i * tq + tq + tk - 1) // tk

    def head_logits(kt, buf, h):
        ks = pl.multiple_of(kt * tk, tk)
        s_ref[buf, h] = _dot_nt(k_ref[pl.ds(ks, tk), qk[h]], q_ref[:, qk[h]])

    def logits(kt, buf):
        for h in range(heads):
            head_logits(kt, buf, h)

    def step(kt, buf, masked, prefetch=True):
        ks = pl.multiple_of(kt * tk, tk)
        masks = [()] * nc
        if masked:
            masks = [(ks + kcol <= qt[:, c * LANE:(c + 1) * LANE],) for c in range(nc)]
        for h in range(heads):
            if prefetch:
                head_logits(kt + 1, 1 - buf, h)
            _softmax_tile_t(s_ref.at[buf, h], [None] * nc, masks, vt_ref[vd[h], pl.ds(ks, tk)],
                            m_ref.at[h], acc_ref.at[h])

    def full_body(kt, c):
        _by_parity(kt, lambda buf: step(kt, buf, False))
        return c

    def diag_body(kt, c):
        _by_parity(kt, lambda buf: step(kt, buf, True))
        return c

    logits(0, 0)
    lax.fori_loop(0, n_full, full_body, 0)
    lax.fori_loop(n_full, n_all - 1, diag_body, 0)
    _by_parity(n_all - 1, lambda buf: step(n_all - 1, buf, True, prefetch=False))
    for h in range(heads):
        o_ref[:, vd[h]] = _softmax_out(acc_ref[h]).T.astype(o_ref.dtype)


def _mla_attn(qp, kp, vt, batch, seq):
    tq, tk, heads = MLA_TQ, MLA_TK, MLA_HEADS_PER_STEP
    nq = seq // tq
    kern = functools.partial(_mla_attn_kernel, tq=tq, tk=tk, heads=heads)
    return pl.pallas_call(
        kern,
        grid=(batch, MLA_HEADS // heads, nq),
        in_specs=[pl.BlockSpec((tq, heads * HEAD_PAD), lambda b, h, i: (b * nq + i, h)),
                  pl.BlockSpec((seq, heads * HEAD_PAD), lambda b, h, i: (b, h)),
                  pl.BlockSpec((heads * MLA_V, seq), lambda b, h, i: (h, b))],
        out_specs=pl.BlockSpec((tq, heads * MLA_V), lambda b, h, i: (b * nq + i, h)),
        out_shape=jax.ShapeDtypeStruct((batch * seq, MLA_HEADS * MLA_V), BF16),
        scratch_shapes=[pltpu.VMEM((2, heads, tk, tq), F32), pltpu.VMEM((heads, 1, tq), F32),
                        pltpu.VMEM((heads, MLA_V + ONES_ROWS, tq), F32)],
        compiler_params=_params("parallel", "parallel", "parallel"),
        name="mla_attn",
    )(qp, kp, vt)


def _compress(z_ref, zf_ref, pe_ref, w1_ref, w2_ref):
    chunks, s, _ = zf_ref.shape
    for c in range(chunks):
        zf_ref[c] = z_ref[:, c * LANE:(c + 1) * LANE].astype(F32)
    d = chunks * LANE
    n = s // CMP_STRIDE
    first = jnp.zeros((n, CMP_HIDDEN), F32)
    second = jnp.zeros((n, CMP_HIDDEN), F32)
    for l in range(CMP_STRIDE):
        toks = [zf_ref[c, pl.ds(l, n, stride=CMP_STRIDE), :] for c in range(chunks)]
        tok = toks[0] if chunks == 1 else jnp.concatenate(toks, axis=1)
        lo, hi = l, CMP_STRIDE + l
        first += _dot((tok + pe_ref[lo:lo + 1, :]).astype(BF16), w1_ref[lo * d:(lo + 1) * d, :])
        second += _dot((tok + pe_ref[hi:hi + 1, :]).astype(BF16), w1_ref[hi * d:(hi + 1) * d, :])
    hidden = first + pltpu.roll(second, n - 1, 0)
    return _dot((hidden * jax.nn.sigmoid(hidden)).astype(BF16), w2_ref[...])


def _nsa_cmp_kernel(zk_ref, zv_ref, pek_ref, pev_ref, w1k_ref, w2k_ref, w1v_ref, w2v_ref,
                    kc_ref, vct_ref, kf_ref, vf_ref):
    kc_ref[...] = _compress(zk_ref, kf_ref, pek_ref, w1k_ref, w2k_ref).astype(BF16)
    vct_ref[...] = _compress(zv_ref, vf_ref, pev_ref, w1v_ref, w2v_ref).T.astype(BF16)


def _nsa_cmp(z, pek, pev, w1k, w2k, w1v, w2v, layer, batch, seq, zoff):
    G = NSA_KV_HEADS
    nh = seq // CMP_STRIDE
    const = lambda i: (layer, 0, 0)
    return pl.pallas_call(
        _nsa_cmp_kernel,
        grid=(batch * G,),
        in_specs=[pl.BlockSpec((seq, HEAD_PAD), lambda i: (i // G, zoff["kc"] // HEAD_PAD + i % G)),
                  pl.BlockSpec((seq, NSA_DV), lambda i: (i // G, zoff["vc"] // NSA_DV + i % G)),
                  pl.BlockSpec((None, CMP_LEN, HEAD_PAD), const),
                  pl.BlockSpec((None, CMP_LEN, NSA_DV), const),
                  pl.BlockSpec((None, CMP_LEN * HEAD_PAD, CMP_HIDDEN), const),
                  pl.BlockSpec((None, CMP_HIDDEN, HEAD_PAD), const),
                  pl.BlockSpec((None, CMP_LEN * NSA_DV, CMP_HIDDEN), const),
                  pl.BlockSpec((None, CMP_HIDDEN, NSA_DV), const)],
        out_specs=[pl.BlockSpec((None, nh, HEAD_PAD), lambda i: (i, 0, 0)),
                   pl.BlockSpec((None, NSA_DV, nh), lambda i: (i, 0, 0))],
        out_shape=[jax.ShapeDtypeStruct((batch * G, nh, HEAD_PAD), BF16),
                   jax.ShapeDtypeStruct((batch * G, NSA_DV, nh), BF16)],
        scratch_shapes=[pltpu.VMEM((HEAD_PAD // LANE, seq, LANE), F32),
                        pltpu.VMEM((NSA_DV // LANE, seq, LANE), F32)],
        compiler_params=_params("parallel"),
        name="nsa_cmp",
    )(z, z, pek, pev, w1k, w2k, w1v, w2v)


def _split3(x):
    a = x.astype(BF16)
    r = x - a.astype(F32)
    b = r.astype(BF16)
    c = (r - b.astype(F32)).astype(BF16)
    return a, b, c


def _lookup(table_row, idx):
    rows, width = idx.shape
    table = jnp.broadcast_to(table_row, (rows, LANE))
    chunks = [jnp.take_along_axis(table, idx[:, c:c + LANE], axis=1, mode="promise_in_bounds")
              for c in range(0, width, LANE)]
    return chunks[0] if len(chunks) == 1 else jnp.concatenate(chunks, axis=1)


def _nsa_attn_kernel(far_ref, q_ref, kc_ref, vct_ref, ks_ref, vst_ref, kw_ref, vwt_ref, ng_ref,
                     prow_ref, pcol_ref, pcmp_ref, tbl_ref, o_ref,
                     s_ref, m_ref, acc_ref, gate_ref, *, tq, tk, seq):
    b, i = pl.program_id(0), pl.program_id(2)
    nq, nk = seq // tq, seq // tk
    J = NSA_GROUP
    n_slc = seq // SLC_LEN
    n_cmp = (seq - CMP_LEN) // CMP_STRIDE + 1
    t0 = pl.multiple_of(i * tq, tq)

    qs = jnp.concatenate([q_ref[:, j * HEAD_PAD:(j + 1) * HEAD_PAD] for j in range(J)], axis=0)
    qpos = prow_ref[:, pl.ds(t0, tq)]
    qt = t0 + lax.broadcasted_iota(jnp.int32, (1, tq), 1)
    tbl = tbl_ref[...]
    far_bias = [tbl[j:j + 1, LANE - 1:LANE] for j in range(J)]
    head = [slice(j * tq, (j + 1) * tq) for j in range(J)]

    blk = lax.broadcasted_iota(jnp.int32, (LANE, 1), 0)
    keep_c = (blk * CMP_STRIDE + (CMP_LEN - 1) <= qt) & (blk < n_cmp)
    idx_c = jnp.clip(qpos - pcmp_ref[...], 0, LANE - 1)
    s_c = _dot_nt(kc_ref[...], qs)
    p_heads = []
    for j in range(J):
        s = jnp.where(keep_c, s_c[:, head[j]] + _lookup(tbl[j:j + 1, :], idx_c), NEG)
        e = jnp.where(keep_c, jnp.exp2(s - jnp.max(s, axis=0, keepdims=True)), 0.0)
        den = jnp.sum(e, axis=0, keepdims=True)
        p_heads.append(e / jnp.where(den > 0.0, den, 1.0))
    o_cmp = _dot(vct_ref[...], jnp.concatenate([p.astype(BF16) for p in p_heads], axis=1))

    rows = 32
    assert n_slc <= rows
    m_row = lax.broadcasted_iota(jnp.int32, (rows, LANE), 0)
    n_col = lax.broadcasted_iota(jnp.int32, (rows, LANE), 1)
    per = SLC_LEN // CMP_STRIDE
    back = (CMP_LEN - 1) // CMP_STRIDE
    overlap = ((n_col >= per * m_row - back) & (n_col <= per * m_row + per - 1)
               & (n_col < n_cmp) & (m_row < n_slc)).astype(BF16)
    imp = sum(_dot(overlap, part) for part in _split3(sum(p_heads)))
    m_blk = lax.broadcasted_iota(jnp.int32, (rows, 1), 0)
    cur = qt // SLC_LEN
    valid = m_blk <= cur
    forced = valid & ((m_blk == 0) | (m_blk >= cur - 1))
    score = jnp.where(forced, FORCED_SCORE, jnp.where(valid, imp, -1.0))
    rank = jnp.zeros((rows, tq), jnp.int32)
    for mp in range(n_slc):
        other = score[mp:mp + 1, :]
        ahead = (other > score) | ((other == score) & (m_blk > mp))
        rank = rank + ahead.astype(jnp.int32)
    selected = valid & (rank < min(SLC_TOPN, n_slc))
    sel_penalty = jnp.where(selected, 0.0, MASKED).astype(BF16)

    SLC, WIN = 0, 1
    _softmax_init(m_ref, acc_ref)
    sub = tk // LANE

    qchunks = [slice(c, c + LANE) for c in range(0, tq, LANE)]

    assert tq == tk

    def slc_masks(kidx, diagonal):
        expand = (kidx // SLC_LEN == lax.broadcasted_iota(jnp.int32, (1, rows), 1)).astype(BF16)
        penalty = _dot(expand, sel_penalty)
        if diagonal:
            return [(penalty[:, c], kidx <= qt[:, c]) for c in qchunks]
        return [(penalty[:, c],) for c in qchunks]

    def win_masks(kidx, diagonal):
        return [(jnp.where(lax.bitcast_convert_type(qt[:, c] - kidx, jnp.uint32) < WINDOW,
                           0.0, MASKED),) for c in qchunks]

    k_refs, vt_refs, masks_of = (ks_ref, kw_ref), (vst_ref, vwt_ref), (slc_masks, win_masks)
    adds = [far_bias[j] for j in range(J) for _ in qchunks]
    last = (t0 + tq + tk - 1) // tk
    first_win = jnp.maximum(t0 - (WINDOW - 1), 0) // tk

    def head_logits(kt, buf, slots, j):
        ks = pl.multiple_of(kt * tk, tk)
        keys = [k_refs[slot][pl.ds(ks, tk), :] for slot in slots]
        s = _dot_nt(keys[0] if len(keys) == 1 else jnp.concatenate(keys, axis=0), qs[head[j]])
        for n, slot in enumerate(slots):
            s_ref[buf, slot, :, head[j]] = s[n * tk:(n + 1) * tk]

    def logits(kt, buf, slots):
        for j in range(J):
            head_logits(kt, buf, slots, j)

    def near_bias(kt, buf, slots):
        ks = pl.multiple_of(kt * tk, tk)
        for hb in range(sub):
            kb = kt * sub + hb
            for qc, qcols in enumerate(qchunks):
                near = ((far_ref[((b * nq + i) * len(qchunks) + qc) * (nk * sub) + kb] == 0)
                        & (kb * LANE < t0 + (qc + 1) * LANE))

                @pl.when(near)
                def _():
                    kpos = pcol_ref[pl.ds(pl.multiple_of(ks + hb * LANE, LANE), LANE), :]
                    idx = jnp.clip(qpos[:, qcols] - kpos, 0, LANE - 1)
                    for j in range(J):
                        delta = _lookup(tbl[j:j + 1, :] - far_bias[j], idx)
                        cols = slice(j * tq + qc * LANE, j * tq + (qc + 1) * LANE)
                        for slot in slots:
                            s_ref[buf, slot, hb * LANE:(hb + 1) * LANE, cols] += delta

    def step(kt, buf, slots, final=False):
        ahead = [] if final else [functools.partial(head_logits, kt + 1, 1 - buf, slots, j)
                                  for j in range(J)]
        share = len(ahead) // len(slots)
        ks = pl.multiple_of(kt * tk, tk)
        kidx = ks + lax.broadcasted_iota(jnp.int32, (tk, 1), 0)
        for n, slot in enumerate(slots):
            _softmax_tile_t(s_ref.at[buf, slot], adds, masks_of[slot](kidx, final) * J,
                            vt_refs[slot][:, pl.ds(ks, tk)], m_ref.at[slot], acc_ref.at[slot],
                            interleave=ahead[n * share:(n + 1) * share])
        if not final:
            near_bias(kt + 1, 1 - buf, slots)

    def slc_body(kt, c):
        _by_parity(kt, lambda buf: step(kt, buf, (SLC,)))
        return c

    def both_body(kt, c):
        _by_parity(kt, lambda buf: step(kt, buf, (SLC, WIN)))
        return c

    def enter_window(buf):
        logits(first_win, buf, (WIN,))
        near_bias(first_win, buf, (WIN,))

    logits(0, 0, (SLC,))
    near_bias(0, 0, (SLC,))
    lax.fori_loop(0, first_win, slc_body, 0)
    _by_parity(first_win, enter_window)
    lax.fori_loop(first_win, last - 1, both_body, 0)
    _by_parity(last - 1, lambda buf: step(last - 1, buf, (SLC, WIN), final=True))
    o_slc = _softmax_out(acc_ref[SLC])
    o_win = _softmax_out(acc_ref[WIN])

    gate_ref[...] = jax.nn.sigmoid(ng_ref[...].astype(F32)).T
    first_gate = pl.program_id(1) * (J * 3)
    for j in range(J):
        gate = [gate_ref[pl.ds(first_gate + 3 * j + c, 1), :] for c in range(3)]
        o = (gate[0] * o_cmp[:, head[j]] + gate[1] * o_slc[:, head[j]]
             + gate[2] * o_win[:, head[j]])
        o_ref[:, j * NSA_DV:(j + 1) * NSA_DV] = o.T.astype(o_ref.dtype)


def _nsa_attn(far, z, kc, vct, v_t, pos_row, pos_col, pos_cmp, tbl, batch, seq, zoff):
    tq, tk = NSA_TQ, NSA_TK
    nq = seq // tq
    G, J = NSA_KV_HEADS, NSA_GROUP
    R = J * tq
    kern = functools.partial(_nsa_attn_kernel, tq=tq, tk=tk, seq=seq)
    grid_spec = pltpu.PrefetchScalarGridSpec(
        num_scalar_prefetch=1,
        grid=(batch, G, nq),
        in_specs=[
            pl.BlockSpec((tq, J * HEAD_PAD), lambda b, g, i, far: (b * nq + i, g)),
            pl.BlockSpec((None, LANE, HEAD_PAD), lambda b, g, i, far: (b * G + g, 0, 0)),
            pl.BlockSpec((None, NSA_DV, LANE), lambda b, g, i, far: (b * G + g, 0, 0)),
            pl.BlockSpec((seq, HEAD_PAD), lambda b, g, i, far: (b, zoff["ks"] // HEAD_PAD + g)),
            pl.BlockSpec((NSA_DV, seq), lambda b, g, i, far: (g, b)),
            pl.BlockSpec((seq, HEAD_PAD), lambda b, g, i, far: (b, zoff["kw"] // HEAD_PAD + g)),
            pl.BlockSpec((NSA_DV, seq), lambda b, g, i, far: (G + g, b)),
            pl.BlockSpec((tq, LANE), lambda b, g, i, far: (b * nq + i, zoff["ng"] // LANE)),
            pl.BlockSpec((None, 1, seq), lambda b, g, i, far: (b, 0, 0)),
            pl.BlockSpec((seq, 1), lambda b, g, i, far: (b, 0)),
            pl.BlockSpec((LANE, 1), lambda b, g, i, far: (b, 0)),
            pl.BlockSpec((None, J, LANE), lambda b, g, i, far: (g, 0, 0)),
        ],
        out_specs=pl.BlockSpec((tq, J * NSA_DV), lambda b, g, i, far: (b * nq + i, g)),
        scratch_shapes=[pltpu.VMEM((2, 2, tk, R), F32), pltpu.VMEM((2, 1, R), F32),
                        pltpu.VMEM((2, NSA_DV + ONES_ROWS, R), F32),
                        pltpu.VMEM((LANE, tq), F32)],
    )
    return pl.pallas_call(
        kern,
        grid_spec=grid_spec,
        out_shape=jax.ShapeDtypeStruct((batch * seq, NSA_HEADS * NSA_DV), BF16),
        compiler_params=_params("parallel", "parallel", "parallel"),
        name="nsa_attn",
    )(far, z, kc, vct, z, v_t, z, v_t, z, pos_row, pos_col, pos_cmp, tbl)


def _merge_kernel(a_ref, b_ref, ga_ref, gb_ref, h_ref, wa_ref, wb_ref, wo_ref, post_g_ref,
                  next_g_ref, o_ref, un_ref):
    ya = _dot(a_ref[...], wa_ref[...])
    yb = _dot(b_ref[...], wb_ref[...])
    m = (jax.nn.sigmoid(ga_ref[...].astype(F32)) * ya
         + jax.nn.sigmoid(gb_ref[...].astype(F32)) * yb).astype(BF16)
    y = _dot(m, wo_ref[...])
    out = h_ref[...] + _rms(y, post_g_ref[...])
    o_ref[...] = out
    un_ref[...] = _rms(out, next_g_ref[...]).astype(BF16)


def _merge(a, bb, z, h, wa, wb, wo, post_g, next_g, layer, zoff):
    m, d = h.shape
    tm = MERGE_TM
    da, db = a.shape[1], bb.shape[1]
    gblk = zoff["mg"] // d
    const = lambda i: (layer, 0, 0)
    return pl.pallas_call(
        _merge_kernel,
        grid=(m // tm,),
        in_specs=[pl.BlockSpec((tm, da), lambda i: (i, 0)),
                  pl.BlockSpec((tm, db), lambda i: (i, 0)),
                  pl.BlockSpec((tm, d), lambda i: (i, gblk)),
                  pl.BlockSpec((tm, d), lambda i: (i, gblk + 1)),
                  pl.BlockSpec((tm, d), lambda i: (i, 0)),
                  pl.BlockSpec((None, da, d), const, pipeline_mode=pl.Buffered(1)),
                  pl.BlockSpec((None, db, d), const, pipeline_mode=pl.Buffered(1)),
                  pl.BlockSpec((None, d, d), const, pipeline_mode=pl.Buffered(1)),
                  pl.BlockSpec((None, 1, d), const),
                  pl.BlockSpec((None, 1, d), const)],
        out_specs=[pl.BlockSpec((tm, d), lambda i: (i, 0)),
                   pl.BlockSpec((tm, d), lambda i: (i, 0))],
        out_shape=[jax.ShapeDtypeStruct((m, d), F32), jax.ShapeDtypeStruct((m, d), BF16)],
        compiler_params=_params("parallel"),
        name="merge_out",
    )(a, bb, z, z, h, wa, wb, wo, post_g, next_g)


def _z_layout(d_model):
    G = NSA_KV_HEADS
    widths = [("q", NSA_HEADS * HEAD_PAD), ("mg", 2 * d_model), ("cq", MLA_Q_LORA),
              ("ckv", MLA_KV_LORA), ("kc", G * HEAD_PAD), ("ks", G * HEAD_PAD),
              ("kw", G * HEAD_PAD), ("vc", G * NSA_DV), ("kr", LANE), ("ng", LANE)]
    off, pos = {}, 0
    for name, w in widths:
        off[name] = pos
        pos += w
    off["total"] = pos
    assert off["q"] == 0 and off["mg"] % d_model == 0
    assert off["cq"] % MLA_Q_LORA == 0 and off["ckv"] % MLA_KV_LORA == 0
    assert all(off[k] % HEAD_PAD == 0 for k in ("kc", "ks", "kw"))
    return off


def _swap_halves(w):
    half = w.shape[-1] // 2
    return jnp.concatenate([-w[..., half:], w[..., :half]], axis=-1)


def _pad_last(w, width):
    return jnp.pad(w, [(0, 0)] * (w.ndim - 1) + [(0, width - w.shape[-1])])


def _w_in_pieces(d_model, zoff):
    G, J = NSA_KV_HEADS, NSA_GROUP
    splits = [MLA_Q_LORA, MLA_KV_LORA, MLA_ROPE, NSA_HEADS * NSA_DK,
              G * NSA_DK, G * NSA_DV, G * NSA_DK, G * NSA_DV, G * NSA_DK, G * NSA_DV,
              NSA_HEADS * 3, 2 * d_model]
    names = ["cq", "ckv", "kr", "q", "kc", "vc", "ks", "vs", "kw", "vw", "ng", "mg"]
    src = dict(zip(names, [0] + [int(v) for v in np.cumsum(splits)[:-1]]))
    half = MLA_ROPE // 2
    pieces = []
    for h in range(NSA_HEADS):
        pieces.append((zoff["q"] + h * HEAD_PAD, HEAD_PAD,
                       [(src["q"] + h * NSA_DK, NSA_DK, NSA_DK ** -0.5 * LOG2E)]))
    pieces.append((zoff["mg"], 2 * d_model, [(src["mg"], 2 * d_model, None)]))
    pieces.append((zoff["cq"], MLA_Q_LORA, [(src["cq"], MLA_Q_LORA, None)]))
    pieces.append((zoff["ckv"], MLA_KV_LORA, [(src["ckv"], MLA_KV_LORA, None)]))
    for name in ("kc", "ks", "kw"):
        for g in range(G):
            pieces.append((zoff[name] + g * HEAD_PAD, HEAD_PAD, [(src[name] + g * NSA_DK, NSA_DK, None)]))
    pieces.append((zoff["vc"], G * NSA_DV, [(src["vc"], G * NSA_DV, None)]))
    pieces.append((zoff["kr"], LANE, [(src["kr"], MLA_ROPE, None), (src["kr"] + half, half, -1.0),
                                      (src["kr"], half, None)]))
    pieces.append((zoff["ng"], LANE, [(src["ng"], NSA_HEADS * 3, None)]))
    return pieces, (src["vs"], src["vw"])


COPY_ROWS = 512


def _copy_rows(w_ref, o_ref, src, dst, n, scale):
    for off in range(0, n, COPY_ROWS):
        m = min(COPY_ROWS, n - off)
        x = w_ref[src + off:src + off + m, :]
        o_ref[dst + off:dst + off + m, :] = (x if scale is None else x * scale).astype(BF16)


def _w_in_layout_kernel(w_ref, o_ref, vt_ref, *, pieces, value_rows):
    tc = w_ref.shape[1]
    dst = 0
    for src, n in value_rows:
        _copy_rows(w_ref, vt_ref, src, dst, n, None)
        dst += n
    for dst, width, parts in pieces:
        used = sum(n for _, n, _ in parts)
        if all(n % ONES_ROWS == 0 for _, n, _ in parts):
            at = dst
            for src, n, scale in parts:
                _copy_rows(w_ref, o_ref, src, at, n, scale)
                at += n
            if used < width:
                o_ref[at:dst + width, :] = jnp.zeros((width - used, tc), BF16)
        else:
            rows = [w_ref[src:src + n, :] if scale is None else w_ref[src:src + n, :] * scale
                    for src, n, scale in parts]
            rows.append(jnp.zeros((width - used, tc), F32))
            o_ref[dst:dst + width, :] = jnp.concatenate(rows, axis=0).astype(BF16)


def _layout_w_in(w_in, zoff):
    L, D, n_in = w_in.shape
    G = NSA_KV_HEADS
    pieces, (vs, vw) = _w_in_pieces(D, zoff)
    tc = 256
    nv = 2 * G * NSA_DV
    value_rows = ((vs, G * NSA_DV), (vw, G * NSA_DV))
    return pl.pallas_call(
        functools.partial(_w_in_layout_kernel, pieces=pieces, value_rows=value_rows),
        grid=(L, D // tc),
        in_specs=[pl.BlockSpec((None, n_in, tc), lambda l, c: (l, 0, c))],
        out_specs=[pl.BlockSpec((None, zoff["total"], tc), lambda l, c: (l, 0, c)),
                   pl.BlockSpec((None, nv, tc), lambda l, c: (l, 0, c))],
        out_shape=[jax.ShapeDtypeStruct((L, zoff["total"], D), BF16),
                   jax.ShapeDtypeStruct((L, nv, D), BF16)],
        compiler_params=_params("parallel", "parallel"),
        name="w_in_layout",
    )(jnp.swapaxes(w_in, 1, 2))


def _layout_w_q_up(w):
    L, r, _ = w.shape
    w = w.reshape(L, r, MLA_HEADS, MLA_NOPE + MLA_ROPE) * ((MLA_NOPE + MLA_ROPE) ** -0.5 * LOG2E)
    rope = w[..., MLA_NOPE:]
    return jnp.concatenate([w[..., :MLA_NOPE], rope, _swap_halves(rope)], axis=-1).reshape(
        L, r, MLA_HEADS * HEAD_PAD).astype(BF16)


def kernel(x, positions, rel_bias, ffn1_pre_g, ffn1_post_g, ffn1_w_gate, ffn1_w_up, ffn1_w_down, mix_pre_g, mix_post_g, w_in, mla_q_norm_g, mla_w_q_up, mla_kv_norm_g, mla_w_uk, mla_w_uv, cmp_pe_k, cmp_w1_k, cmp_w2_k, cmp_pe_v, cmp_w1_v, cmp_w2_v, w_branch_mla, w_branch_nsa, w_out, ffn2_pre_g, ffn2_post_g, ffn2_w_gate, ffn2_w_up, ffn2_w_down):
    B, S, D = x.shape
    L = w_in.shape[0]
    M = B * S
    G, J = NSA_KV_HEADS, NSA_GROUP
    zoff = _z_layout(D)
    n_half = S // CMP_STRIDE
    n_cmp = (S - CMP_LEN) // CMP_STRIDE + 1
    assert n_half == LANE and n_cmp <= LANE

    gain = lambda g: g.reshape(L, 1, -1)
    bf = lambda w: w.astype(BF16)
    w_z, w_vt = _layout_w_in(w_in, zoff)
    w_q = _layout_w_q_up(mla_w_q_up)
    w_uk, w_uv_t = bf(mla_w_uk), bf(mla_w_uv).transpose(0, 2, 1)
    pe_k = _pad_last(cmp_pe_k, HEAD_PAD)
    w1_k = bf(jnp.pad(cmp_w1_k.reshape(L, CMP_LEN, NSA_DK, CMP_HIDDEN),
                      ((0, 0), (0, 0), (0, HEAD_PAD - NSA_DK), (0, 0)))
              ).reshape(L, CMP_LEN * HEAD_PAD, CMP_HIDDEN)
    w2_k = bf(_pad_last(cmp_w2_k, HEAD_PAD))
    f1 = (bf(ffn1_w_gate), bf(ffn1_w_up), ffn1_w_down)
    f2 = (bf(ffn2_w_gate), bf(ffn2_w_up), ffn2_w_down)
    w1_v, w2_v = bf(cmp_w1_v), bf(cmp_w2_v)
    w_a, w_b, w_o = bf(w_branch_mla), bf(w_branch_nsa), bf(w_out)
    g_f1pre, g_f1post, g_mpre, g_mpost = gain(ffn1_pre_g), gain(ffn1_post_g), gain(mix_pre_g), gain(mix_post_g)
    g_f2pre, g_f2post, g_q, g_kv = gain(ffn2_pre_g), gain(ffn2_post_g), gain(mla_q_norm_g), gain(mla_kv_norm_g)

    pos_col = positions.reshape(M, 1)
    pos_row = positions.reshape(B, 1, S)
    pos_cmp = _pad_last(positions[:, CMP_LEN - 1::CMP_STRIDE][:, :n_cmp], LANE).reshape(B * LANE, 1)
    q_min = positions.reshape(B, S // LANE, LANE).min(axis=-1)
    k_max = positions.reshape(B, S // LANE, LANE).max(axis=-1)
    far = (q_min[:, :, None] - k_max[:, None, :] >= FAR_DIST).astype(jnp.int32).reshape(-1)
    tbl = (jnp.take(rel_bias, jnp.asarray(BUCKET_OF_DIST), axis=0).T * LOG2E).reshape(G, J, LANE)
    half = MLA_ROPE // 2
    inv = ROPE_BASE ** (-jnp.arange(half, dtype=F32) * 2.0 / MLA_ROPE)
    cs = _rope_table(pos_col, jnp.concatenate([inv, inv]).reshape(1, MLA_ROPE))

    h = x.reshape(M, D)
    u = _rmsnorm(h, g_f1pre, 0)
    for l in range(L):
        h, u = _ffn(h, u, 0.5 * g_f1post, g_mpre, l, *f1, l)
        z = _in_proj(u, w_z, l)
        v_t = _in_proj_t(u, w_vt, l)
        qp, kp, vt = _mla_prep(z, cs, g_q, g_kv, w_q, w_uk, w_uv_t, l, zoff)
        a = _mla_attn(qp, kp, vt, B, S)

        kc, vct = _nsa_cmp(z, pe_k, cmp_pe_v, w1_k, w2_k, w1_v, w2_v, l, B, S, zoff)
        nsa = _nsa_attn(far, z, kc, vct, v_t, pos_row, pos_col, pos_cmp, tbl, B, S, zoff)
        h, u = _merge(a, nsa, z, h, w_a, w_b, w_o, g_mpost, g_f2pre, l, zoff)
        h, u = _ffn(h, u, 0.5 * g_f2post, g_f1pre, min(l + 1, L - 1), *f2, l)
    return h.reshape(B, S, D)
```

```python
import functools
import math

import numpy as np
import jax
import jax.numpy as jnp
from jax import lax
from jax.experimental import pallas as pl
from jax.experimental.pallas import tpu as pltpu

EPS = 1e-6
MLA_HEADS = 8
MLA_Q_LORA = 512
MLA_KV_LORA = 512
MLA_NOPE = 128
MLA_ROPE = 64
MLA_V = 128
ROPE_BASE = 10000.0
NSA_HEADS = 8
NSA_KV_HEADS = 2
NSA_GROUP = NSA_HEADS // NSA_KV_HEADS
NSA_DK = 192
NSA_DV = 128
CMP_LEN = 32
CMP_STRIDE = 16
CMP_HIDDEN = 256
SLC_LEN = 64
SLC_TOPN = 16
WINDOW = 512
FORCED_SCORE = 1e6
REL_BUCKETS = 32
REL_MAX_DIST = 128
NEG = -1e30
MASKED = 2 * NEG
LOG2E = 1.0 / math.log(2.0)

LANE = 128
ONES_ROWS = 16
HEAD_PAD = 256
VMEM_LIMIT = 56 * 1024 * 1024
BF16 = jnp.bfloat16
F32 = jnp.float32

FFN_TM, FFN_TF = 512, 512
PROJ_TM = 1024
PROJ_TN_CAP = 2560
PREP_TM = 512
MLA_TQ, MLA_TK = 512, 512
MLA_HEADS_PER_STEP = 8
NSA_TQ, NSA_TK = 256, 256
MERGE_TM = 512


def _bucket_of_distance():
    n = np.arange(LANE)
    max_exact = REL_BUCKETS // 2
    large = max_exact + (np.log(np.maximum(n, 1) / max_exact) / math.log(REL_MAX_DIST / max_exact)
                         * (REL_BUCKETS - max_exact)).astype(np.int32)
    bucket = np.where(n < max_exact, n, np.minimum(large, REL_BUCKETS - 1)).astype(np.int32)
    assert bucket[-1] == REL_BUCKETS - 1
    return bucket


BUCKET_OF_DIST = _bucket_of_distance()
FAR_DIST = int(np.max(np.nonzero(BUCKET_OF_DIST != REL_BUCKETS - 1)[0])) + 1


def _params(*sem):
    return pltpu.CompilerParams(dimension_semantics=sem, vmem_limit_bytes=VMEM_LIMIT)


def _rms(x, g):
    return x * lax.rsqrt(jnp.mean(x * x, axis=-1, keepdims=True) + EPS) * g


def _dot(a, b):
    return jnp.dot(a, b, preferred_element_type=F32)


def _dot_nt(a, b):
    return lax.dot_general(a, b, (((1,), (1,)), ((), ())), preferred_element_type=F32)


def _tile_n(n, cap):
    best = LANE
    for t in range(LANE, cap + 1, LANE):
        if n % t == 0:
            best = t
    return best


def _rmsnorm_kernel(x_ref, g_ref, o_ref):
    o_ref[...] = _rms(x_ref[...], g_ref[...]).astype(o_ref.dtype)


def _rmsnorm(x, g, layer):
    m, d = x.shape
    tm = 512
    return pl.pallas_call(
        _rmsnorm_kernel,
        grid=(m // tm,),
        in_specs=[pl.BlockSpec((tm, d), lambda i: (i, 0)),
                  pl.BlockSpec((None, 1, d), lambda i: (layer, 0, 0))],
        out_specs=pl.BlockSpec((tm, d), lambda i: (i, 0)),
        out_shape=jax.ShapeDtypeStruct((m, d), BF16),
        compiler_params=_params("parallel"),
        name="rmsnorm",
    )(x, g)


def _rope_table_kernel(pos_ref, inv_ref, o_ref):
    ang = pos_ref[...].astype(F32) * inv_ref[...]
    o_ref[...] = jnp.concatenate([jnp.cos(ang), jnp.sin(ang)], axis=1)


def _rope_table(pos_col, inv):
    m = pos_col.shape[0]
    tm = 512
    return pl.pallas_call(
        _rope_table_kernel,
        grid=(m // tm,),
        in_specs=[pl.BlockSpec((tm, 1), lambda i: (i, 0)),
                  pl.BlockSpec((1, MLA_ROPE), lambda i: (0, 0))],
        out_specs=pl.BlockSpec((tm, 2 * MLA_ROPE), lambda i: (i, 0)),
        out_shape=jax.ShapeDtypeStruct((m, 2 * MLA_ROPE), F32),
        compiler_params=_params("parallel"),
        name="rope_table",
    )(pos_col, inv)


FFN_X_CHUNKS, FFN_U_CHUNKS = 4, 2


def _ffn_tile_copy(x_hbm, u_hbm, xbuf, ubuf, sems, tile, slot, c):
    tm = xbuf.shape[1]
    if c < FFN_X_CHUNKS:
        src, dst, rows, k = x_hbm, xbuf, tm // FFN_X_CHUNKS, c
    else:
        src, dst, rows, k = u_hbm, ubuf, tm // FFN_U_CHUNKS, c - FFN_X_CHUNKS
    return pltpu.make_async_copy(src.at[pl.ds(tile * tm + k * rows, rows), :],
                                 dst.at[slot, pl.ds(k * rows, rows), :], sems.at[slot, c])


def _ffn_kernel(x_hbm, u_hbm, post_g_ref, next_g_ref, wg_ref, wu_ref, wd_ref,
                o_ref, un_ref, acc_ref, xbuf, ubuf, sems):
    i, j = pl.program_id(0), pl.program_id(1)
    slot = i % 2
    pieces = FFN_X_CHUNKS + FFN_U_CHUNKS
    copy = functools.partial(_ffn_tile_copy, x_hbm, u_hbm, xbuf, ubuf, sems)

    @pl.when(j == 0)
    def _():
        @pl.when(i == 0)
        def _():
            for c in range(pieces):
                copy(0, 0, c).start()

        for c in range(pieces):
            copy(i, slot, c).wait()
        acc_ref[...] = jnp.zeros(acc_ref.shape, F32)

    for c in range(pieces):
        @pl.when((j == c + 1) & (i + 1 < pl.num_programs(0)))
        def _():
            copy(i + 1, 1 - slot, c).start()

    u = ubuf[slot]
    gate = _dot(u, wg_ref[...])
    up = _dot(u, wu_ref[...])
    hidden = (gate * jax.nn.sigmoid(gate) * up).astype(BF16)
    acc_ref[...] += _dot(hidden, wd_ref[...].astype(BF16))

    @pl.when(j == pl.num_programs(1) - 1)
    def _():
        out = xbuf[slot] + _rms(acc_ref[...], post_g_ref[...])
        o_ref[...] = out
        un_ref[...] = _rms(out, next_g_ref[...]).astype(BF16)


def _ffn(x, u, post_g, next_g, next_layer, wg, wu, wd, layer):
    m, d = x.shape
    f = wg.shape[-1]
    tm, tf = FFN_TM, _tile_n(f, FFN_TF)
    pieces = FFN_X_CHUNKS + FFN_U_CHUNKS
    assert f // tf > pieces
    return pl.pallas_call(
        _ffn_kernel,
        grid=(m // tm, f // tf),
        in_specs=[pl.BlockSpec(memory_space=pl.ANY),
                  pl.BlockSpec(memory_space=pl.ANY),
                  pl.BlockSpec((None, 1, d), lambda i, j: (layer, 0, 0)),
                  pl.BlockSpec((None, 1, d), lambda i, j: (next_layer, 0, 0)),
                  pl.BlockSpec((None, d, tf), lambda i, j: (layer, 0, j)),
                  pl.BlockSpec((None, d, tf), lambda i, j: (layer, 0, j)),
                  pl.BlockSpec((None, tf, d), lambda i, j: (layer, j, 0))],
        out_specs=[pl.BlockSpec((tm, d), lambda i, j: (i, 0)),
                   pl.BlockSpec((tm, d), lambda i, j: (i, 0))],
        out_shape=[jax.ShapeDtypeStruct((m, d), F32), jax.ShapeDtypeStruct((m, d), BF16)],
        scratch_shapes=[pltpu.VMEM((tm, d), F32), pltpu.VMEM((2, tm, d), F32),
                        pltpu.VMEM((2, tm, d), BF16), pltpu.SemaphoreType.DMA((2, pieces))],
        compiler_params=_params("arbitrary", "arbitrary"),
        name="ffn",
    )(x, u, post_g, next_g, wg, wu, wd)


def _matmul_nt_kernel(x_ref, w_ref, o_ref):
    o_ref[...] = _dot_nt(x_ref[...], w_ref[...]).astype(o_ref.dtype)


def _in_proj(u, w_t, layer):
    m, d = u.shape
    n = w_t.shape[1]
    tm, tn = PROJ_TM, _tile_n(n, PROJ_TN_CAP)
    return pl.pallas_call(
        _matmul_nt_kernel,
        grid=(n // tn, m // tm),
        in_specs=[pl.BlockSpec((tm, d), lambda j, i: (i, 0)),
                  pl.BlockSpec((None, tn, d), lambda j, i: (layer, j, 0))],
        out_specs=pl.BlockSpec((tm, tn), lambda j, i: (i, j)),
        out_shape=jax.ShapeDtypeStruct((m, n), BF16),
        compiler_params=_params("parallel", "parallel"),
        name="in_proj",
    )(u, w_t)


def _matmul_t_kernel(w_ref, x_ref, o_ref):
    o_ref[...] = _dot_nt(w_ref[...], x_ref[...]).astype(o_ref.dtype)


def _in_proj_t(u, w_t, layer):
    m, d = u.shape
    n = w_t.shape[1]
    tm = PROJ_TM
    return pl.pallas_call(
        _matmul_t_kernel,
        grid=(m // tm,),
        in_specs=[pl.BlockSpec((None, n, d), lambda i: (layer, 0, 0)),
                  pl.BlockSpec((tm, d), lambda i: (i, 0))],
        out_specs=pl.BlockSpec((n, tm), lambda i: (0, i)),
        out_shape=jax.ShapeDtypeStruct((n, m), BF16),
        compiler_params=_params("parallel"),
        name="in_proj_t",
    )(w_t, u)


def _softmax_tile_t(s_ref, adds, masks, v_t, m_ref, acc_ref, interleave=()):
    probs, alphas = [], []
    n_chunks = s_ref.shape[1] // LANE
    for c in range(n_chunks):
        if interleave and c % (n_chunks // len(interleave)) == 0:
            interleave[c // (n_chunks // len(interleave))]()
        cols = slice(c * LANE, (c + 1) * LANE)
        s = s_ref[:, cols]
        for mask in masks[c]:
            s = jnp.where(mask, s, MASKED) if mask.dtype == jnp.bool_ else s + mask
        m_old = m_ref[:, cols]
        m_tile = jnp.max(s, axis=0, keepdims=True)
        if adds[c] is not None:
            m_tile = m_tile + adds[c]
        m_new = jnp.maximum(m_old, m_tile)
        alpha = jnp.exp2(m_old - m_new)
        p = jnp.exp2(s - (m_new if adds[c] is None else m_new - adds[c]))
        m_ref[:, cols] = m_new
        probs.append(p.astype(BF16))
        alphas.append(alpha)
    p_t = probs[0] if len(probs) == 1 else jnp.concatenate(probs, axis=1)
    alpha = alphas[0] if len(alphas) == 1 else jnp.concatenate(alphas, axis=1)
    v_ones = jnp.concatenate([v_t, jnp.ones((ONES_ROWS, v_t.shape[1]), BF16)], axis=0)
    acc_ref[...] = alpha * acc_ref[...] + _dot(v_ones, p_t)


def _softmax_init(m_ref, acc_ref):
    m_ref[...] = jnp.full(m_ref.shape, NEG, F32)
    acc_ref[...] = jnp.zeros(acc_ref.shape, F32)


def _softmax_out(acc):
    dv = acc.shape[0] - ONES_ROWS
    return acc[:dv] / acc[dv:dv + 1]


def _mla_prep_kernel(cq_ref, ckv_ref, kr_ref, cs_ref, gq_ref, gkv_ref, wq_ref, wuk_ref, wuvt_ref,
                     q_ref, k_ref, vt_ref):
    cs = cs_ref[...]
    qn = _rms(cq_ref[...].astype(F32), gq_ref[...]).astype(BF16)
    q = _dot(qn, wq_ref[...])
    kvn = _rms(ckv_ref[...].astype(F32), gkv_ref[...]).astype(BF16)
    k_nope = _dot(kvn, wuk_ref[...])
    t = kr_ref[...].astype(F32) * cs
    k_pe = (t + pltpu.roll(t, MLA_ROPE, 1)).astype(BF16)
    for h in range(MLA_HEADS):
        lo = h * HEAD_PAD
        q_ref[:, lo:lo + MLA_NOPE] = q[:, lo:lo + MLA_NOPE].astype(BF16)
        q_ref[:, lo + MLA_NOPE:lo + HEAD_PAD] = (q[:, lo + MLA_NOPE:lo + HEAD_PAD] * cs).astype(BF16)
        k_ref[:, lo:lo + MLA_NOPE] = k_nope[:, h * MLA_NOPE:(h + 1) * MLA_NOPE].astype(BF16)
        k_ref[:, lo + MLA_NOPE:lo + HEAD_PAD] = k_pe
    vt_ref[...] = _dot_nt(wuvt_ref[...], kvn).astype(BF16)


def _mla_prep(z, cs, gq, gkv, wq, wuk, wuv_t, layer, zoff):
    m = z.shape[0]
    tm = PREP_TM
    hq = MLA_HEADS * HEAD_PAD
    hv = MLA_HEADS * MLA_V
    const = lambda i: (layer, 0, 0)
    return pl.pallas_call(
        _mla_prep_kernel,
        grid=(m // tm,),
        in_specs=[pl.BlockSpec((tm, MLA_Q_LORA), lambda i: (i, zoff["cq"] // MLA_Q_LORA)),
                  pl.BlockSpec((tm, MLA_KV_LORA), lambda i: (i, zoff["ckv"] // MLA_KV_LORA)),
                  pl.BlockSpec((tm, LANE), lambda i: (i, zoff["kr"] // LANE)),
                  pl.BlockSpec((tm, LANE), lambda i: (i, 0)),
                  pl.BlockSpec((None, 1, MLA_Q_LORA), const),
                  pl.BlockSpec((None, 1, MLA_KV_LORA), const),
                  pl.BlockSpec((None, MLA_Q_LORA, hq), const),
                  pl.BlockSpec((None, MLA_KV_LORA, MLA_HEADS * MLA_NOPE), const),
                  pl.BlockSpec((None, hv, MLA_KV_LORA), const)],
        out_specs=[pl.BlockSpec((tm, hq), lambda i: (i, 0)),
                   pl.BlockSpec((tm, hq), lambda i: (i, 0)),
                   pl.BlockSpec((hv, tm), lambda i: (0, i))],
        out_shape=[jax.ShapeDtypeStruct((m, hq), BF16),
                   jax.ShapeDtypeStruct((m, hq), BF16),
                   jax.ShapeDtypeStruct((hv, m), BF16)],
        compiler_params=_params("parallel"),
        name="mla_prep",
    )(z, z, z, cs, gq, gkv, wq, wuk, wuv_t)


def _by_parity(kt, fn):
    @pl.when(kt % 2 == 0)
    def _():
        fn(0)

    @pl.when(kt % 2 == 1)
    def _():
        fn(1)


def _mla_attn_kernel(q_ref, k_ref, vt_ref, o_ref, s_ref, m_ref, acc_ref, *, tq, tk, heads):
    i = pl.program_id(2)
    nc = tq // LANE
    _softmax_init(m_ref, acc_ref)
    qt = i * tq + lax.broadcasted_iota(jnp.int32, (1, tq), 1)
    kcol = lax.broadcasted_iota(jnp.int32, (tk, 1), 0)
    qk = [slice(h * HEAD_PAD, (h + 1) * HEAD_PAD) for h in range(heads)]
    vd = [slice(h * MLA_V, (h + 1) * MLA_V) for h in range(heads)]
    n_full = (i * tq) // tk
    n_all = (i * tq + tq + tk - 1) // tk

    def head_logits(kt, buf, h):
        ks = pl.multiple_of(kt * tk, tk)
        s_ref[buf, h] = _dot_nt(k_ref[pl.ds(ks, tk), qk[h]], q_ref[:, qk[h]])

    def logits(kt, buf):
        for h in range(heads):
            head_logits(kt, buf, h)

    def step(kt, buf, masked, prefetch=True):
        ks = pl.multiple_of(kt * tk, tk)
        masks = [()] * nc
        if masked:
            masks = [(ks + kcol <= qt[:, c * LANE:(c + 1) * LANE],) for c in range(nc)]
        for h in range(heads):
            if prefetch:
                head_logits(kt + 1, 1 - buf, h)
            _softmax_tile_t(s_ref.at[buf, h], [None] * nc, masks, vt_ref[vd[h], pl.ds(ks, tk)],
                            m_ref.at[h], acc_ref.at[h])

    def full_body(kt, c):
        _by_parity(kt, lambda buf: step(kt, buf, False))
        return c

    def diag_body(kt, c):
        _by_parity(kt, lambda buf: step(kt, buf, True))
        return c

    logits(0, 0)
    lax.fori_loop(0, n_full, full_body, 0)
    lax.fori_loop(n_full, n_all - 1, diag_body, 0)
    _by_parity(n_all - 1, lambda buf: step(n_all - 1, buf, True, prefetch=False))
    for h in range(heads):
        o_ref[:, vd[h]] = _softmax_out(acc_ref[h]).T.astype(o_ref.dtype)


def _mla_attn(qp, kp, vt, batch, seq):
    tq, tk, heads = MLA_TQ, MLA_TK, MLA_HEADS_PER_STEP
    nq = seq // tq
    kern = functools.partial(_mla_attn_kernel, tq=tq, tk=tk, heads=heads)
    return pl.pallas_call(
        kern,
        grid=(batch, MLA_HEADS // heads, nq),
        in_specs=[pl.BlockSpec((tq, heads * HEAD_PAD), lambda b, h, i: (b * nq + i, h)),
                  pl.BlockSpec((seq, heads * HEAD_PAD), lambda b, h, i: (b, h)),
                  pl.BlockSpec((heads * MLA_V, seq), lambda b, h, i: (h, b))],
        out_specs=pl.BlockSpec((tq, heads * MLA_V), lambda b, h, i: (b * nq + i, h)),
        out_shape=jax.ShapeDtypeStruct((batch * seq, MLA_HEADS * MLA_V), BF16),
        scratch_shapes=[pltpu.VMEM((2, heads, tk, tq), F32), pltpu.VMEM((heads, 1, tq), F32),
                        pltpu.VMEM((heads, MLA_V + ONES_ROWS, tq), F32)],
        compiler_params=_params("parallel", "parallel", "parallel"),
        name="mla_attn",
    )(qp, kp, vt)


def _compress(z_ref, zf_ref, pe_ref, w1_ref, w2_ref):
    chunks, s, _ = zf_ref.shape
    for c in range(chunks):
        zf_ref[c] = z_ref[:, c * LANE:(c + 1) * LANE].astype(F32)
    d = chunks * LANE
    n = s // CMP_STRIDE
    first = jnp.zeros((n, CMP_HIDDEN), F32)
    second = jnp.zeros((n, CMP_HIDDEN), F32)
    for l in range(CMP_STRIDE):
        toks = [zf_ref[c, pl.ds(l, n, stride=CMP_STRIDE), :] for c in range(chunks)]
        tok = toks[0] if chunks == 1 else jnp.concatenate(toks, axis=1)
        lo, hi = l, CMP_STRIDE + l
        first += _dot((tok + pe_ref[lo:lo + 1, :]).astype(BF16), w1_ref[lo * d:(lo + 1) * d, :])
        second += _dot((tok + pe_ref[hi:hi + 1, :]).astype(BF16), w1_ref[hi * d:(hi + 1) * d, :])
    hidden = first + pltpu.roll(second, n - 1, 0)
    return _dot((hidden * jax.nn.sigmoid(hidden)).astype(BF16), w2_ref[...])


def _nsa_cmp_kernel(zk_ref, zv_ref, pek_ref, pev_ref, w1k_ref, w2k_ref, w1v_ref, w2v_ref,
                    kc_ref, vct_ref, kf_ref, vf_ref):
    kc_ref[...] = _compress(zk_ref, kf_ref, pek_ref, w1k_ref, w2k_ref).astype(BF16)
    vct_ref[...] = _compress(zv_ref, vf_ref, pev_ref, w1v_ref, w2v_ref).T.astype(BF16)


def _nsa_cmp(z, pek, pev, w1k, w2k, w1v, w2v, layer, batch, seq, zoff):
    G = NSA_KV_HEADS
    nh = seq // CMP_STRIDE
    const = lambda i: (layer, 0, 0)
    return pl.pallas_call(
        _nsa_cmp_kernel,
        grid=(batch * G,),
        in_specs=[pl.BlockSpec((seq, HEAD_PAD), lambda i: (i // G, zoff["kc"] // HEAD_PAD + i % G)),
                  pl.BlockSpec((seq, NSA_DV), lambda i: (i // G, zoff["vc"] // NSA_DV + i % G)),
                  pl.BlockSpec((None, CMP_LEN, HEAD_PAD), const),
                  pl.BlockSpec((None, CMP_LEN, NSA_DV), const),
                  pl.BlockSpec((None, CMP_LEN * HEAD_PAD, CMP_HIDDEN), const),
                  pl.BlockSpec((None, CMP_HIDDEN, HEAD_PAD), const),
                  pl.BlockSpec((None, CMP_LEN * NSA_DV, CMP_HIDDEN), const),
                  pl.BlockSpec((None, CMP_HIDDEN, NSA_DV), const)],
        out_specs=[pl.BlockSpec((None, nh, HEAD_PAD), lambda i: (i, 0, 0)),
                   pl.BlockSpec((None, NSA_DV, nh), lambda i: (i, 0, 0))],
        out_shape=[jax.ShapeDtypeStruct((batch * G, nh, HEAD_PAD), BF16),
                   jax.ShapeDtypeStruct((batch * G, NSA_DV, nh), BF16)],
        scratch_shapes=[pltpu.VMEM((HEAD_PAD // LANE, seq, LANE), F32),
                        pltpu.VMEM((NSA_DV // LANE, seq, LANE), F32)],
        compiler_params=_params("parallel"),
        name="nsa_cmp",
    )(z, z, pek, pev, w1k, w2k, w1v, w2v)


def _split3(x):
    a = x.astype(BF16)
    r = x - a.astype(F32)
    b = r.astype(BF16)
    c = (r - b.astype(F32)).astype(BF16)
    return a, b, c


def _lookup(table_row, idx):
    rows, width = idx.shape
    table = jnp.broadcast_to(table_row, (rows, LANE))
    chunks = [jnp.take_along_axis(table, idx[:, c:c + LANE], axis=1, mode="promise_in_bounds")
              for c in range(0, width, LANE)]
    return chunks[0] if len(chunks) == 1 else jnp.concatenate(chunks, axis=1)


def _nsa_attn_kernel(far_ref, q_ref, kc_ref, vct_ref, ks_ref, vst_ref, kw_ref, vwt_ref, ng_ref,
                     prow_ref, pcol_ref, pcmp_ref, tbl_ref, o_ref,
                     s_ref, m_ref, acc_ref, gate_ref, *, tq, tk, seq):
    b, i = pl.program_id(0), pl.program_id(2)
    nq, nk = seq // tq, seq // tk
    J = NSA_GROUP
    n_slc = seq // SLC_LEN
    n_cmp = (seq - CMP_LEN) // CMP_STRIDE + 1
    t0 = pl.multiple_of(i * tq, tq)

    qs = jnp.concatenate([q_ref[:, j * HEAD_PAD:(j + 1) * HEAD_PAD] for j in range(J)], axis=0)
    qpos = prow_ref[:, pl.ds(t0, tq)]
    qt = t0 + lax.broadcasted_iota(jnp.int32, (1, tq), 1)
    tbl = tbl_ref[...]
    far_bias = [tbl[j:j + 1, LANE - 1:LANE] for j in range(J)]
    head = [slice(j * tq, (j + 1) * tq) for j in range(J)]

    blk = lax.broadcasted_iota(jnp.int32, (LANE, 1), 0)
    keep_c = (blk * CMP_STRIDE + (CMP_LEN - 1) <= qt) & (blk < n_cmp)
    idx_c = jnp.clip(qpos - pcmp_ref[...], 0, LANE - 1)
    s_c = _dot_nt(kc_ref[...], qs)
    p_heads = []
    for j in range(J):
        s = jnp.where(keep_c, s_c[:, head[j]] + _lookup(tbl[j:j + 1, :], idx_c), NEG)
        e = jnp.where(keep_c, jnp.exp2(s - jnp.max(s, axis=0, keepdims=True)), 0.0)
        den = jnp.sum(e, axis=0, keepdims=True)
        p_heads.append(e / jnp.where(den > 0.0, den, 1.0))
    o_cmp = _dot(vct_ref[...], jnp.concatenate([p.astype(BF16) for p in p_heads], axis=1))

    rows = 32
    assert n_slc <= rows
    m_row = lax.broadcasted_iota(jnp.int32, (rows, LANE), 0)
    n_col = lax.broadcasted_iota(jnp.int32, (rows, LANE), 1)
    per = SLC_LEN // CMP_STRIDE
    back = (CMP_LEN - 1) // CMP_STRIDE
    overlap = ((n_col >= per * m_row - back) & (n_col <= per * m_row + per - 1)
               & (n_col < n_cmp) & (m_row < n_slc)).astype(BF16)
    imp = sum(_dot(overlap, part) for part in _split3(sum(p_heads)))
    m_blk = lax.broadcasted_iota(jnp.int32, (rows, 1), 0)
    cur = qt // SLC_LEN
    valid = m_blk <= cur
    forced = valid & ((m_blk == 0) | (m_blk >= cur - 1))
    score = jnp.where(forced, FORCED_SCORE, jnp.where(valid, imp, -1.0))
    rank = jnp.zeros((rows, tq), jnp.int32)
    for mp in range(n_slc):
        other = score[mp:mp + 1, :]
        ahead = (other > score) | ((other == score) & (m_blk > mp))
        rank = rank + ahead.astype(jnp.int32)
    selected = valid & (rank < min(SLC_TOPN, n_slc))
    sel_penalty = jnp.where(selected, 0.0, MASKED).astype(BF16)

    SLC, WIN = 0, 1
    _softmax_init(m_ref, acc_ref)
    sub = tk // LANE

    qchunks = [slice(c, c + LANE) for c in range(0, tq, LANE)]

    assert tq == tk

    def slc_masks(kidx, diagonal):
        expand = (kidx // SLC_LEN == lax.broadcasted_iota(jnp.int32, (1, rows), 1)).astype(BF16)
        penalty = _dot(expand, sel_penalty)
        if diagonal:
            return [(penalty[:, c], kidx <= qt[:, c]) for c in qchunks]
        return [(penalty[:, c],) for c in qchunks]

    def win_masks(kidx, diagonal):
        return [(jnp.where(lax.bitcast_convert_type(qt[:, c] - kidx, jnp.uint32) < WINDOW,
                           0.0, MASKED),) for c in qchunks]

    k_refs, vt_refs, masks_of = (ks_ref, kw_ref), (vst_ref, vwt_ref), (slc_masks, win_masks)
    adds = [far_bias[j] for j in range(J) for _ in qchunks]
    last = (t0 + tq + tk - 1) // tk
    first_win = jnp.maximum(t0 - (WINDOW - 1), 0) // tk

    def head_logits(kt, buf, slots, j):
        ks = pl.multiple_of(kt * tk, tk)
        keys = [k_refs[slot][pl.ds(ks, tk), :] for slot in slots]
        s = _dot_nt(keys[0] if len(keys) == 1 else jnp.concatenate(keys, axis=0), qs[head[j]])
        for n, slot in enumerate(slots):
            s_ref[buf, slot, :, head[j]] = s[n * tk:(n + 1) * tk]

    def logits(kt, buf, slots):
        for j in range(J):
            head_logits(kt, buf, slots, j)

    def near_bias(kt, buf, slots):
        ks = pl.multiple_of(kt * tk, tk)
        for hb in range(sub):
            kb = kt * sub + hb
            for qc, qcols in enumerate(qchunks):
                near = ((far_ref[((b * nq + i) * len(qchunks) + qc) * (nk * sub) + kb] == 0)
                        & (kb * LANE < t0 + (qc + 1) * LANE))

                @pl.when(near)
                def _():
                    kpos = pcol_ref[pl.ds(pl.multiple_of(ks + hb * LANE, LANE), LANE), :]
                    idx = jnp.clip(qpos[:, qcols] - kpos, 0, LANE - 1)
                    for j in range(J):
                        delta = _lookup(tbl[j:j + 1, :] - far_bias[j], idx)
                        cols = slice(j * tq + qc * LANE, j * tq + (qc + 1) * LANE)
                        for slot in slots:
                            s_ref[buf, slot, hb * LANE:(hb + 1) * LANE, cols] += delta

    def step(kt, buf, slots, final=False):
        ahead = [] if final else [functools.partial(head_logits, kt + 1, 1 - buf, slots, j)
                                  for j in range(J)]
        share = len(ahead) // len(slots)
        ks = pl.multiple_of(kt * tk, tk)
        kidx = ks + lax.broadcasted_iota(jnp.int32, (tk, 1), 0)
        for n, slot in enumerate(slots):
            _softmax_tile_t(s_ref.at[buf, slot], adds, masks_of[slot](kidx, final) * J,
                            vt_refs[slot][:, pl.ds(ks, tk)], m_ref.at[slot], acc_ref.at[slot],
                            interleave=ahead[n * share:(n + 1) * share])
        if not final:
            near_bias(kt + 1, 1 - buf, slots)

    def slc_body(kt, c):
        _by_parity(kt, lambda buf: step(kt, buf, (SLC,)))
        return c

    def both_body(kt, c):
        _by_parity(kt, lambda buf: step(kt, buf, (SLC, WIN)))
        return c

    def enter_window(buf):
        logits(first_win, buf, (WIN,))
        near_bias(first_win, buf, (WIN,))

    logits(0, 0, (SLC,))
    near_bias(0, 0, (SLC,))
    lax.fori_loop(0, first_win, slc_body, 0)
    _by_parity(first_win, enter_window)
    lax.fori_loop(first_win, last - 1, both_body, 0)
    _by_parity(last - 1, lambda buf: step(last - 1, buf, (SLC, WIN), final=True))
    o_slc = _softmax_out(acc_ref[SLC])
    o_win = _softmax_out(acc_ref[WIN])

    gate_ref[...] = jax.nn.sigmoid(ng_ref[...].astype(F32)).T
    first_gate = pl.program_id(1) * (J * 3)
    for j in range(J):
        gate = [gate_ref[pl.ds(first_gate + 3 * j + c, 1), :] for c in range(3)]
        o = (gate[0] * o_cmp[:, head[j]] + gate[1] * o_slc[:, head[j]]
             + gate[2] * o_win[:, head[j]])
        o_ref[:, j * NSA_DV:(j + 1) * NSA_DV] = o.T.astype(o_ref.dtype)


def _nsa_attn(far, z, kc, vct, v_t, pos_row, pos_col, pos_cmp, tbl, batch, seq, zoff):
    tq, tk = NSA_TQ, NSA_TK
    nq = seq // tq
    G, J = NSA_KV_HEADS, NSA_GROUP
    R = J * tq
    kern = functools.partial(_nsa_attn_kernel, tq=tq, tk=tk, seq=seq)
    grid_spec = pltpu.PrefetchScalarGridSpec(
        num_scalar_prefetch=1,
        grid=(batch, G, nq),
        in_specs=[
            pl.BlockSpec((tq, J * HEAD_PAD), lambda b, g, i, far: (b * nq + i, g)),
            pl.BlockSpec((None, LANE, HEAD_PAD), lambda b, g, i, far: (b * G + g, 0, 0)),
            pl.BlockSpec((None, NSA_DV, LANE), lambda b, g, i, far: (b * G + g, 0, 0)),
            pl.BlockSpec((seq, HEAD_PAD), lambda b, g, i, far: (b, zoff["ks"] // HEAD_PAD + g)),
            pl.BlockSpec((NSA_DV, seq), lambda b, g, i, far: (g, b)),
            pl.BlockSpec((seq, HEAD_PAD), lambda b, g, i, far: (b, zoff["kw"] // HEAD_PAD + g)),
            pl.BlockSpec((NSA_DV, seq), lambda b, g, i, far: (G + g, b)),
            pl.BlockSpec((tq, LANE), lambda b, g, i, far: (b * nq + i, zoff["ng"] // LANE)),
            pl.BlockSpec((None, 1, seq), lambda b, g, i, far: (b, 0, 0)),
            pl.BlockSpec((seq, 1), lambda b, g, i, far: (b, 0)),
            pl.BlockSpec((LANE, 1), lambda b, g, i, far: (b, 0)),
            pl.BlockSpec((None, J, LANE), lambda b, g, i, far: (g, 0, 0)),
        ],
        out_specs=pl.BlockSpec((tq, J * NSA_DV), lambda b, g, i, far: (b * nq + i, g)),
        scratch_shapes=[pltpu.VMEM((2, 2, tk, R), F32), pltpu.VMEM((2, 1, R), F32),
                        pltpu.VMEM((2, NSA_DV + ONES_ROWS, R), F32),
                        pltpu.VMEM((LANE, tq), F32)],
    )
    return pl.pallas_call(
        kern,
        grid_spec=grid_spec,
        out_shape=jax.ShapeDtypeStruct((batch * seq, NSA_HEADS * NSA_DV), BF16),
        compiler_params=_params("parallel", "parallel", "parallel"),
        name="nsa_attn",
    )(far, z, kc, vct, z, v_t, z, v_t, z, pos_row, pos_col, pos_cmp, tbl)


def _merge_kernel(a_ref, b_ref, ga_ref, gb_ref, h_ref, wa_ref, wb_ref, wo_ref, post_g_ref,
                  next_g_ref, o_ref, un_ref):
    ya = _dot(a_ref[...], wa_ref[...])
    yb = _dot(b_ref[...], wb_ref[...])
    m = (jax.nn.sigmoid(ga_ref[...].astype(F32)) * ya
         + jax.nn.sigmoid(gb_ref[...].astype(F32)) * yb).astype(BF16)
    y = _dot(m, wo_ref[...])
    out = h_ref[...] + _rms(y, post_g_ref[...])
    o_ref[...] = out
    un_ref[...] = _rms(out, next_g_ref[...]).astype(BF16)


def _merge(a, bb, z, h, wa, wb, wo, post_g, next_g, layer, zoff):
    m, d = h.shape
    tm = MERGE_TM
    da, db = a.shape[1], bb.shape[1]
    gblk = zoff["mg"] // d
    const = lambda i: (layer, 0, 0)
    return pl.pallas_call(
        _merge_kernel,
        grid=(m // tm,),
        in_specs=[pl.BlockSpec((tm, da), lambda i: (i, 0)),
                  pl.BlockSpec((tm, db), lambda i: (i, 0)),
                  pl.BlockSpec((tm, d), lambda i: (i, gblk)),
                  pl.BlockSpec((tm, d), lambda i: (i, gblk + 1)),
                  pl.BlockSpec((tm, d), lambda i: (i, 0)),
                  pl.BlockSpec((None, da, d), const, pipeline_mode=pl.Buffered(1)),
                  pl.BlockSpec((None, db, d), const, pipeline_mode=pl.Buffered(1)),
                  pl.BlockSpec((None, d, d), const, pipeline_mode=pl.Buffered(1)),
                  pl.BlockSpec((None, 1, d), const),
                  pl.BlockSpec((None, 1, d), const)],
        out_specs=[pl.BlockSpec((tm, d), lambda i: (i, 0)),
                   pl.BlockSpec((tm, d), lambda i: (i, 0))],
        out_shape=[jax.ShapeDtypeStruct((m, d), F32), jax.ShapeDtypeStruct((m, d), BF16)],
        compiler_params=_params("parallel"),
        name="merge_out",
    )(a, bb, z, z, h, wa, wb, wo, post_g, next_g)


def _z_layout(d_model):
    G = NSA_KV_HEADS
    widths = [("q", NSA_HEADS * HEAD_PAD), ("mg", 2 * d_model), ("cq", MLA_Q_LORA),
              ("ckv", MLA_KV_LORA), ("kc", G * HEAD_PAD), ("ks", G * HEAD_PAD),
              ("kw", G * HEAD_PAD), ("vc", G * NSA_DV), ("kr", LANE), ("ng", LANE)]
    off, pos = {}, 0
    for name, w in widths:
        off[name] = pos
        pos += w
    off["total"] = pos
    assert off["q"] == 0 and off["mg"] % d_model == 0
    assert off["cq"] % MLA_Q_LORA == 0 and off["ckv"] % MLA_KV_LORA == 0
    assert all(off[k] % HEAD_PAD == 0 for k in ("kc", "ks", "kw"))
    return off


def _swap_halves(w):
    half = w.shape[-1] // 2
    return jnp.concatenate([-w[..., half:], w[..., :half]], axis=-1)


def _pad_last(w, width):
    return jnp.pad(w, [(0, 0)] * (w.ndim - 1) + [(0, width - w.shape[-1])])


def _w_in_pieces(d_model, zoff):
    G, J = NSA_KV_HEADS, NSA_GROUP
    splits = [MLA_Q_LORA, MLA_KV_LORA, MLA_ROPE, NSA_HEADS * NSA_DK,
              G * NSA_DK, G * NSA_DV, G * NSA_DK, G * NSA_DV, G * NSA_DK, G * NSA_DV,
              NSA_HEADS * 3, 2 * d_model]
    names = ["cq", "ckv", "kr", "q", "kc", "vc", "ks", "vs", "kw", "vw", "ng", "mg"]
    src = dict(zip(names, [0] + [int(v) for v in np.cumsum(splits)[:-1]]))
    half = MLA_ROPE // 2
    pieces = []
    for h in range(NSA_HEADS):
        pieces.append((zoff["q"] + h * HEAD_PAD, HEAD_PAD,
                       [(src["q"] + h * NSA_DK, NSA_DK, NSA_DK ** -0.5 * LOG2E)]))
    pieces.append((zoff["mg"], 2 * d_model, [(src["mg"], 2 * d_model, None)]))
    pieces.append((zoff["cq"], MLA_Q_LORA, [(src["cq"], MLA_Q_LORA, None)]))
    pieces.append((zoff["ckv"], MLA_KV_LORA, [(src["ckv"], MLA_KV_LORA, None)]))
    for name in ("kc", "ks", "kw"):
        for g in range(G):
            pieces.append((zoff[name] + g * HEAD_PAD, HEAD_PAD, [(src[name] + g * NSA_DK, NSA_DK, None)]))
    pieces.append((zoff["vc"], G * NSA_DV, [(src["vc"], G * NSA_DV, None)]))
    pieces.append((zoff["kr"], LANE, [(src["kr"], MLA_ROPE, None), (src["kr"] + half, half, -1.0),
                                      (src["kr"], half, None)]))
    pieces.append((zoff["ng"], LANE, [(src["ng"], NSA_HEADS * 3, None)]))
    return pieces, (src["vs"], src["vw"])


COPY_ROWS = 512


def _copy_rows(w_ref, o_ref, src, dst, n, scale):
    for off in range(0, n, COPY_ROWS):
        m = min(COPY_ROWS, n - off)
        x = w_ref[src + off:src + off + m, :]
        o_ref[dst + off:dst + off + m, :] = (x if scale is None else x * scale).astype(BF16)


def _w_in_layout_kernel(w_ref, o_ref, vt_ref, *, pieces, value_rows):
    tc = w_ref.shape[1]
    dst = 0
    for src, n in value_rows:
        _copy_rows(w_ref, vt_ref, src, dst, n, None)
        dst += n
    for dst, width, parts in pieces:
        used = sum(n for _, n, _ in parts)
        if all(n % ONES_ROWS == 0 for _, n, _ in parts):
            at = dst
            for src, n, scale in parts:
                _copy_rows(w_ref, o_ref, src, at, n, scale)
                at += n
            if used < width:
                o_ref[at:dst + width, :] = jnp.zeros((width - used, tc), BF16)
        else:
            rows = [w_ref[src:src + n, :] if scale is None else w_ref[src:src + n, :] * scale
                    for src, n, scale in parts]
            rows.append(jnp.zeros((width - used, tc), F32))
            o_ref[dst:dst + width, :] = jnp.concatenate(rows, axis=0).astype(BF16)


def _layout_w_in(w_in, zoff):
    L, D, n_in = w_in.shape
    G = NSA_KV_HEADS
    pieces, (vs, vw) = _w_in_pieces(D, zoff)
    tc = 256
    nv = 2 * G * NSA_DV
    value_rows = ((vs, G * NSA_DV), (vw, G * NSA_DV))
    return pl.pallas_call(
        functools.partial(_w_in_layout_kernel, pieces=pieces, value_rows=value_rows),
        grid=(L, D // tc),
        in_specs=[pl.BlockSpec((None, n_in, tc), lambda l, c: (l, 0, c))],
        out_specs=[pl.BlockSpec((None, zoff["total"], tc), lambda l, c: (l, 0, c)),
                   pl.BlockSpec((None, nv, tc), lambda l, c: (l, 0, c))],
        out_shape=[jax.ShapeDtypeStruct((L, zoff["total"], D), BF16),
                   jax.ShapeDtypeStruct((L, nv, D), BF16)],
        compiler_params=_params("parallel", "parallel"),
        name="w_in_layout",
    )(jnp.swapaxes(w_in, 1, 2))


def _layout_w_q_up(w):
    L, r, _ = w.shape
    w = w.reshape(L, r, MLA_HEADS, MLA_NOPE + MLA_ROPE) * ((MLA_NOPE + MLA_ROPE) ** -0.5 * LOG2E)
    rope = w[..., MLA_NOPE:]
    return jnp.concatenate([w[..., :MLA_NOPE], rope, _swap_halves(rope)], axis=-1).reshape(
        L, r, MLA_HEADS * HEAD_PAD).astype(BF16)


def kernel(x, positions, rel_bias, ffn1_pre_g, ffn1_post_g, ffn1_w_gate, ffn1_w_up, ffn1_w_down, mix_pre_g, mix_post_g, w_in, mla_q_norm_g, mla_w_q_up, mla_kv_norm_g, mla_w_uk, mla_w_uv, cmp_pe_k, cmp_w1_k, cmp_w2_k, cmp_pe_v, cmp_w1_v, cmp_w2_v, w_branch_mla, w_branch_nsa, w_out, ffn2_pre_g, ffn2_post_g, ffn2_w_gate, ffn2_w_up, ffn2_w_down):
    B, S, D = x.shape
    L = w_in.shape[0]
    M = B * S
    G, J = NSA_KV_HEADS, NSA_GROUP
    zoff = _z_layout(D)
    n_half = S // CMP_STRIDE
    n_cmp = (S - CMP_LEN) // CMP_STRIDE + 1
    assert n_half == LANE and n_cmp <= LANE

    gain = lambda g: g.reshape(L, 1, -1)
    bf = lambda w: w.astype(BF16)
    w_z, w_vt = _layout_w_in(w_in, zoff)
    w_q = _layout_w_q_up(mla_w_q_up)
    w_uk, w_uv_t = bf(mla_w_uk), bf(mla_w_uv).transpose(0, 2, 1)
    pe_k = _pad_last(cmp_pe_k, HEAD_PAD)
    w1_k = bf(jnp.pad(cmp_w1_k.reshape(L, CMP_LEN, NSA_DK, CMP_HIDDEN),
                      ((0, 0), (0, 0), (0, HEAD_PAD - NSA_DK), (0, 0)))
              ).reshape(L, CMP_LEN * HEAD_PAD, CMP_HIDDEN)
    w2_k = bf(_pad_last(cmp_w2_k, HEAD_PAD))
    f1 = (bf(ffn1_w_gate), bf(ffn1_w_up), ffn1_w_down)
    f2 = (bf(ffn2_w_gate), bf(ffn2_w_up), ffn2_w_down)
    w1_v, w2_v = bf(cmp_w1_v), bf(cmp_w2_v)
    w_a, w_b, w_o = bf(w_branch_mla), bf(w_branch_nsa), bf(w_out)
    g_f1pre, g_f1post, g_mpre, g_mpost = gain(ffn1_pre_g), gain(ffn1_post_g), gain(mix_pre_g), gain(mix_post_g)
    g_f2pre, g_f2post, g_q, g_kv = gain(ffn2_pre_g), gain(ffn2_post_g), gain(mla_q_norm_g), gain(mla_kv_norm_g)

    pos_col = positions.reshape(M, 1)
    pos_row = positions.reshape(B, 1, S)
    pos_cmp = _pad_last(positions[:, CMP_LEN - 1::CMP_STRIDE][:, :n_cmp], LANE).reshape(B * LANE, 1)
    q_min = positions.reshape(B, S // LANE, LANE).min(axis=-1)
    k_max = positions.reshape(B, S // LANE, LANE).max(axis=-1)
    far = (q_min[:, :, None] - k_max[:, None, :] >= FAR_DIST).astype(jnp.int32).reshape(-1)
    tbl = (jnp.take(rel_bias, jnp.asarray(BUCKET_OF_DIST), axis=0).T * LOG2E).reshape(G, J, LANE)
    half = MLA_ROPE // 2
    inv = ROPE_BASE ** (-jnp.arange(half, dtype=F32) * 2.0 / MLA_ROPE)
    cs = _rope_table(pos_col, jnp.concatenate([inv, inv]).reshape(1, MLA_ROPE))

    h = x.reshape(M, D)
    u = _rmsnorm(h, g_f1pre, 0)
    for l in range(L):
        h, u = _ffn(h, u, 0.5 * g_f1post, g_mpre, l, *f1, l)
        z = _in_proj(u, w_z, l)
        v_t = _in_proj_t(u, w_vt, l)
        qp, kp, vt = _mla_prep(z, cs, g_q, g_kv, w_q, w_uk, w_uv_t, l, zoff)
        a = _mla_attn(qp, kp, vt, B, S)

        kc, vct = _nsa_cmp(z, pe_k, cmp_pe_v, w1_k, w2_k, w1_v, w2_v, l, B, S, zoff)
        nsa = _nsa_attn(far, z, kc, vct, v_t, pos_row, pos_col, pos_cmp, tbl, B, S, zoff)
        h, u = _merge(a, nsa, z, h, w_a, w_b, w_o, g_mpost, g_f2pre, l, zoff)
        h, u = _ffn(h, u, 0.5 * g_f2post, g_f1pre, min(l + 1, L - 1), *f2, l)
    return h.reshape(B, S, D)
```

```python
import functools
import math

import numpy as np
import jax
import jax.numpy as jnp
from jax import lax
from jax.experimental import pallas as pl
from jax.experimental.pallas import tpu as pltpu

EPS = 1e-6
MLA_HEADS = 8
MLA_Q_LORA = 512
MLA_KV_LORA = 512
MLA_NOPE = 128
MLA_ROPE = 64
MLA_V = 128
ROPE_BASE = 10000.0
NSA_HEADS = 8
NSA_KV_HEADS = 2
NSA_GROUP = NSA_HEADS // NSA_KV_HEADS
NSA_DK = 192
NSA_DV = 128
CMP_LEN = 32
CMP_STRIDE = 16
CMP_HIDDEN = 256
SLC_LEN = 64
SLC_TOPN = 16
WINDOW = 512
FORCED_SCORE = 1e6
REL_BUCKETS = 32
REL_MAX_DIST = 128
NEG = -1e30
MASKED = 2 * NEG
LOG2E = 1.0 / math.log(2.0)

LANE = 128
ONES_ROWS = 16
HEAD_PAD = 256
VMEM_LIMIT = 56 * 1024 * 1024
BF16 = jnp.bfloat16
F32 = jnp.float32

FFN_TM, FFN_TF = 512, 512
PROJ_TM = 1024
PROJ_TN_CAP = 2560
PREP_TM = 512
MLA_TQ, MLA_TK = 512, 512
MLA_HEADS_PER_STEP = 8
NSA_TQ, NSA_TK = 256, 256
MERGE_TM = 512


def _bucket_of_distance():
    n = np.arange(LANE)
    max_exact = REL_BUCKETS // 2
    large = max_exact + (np.log(np.maximum(n, 1) / max_exact) / math.log(REL_MAX_DIST / max_exact)
                         * (REL_BUCKETS - max_exact)).astype(np.int32)
    bucket = np.where(n < max_exact, n, np.minimum(large, REL_BUCKETS - 1)).astype(np.int32)
    assert bucket[-1] == REL_BUCKETS - 1
    return bucket


BUCKET_OF_DIST = _bucket_of_distance()
FAR_DIST = int(np.max(np.nonzero(BUCKET_OF_DIST != REL_BUCKETS - 1)[0])) + 1


def _params(*sem):
    return pltpu.CompilerParams(dimension_semantics=sem, vmem_limit_bytes=VMEM_LIMIT)


def _rms(x, g):
    return x * lax.rsqrt(jnp.mean(x * x, axis=-1, keepdims=True) + EPS) * g


def _dot(a, b):
    return jnp.dot(a, b, preferred_element_type=F32)


def _dot_nt(a, b):
    return lax.dot_general(a, b, (((1,), (1,)), ((), ())), preferred_element_type=F32)


def _tile_n(n, cap):
    best = LANE
    for t in range(LANE, cap + 1, LANE):
        if n % t == 0:
            best = t
    return best


def _rmsnorm_kernel(x_ref, g_ref, o_ref):
    o_ref[...] = _rms(x_ref[...], g_ref[...]).astype(o_ref.dtype)


def _rmsnorm(x, g, layer):
    m, d = x.shape
    tm = 512
    return pl.pallas_call(
        _rmsnorm_kernel,
        grid=(m // tm,),
        in_specs=[pl.BlockSpec((tm, d), lambda i: (i, 0)),
                  pl.BlockSpec((None, 1, d), lambda i: (layer, 0, 0))],
        out_specs=pl.BlockSpec((tm, d), lambda i: (i, 0)),
        out_shape=jax.ShapeDtypeStruct((m, d), BF16),
        compiler_params=_params("parallel"),
        name="rmsnorm",
    )(x, g)


def _rope_table_kernel(pos_ref, inv_ref, o_ref):
    ang = pos_ref[...].astype(F32) * inv_ref[...]
    o_ref[...] = jnp.concatenate([jnp.cos(ang), jnp.sin(ang)], axis=1)


def _rope_table(pos_col, inv):
    m = pos_col.shape[0]
    tm = 512
    return pl.pallas_call(
        _rope_table_kernel,
        grid=(m // tm,),
        in_specs=[pl.BlockSpec((tm, 1), lambda i: (i, 0)),
                  pl.BlockSpec((1, MLA_ROPE), lambda i: (0, 0))],
        out_specs=pl.BlockSpec((tm, 2 * MLA_ROPE), lambda i: (i, 0)),
        out_shape=jax.ShapeDtypeStruct((m, 2 * MLA_ROPE), F32),
        compiler_params=_params("parallel"),
        name="rope_table",
    )(pos_col, inv)


def _ffn_kernel(x_ref, u_ref, post_g_ref, next_g_ref, wg_ref, wu_ref, wd_ref,
                o_ref, un_ref, acc_ref):
    j = pl.program_id(1)

    @pl.when(j == 0)
    def _():
        acc_ref[...] = jnp.zeros(acc_ref.shape, F32)

    u = u_ref[...]
    gate = _dot(u, wg_ref[...])
    up = _dot(u, wu_ref[...])
    hidden = (gate * jax.nn.sigmoid(gate) * up).astype(BF16)
    acc_ref[...] += _dot(hidden, wd_ref[...].astype(BF16))

    @pl.when(j == pl.num_programs(1) - 1)
    def _():
        out = x_ref[...] + _rms(acc_ref[...], post_g_ref[...])
        o_ref[...] = out
        un_ref[...] = _rms(out, next_g_ref[...]).astype(BF16)


def _ffn(x, u, post_g, next_g, next_layer, wg, wu, wd, layer):
    m, d = x.shape
    f = wg.shape[-1]
    tm, tf = FFN_TM, _tile_n(f, FFN_TF)
    return pl.pallas_call(
        _ffn_kernel,
        grid=(m // tm, f // tf),
        in_specs=[pl.BlockSpec((tm, d), lambda i, j: (i, 0)),
                  pl.BlockSpec((tm, d), lambda i, j: (i, 0)),
                  pl.BlockSpec((None, 1, d), lambda i, j: (layer, 0, 0)),
                  pl.BlockSpec((None, 1, d), lambda i, j: (next_layer, 0, 0)),
                  pl.BlockSpec((None, d, tf), lambda i, j: (layer, 0, j)),
                  pl.BlockSpec((None, d, tf), lambda i, j: (layer, 0, j)),
                  pl.BlockSpec((None, tf, d), lambda i, j: (layer, j, 0))],
        out_specs=[pl.BlockSpec((tm, d), lambda i, j: (i, 0)),
                   pl.BlockSpec((tm, d), lambda i, j: (i, 0))],
        out_shape=[jax.ShapeDtypeStruct((m, d), F32), jax.ShapeDtypeStruct((m, d), BF16)],
        scratch_shapes=[pltpu.VMEM((tm, d), F32)],
        compiler_params=_params("parallel", "arbitrary"),
        name="ffn",
    )(x, u, post_g, next_g, wg, wu, wd)


def _matmul_nt_kernel(x_ref, w_ref, o_ref):
    o_ref[...] = _dot_nt(x_ref[...], w_ref[...]).astype(o_ref.dtype)


def _in_proj(u, w_t, layer):
    m, d = u.shape
    n = w_t.shape[1]
    tm, tn = PROJ_TM, _tile_n(n, PROJ_TN_CAP)
    return pl.pallas_call(
        _matmul_nt_kernel,
        grid=(n // tn, m // tm),
        in_specs=[pl.BlockSpec((tm, d), lambda j, i: (i, 0)),
                  pl.BlockSpec((None, tn, d), lambda j, i: (layer, j, 0))],
        out_specs=pl.BlockSpec((tm, tn), lambda j, i: (i, j)),
        out_shape=jax.ShapeDtypeStruct((m, n), BF16),
        compiler_params=_params("parallel", "parallel"),
        name="in_proj",
    )(u, w_t)


def _matmul_t_kernel(w_ref, x_ref, o_ref):
    o_ref[...] = _dot_nt(w_ref[...], x_ref[...]).astype(o_ref.dtype)


def _in_proj_t(u, w_t, layer):
    m, d = u.shape
    n = w_t.shape[1]
    tm = PROJ_TM
    return pl.pallas_call(
        _matmul_t_kernel,
        grid=(m // tm,),
        in_specs=[pl.BlockSpec((None, n, d), lambda i: (layer, 0, 0)),
                  pl.BlockSpec((tm, d), lambda i: (i, 0))],
        out_specs=pl.BlockSpec((n, tm), lambda i: (0, i)),
        out_shape=jax.ShapeDtypeStruct((n, m), BF16),
        compiler_params=_params("parallel"),
        name="in_proj_t",
    )(w_t, u)


def _softmax_tile_t(s_ref, adds, masks, v_t, m_ref, acc_ref, interleave=()):
    probs, alphas = [], []
    n_chunks = s_ref.shape[1] // LANE
    for c in range(n_chunks):
        if interleave and c % (n_chunks // len(interleave)) == 0:
            interleave[c // (n_chunks // len(interleave))]()
        cols = slice(c * LANE, (c + 1) * LANE)
        s = s_ref[:, cols]
        for mask in masks[c]:
            s = jnp.where(mask, s, MASKED) if mask.dtype == jnp.bool_ else s + mask
        m_old = m_ref[:, cols]
        m_tile = jnp.max(s, axis=0, keepdims=True)
        if adds[c] is not None:
            m_tile = m_tile + adds[c]
        m_new = jnp.maximum(m_old, m_tile)
        alpha = jnp.exp2(m_old - m_new)
        p = jnp.exp2(s - (m_new if adds[c] is None else m_new - adds[c]))
        m_ref[:, cols] = m_new
        probs.append(p.astype(BF16))
        alphas.append(alpha)
    p_t = probs[0] if len(probs) == 1 else jnp.concatenate(probs, axis=1)
    alpha = alphas[0] if len(alphas) == 1 else jnp.concatenate(alphas, axis=1)
    v_ones = jnp.concatenate([v_t, jnp.ones((ONES_ROWS, v_t.shape[1]), BF16)], axis=0)
    acc_ref[...] = alpha * acc_ref[...] + _dot(v_ones, p_t)


def _softmax_init(m_ref, acc_ref):
    m_ref[...] = jnp.full(m_ref.shape, NEG, F32)
    acc_ref[...] = jnp.zeros(acc_ref.shape, F32)


def _softmax_out(acc):
    dv = acc.shape[0] - ONES_ROWS
    return acc[:dv] / acc[dv:dv + 1]


def _mla_prep_kernel(cq_ref, ckv_ref, kr_ref, cs_ref, gq_ref, gkv_ref, wq_ref, wuk_ref, wuvt_ref,
                     q_ref, k_ref, vt_ref):
    cs = cs_ref[...]
    qn = _rms(cq_ref[...].astype(F32), gq_ref[...]).astype(BF16)
    q = _dot(qn, wq_ref[...])
    kvn = _rms(ckv_ref[...].astype(F32), gkv_ref[...]).astype(BF16)
    k_nope = _dot(kvn, wuk_ref[...])
    t = kr_ref[...].astype(F32) * cs
    k_pe = (t + pltpu.roll(t, MLA_ROPE, 1)).astype(BF16)
    for h in range(MLA_HEADS):
        lo = h * HEAD_PAD
        q_ref[:, lo:lo + MLA_NOPE] = q[:, lo:lo + MLA_NOPE].astype(BF16)
        q_ref[:, lo + MLA_NOPE:lo + HEAD_PAD] = (q[:, lo + MLA_NOPE:lo + HEAD_PAD] * cs).astype(BF16)
        k_ref[:, lo:lo + MLA_NOPE] = k_nope[:, h * MLA_NOPE:(h + 1) * MLA_NOPE].astype(BF16)
        k_ref[:, lo + MLA_NOPE:lo + HEAD_PAD] = k_pe
    vt_ref[...] = _dot_nt(wuvt_ref[...], kvn).astype(BF16)


def _mla_prep(z, cs, gq, gkv, wq, wuk, wuv_t, layer, zoff):
    m = z.shape[0]
    tm = PREP_TM
    hq = MLA_HEADS * HEAD_PAD
    hv = MLA_HEADS * MLA_V
    const = lambda i: (layer, 0, 0)
    return pl.pallas_call(
        _mla_prep_kernel,
        grid=(m // tm,),
        in_specs=[pl.BlockSpec((tm, MLA_Q_LORA), lambda i: (i, zoff["cq"] // MLA_Q_LORA)),
                  pl.BlockSpec((tm, MLA_KV_LORA), lambda i: (i, zoff["ckv"] // MLA_KV_LORA)),
                  pl.BlockSpec((tm, LANE), lambda i: (i, zoff["kr"] // LANE)),
                  pl.BlockSpec((tm, LANE), lambda i: (i, 0)),
                  pl.BlockSpec((None, 1, MLA_Q_LORA), const),
                  pl.BlockSpec((None, 1, MLA_KV_LORA), const),
                  pl.BlockSpec((None, MLA_Q_LORA, hq), const),
                  pl.BlockSpec((None, MLA_KV_LORA, MLA_HEADS * MLA_NOPE), const),
                  pl.BlockSpec((None, hv, MLA_KV_LORA), const)],
        out_specs=[pl.BlockSpec((tm, hq), lambda i: (i, 0)),
                   pl.BlockSpec((tm, hq), lambda i: (i, 0)),
                   pl.BlockSpec((hv, tm), lambda i: (0, i))],
        out_shape=[jax.ShapeDtypeStruct((m, hq), BF16),
                   jax.ShapeDtypeStruct((m, hq), BF16),
                   jax.ShapeDtypeStruct((hv, m), BF16)],
        compiler_params=_params("parallel"),
        name="mla_prep",
    )(z, z, z, cs, gq, gkv, wq, wuk, wuv_t)


def _by_parity(kt, fn):
    @pl.when(kt % 2 == 0)
    def _():
        fn(0)

    @pl.when(kt % 2 == 1)
    def _():
        fn(1)


def _mla_attn_kernel(q_ref, k_ref, vt_ref, o_ref, s_ref, m_ref, acc_ref, *, tq, tk, heads):
    i = pl.program_id(2)
    nc = tq // LANE
    _softmax_init(m_ref, acc_ref)
    qt = i * tq + lax.broadcasted_iota(jnp.int32, (1, tq), 1)
    kcol = lax.broadcasted_iota(jnp.int32, (tk, 1), 0)
    qk = [slice(h * HEAD_PAD, (h + 1) * HEAD_PAD) for h in range(heads)]
    vd = [slice(h * MLA_V, (h + 1) * MLA_V) for h in range(heads)]
    n_full = (i * tq) // tk
    n_all = (i * tq + tq + tk - 1) // tk

    def head_logits(kt, buf, h):
        ks = pl.multiple_of(kt * tk, tk)
        s_ref[buf, h] = _dot_nt(k_ref[pl.ds(ks, tk), qk[h]], q_ref[:, qk[h]])

    def logits(kt, buf):
        for h in range(heads):
            head_logits(kt, buf, h)

    def step(kt, buf, masked, prefetch=True):
        ks = pl.multiple_of(kt * tk, tk)
        masks = [()] * nc
        if masked:
            masks = [(ks + kcol <= qt[:, c * LANE:(c + 1) * LANE],) for c in range(nc)]
        for h in range(heads):
            if prefetch:
                head_logits(kt + 1, 1 - buf, h)
            _softmax_tile_t(s_ref.at[buf, h], [None] * nc, masks, vt_ref[vd[h], pl.ds(ks, tk)],
                            m_ref.at[h], acc_ref.at[h])

    def full_body(kt, c):
        _by_parity(kt, lambda buf: step(kt, buf, False))
        return c

    def diag_body(kt, c):
        _by_parity(kt, lambda buf: step(kt, buf, True))
        return c

    logits(0, 0)
    lax.fori_loop(0, n_full, full_body, 0)
    lax.fori_loop(n_full, n_all - 1, diag_body, 0)
    _by_parity(n_all - 1, lambda buf: step(n_all - 1, buf, True, prefetch=False))
    for h in range(heads):
        o_ref[:, vd[h]] = _softmax_out(acc_ref[h]).T.astype(o_ref.dtype)


def _mla_attn(qp, kp, vt, batch, seq):
    tq, tk, heads = MLA_TQ, MLA_TK, MLA_HEADS_PER_STEP
    nq = seq // tq
    kern = functools.partial(_mla_attn_kernel, tq=tq, tk=tk, heads=heads)
    return pl.pallas_call(
        kern,
        grid=(batch, MLA_HEADS // heads, nq),
        in_specs=[pl.BlockSpec((tq, heads * HEAD_PAD), lambda b, h, i: (b * nq + i, h)),
                  pl.BlockSpec((seq, heads * HEAD_PAD), lambda b, h, i: (b, h)),
                  pl.BlockSpec((heads * MLA_V, seq), lambda b, h, i: (h, b))],
        out_specs=pl.BlockSpec((tq, heads * MLA_V), lambda b, h, i: (b * nq + i, h)),
        out_shape=jax.ShapeDtypeStruct((batch * seq, MLA_HEADS * MLA_V), BF16),
        scratch_shapes=[pltpu.VMEM((2, heads, tk, tq), F32), pltpu.VMEM((heads, 1, tq), F32),
                        pltpu.VMEM((heads, MLA_V + ONES_ROWS, tq), F32)],
        compiler_params=_params("parallel", "parallel", "parallel"),
        name="mla_attn",
    )(qp, kp, vt)


def _compress(z_ref, zf_ref, pe_ref, w1_ref, w2_ref):
    chunks, s, _ = zf_ref.shape
    for c in range(chunks):
        zf_ref[c] = z_ref[:, c * LANE:(c + 1) * LANE].astype(F32)
    d = chunks * LANE
    n = s // CMP_STRIDE
    first = jnp.zeros((n, CMP_HIDDEN), F32)
    second = jnp.zeros((n, CMP_HIDDEN), F32)
    for l in range(CMP_STRIDE):
        toks = [zf_ref[c, pl.ds(l, n, stride=CMP_STRIDE), :] for c in range(chunks)]
        tok = toks[0] if chunks == 1 else jnp.concatenate(toks, axis=1)
        lo, hi = l, CMP_STRIDE + l
        first += _dot((tok + pe_ref[lo:lo + 1, :]).astype(BF16), w1_ref[lo * d:(lo + 1) * d, :])
        second += _dot((tok + pe_ref[hi:hi + 1, :]).astype(BF16), w1_ref[hi * d:(hi + 1) * d, :])
    hidden = first + pltpu.roll(second, n - 1, 0)
    return _dot((hidden * jax.nn.sigmoid(hidden)).astype(BF16), w2_ref[...])


def _nsa_cmp_kernel(zk_ref, zv_ref, pek_ref, pev_ref, w1k_ref, w2k_ref, w1v_ref, w2v_ref,
                    kc_ref, vct_ref, kf_ref, vf_ref):
    kc_ref[...] = _compress(zk_ref, kf_ref, pek_ref, w1k_ref, w2k_ref).astype(BF16)
    vct_ref[...] = _compress(zv_ref, vf_ref, pev_ref, w1v_ref, w2v_ref).T.astype(BF16)


def _nsa_cmp(z, pek, pev, w1k, w2k, w1v, w2v, layer, batch, seq, zoff):
    G = NSA_KV_HEADS
    nh = seq // CMP_STRIDE
    const = lambda i: (layer, 0, 0)
    return pl.pallas_call(
        _nsa_cmp_kernel,
        grid=(batch * G,),
        in_specs=[pl.BlockSpec((seq, HEAD_PAD), lambda i: (i // G, zoff["kc"] // HEAD_PAD + i % G)),
                  pl.BlockSpec((seq, NSA_DV), lambda i: (i // G, zoff["vc"] // NSA_DV + i % G)),
                  pl.BlockSpec((None, CMP_LEN, HEAD_PAD), const),
                  pl.BlockSpec((None, CMP_LEN, NSA_DV), const),
                  pl.BlockSpec((None, CMP_LEN * HEAD_PAD, CMP_HIDDEN), const),
                  pl.BlockSpec((None, CMP_HIDDEN, HEAD_PAD), const),
                  pl.BlockSpec((None, CMP_LEN * NSA_DV, CMP_HIDDEN), const),
                  pl.BlockSpec((None, CMP_HIDDEN, NSA_DV), const)],
        out_specs=[pl.BlockSpec((None, nh, HEAD_PAD), lambda i: (i, 0, 0)),
                   pl.BlockSpec((None, NSA_DV, nh), lambda i: (i, 0, 0))],
        out_shape=[jax.ShapeDtypeStruct((batch * G, nh, HEAD_PAD), BF16),
                   jax.ShapeDtypeStruct((batch * G, NSA_DV, nh), BF16)],
        scratch_shapes=[pltpu.VMEM((HEAD_PAD // LANE, seq, LANE), F32),
                        pltpu.VMEM((NSA_DV // LANE, seq, LANE), F32)],
        compiler_params=_params("parallel"),
        name="nsa_cmp",
    )(z, z, pek, pev, w1k, w2k, w1v, w2v)


def _split3(x):
    a = x.astype(BF16)
    r = x - a.astype(F32)
    b = r.astype(BF16)
    c = (r - b.astype(F32)).astype(BF16)
    return a, b, c


def _lookup(table_row, idx):
    rows, width = idx.shape
    table = jnp.broadcast_to(table_row, (rows, LANE))
    chunks = [jnp.take_along_axis(table, idx[:, c:c + LANE], axis=1, mode="promise_in_bounds")
              for c in range(0, width, LANE)]
    return chunks[0] if len(chunks) == 1 else jnp.concatenate(chunks, axis=1)


def _nsa_attn_kernel(far_ref, q_ref, kc_ref, vct_ref, ks_ref, vst_ref, kw_ref, vwt_ref, ng_ref,
                     prow_ref, pcol_ref, pcmp_ref, tbl_ref, o_ref,
                     s_ref, m_ref, acc_ref, gate_ref, *, tq, tk, seq):
    b, i = pl.program_id(0), pl.program_id(2)
    nq, nk = seq // tq, seq // tk
    J = NSA_GROUP
    n_slc = seq // SLC_LEN
    n_cmp = (seq - CMP_LEN) // CMP_STRIDE + 1
    t0 = pl.multiple_of(i * tq, tq)

    def q_head(j):
        return q_ref[:, j * HEAD_PAD:(j + 1) * HEAD_PAD]

    qpos = prow_ref[:, pl.ds(t0, tq)]
    qt = t0 + lax.broadcasted_iota(jnp.int32, (1, tq), 1)
    tbl = tbl_ref[...]
    far_bias = [tbl[j:j + 1, LANE - 1:LANE] for j in range(J)]
    head = [slice(j * tq, (j + 1) * tq) for j in range(J)]

    blk = lax.broadcasted_iota(jnp.int32, (LANE, 1), 0)
    keep_c = (blk * CMP_STRIDE + (CMP_LEN - 1) <= qt) & (blk < n_cmp)
    idx_c = jnp.clip(qpos - pcmp_ref[...], 0, LANE - 1)
    p_heads = []
    for j in range(J):
        s_c = _dot_nt(kc_ref[...], q_head(j))
        s = jnp.where(keep_c, s_c + _lookup(tbl[j:j + 1, :], idx_c), NEG)
        e = jnp.where(keep_c, jnp.exp2(s - jnp.max(s, axis=0, keepdims=True)), 0.0)
        den = jnp.sum(e, axis=0, keepdims=True)
        p_heads.append(e / jnp.where(den > 0.0, den, 1.0))
    o_cmp = _dot(vct_ref[...], jnp.concatenate([p.astype(BF16) for p in p_heads], axis=1))

    rows = 32
    assert n_slc <= rows
    m_row = lax.broadcasted_iota(jnp.int32, (rows, LANE), 0)
    n_col = lax.broadcasted_iota(jnp.int32, (rows, LANE), 1)
    per = SLC_LEN // CMP_STRIDE
    back = (CMP_LEN - 1) // CMP_STRIDE
    overlap = ((n_col >= per * m_row - back) & (n_col <= per * m_row + per - 1)
               & (n_col < n_cmp) & (m_row < n_slc)).astype(BF16)
    imp = sum(_dot(overlap, part) for part in _split3(sum(p_heads)))
    m_blk = lax.broadcasted_iota(jnp.int32, (rows, 1), 0)
    cur = qt // SLC_LEN
    valid = m_blk <= cur
    forced = valid & ((m_blk == 0) | (m_blk >= cur - 1))
    score = jnp.where(forced, FORCED_SCORE, jnp.where(valid, imp, -1.0))
    rank = jnp.zeros((rows, tq), jnp.int32)
    for mp in range(n_slc):
        other = score[mp:mp + 1, :]
        ahead = (other > score) | ((other == score) & (m_blk > mp))
        rank = rank + ahead.astype(jnp.int32)
    selected = valid & (rank < min(SLC_TOPN, n_slc))
    sel_penalty = jnp.where(selected, 0.0, MASKED).astype(BF16)

    SLC, WIN = 0, 1
    _softmax_init(m_ref, acc_ref)
    sub = tk // LANE

    qchunks = [slice(c, c + LANE) for c in range(0, tq, LANE)]

    assert tq == tk

    def slc_masks(kidx, diagonal):
        expand = (kidx // SLC_LEN == lax.broadcasted_iota(jnp.int32, (1, rows), 1)).astype(BF16)
        penalty = _dot(expand, sel_penalty)
        if diagonal:
            return [(penalty[:, c], kidx <= qt[:, c]) for c in qchunks]
        return [(penalty[:, c],) for c in qchunks]

    def win_masks(kidx, diagonal):
        return [(jnp.where(lax.bitcast_convert_type(qt[:, c] - kidx, jnp.uint32) < WINDOW,
                           0.0, MASKED),) for c in qchunks]

    k_refs, vt_refs, masks_of = (ks_ref, kw_ref), (vst_ref, vwt_ref), (slc_masks, win_masks)
    adds = [far_bias[j] for j in range(J) for _ in qchunks]
    last = (t0 + tq + tk - 1) // tk
    first_win = jnp.maximum(t0 - (WINDOW - 1), 0) // tk

    def head_logits(kt, buf, slots, j):
        ks = pl.multiple_of(kt * tk, tk)
        keys = [k_refs[slot][pl.ds(ks, tk), :] for slot in slots]
        s = _dot_nt(keys[0] if len(keys) == 1 else jnp.concatenate(keys, axis=0), q_head(j))
        for n, slot in enumerate(slots):
            s_ref[buf, slot, :, head[j]] = s[n * tk:(n + 1) * tk]

    def logits(kt, buf, slots):
        for j in range(J):
            head_logits(kt, buf, slots, j)

    def near_bias(kt, buf, slots):
        ks = pl.multiple_of(kt * tk, tk)
        for hb in range(sub):
            kb = kt * sub + hb
            for qc, qcols in enumerate(qchunks):
                near = ((far_ref[((b * nq + i) * len(qchunks) + qc) * (nk * sub) + kb] == 0)
                        & (kb * LANE < t0 + (qc + 1) * LANE))

                @pl.when(near)
                def _():
                    kpos = pcol_ref[pl.ds(pl.multiple_of(ks + hb * LANE, LANE), LANE), :]
                    idx = jnp.clip(qpos[:, qcols] - kpos, 0, LANE - 1)
                    for j in range(J):
                        delta = _lookup(tbl[j:j + 1, :] - far_bias[j], idx)
                        cols = slice(j * tq + qc * LANE, j * tq + (qc + 1) * LANE)
                        for slot in slots:
                            s_ref[buf, slot, hb * LANE:(hb + 1) * LANE, cols] += delta

    def step(kt, buf, slots, final=False):
        ahead = [] if final else [functools.partial(head_logits, kt + 1, 1 - buf, slots, j)
                                  for j in range(J)]
        share = len(ahead) // len(slots)
        ks = pl.multiple_of(kt * tk, tk)
        kidx = ks + lax.broadcasted_iota(jnp.int32, (tk, 1), 0)
        for n, slot in enumerate(slots):
            _softmax_tile_t(s_ref.at[buf, slot], adds, masks_of[slot](kidx, final) * J,
                            vt_refs[slot][:, pl.ds(ks, tk)], m_ref.at[slot], acc_ref.at[slot],
                            interleave=ahead[n * share:(n + 1) * share])
        if not final:
            near_bias(kt + 1, 1 - buf, slots)

    def slc_body(kt, c):
        _by_parity(kt, lambda buf: step(kt, buf, (SLC,)))
        return c

    def both_body(kt, c):
        _by_parity(kt, lambda buf: step(kt, buf, (SLC, WIN)))
        return c

    def enter_window(buf):
        logits(first_win, buf, (WIN,))
        near_bias(first_win, buf, (WIN,))

    logits(0, 0, (SLC,))
    near_bias(0, 0, (SLC,))
    lax.fori_loop(0, first_win, slc_body, 0)
    _by_parity(first_win, enter_window)
    lax.fori_loop(first_win, last - 1, both_body, 0)
    _by_parity(last - 1, lambda buf: step(last - 1, buf, (SLC, WIN), final=True))
    o_slc = _softmax_out(acc_ref[SLC])
    o_win = _softmax_out(acc_ref[WIN])

    gate_ref[...] = jax.nn.sigmoid(ng_ref[...].astype(F32)).T
    first_gate = pl.program_id(1) * (J * 3)
    for j in range(J):
        gate = [gate_ref[pl.ds(first_gate + 3 * j + c, 1), :] for c in range(3)]
        o = (gate[0] * o_cmp[:, head[j]] + gate[1] * o_slc[:, head[j]]
             + gate[2] * o_win[:, head[j]])
        o_ref[:, j * NSA_DV:(j + 1) * NSA_DV] = o.T.astype(o_ref.dtype)


def _nsa_attn(far, z, kc, vct, v_t, pos_row, pos_col, pos_cmp, tbl, batch, seq, zoff):
    tq, tk = NSA_TQ, NSA_TK
    nq = seq // tq
    G, J = NSA_KV_HEADS, NSA_GROUP
    R = J * tq
    kern = functools.partial(_nsa_attn_kernel, tq=tq, tk=tk, seq=seq)
    grid_spec = pltpu.PrefetchScalarGridSpec(
        num_scalar_prefetch=1,
        grid=(batch, G, nq),
        in_specs=[
            pl.BlockSpec((tq, J * HEAD_PAD), lambda b, g, i, far: (b * nq + i, g)),
            pl.BlockSpec((None, LANE, HEAD_PAD), lambda b, g, i, far: (b * G + g, 0, 0)),
            pl.BlockSpec((None, NSA_DV, LANE), lambda b, g, i, far: (b * G + g, 0, 0)),
            pl.BlockSpec((seq, HEAD_PAD), lambda b, g, i, far: (b, zoff["ks"] // HEAD_PAD + g)),
            pl.BlockSpec((NSA_DV, seq), lambda b, g, i, far: (g, b)),
            pl.BlockSpec((seq, HEAD_PAD), lambda b, g, i, far: (b, zoff["kw"] // HEAD_PAD + g)),
            pl.BlockSpec((NSA_DV, seq), lambda b, g, i, far: (G + g, b)),
            pl.BlockSpec((tq, LANE), lambda b, g, i, far: (b * nq + i, zoff["ng"] // LANE)),
            pl.BlockSpec((None, 1, seq), lambda b, g, i, far: (b, 0, 0)),
            pl.BlockSpec((seq, 1), lambda b, g, i, far: (b, 0)),
            pl.BlockSpec((LANE, 1), lambda b, g, i, far: (b, 0)),
            pl.BlockSpec((None, J, LANE), lambda b, g, i, far: (g, 0, 0)),
        ],
        out_specs=pl.BlockSpec((tq, J * NSA_DV), lambda b, g, i, far: (b * nq + i, g)),
        scratch_shapes=[pltpu.VMEM((2, 2, tk, R), F32), pltpu.VMEM((2, 1, R), F32),
                        pltpu.VMEM((2, NSA_DV + ONES_ROWS, R), F32),
                        pltpu.VMEM((LANE, tq), F32)],
    )
    return pl.pallas_call(
        kern,
        grid_spec=grid_spec,
        out_shape=jax.ShapeDtypeStruct((batch * seq, NSA_HEADS * NSA_DV), BF16),
        compiler_params=_params("parallel", "parallel", "parallel"),
        name="nsa_attn",
    )(far, z, kc, vct, z, v_t, z, v_t, z, pos_row, pos_col, pos_cmp, tbl)


def _merge_kernel(a_ref, b_ref, ga_ref, gb_ref, h_ref, wa_ref, wb_ref, wo_ref, post_g_ref,
                  next_g_ref, o_ref, un_ref):
    ya = _dot(a_ref[...], wa_ref[...])
    yb = _dot(b_ref[...], wb_ref[...])
    m = (jax.nn.sigmoid(ga_ref[...].astype(F32)) * ya
         + jax.nn.sigmoid(gb_ref[...].astype(F32)) * yb).astype(BF16)
    y = _dot(m, wo_ref[...])
    out = h_ref[...] + _rms(y, post_g_ref[...])
    o_ref[...] = out
    un_ref[...] = _rms(out, next_g_ref[...]).astype(BF16)


def _merge(a, bb, z, h, wa, wb, wo, post_g, next_g, layer, zoff):
    m, d = h.shape
    tm = MERGE_TM
    da, db = a.shape[1], bb.shape[1]
    gblk = zoff["mg"] // d
    const = lambda i: (layer, 0, 0)
    return pl.pallas_call(
        _merge_kernel,
        grid=(m // tm,),
        in_specs=[pl.BlockSpec((tm, da), lambda i: (i, 0)),
                  pl.BlockSpec((tm, db), lambda i: (i, 0)),
                  pl.BlockSpec((tm, d), lambda i: (i, gblk)),
                  pl.BlockSpec((tm, d), lambda i: (i, gblk + 1)),
                  pl.BlockSpec((tm, d), lambda i: (i, 0)),
                  pl.BlockSpec((None, da, d), const, pipeline_mode=pl.Buffered(1)),
                  pl.BlockSpec((None, db, d), const, pipeline_mode=pl.Buffered(1)),
                  pl.BlockSpec((None, d, d), const, pipeline_mode=pl.Buffered(1)),
                  pl.BlockSpec((None, 1, d), const),
                  pl.BlockSpec((None, 1, d), const)],
        out_specs=[pl.BlockSpec((tm, d), lambda i: (i, 0)),
                   pl.BlockSpec((tm, d), lambda i: (i, 0))],
        out_shape=[jax.ShapeDtypeStruct((m, d), F32), jax.ShapeDtypeStruct((m, d), BF16)],
        compiler_params=_params("parallel"),
        name="merge_out",
    )(a, bb, z, z, h, wa, wb, wo, post_g, next_g)


def _z_layout(d_model):
    G = NSA_KV_HEADS
    widths = [("q", NSA_HEADS * HEAD_PAD), ("mg", 2 * d_model), ("cq", MLA_Q_LORA),
              ("ckv", MLA_KV_LORA), ("kc", G * HEAD_PAD), ("ks", G * HEAD_PAD),
              ("kw", G * HEAD_PAD), ("vc", G * NSA_DV), ("kr", LANE), ("ng", LANE)]
    off, pos = {}, 0
    for name, w in widths:
        off[name] = pos
        pos += w
    off["total"] = pos
    assert off["q"] == 0 and off["mg"] % d_model == 0
    assert off["cq"] % MLA_Q_LORA == 0 and off["ckv"] % MLA_KV_LORA == 0
    assert all(off[k] % HEAD_PAD == 0 for k in ("kc", "ks", "kw"))
    return off


def _swap_halves(w):
    half = w.shape[-1] // 2
    return jnp.concatenate([-w[..., half:], w[..., :half]], axis=-1)


def _pad_last(w, width):
    return jnp.pad(w, [(0, 0)] * (w.ndim - 1) + [(0, width - w.shape[-1])])


def _w_in_pieces(d_model, zoff):
    G, J = NSA_KV_HEADS, NSA_GROUP
    splits = [MLA_Q_LORA, MLA_KV_LORA, MLA_ROPE, NSA_HEADS * NSA_DK,
              G * NSA_DK, G * NSA_DV, G * NSA_DK, G * NSA_DV, G * NSA_DK, G * NSA_DV,
              NSA_HEADS * 3, 2 * d_model]
    names = ["cq", "ckv", "kr", "q", "kc", "vc", "ks", "vs", "kw", "vw", "ng", "mg"]
    src = dict(zip(names, [0] + [int(v) for v in np.cumsum(splits)[:-1]]))
    half = MLA_ROPE // 2
    pieces = []
    for h in range(NSA_HEADS):
        pieces.append((zoff["q"] + h * HEAD_PAD, HEAD_PAD,
                       [(src["q"] + h * NSA_DK, NSA_DK, NSA_DK ** -0.5 * LOG2E)]))
    pieces.append((zoff["mg"], 2 * d_model, [(src["mg"], 2 * d_model, None)]))
    pieces.append((zoff["cq"], MLA_Q_LORA, [(src["cq"], MLA_Q_LORA, None)]))
    pieces.append((zoff["ckv"], MLA_KV_LORA, [(src["ckv"], MLA_KV_LORA, None)]))
    for name in ("kc", "ks", "kw"):
        for g in range(G):
            pieces.append((zoff[name] + g * HEAD_PAD, HEAD_PAD, [(src[name] + g * NSA_DK, NSA_DK, None)]))
    pieces.append((zoff["vc"], G * NSA_DV, [(src["vc"], G * NSA_DV, None)]))
    pieces.append((zoff["kr"], LANE, [(src["kr"], MLA_ROPE, None), (src["kr"] + half, half, -1.0),
                                      (src["kr"], half, None)]))
    pieces.append((zoff["ng"], LANE, [(src["ng"], NSA_HEADS * 3, None)]))
    return pieces, (src["vs"], src["vw"])


COPY_ROWS = 512


def _copy_rows(w_ref, o_ref, src, dst, n, scale):
    for off in range(0, n, COPY_ROWS):
        m = min(COPY_ROWS, n - off)
        x = w_ref[src + off:src + off + m, :]
        o_ref[dst + off:dst + off + m, :] = (x if scale is None else x * scale).astype(BF16)


def _w_in_layout_kernel(w_ref, o_ref, vt_ref, *, pieces, value_rows):
    tc = w_ref.shape[1]
    dst = 0
    for src, n in value_rows:
        _copy_rows(w_ref, vt_ref, src, dst, n, None)
        dst += n
    for dst, width, parts in pieces:
        used = sum(n for _, n, _ in parts)
        if all(n % ONES_ROWS == 0 for _, n, _ in parts):
            at = dst
            for src, n, scale in parts:
                _copy_rows(w_ref, o_ref, src, at, n, scale)
                at += n
            if used < width:
                o_ref[at:dst + width, :] = jnp.zeros((width - used, tc), BF16)
        else:
            rows = [w_ref[src:src + n, :] if scale is None else w_ref[src:src + n, :] * scale
                    for src, n, scale in parts]
            rows.append(jnp.zeros((width - used, tc), F32))
            o_ref[dst:dst + width, :] = jnp.concatenate(rows, axis=0).astype(BF16)


def _layout_w_in(w_in, zoff):
    L, D, n_in = w_in.shape
    G = NSA_KV_HEADS
    pieces, (vs, vw) = _w_in_pieces(D, zoff)
    tc = 256
    nv = 2 * G * NSA_DV
    value_rows = ((vs, G * NSA_DV), (vw, G * NSA_DV))
    return pl.pallas_call(
        functools.partial(_w_in_layout_kernel, pieces=pieces, value_rows=value_rows),
        grid=(L, D // tc),
        in_specs=[pl.BlockSpec((None, n_in, tc), lambda l, c: (l, 0, c))],
        out_specs=[pl.BlockSpec((None, zoff["total"], tc), lambda l, c: (l, 0, c)),
                   pl.BlockSpec((None, nv, tc), lambda l, c: (l, 0, c))],
        out_shape=[jax.ShapeDtypeStruct((L, zoff["total"], D), BF16),
                   jax.ShapeDtypeStruct((L, nv, D), BF16)],
        compiler_params=_params("parallel", "parallel"),
        name="w_in_layout",
    )(jnp.swapaxes(w_in, 1, 2))


def _layout_w_q_up(w):
    L, r, _ = w.shape
    w = w.reshape(L, r, MLA_HEADS, MLA_NOPE + MLA_ROPE) * ((MLA_NOPE + MLA_ROPE) ** -0.5 * LOG2E)
    rope = w[..., MLA_NOPE:]
    return jnp.concatenate([w[..., :MLA_NOPE], rope, _swap_halves(rope)], axis=-1).reshape(
        L, r, MLA_HEADS * HEAD_PAD).astype(BF16)


def kernel(x, positions, rel_bias, ffn1_pre_g, ffn1_post_g, ffn1_w_gate, ffn1_w_up, ffn1_w_down, mix_pre_g, mix_post_g, w_in, mla_q_norm_g, mla_w_q_up, mla_kv_norm_g, mla_w_uk, mla_w_uv, cmp_pe_k, cmp_w1_k, cmp_w2_k, cmp_pe_v, cmp_w1_v, cmp_w2_v, w_branch_mla, w_branch_nsa, w_out, ffn2_pre_g, ffn2_post_g, ffn2_w_gate, ffn2_w_up, ffn2_w_down):
    B, S, D = x.shape
    L = w_in.shape[0]
    M = B * S
    G, J = NSA_KV_HEADS, NSA_GROUP
    zoff = _z_layout(D)
    n_half = S // CMP_STRIDE
    n_cmp = (S - CMP_LEN) // CMP_STRIDE + 1
    assert n_half == LANE and n_cmp <= LANE

    gain = lambda g: g.reshape(L, 1, -1)
    bf = lambda w: w.astype(BF16)
    w_z, w_vt = _layout_w_in(w_in, zoff)
    w_q = _layout_w_q_up(mla_w_q_up)
    w_uk, w_uv_t = bf(mla_w_uk), bf(mla_w_uv).transpose(0, 2, 1)
    pe_k = _pad_last(cmp_pe_k, HEAD_PAD)
    w1_k = bf(jnp.pad(cmp_w1_k.reshape(L, CMP_LEN, NSA_DK, CMP_HIDDEN),
                      ((0, 0), (0, 0), (0, HEAD_PAD - NSA_DK), (0, 0)))
              ).reshape(L, CMP_LEN * HEAD_PAD, CMP_HIDDEN)
    w2_k = bf(_pad_last(cmp_w2_k, HEAD_PAD))
    f1 = (bf(ffn1_w_gate), bf(ffn1_w_up), ffn1_w_down)
    f2 = (bf(ffn2_w_gate), bf(ffn2_w_up), ffn2_w_down)
    w1_v, w2_v = bf(cmp_w1_v), bf(cmp_w2_v)
    w_a, w_b, w_o = bf(w_branch_mla), bf(w_branch_nsa), bf(w_out)
    g_f1pre, g_f1post, g_mpre, g_mpost = gain(ffn1_pre_g), gain(ffn1_post_g), gain(mix_pre_g), gain(mix_post_g)
    g_f2pre, g_f2post, g_q, g_kv = gain(ffn2_pre_g), gain(ffn2_post_g), gain(mla_q_norm_g), gain(mla_kv_norm_g)

    pos_col = positions.reshape(M, 1)
    pos_row = positions.reshape(B, 1, S)
    pos_cmp = _pad_last(positions[:, CMP_LEN - 1::CMP_STRIDE][:, :n_cmp], LANE).reshape(B * LANE, 1)
    q_min = positions.reshape(B, S // LANE, LANE).min(axis=-1)
    k_max = positions.reshape(B, S // LANE, LANE).max(axis=-1)
    far = (q_min[:, :, None] - k_max[:, None, :] >= FAR_DIST).astype(jnp.int32).reshape(-1)
    tbl = (jnp.take(rel_bias, jnp.asarray(BUCKET_OF_DIST), axis=0).T * LOG2E).reshape(G, J, LANE)
    half = MLA_ROPE // 2
    inv = ROPE_BASE ** (-jnp.arange(half, dtype=F32) * 2.0 / MLA_ROPE)
    cs = _rope_table(pos_col, jnp.concatenate([inv, inv]).reshape(1, MLA_ROPE))

    h = x.reshape(M, D)
    u = _rmsnorm(h, g_f1pre, 0)
    for l in range(L):
        h, u = _ffn(h, u, 0.5 * g_f1post, g_mpre, l, *f1, l)
        z = _in_proj(u, w_z, l)
        v_t = _in_proj_t(u, w_vt, l)
        qp, kp, vt = _mla_prep(z, cs, g_q, g_kv, w_q, w_uk, w_uv_t, l, zoff)
        a = _mla_attn(qp, kp, vt, B, S)

        kc, vct = _nsa_cmp(z, pe_k, cmp_pe_v, w1_k, w2_k, w1_v, w2_v, l, B, S, zoff)
        nsa = _nsa_attn(far, z, kc, vct, v_t, pos_row, pos_col, pos_cmp, tbl, B, S, zoff)
        h, u = _merge(a, nsa, z, h, w_a, w_b, w_o, g_mpost, g_f2pre, l, zoff)
        h, u = _ffn(h, u, 0.5 * g_f2post, g_f1pre, min(l + 1, L - 1), *f2, l)
    return h.reshape(B, S, D)
```
